```python
import jax, jax.numpy as jnp
from jax import lax
import numpy as np

D_MODEL = 1024
BATCH = 4
SEQ = 4096
DEPTH = 4
DEC_BATCH = 32
DEC_SEQ = 1
PAST_LEN = 8192
PAGE_SIZE = 128

N_MIXERS = 4
MIXER_STICK = 0
MIXER_FOX = 1
MIXER_MOBA = 2
MIXER_DSA = 3
N_HEADS = 16
N_KV_HEADS = 4
HEAD_DIM = D_MODEL // N_HEADS
KV_GROUP = N_HEADS // N_KV_HEADS
D_FF = 2816
CONV_WIDTH = 3
ROPE_THETA = 10000.0
RMS_EPS = 1e-6
Q_BLOCK = 128
MOBA_BLOCK = 256
MOBA_TOPK = 3
MOBA_Q_CHUNK = 16
IDX_HEADS = 8
IDX_DIM = 64
IDX_TOPK_MAX = 256
FORGET_BIAS_INIT = 3.0

kernel_name = "hybrid_stickbreak_fox_moba_dsa_step"


def rms_norm(x, g):
    xf = x.astype(jnp.float32)
    y = xf * lax.rsqrt(jnp.mean(xf * xf, axis=-1, keepdims=True) + RMS_EPS)
    return (y * g.astype(jnp.float32)).astype(x.dtype)


def rotary(x, pos):
    d = x.shape[-1]
    half = d // 2
    inv_freq = jnp.power(ROPE_THETA, -jnp.arange(half, dtype=jnp.float32) * (2.0 / d))
    ang = pos.astype(jnp.float32)[:, None] * inv_freq[None, :]
    cos = jnp.cos(ang)[None, :, None, :]
    sin = jnp.sin(ang)[None, :, None, :]
    xf = x.astype(jnp.float32)
    x1, x2 = xf[..., :half], xf[..., half:]
    return jnp.concatenate([x1 * cos - x2 * sin, x2 * cos + x1 * sin], axis=-1).astype(x.dtype)


def split_qkv(qkv, B, T):
    nq = N_HEADS * HEAD_DIM
    nk = N_KV_HEADS * HEAD_DIM
    q = qkv[..., :nq].reshape(B, T, N_HEADS, HEAD_DIM)
    k = qkv[..., nq:nq + nk].reshape(B, T, N_KV_HEADS, HEAD_DIM)
    v = qkv[..., nq + nk:].reshape(B, T, N_KV_HEADS, HEAD_DIM)
    return q, k, v


def gather_pages(pool, page_table):
    rows = pool[page_table]
    db, n_pages, page = rows.shape[:3]
    return rows.reshape((db, n_pages * page) + rows.shape[3:])


def extend_with_past(new, pool, page_table):
    if page_table is None:
        return new
    return jnp.concatenate([gather_pages(pool, page_table).astype(new.dtype), new], axis=1)


def sweep_queries(fn, q_arrays, q_off, block):
    tq = q_arrays[0].shape[1]
    blk = block if tq % block == 0 else tq
    nb = tq // blk
    pos = (q_off + jnp.arange(tq, dtype=jnp.int32)).reshape(nb, blk)
    split = tuple(jnp.moveaxis(a.reshape((a.shape[0], nb, blk) + a.shape[2:]), 1, 0) for a in q_arrays)
    out = lax.map(lambda args: fn(args[0], *args[1]), (pos, split))
    out = jnp.moveaxis(out, 0, 1)
    return out.reshape((out.shape[0], tq) + out.shape[3:])


def stick_breaking_attention(q, k, v, q_off):
    B, _, H, Dh = q.shape
    L = k.shape[1]
    key_pos = jnp.arange(L, dtype=jnp.int32)
    scale = Dh ** -0.5

    def block(pos, qb):
        Q = qb.shape[1]
        qg = qb.reshape(B, Q, N_KV_HEADS, KV_GROUP, Dh)
        z = jnp.einsum('bqngd,bsnd->bngqs', qg, k, preferred_element_type=jnp.float32) * scale
        past = key_pos[None, :] < pos[:, None]
        log_keep = jnp.where(past, jax.nn.log_sigmoid(-z), 0.0)
        log_between = lax.cumsum(log_keep, axis=z.ndim - 1, reverse=True) - log_keep
        w = jnp.where(past, jnp.exp(jax.nn.log_sigmoid(z) + log_between), 0.0)
        o = jnp.einsum('bngqs,bsnd->bqngd', w.astype(v.dtype), v)
        return o.reshape(B, Q, H, Dh)

    return sweep_queries(block, (q,), q_off, Q_BLOCK)


def forgetting_attention(q, k, v, log_f, q_off):
    B, _, H, Dh = q.shape
    L = k.shape[1]
    key_pos = jnp.arange(L, dtype=jnp.int32)
    scale = Dh ** -0.5
    cum = lax.cumsum(log_f.astype(jnp.float32), axis=1)
    cum = jnp.transpose(cum, (0, 2, 1)).reshape(B, N_KV_HEADS, KV_GROUP, L)

    def block(pos, qb):
        Q = qb.shape[1]
        qg = qb.reshape(B, Q, N_KV_HEADS, KV_GROUP, Dh)
        z = jnp.einsum('bqngd,bsnd->bngqs', qg, k, preferred_element_type=jnp.float32) * scale
        decay = cum[..., pos][..., None] - cum[..., None, :]
        visible = key_pos[None, :] <= pos[:, None]
        p = jax.nn.softmax(jnp.where(visible, z + decay, -jnp.inf), axis=-1)
        o = jnp.einsum('bngqs,bsnd->bqngd', p.astype(v.dtype), v)
        return o.reshape(B, Q, H, Dh)

    return sweep_queries(block, (q,), q_off, Q_BLOCK)


def moba_attention(q, k, v, q_off):
    B, _, H, Dh = q.shape
    L = k.shape[1]
    nb = -(-L // MOBA_BLOCK)
    pad = nb * MOBA_BLOCK - L
    scale = Dh ** -0.5

    def to_blocks(a):
        a = jnp.pad(a, ((0, 0), (0, pad), (0, 0), (0, 0)))
        return a.reshape(B, nb, MOBA_BLOCK, N_KV_HEADS, Dh).transpose(0, 3, 1, 2, 4)

    kb, vb = to_blocks(k), to_blocks(v)
    kv_head = jnp.arange(H) // KV_GROUP
    k_mean = jnp.mean(kb.astype(jnp.float32), axis=3)[:, kv_head]
    n_sel = min(MOBA_TOPK, nb)
    bidx = jnp.arange(B)[:, None, None, None]
    hidx = kv_head[None, :, None, None]
    in_block = jnp.arange(MOBA_BLOCK, dtype=jnp.int32)
    blk_ids = jnp.arange(nb, dtype=jnp.int32)

    def block(pos, qb):
        Q = qb.shape[1]
        qh = jnp.transpose(qb, (0, 2, 1, 3))
        own = pos // MOBA_BLOCK
        gate = jnp.einsum('bhqd,bhnd->bhqn', qh.astype(jnp.float32), k_mean)
        gate = jnp.where(blk_ids[None, :] < own[:, None], gate, -jnp.inf)
        _, sel = lax.top_k(gate, n_sel)
        sel = sel.astype(jnp.int32)
        own_b = jnp.broadcast_to(own[None, None, :, None], (B, H, Q, 1))
        chosen = jnp.concatenate([sel, own_b], axis=-1)
        chosen_ok = jnp.concatenate([sel < own[None, None, :, None], jnp.ones((B, H, Q, 1), dtype=bool)], axis=-1)
        kg = kb[bidx, hidx, chosen]
        vg = vb[bidx, hidx, chosen]
        s = jnp.einsum('bhqd,bhqnmd->bhqnm', qh, kg, preferred_element_type=jnp.float32) * scale
        key_pos = chosen[..., None] * MOBA_BLOCK + in_block
        ok = chosen_ok[..., None] & (key_pos <= pos[None, None, :, None, None])
        s = jnp.where(ok, s, -jnp.inf)
        ns = chosen.shape[-1]
        p = jax.nn.softmax(s.reshape(B, H, Q, ns * MOBA_BLOCK), axis=-1).reshape(s.shape)
        return jnp.einsum('bhqnm,bhqnmd->bqhd', p.astype(v.dtype), vg)

    return sweep_queries(block, (q,), q_off, MOBA_Q_CHUNK)


def dsa_attention(q, q_idx, w_idx, k, v, k_idx, q_off):
    B, _, H, Dh = q.shape
    L = k.shape[1]
    n_keep = min(IDX_TOPK_MAX, L // 4)
    key_pos = jnp.arange(L, dtype=jnp.int32)
    bidx = jnp.arange(B)[:, None, None]
    scale = Dh ** -0.5

    def block(pos, qb, qib, wib):
        Q = qb.shape[1]
        dots = jnp.einsum('bqid,bsd->bqis', qib, k_idx, preferred_element_type=jnp.float32)
        score = jnp.einsum('bqi,bqis->bqs', wib.astype(jnp.float32), jax.nn.relu(dots))
        score = jnp.where(key_pos[None, None, :] <= pos[None, :, None], score, -jnp.inf)
        _, sel = lax.top_k(score, n_keep)
        sel_ok = sel <= pos[None, :, None]
        kg = k[bidx, sel]
        vg = v[bidx, sel]
        qg = qb.reshape(B, Q, N_KV_HEADS, KV_GROUP, Dh)
        s = jnp.einsum('bqngd,bqknd->bqngk', qg, kg, preferred_element_type=jnp.float32) * scale
        p = jax.nn.softmax(jnp.where(sel_ok[:, :, None, None, :], s, -jnp.inf), axis=-1)
        o = jnp.einsum('bqngk,bqknd->bqngd', p.astype(v.dtype), vg)
        return o.reshape(B, Q, H, Dh)

    return sweep_queries(block, (q, q_idx, w_idx), q_off, Q_BLOCK)


def conv_ffn(h, prev, w_a, w_b, conv_w, conv_b, w_down):
    a = h @ w_a
    g = h @ w_b
    a_ext = jnp.concatenate([prev.astype(a.dtype), a], axis=1)
    T = a.shape[1]
    conv = conv_b + sum(conv_w[j] * a_ext[:, j:j + T] for j in range(CONV_WIDTH))
    y = (jax.nn.gelu(conv) * g) @ w_down
    return y, a_ext[:, -(CONV_WIDTH - 1):]


def decoder_trunk(x, q_off, cache_k, cache_v, cache_logf, cache_kidx, state_conv, page_table,
                  norm_mix, norm_ffn, norm_final, w_qkv, w_o, fox_w_f, fox_b_f,
                  idx_w_q, idx_w_k, idx_w_w, ffn_w_a, ffn_w_b, ffn_conv_w, ffn_conv_b, ffn_w_down):
    B, T, _ = x.shape
    decoding = page_table is not None
    pos = q_off + jnp.arange(T, dtype=jnp.int32)
    new_k, new_v, new_conv = [], [], []
    new_logf = None
    new_kidx = None
    for i in range(DEPTH):
        kind = i % N_MIXERS
        h = rms_norm(x, norm_mix[i])
        q, k, v = split_qkv(h @ w_qkv[i], B, T)
        if kind in (MIXER_MOBA, MIXER_DSA):
            q = rotary(q, pos)
            k = rotary(k, pos)
        new_k.append(k)
        new_v.append(v)
        k_all = extend_with_past(k, cache_k[i] if decoding else None, page_table)
        v_all = extend_with_past(v, cache_v[i] if decoding else None, page_table)
        if kind == MIXER_STICK:
            o = stick_breaking_attention(q, k_all, v_all, q_off)
        elif kind == MIXER_FOX:
            logf = jax.nn.log_sigmoid((h @ fox_w_f + fox_b_f).astype(jnp.float32))
            new_logf = logf
            o = forgetting_attention(q, k_all, v_all, extend_with_past(logf, cache_logf, page_table), q_off)
        elif kind == MIXER_MOBA:
            o = moba_attention(q, k_all, v_all, q_off)
        else:
            qi = rotary((h @ idx_w_q).reshape(B, T, IDX_HEADS, IDX_DIM), pos)
            ki = rotary((h @ idx_w_k).reshape(B, T, 1, IDX_DIM), pos)[:, :, 0]
            wi = (h @ idx_w_w) * (IDX_HEADS ** -0.5 * IDX_DIM ** -0.5)
            new_kidx = ki
            o = dsa_attention(q, qi, wi, k_all, v_all, extend_with_past(ki, cache_kidx, page_table), q_off)
        x = x + o.reshape(B, T, D_MODEL) @ w_o[i]
        h = rms_norm(x, norm_ffn[i])
        prev = state_conv[i] if decoding else jnp.zeros((B, CONV_WIDTH - 1, D_FF), x.dtype)
        y, conv_rows = conv_ffn(h, prev, ffn_w_a[i], ffn_w_b[i], ffn_conv_w[i], ffn_conv_b[i], ffn_w_down[i])
        new_conv.append(conv_rows)
        x = x + y
    return (rms_norm(x, norm_final), jnp.stack(new_k), jnp.stack(new_v), new_logf, new_kidx, jnp.stack(new_conv))


def setup_inputs(seed: int = 0) -> dict:
    key = jax.random.key(seed)
    ks = jax.random.split(key, 24)
    f32 = jnp.float32
    n_pages = PAST_LEN // PAGE_SIZE
    n_used = DEC_BATCH * n_pages
    n_pool = n_used + max(1, n_used // 4)
    qkv_cols = (N_HEADS + 2 * N_KV_HEADS) * HEAD_DIM

    def nrm(k, shape, scale=1.0):
        return jax.random.normal(k, shape, f32) * scale

    x_prompt = nrm(ks[0], (BATCH, SEQ, D_MODEL))
    x_sample = nrm(ks[1], (DEC_BATCH, DEC_SEQ, D_MODEL))
    cache_k = nrm(ks[2], (DEPTH, n_pool, PAGE_SIZE, N_KV_HEADS, HEAD_DIM))
    cache_v = nrm(ks[3], (DEPTH, n_pool, PAGE_SIZE, N_KV_HEADS, HEAD_DIM))
    cache_logf = jax.nn.log_sigmoid(FORGET_BIAS_INIT + nrm(ks[4], (n_pool, PAGE_SIZE, N_HEADS)))
    cache_kidx = nrm(ks[5], (n_pool, PAGE_SIZE, IDX_DIM))
    state_conv = nrm(ks[6], (DEPTH, DEC_BATCH, CONV_WIDTH - 1, D_FF))
    page_table = jax.random.permutation(ks[7], n_pool)[:n_used].reshape(DEC_BATCH, n_pages).astype(jnp.int32)

    norm_mix = 1.0 + nrm(ks[8], (DEPTH, D_MODEL), 0.01)
    norm_ffn = 1.0 + nrm(ks[9], (DEPTH, D_MODEL), 0.01)
    norm_final = 1.0 + nrm(ks[10], (D_MODEL,), 0.01)
    w_qkv = nrm(ks[11], (DEPTH, D_MODEL, qkv_cols), D_MODEL ** -0.5)
    w_o = nrm(ks[12], (DEPTH, D_MODEL, D_MODEL), D_MODEL ** -0.5)
    fox_w_f = nrm(ks[13], (D_MODEL, N_HEADS), D_MODEL ** -0.5)
    fox_b_f = FORGET_BIAS_INIT + nrm(ks[14], (N_HEADS,), 0.1)
    idx_w_q = nrm(ks[15], (D_MODEL, IDX_HEADS * IDX_DIM), D_MODEL ** -0.5)
    idx_w_k = nrm(ks[16], (D_MODEL, IDX_DIM), D_MODEL ** -0.5)
    idx_w_w = nrm(ks[17], (D_MODEL, IDX_HEADS), D_MODEL ** -0.5)
    ffn_w_a = nrm(ks[18], (DEPTH, D_MODEL, D_FF), D_MODEL ** -0.5)
    ffn_w_b = nrm(ks[19], (DEPTH, D_MODEL, D_FF), D_MODEL ** -0.5)
    ffn_conv_w = nrm(ks[20], (DEPTH, CONV_WIDTH, D_FF), CONV_WIDTH ** -0.5)
    ffn_conv_b = nrm(ks[21], (DEPTH, D_FF), 0.01)
    ffn_w_down = nrm(ks[22], (DEPTH, D_FF, D_MODEL), D_FF ** -0.5)
    return {"x_prompt": x_prompt, "x_sample": x_sample, "cache_k": cache_k, "cache_v": cache_v,
            "cache_logf": cache_logf, "cache_kidx": cache_kidx, "state_conv": state_conv,
            "page_table": page_table, "norm_mix": norm_mix, "norm_ffn": norm_ffn, "norm_final": norm_final,
            "w_qkv": w_qkv, "w_o": w_o, "fox_w_f": fox_w_f, "fox_b_f": fox_b_f, "idx_w_q": idx_w_q,
            "idx_w_k": idx_w_k, "idx_w_w": idx_w_w, "ffn_w_a": ffn_w_a, "ffn_w_b": ffn_w_b,
            "ffn_conv_w": ffn_conv_w, "ffn_conv_b": ffn_conv_b, "ffn_w_down": ffn_w_down}


def reference(x_prompt, x_sample, cache_k, cache_v, cache_logf, cache_kidx, state_conv, page_table,
              norm_mix, norm_ffn, norm_final, w_qkv, w_o, fox_w_f, fox_b_f, idx_w_q, idx_w_k, idx_w_w,
              ffn_w_a, ffn_w_b, ffn_conv_w, ffn_conv_b, ffn_w_down):
    weights = (norm_mix, norm_ffn, norm_final, w_qkv, w_o, fox_w_f, fox_b_f, idx_w_q, idx_w_k, idx_w_w,
               ffn_w_a, ffn_w_b, ffn_conv_w, ffn_conv_b, ffn_w_down)
    y_prompt, k_p, v_p, logf_p, kidx_p, conv_p = decoder_trunk(
        x_prompt, 0, None, None, None, None, None, None, *weights)
    past_len = page_table.shape[1] * cache_k.shape[2]
    y_sample, k_s, v_s, logf_s, kidx_s, conv_s = decoder_trunk(
        x_sample, past_len, cache_k, cache_v, cache_logf, cache_kidx, state_conv, page_table, *weights)
    return (y_prompt, y_sample, k_p, v_p, k_s, v_s, logf_p, logf_s, kidx_p, kidx_s, conv_p, conv_s)
```

```python
import functools

import jax
import jax.numpy as jnp
import numpy as np
from jax import lax
from jax.experimental import pallas as pl
from jax.experimental.pallas import tpu as pltpu

F32, BF16, I32 = jnp.float32, jnp.bfloat16, jnp.int32

D_MODEL = 1024
N_HEADS = 16
N_KV = 4
GROUP = N_HEADS // N_KV
HD = 64
KV_W = N_KV * HD
D_FF = 2816
CONV_W = 3
PAGE = 128
MOBA_BLOCK = 256
MOBA_TOPK = 3
IDX_HEADS = 8
IDX_DIM = 64
IDX_TOPK = 256
ROPE_THETA = 10000.0
RMS_EPS = 1e-6
STICK, FOX, MOBA, DSA = 0, 1, 2, 3
NEG = -1e30
INT_MIN = -2147483648
VMEM_LIMIT = 48 * 1024 * 1024
FF_CHUNK = 256
PROJ_TM = 256
KV_TILE = 256


def _cp(*sem):
    return pltpu.CompilerParams(dimension_semantics=sem, vmem_limit_bytes=VMEM_LIMIT)


def _dot(a, b):
    return jnp.dot(a, b, preferred_element_type=F32)


def _dot_nt(a, b):
    return lax.dot_general(a, b, (((1,), (1,)), ((), ())), preferred_element_type=F32)


def _split2(x):
    hi = x.astype(BF16)
    lo = (x - hi.astype(F32)).astype(BF16)
    return hi, lo


def _split3(x):
    hi = x.astype(BF16)
    r = x - hi.astype(F32)
    mid = r.astype(BF16)
    lo = (r - mid.astype(F32)).astype(BF16)
    return hi, mid, lo


def _dot_nt_f32(a, b):
    ah, al = _split2(a)
    bh, bl = _split2(b)
    return _dot_nt(ah, bh) + _dot_nt(al, bh) + _dot_nt(ah, bl)


def _log_sigmoid_pair(z):
    l1p = jnp.log(1.0 + jnp.exp(-jnp.abs(z)))
    return jnp.minimum(z, 0.0) - l1p, jnp.minimum(-z, 0.0) - l1p


def _iota(shape, dim):
    return lax.broadcasted_iota(I32, shape, dim)


def _tri(n, kind):
    r, c = _iota((n, n), 0), _iota((n, n), 1)
    m = {"gt": r > c, "lt": r < c, "ge": r >= c}[kind]
    return jnp.where(m, 1.0, 0.0).astype(BF16)


def _rms_bf16(x, g):
    y = x * lax.rsqrt(jnp.mean(x * x, axis=-1, keepdims=True) + RMS_EPS)
    return (y * g).astype(BF16)


def _rope128(xc, cos, sin):
    lane = _iota(xc.shape, 1)
    first = (lane % HD) < (HD // 2)
    rolled = jnp.where(first, pltpu.roll(xc, LROLL_UP, 1), pltpu.roll(xc, HD // 2, 1))
    return xc * cos + rolled * sin


LROLL_UP = 128 - HD // 2


def _halves(xc):
    return xc[:, :HD], pltpu.roll(xc, HD, 1)[:, :HD]


def _place(parts):
    r, c = _iota((HD, KV_W), 0), _iota((HD, KV_W), 1)
    out = None
    for j, pj in enumerate(parts):
        e = jnp.where(c == r + HD * j, 1.0, 0.0).astype(BF16)
        t = _dot(pj, e)
        out = t if out is None else out + t
    return out


def _online(s, valid, vt, m_ref, l_ref, acc_ref, idx):
    if valid is not None:
        s = jnp.where(valid, s, NEG)
    m_old = m_ref[idx]
    m_new = jnp.maximum(m_old, jnp.max(s, axis=1, keepdims=True))
    alpha = jnp.exp(m_old - m_new)
    p = jnp.exp(s - m_new)
    if valid is not None:
        p = jnp.where(valid, p, 0.0)
    l_ref[idx] = alpha * l_ref[idx] + jnp.sum(p, axis=1, keepdims=True)
    acc_ref[idx] = alpha * acc_ref[idx] + _dot(p.astype(BF16), vt)
    m_ref[idx] = m_new


def _topk_rank_select(gate, n_valid, k, nb):
    lane = _iota(gate.shape, 1)
    gm = jnp.where(lane < n_valid, gate, -jnp.inf)
    sel = jnp.zeros(gate.shape, F32)
    for n in range(nb):
        gn = gm[:, n:n + 1]
        beats = jnp.where(gm > gn, 1.0, jnp.where((gm == gn) & (lane < n), 1.0, 0.0))
        rank = jnp.sum(beats, axis=1, keepdims=True)
        ok = jnp.where((rank < k) & (n < n_valid), 1.0, 0.0)
        sel = jnp.where(lane == n, ok, sel)
    return sel


def _order_key(score):
    bits = lax.bitcast_convert_type(score, I32)
    return jnp.where(bits < 0, bits ^ jnp.int32(0x7FFFFFFF), bits)


def _proj_kernel(*refs, kind, rope, prompt):
    it = iter(refs)
    x_ref, g_ref, w_ref = next(it), next(it), next(it)
    bf_ref = next(it) if kind == FOX else None
    cos_ref, sin_ref = (next(it), next(it)) if rope else (None, None)
    qhm_ref, k32_ref, v32_ref, khm_ref, vhm_ref = next(it), next(it), next(it), next(it), next(it)
    h = _rms_bf16(x_ref[0], g_ref[...])
    y = _dot(h, w_ref[...])
    cos = cos_ref[...] if rope else None
    sin = sin_ref[...] if rope else None

    def chunk(c, rot):
        xc = y[:, 128 * c:128 * (c + 1)]
        return _rope128(xc, cos, sin) if rot else xc

    q32hm_ref = next(it) if kind == MOBA else None
    for c in range(N_HEADS // 2):
        qc = chunk(c, rope)
        lo, hi = _halves(qc)
        qhm_ref[0, 2 * c] = (lo * 0.125).astype(BF16)
        qhm_ref[0, 2 * c + 1] = (hi * 0.125).astype(BF16)
        if kind == MOBA:
            q32hm_ref[0, 2 * c] = lo
            q32hm_ref[0, 2 * c + 1] = hi
    base = D_MODEL // 128
    for c in range(N_KV // 2):
        kc = chunk(base + c, rope)
        k32_ref[0, :, 128 * c:128 * (c + 1)] = kc
        vc = chunk(base + N_KV // 2 + c, False)
        v32_ref[0, :, 128 * c:128 * (c + 1)] = vc
        klo, khi = _halves(kc)
        khm_ref[0, 2 * c] = klo.astype(BF16)
        khm_ref[0, 2 * c + 1] = khi.astype(BF16)
        vlo, vhi = _halves(vc)
        vhm_ref[0, 2 * c] = vlo.astype(BF16)
        vhm_ref[0, 2 * c + 1] = vhi.astype(BF16)
    base = (D_MODEL + 2 * KV_W) // 128
    if kind == FOX:
        logf_ref = next(it)
        zf = chunk(base, False)[:, :N_HEADS] + bf_ref[...]
        logf_ref[0] = _log_sigmoid_pair(zf)[0]
    if kind == DSA:
        qihm_ref, ki_ref, wi_ref = next(it), next(it), next(it)
        for c in range(IDX_HEADS // 2):
            lo, hi = _halves(chunk(base + c, True))
            qihm_ref[0, 2 * c] = lo
            qihm_ref[0, 2 * c + 1] = hi
        ki_ref[0] = chunk(base + IDX_HEADS // 2, True)[:, :IDX_DIM]
        wi_ref[0] = chunk(base + IDX_HEADS // 2 + 1, False)[:, :IDX_HEADS] * (IDX_HEADS ** -0.5 * IDX_DIM ** -0.5)


def _proj(x, g, w_all, kind, rope_tabs, fox_b, prompt):
    B, T, _ = x.shape
    tm = min(PROJ_TM, T)
    n_cols = w_all.shape[1]
    rope = rope_tabs is not None
    row = lambda b, i: (b, i, 0)
    hm = lambda b, i: (b, 0, i, 0)
    in_specs = [pl.BlockSpec((1, tm, D_MODEL), row), pl.BlockSpec((1, D_MODEL), lambda b, i: (0, 0)),
                pl.BlockSpec((D_MODEL, n_cols), lambda b, i: (0, 0))]
    args = [x, g.reshape(1, D_MODEL), w_all]
    if kind == FOX:
        in_specs.append(pl.BlockSpec((1, N_HEADS), lambda b, i: (0, 0)))
        args.append(fox_b.reshape(1, N_HEADS))
    if rope:
        in_specs += [pl.BlockSpec((tm, 128), lambda b, i: (i, 0))] * 2
        args += list(rope_tabs)
    out_shape = [jax.ShapeDtypeStruct((B, N_HEADS, T, HD), BF16), jax.ShapeDtypeStruct((B, T, KV_W), F32),
                 jax.ShapeDtypeStruct((B, T, KV_W), F32), jax.ShapeDtypeStruct((B, N_KV, T, HD), BF16),
                 jax.ShapeDtypeStruct((B, N_KV, T, HD), BF16)]
    out_specs = [pl.BlockSpec((1, N_HEADS, tm, HD), hm), pl.BlockSpec((1, tm, KV_W), row),
                 pl.BlockSpec((1, tm, KV_W), row), pl.BlockSpec((1, N_KV, tm, HD), hm),
                 pl.BlockSpec((1, N_KV, tm, HD), hm)]
    if kind == MOBA:
        out_shape.append(jax.ShapeDtypeStruct((B, N_HEADS, T, HD), F32))
        out_specs.append(pl.BlockSpec((1, N_HEADS, tm, HD), hm))
    if kind == FOX:
        out_shape.append(jax.ShapeDtypeStruct((B, T, N_HEADS), F32))
        out_specs.append(pl.BlockSpec((1, tm, N_HEADS), row))
    if kind == DSA:
        out_shape += [jax.ShapeDtypeStruct((B, IDX_HEADS, T, IDX_DIM), F32),
                      jax.ShapeDtypeStruct((B, T, IDX_DIM), F32), jax.ShapeDtypeStruct((B, T, IDX_HEADS), F32)]
        out_specs += [pl.BlockSpec((1, IDX_HEADS, tm, IDX_DIM), hm), pl.BlockSpec((1, tm, IDX_DIM), row),
                      pl.BlockSpec((1, tm, IDX_HEADS), row)]
    return pl.pallas_call(
        functools.partial(_proj_kernel, kind=kind, rope=rope, prompt=prompt),
        grid=(B, T // tm), in_specs=in_specs, out_specs=out_specs, out_shape=out_shape,
        compiler_params=_cp("parallel", "arbitrary"), name=f"proj_{kind}_{'p' if prompt else 's'}")(*args)


def _out_proj_kernel(x_ref, o_ref, w_ref, y_ref):
    y_ref[...] = x_ref[...] + _dot(o_ref[...], w_ref[...])


def _out_proj(x2, o2, w_o):
    M = x2.shape[0]
    tm = min(512, M)
    return pl.pallas_call(
        _out_proj_kernel, grid=(M // tm,),
        in_specs=[pl.BlockSpec((tm, D_MODEL), lambda i: (i, 0)), pl.BlockSpec((tm, D_MODEL), lambda i: (i, 0)),
                  pl.BlockSpec((D_MODEL, D_MODEL), lambda i: (0, 0))],
        out_specs=pl.BlockSpec((tm, D_MODEL), lambda i: (i, 0)),
        out_shape=jax.ShapeDtypeStruct((M, D_MODEL), F32), compiler_params=_cp("parallel"), name="out_proj")(x2, o2, w_o)


def _gelu_tanh(x):
    return 0.5 * x * (1.0 + jnp.tanh(0.7978845608028654 * (x + 0.044715 * x * x * x)))


def _ffn_kernel(*refs, prompt, tiles_per_seq):
    if prompt:
        x_ref, g_ref, wa_ref, wb_ref, cw_ref, cb_ref, wd_ref, y_ref, conv_ref, h_ref, acc_ref, carry_ref = refs
    else:
        x_ref, g_ref, wa_ref, wb_ref, cw_ref, cb_ref, wd_ref, p2_ref, p1_ref, y_ref, a_ref, h_ref, acc_ref = refs
    i, c = pl.program_id(0), pl.program_id(1)

    @pl.when(c == 0)
    def _():
        h_ref[...] = _rms_bf16(x_ref[...], g_ref[...])
        acc_ref[...] = jnp.zeros_like(acc_ref)

    h = h_ref[...]
    a = _dot(h, wa_ref[...])
    gate = _dot(h, wb_ref[...])
    tm = a.shape[0]
    if prompt:
        @pl.when(i % tiles_per_seq == 0)
        def _():
            carry_ref[c] = jnp.zeros(carry_ref.shape[1:], F32)

        prev2, prev1 = carry_ref[c, 0:1, :], carry_ref[c, 1:2, :]
        row = _iota(a.shape, 0)
        back2 = pltpu.roll(a, 2, 0)
        a_m1 = jnp.where(row == 0, prev1, pltpu.roll(a, 1, 0))
        a_m2 = jnp.where(row == 0, prev2, jnp.where(row == 1, prev1, back2))
        carry_ref[c, 0:2, :] = back2[0:2, :]
        conv_ref[0] = back2[0:2, :]
    else:
        a_m2, a_m1 = p2_ref[...], p1_ref[...]
        a_ref[...] = a
    conv = cb_ref[...] + cw_ref[0:1, :] * a_m2 + cw_ref[1:2, :] * a_m1 + cw_ref[2:3, :] * a
    u = (_gelu_tanh(conv) * gate).astype(BF16)
    acc_ref[...] += _dot(u, wd_ref[...])

    @pl.when(c == pl.num_programs(1) - 1)
    def _():
        y_ref[...] = x_ref[...] + acc_ref[...]


def _ffn(x2, g, w_a, w_b, conv_w, conv_b, w_down, seq_len, prev=None):
    M = x2.shape[0]
    prompt = prev is None
    tm = min(1024, seq_len) if prompt else M
    n_c = D_FF // FF_CHUNK
    tps = seq_len // tm if prompt else 1
    xs = pl.BlockSpec((tm, D_MODEL), lambda i, c: (i, 0))
    in_specs = [xs, pl.BlockSpec((1, D_MODEL), lambda i, c: (0, 0)),
                pl.BlockSpec((D_MODEL, FF_CHUNK), lambda i, c: (0, c)), pl.BlockSpec((D_MODEL, FF_CHUNK), lambda i, c: (0, c)),
                pl.BlockSpec((CONV_W, FF_CHUNK), lambda i, c: (0, c)), pl.BlockSpec((1, FF_CHUNK), lambda i, c: (0, c)),
                pl.BlockSpec((FF_CHUNK, D_MODEL), lambda i, c: (c, 0))]
    args = [x2, g.reshape(1, D_MODEL), w_a, w_b, conv_w, conv_b.reshape(1, D_FF), w_down]
    scratch = [pltpu.VMEM((tm, D_MODEL), BF16), pltpu.VMEM((tm, D_MODEL), F32)]
    if prompt:
        out_shape = [jax.ShapeDtypeStruct((M, D_MODEL), F32), jax.ShapeDtypeStruct((M // tm, CONV_W - 1, D_FF), F32)]
        out_specs = [xs, pl.BlockSpec((1, CONV_W - 1, FF_CHUNK), lambda i, c: (i, 0, c))]
        scratch.append(pltpu.VMEM((n_c, 8, FF_CHUNK), F32))
    else:
        ps = pl.BlockSpec((tm, FF_CHUNK), lambda i, c: (i, c))
        in_specs += [ps, ps]
        args += list(prev)
        out_shape = [jax.ShapeDtypeStruct((M, D_MODEL), F32), jax.ShapeDtypeStruct((M, D_FF), F32)]
        out_specs = [xs, ps]
    y, extra = pl.pallas_call(
        functools.partial(_ffn_kernel, prompt=prompt, tiles_per_seq=tps), grid=(M // tm, n_c),
        in_specs=in_specs, out_specs=out_specs, out_shape=out_shape, scratch_shapes=scratch,
        compiler_params=_cp("arbitrary", "arbitrary"), name="ffn_p" if prompt else "ffn_s")(*args)
    return y, (extra[tps - 1::tps] if prompt else extra)


def _final_norm_kernel(x_ref, g_ref, y_ref):
    x = x_ref[...]
    y_ref[...] = x * lax.rsqrt(jnp.mean(x * x, axis=-1, keepdims=True) + RMS_EPS) * g_ref[...]


def _final_norm(x2, g):
    M = x2.shape[0]
    tm = min(1024, M)
    return pl.pallas_call(
        _final_norm_kernel, grid=(M // tm,),
        in_specs=[pl.BlockSpec((tm, D_MODEL), lambda i: (i, 0)), pl.BlockSpec((1, D_MODEL), lambda i: (0, 0))],
        out_specs=pl.BlockSpec((tm, D_MODEL), lambda i: (i, 0)),
        out_shape=jax.ShapeDtypeStruct((M, D_MODEL), F32), compiler_params=_cp("parallel"), name="final_norm")(x2, g.reshape(1, D_MODEL))


def _stack_heads(q_ref):
    return jnp.concatenate([q_ref[0, j] for j in range(GROUP)], axis=0)


def _store_group(o_ref, rows, tq):
    o_ref[0] = _place([rows[j * tq:(j + 1) * tq].astype(BF16) for j in range(GROUP)]).astype(BF16)


def _stick_kernel(q_ref, k_ref, v_ref, o_ref, r_ref, acc_ref, *, tq, tk):
    q0 = pl.program_id(2) * tq
    qs = _stack_heads(q_ref)
    m = GROUP * tq
    u = _tri(tk, "gt")
    qpos = q0 + _iota((m, 1), 0) % tq
    r_ref[...] = jnp.zeros_like(r_ref)
    acc_ref[...] = jnp.zeros_like(acc_ref)

    def tile(j, masked):
        ks = pl.multiple_of(j * tk, tk)
        kt, vt = k_ref[0, 0, pl.ds(ks, tk), :], v_ref[0, 0, pl.ds(ks, tk), :]
        z = _dot_nt(qs, kt)
        ls, lk = _log_sigmoid_pair(z)
        if masked:
            past = (ks + _iota((m, tk), 1)) < qpos
            lk = jnp.where(past, lk, 0.0)
        hi, lo = _split2(lk)
        cs = _dot(hi, u) + _dot(lo, u)
        w = jnp.exp(ls + cs + r_ref[...])
        if masked:
            w = jnp.where(past, w, 0.0)
        acc_ref[...] += _dot(w.astype(BF16), vt)
        r_ref[...] += jnp.sum(lk, axis=1, keepdims=True)

    jd = q0 // tk
    tile(jd, True)

    def body(n, carry):
        tile(jd - 1 - n, False)
        return carry

    lax.fori_loop(0, jd, body, 0)
    _store_group(o_ref, acc_ref[...], tq)


def _stick_prompt(qhm, khm, vhm):
    B, _, T, _ = qhm.shape
    tq, tk = 128, KV_TILE
    m = GROUP * tq
    return pl.pallas_call(
        functools.partial(_stick_kernel, tq=tq, tk=tk), grid=(B, N_KV, T // tq),
        in_specs=[pl.BlockSpec((1, GROUP, tq, HD), lambda b, g, i: (b, g, i, 0)),
                  pl.BlockSpec((1, 1, T, HD), lambda b, g, i: (b, g, 0, 0)),
                  pl.BlockSpec((1, 1, T, HD), lambda b, g, i: (b, g, 0, 0))],
        out_specs=pl.BlockSpec((1, tq, KV_W), lambda b, g, i: (b, i, g)),
        out_shape=jax.ShapeDtypeStruct((B, T, D_MODEL), BF16),
        scratch_shapes=[pltpu.VMEM((m, 1), F32), pltpu.VMEM((m, HD), F32)],
        compiler_params=_cp("parallel", "parallel", "arbitrary"), name="stick_p")(qhm, khm, vhm)


def _cumsum_kernel(x_ref, o_ref, carry_ref):
    @pl.when(pl.program_id(1) == 0)
    def _():
        carry_ref[...] = jnp.zeros_like(carry_ref)

    n = x_ref.shape[1]
    low = _tri(n, "ge")
    hi, mid, lo = _split3(x_ref[0])
    c = _dot(low, hi) + _dot(low, mid) + _dot(low, lo) + carry_ref[...]
    o_ref[0] = c
    carry_ref[...] = c[n - 1:n, :]


def _cumsum_time(x):
    B, T, C = x.shape
    n = 256
    return pl.pallas_call(
        _cumsum_kernel, grid=(B, T // n),
        in_specs=[pl.BlockSpec((1, n, C), lambda b, i: (b, i, 0))],
        out_specs=pl.BlockSpec((1, n, C), lambda b, i: (b, i, 0)),
        out_shape=jax.ShapeDtypeStruct((B, T, C), F32), scratch_shapes=[pltpu.VMEM((1, C), F32)],
        compiler_params=_cp("parallel", "arbitrary"), name="cumsum_time")(x)


def _init_softmax(m_ref, l_ref, acc_ref):
    m_ref[...] = jnp.full(m_ref.shape, NEG, F32)
    l_ref[...] = jnp.zeros_like(l_ref)
    acc_ref[...] = jnp.zeros_like(acc_ref)


def _fox_kernel(q_ref, k_ref, v_ref, cq_ref, ck_ref, o_ref, m_ref, l_ref, acc_ref, *, tq, tk):
    q0 = pl.program_id(2) * tq
    _init_softmax(m_ref, l_ref, acc_ref)
    qpos = q0 + _iota((tq, 1), 0)

    def tile(j, masked):
        ks = pl.multiple_of(j * tk, tk)
        kt, vt = k_ref[0, 0, pl.ds(ks, tk), :], v_ref[0, 0, pl.ds(ks, tk), :]
        valid = (ks + _iota((tq, tk), 1)) <= qpos if masked else None
        for h in range(GROUP):
            s = _dot_nt(q_ref[0, h], kt) + (cq_ref[0, 0, :, h:h + 1] - ck_ref[0, 0, j, h:h + 1, :])
            _online(s, valid, vt, m_ref, l_ref, acc_ref, h)

    jd = q0 // tk
    tile(jd, True)

    def body(n, carry):
        tile(jd - 1 - n, False)
        return carry

    lax.fori_loop(0, jd, body, 0)
    o_ref[0] = _place([(acc_ref[h] / l_ref[h]).astype(BF16) for h in range(GROUP)]).astype(BF16)


def _fox_prompt(qhm, khm, vhm, cum):
    B, _, T, _ = qhm.shape
    tq, tk = 256, KV_TILE
    cq = cum.reshape(B, T, N_KV, GROUP).transpose(0, 2, 1, 3)
    ck = cum.reshape(B, T // tk, tk, N_KV, GROUP).transpose(0, 3, 1, 4, 2)
    return pl.pallas_call(
        functools.partial(_fox_kernel, tq=tq, tk=tk), grid=(B, N_KV, T // tq),
        in_specs=[pl.BlockSpec((1, GROUP, tq, HD), lambda b, g, i: (b, g, i, 0)),
                  pl.BlockSpec((1, 1, T, HD), lambda b, g, i: (b, g, 0, 0)),
                  pl.BlockSpec((1, 1, T, HD), lambda b, g, i: (b, g, 0, 0)),
                  pl.BlockSpec((1, 1, tq, GROUP), lambda b, g, i: (b, g, i, 0)),
                  pl.BlockSpec((1, 1, T // tk, GROUP, tk), lambda b, g, i: (b, g, 0, 0, 0))],
        out_specs=pl.BlockSpec((1, tq, KV_W), lambda b, g, i: (b, i, g)),
        out_shape=jax.ShapeDtypeStruct((B, T, D_MODEL), BF16),
        scratch_shapes=[pltpu.VMEM((GROUP, tq, 1), F32), pltpu.VMEM((GROUP, tq, 1), F32), pltpu.VMEM((GROUP, tq, HD), F32)],
        compiler_params=_cp("parallel", "parallel", "arbitrary"), name="fox_p")(qhm, khm, vhm, cq, ck)


def _block_mean_kernel(k_ref, o_ref):
    k = k_ref[0]
    nb = k.shape[0] // MOBA_BLOCK
    o_ref[0] = jnp.sum(k.reshape(nb, MOBA_BLOCK, KV_W), axis=1) * (1.0 / MOBA_BLOCK)


def _block_mean(k32):
    B, T, _ = k32.shape
    nb = T // MOBA_BLOCK
    return pl.pallas_call(
        _block_mean_kernel, grid=(B,), in_specs=[pl.BlockSpec((1, T, KV_W), lambda b: (b, 0, 0))],
        out_specs=pl.BlockSpec((1, nb, KV_W), lambda b: (b, 0, 0)),
        out_shape=jax.ShapeDtypeStruct((B, nb, KV_W), F32), compiler_params=_cp("parallel"), name="block_mean")(k32)


def _moba_kernel(q_ref, q32_ref, km_ref, k_ref, v_ref, o_ref, sel_ref, m_ref, l_ref, acc_ref, *, tq, nb):
    tk = MOBA_BLOCK
    q0 = pl.program_id(2) * tq
    own = q0 // tk
    _init_softmax(m_ref, l_ref, acc_ref)
    km = km_ref[0, 0]
    for h in range(GROUP):
        sel_ref[h] = _topk_rank_select(_dot_nt_f32(q32_ref[0, h], km), own, MOBA_TOPK, nb)
    qpos = q0 + _iota((tq, 1), 0)
    lane = _iota((tq, nb), 1)

    def tile(j, diag):
        ks = pl.multiple_of(j * tk, tk)
        kt, vt = k_ref[0, 0, pl.ds(ks, tk), :], v_ref[0, 0, pl.ds(ks, tk), :]
        for h in range(GROUP):
            if diag:
                valid = (ks + _iota((tq, tk), 1)) <= qpos
            else:
                picked = jnp.sum(jnp.where(lane == j, sel_ref[h], 0.0), axis=1, keepdims=True) > 0.5
                valid = jnp.broadcast_to(picked, (tq, tk))
            _online(_dot_nt(q_ref[0, h], kt), valid, vt, m_ref, l_ref, acc_ref, h)

    tile(own, True)

    def body(n, carry):
        tile(n, False)
        return carry

    lax.fori_loop(0, own, body, 0)
    o_ref[0] = _place([(acc_ref[h] / l_ref[h]).astype(BF16) for h in range(GROUP)]).astype(BF16)


def _moba_prompt(qhm, q32hm, kmean, khm, vhm):
    B, _, T, _ = qhm.shape
    tq = 128
    nb = T // MOBA_BLOCK
    km = kmean.reshape(B, nb, N_KV, HD).transpose(0, 2, 1, 3)
    qspec = pl.BlockSpec((1, GROUP, tq, HD), lambda b, g, i: (b, g, i, 0))
    kvspec = pl.BlockSpec((1, 1, T, HD), lambda b, g, i: (b, g, 0, 0))
    return pl.pallas_call(
        functools.partial(_moba_kernel, tq=tq, nb=nb), grid=(B, N_KV, T // tq),
        in_specs=[qspec, qspec, pl.BlockSpec((1, 1, nb, HD), lambda b, g, i: (b, g, 0, 0)), kvspec, kvspec],
        out_specs=pl.BlockSpec((1, tq, KV_W), lambda b, g, i: (b, i, g)),
        out_shape=jax.ShapeDtypeStruct((B, T, D_MODEL), BF16),
        scratch_shapes=[pltpu.VMEM((GROUP, tq, nb), F32), pltpu.VMEM((GROUP, tq, 1), F32),
                        pltpu.VMEM((GROUP, tq, 1), F32), pltpu.VMEM((GROUP, tq, HD), F32)],
        compiler_params=_cp("parallel", "parallel", "arbitrary"), name="moba_p")(qhm, q32hm, km, khm, vhm)


def _kth_largest_key(count_ge, shape, k):
    def body(it, v):
        cand = v + lax.shift_left(jnp.int32(1), 31 - it)
        return jnp.where(count_ge(cand) >= k, cand, v)

    return lax.fori_loop(0, 32, body, jnp.full(shape, INT_MIN, I32))


def _dsa_kernel(q_ref, qi_ref, wi_ref, ki_ref, k_ref, v_ref, o_ref, key_ref, m_ref, l_ref, acc_ref, *, tq, n_keep):
    tk = KV_TILE
    q0 = pl.program_id(1) * tq
    n_proc = (q0 + tq + tk - 1) // tk
    qpos = q0 + _iota((tq, 1), 0)

    def visible(c):
        return (c * tk + _iota((tq, tk), 1)) <= qpos

    def score_tile(c, carry):
        ks = pl.multiple_of(c * tk, tk)
        kh, kl = _split2(ki_ref[0, pl.ds(ks, tk), :])
        sc = jnp.zeros((tq, tk), F32)
        for i in range(IDX_HEADS):
            qh, ql = _split2(qi_ref[0, i])
            dots = _dot_nt(qh, kh) + _dot_nt(ql, kh) + _dot_nt(qh, kl)
            sc = sc + wi_ref[0, :, i:i + 1] * jnp.maximum(dots, 0.0)
        key_ref[c] = _order_key(jnp.where(visible(c), sc + 0.0, -jnp.inf))
        return carry

    lax.fori_loop(0, n_proc, score_tile, 0)

    def count(pred_of_tile):
        def body(c, part):
            return part + jnp.where(pred_of_tile(key_ref[c]), 1.0, 0.0)

        return jnp.sum(lax.fori_loop(0, n_proc, body, jnp.zeros((tq, tk), F32)), axis=1, keepdims=True)

    thr = _kth_largest_key(lambda cand: count(lambda k: k >= cand), (tq, 1), n_keep)
    need = n_keep - count(lambda k: k > thr)
    m = GROUP * tq
    _init_softmax(m_ref, l_ref, acc_ref)
    before = _tri(tk, "lt")

    def attend(c, ties_seen):
        ks = pl.multiple_of(c * tk, tk)
        key = key_ref[c]
        tied = jnp.where(key == thr, 1.0, 0.0)
        rank = _dot(tied.astype(BF16), before) + ties_seen
        keep = (key > thr) | ((key == thr) & (rank < need))
        keep = jnp.where(keep & visible(c), 1.0, 0.0)
        valid = jnp.concatenate([keep] * GROUP, axis=0) > 0.5
        for g in range(N_KV):
            qs = jnp.concatenate([q_ref[0, GROUP * g + j] for j in range(GROUP)], axis=0)
            z = _dot_nt(qs, k_ref[0, g, pl.ds(ks, tk), :])
            _online(z, valid, v_ref[0, g, pl.ds(ks, tk), :], m_ref, l_ref, acc_ref, g)
        return ties_seen + jnp.sum(tied, axis=1, keepdims=True)

    lax.fori_loop(0, n_proc, attend, jnp.zeros((tq, 1), F32))
    for g in range(N_KV):
        rows = acc_ref[g] / l_ref[g]
        o_ref[0, :, KV_W * g:KV_W * (g + 1)] = _place(
            [rows[j * tq:(j + 1) * tq].astype(BF16) for j in range(GROUP)]).astype(BF16)


def _dsa_prompt(qhm, qihm, wi, ki, khm, vhm):
    B, _, T, _ = qhm.shape
    tq = 128
    m = GROUP * tq
    return pl.pallas_call(
        functools.partial(_dsa_kernel, tq=tq, n_keep=min(IDX_TOPK, T // 4)), grid=(B, T // tq),
        in_specs=[pl.BlockSpec((1, N_HEADS, tq, HD), lambda b, i: (b, 0, i, 0)),
                  pl.BlockSpec((1, IDX_HEADS, tq, IDX_DIM), lambda b, i: (b, 0, i, 0)),
                  pl.BlockSpec((1, tq, IDX_HEADS), lambda b, i: (b, i, 0)),
                  pl.BlockSpec((1, T, IDX_DIM), lambda b, i: (b, 0, 0)),
                  pl.BlockSpec((1, N_KV, T, HD), lambda b, i: (b, 0, 0, 0)),
                  pl.BlockSpec((1, N_KV, T, HD), lambda b, i: (b, 0, 0, 0))],
        out_specs=pl.BlockSpec((1, tq, D_MODEL), lambda b, i: (b, i, 0)),
        out_shape=jax.ShapeDtypeStruct((B, T, D_MODEL), BF16),
        scratch_shapes=[pltpu.VMEM((T // KV_TILE, tq, KV_TILE), I32), pltpu.VMEM((N_KV, m, 1), F32),
                        pltpu.VMEM((N_KV, m, 1), F32), pltpu.VMEM((N_KV, m, HD), F32)],
        compiler_params=_cp("parallel", "arbitrary"), name="dsa_p")(qhm, qihm, wi, ki, khm, vhm)


def _page_specs(layer, n_pages, reverse):
    def idx(b, p, pt):
        return (layer, pt[b, n_pages - 1 - p if reverse else p], 0, 0)

    return pl.BlockSpec((None, None, PAGE, KV_W), idx)


def _row_spec(width):
    return pl.BlockSpec((1, 1, width), lambda b, p, pt: (b, 0, 0))


def _head_spec(width):
    return pl.BlockSpec((1, N_HEADS, width), lambda b, p, pt: (b, 0, 0))


def _new_key_start(q, knew, vnew, m_ref, l_ref, acc_ref):
    z = jnp.sum(q.astype(F32) * knew.astype(BF16).astype(F32), axis=1, keepdims=True)
    m_ref[0] = z
    l_ref[0] = jnp.ones_like(z)
    acc_ref[0] = jnp.broadcast_to(vnew.astype(BF16).astype(F32), acc_ref.shape[1:])


def _dec_stick_kernel(pt_ref, q_ref, k_ref, v_ref, o_ref, r_ref, acc_ref):
    p = pl.program_id(1)

    @pl.when(p == 0)
    def _():
        r_ref[...] = jnp.zeros_like(r_ref)
        acc_ref[...] = jnp.zeros_like(acc_ref)

    z = _dot_nt(q_ref[0], k_ref[...].astype(BF16))
    ls, lk = _log_sigmoid_pair(z)
    u = _tri(PAGE, "gt")
    hi, lo = _split2(lk)
    w = jnp.exp(ls + _dot(hi, u) + _dot(lo, u) + r_ref[...])
    acc_ref[...] += _dot(w.astype(BF16), v_ref[...].astype(BF16))
    r_ref[...] += jnp.sum(lk, axis=1, keepdims=True)

    @pl.when(p == pl.num_programs(1) - 1)
    def _():
        o_ref[0] = acc_ref[...]


def _dec_stick(qx, cache_k4, cache_v4, page_table, layer):
    DB, n_pages = page_table.shape
    grid_spec = pltpu.PrefetchScalarGridSpec(
        num_scalar_prefetch=1, grid=(DB, n_pages),
        in_specs=[_head_spec(KV_W), _page_specs(layer, n_pages, True), _page_specs(layer, n_pages, True)],
        out_specs=_head_spec(KV_W),
        scratch_shapes=[pltpu.VMEM((N_HEADS, 1), F32), pltpu.VMEM((N_HEADS, KV_W), F32)])
    return pl.pallas_call(
        _dec_stick_kernel, grid_spec=grid_spec, out_shape=jax.ShapeDtypeStruct((DB, N_HEADS, KV_W), F32),
        compiler_params=_cp("parallel", "arbitrary"), name="stick_s")(page_table, qx, cache_k4, cache_v4)


def _dec_fox_kernel(pt_ref, q_ref, k_ref, v_ref, lf_ref, kn_ref, vn_ref, lfn_ref, o_ref, s_ref, m_ref, l_ref, acc_ref):
    p = pl.program_id(1)

    @pl.when(p == 0)
    def _():
        s_ref[...] = lfn_ref[0]
        _new_key_start(q_ref[0], kn_ref[0], vn_ref[0], m_ref, l_ref, acc_ref)

    lf = lf_ref[...]
    u = _tri(PAGE, "gt")
    hi, mid, lo = _split3(lf)
    decay = _dot(hi, u) + _dot(mid, u) + _dot(lo, u) + s_ref[...]
    s = _dot_nt(q_ref[0], k_ref[...].astype(BF16)) + decay
    _online(s, None, v_ref[...].astype(BF16), m_ref, l_ref, acc_ref, 0)
    s_ref[...] += jnp.sum(lf, axis=1, keepdims=True)

    @pl.when(p == pl.num_programs(1) - 1)
    def _():
        o_ref[0] = acc_ref[0] / l_ref[0]


def _softmax_scratch():
    return [pltpu.VMEM((1, N_HEADS, 1), F32), pltpu.VMEM((1, N_HEADS, 1), F32), pltpu.VMEM((1, N_HEADS, KV_W), F32)]


def _dec_fox(qx, cache_k4, cache_v4, logf_t, knew, vnew, lfnew, page_table, layer):
    DB, n_pages = page_table.shape
    grid_spec = pltpu.PrefetchScalarGridSpec(
        num_scalar_prefetch=1, grid=(DB, n_pages),
        in_specs=[_head_spec(KV_W), _page_specs(layer, n_pages, True), _page_specs(layer, n_pages, True),
                  pl.BlockSpec((None, N_HEADS, PAGE), lambda b, p, pt: (pt[b, n_pages - 1 - p], 0, 0)),
                  _row_spec(KV_W), _row_spec(KV_W), _head_spec(1)],
        out_specs=_head_spec(KV_W),
        scratch_shapes=[pltpu.VMEM((N_HEADS, 1), F32)] + _softmax_scratch())
    return pl.pallas_call(
        _dec_fox_kernel, grid_spec=grid_spec, out_shape=jax.ShapeDtypeStruct((DB, N_HEADS, KV_W), F32),
        compiler_params=_cp("parallel", "arbitrary"), name="fox_s")(
            page_table, qx, cache_k4, cache_v4, logf_t, knew, vnew, lfnew)


def _dec_kmean_kernel(pt_ref, k_ref, o_ref):
    p = pl.program_id(1)

    @pl.when(p == 0)
    def _():
        o_ref[...] = jnp.zeros_like(o_ref)

    part = jnp.sum(k_ref[...], axis=0, keepdims=True) * (1.0 / MOBA_BLOCK)
    blk = p // (MOBA_BLOCK // PAGE)
    o_ref[0, pl.ds(blk, 1), :] = o_ref[0, pl.ds(blk, 1), :] + part


def _dec_kmean(cache_k4, page_table, layer):
    DB, n_pages = page_table.shape
    nb = n_pages * PAGE // MOBA_BLOCK
    grid_spec = pltpu.PrefetchScalarGridSpec(
        num_scalar_prefetch=1, grid=(DB, n_pages), in_specs=[_page_specs(layer, n_pages, False)],
        out_specs=pl.BlockSpec((1, nb, KV_W), lambda b, p, pt: (b, 0, 0)))
    return pl.pallas_call(
        _dec_kmean_kernel, grid_spec=grid_spec, out_shape=jax.ShapeDtypeStruct((DB, nb, KV_W), F32),
        compiler_params=_cp("parallel", "arbitrary"), name="kmean_s")(page_table, cache_k4)


def _dec_moba_kernel(pt_ref, q_ref, q32_ref, km_ref, k_ref, v_ref, kn_ref, vn_ref, o_ref, sel_ref, m_ref, l_ref, acc_ref, *, nb):
    p = pl.program_id(1)

    @pl.when(p == 0)
    def _():
        sel_ref[...] = _topk_rank_select(_dot_nt_f32(q32_ref[0], km_ref[0]), nb, MOBA_TOPK, nb)
        _new_key_start(q_ref[0], kn_ref[0], vn_ref[0], m_ref, l_ref, acc_ref)

    blk = p // (MOBA_BLOCK // PAGE)
    lane = _iota((N_HEADS, nb), 1)
    picked = jnp.sum(jnp.where(lane == blk, sel_ref[...], 0.0), axis=1, keepdims=True) > 0.5
    s = _dot_nt(q_ref[0], k_ref[...].astype(BF16))
    _online(s, jnp.broadcast_to(picked, s.shape), v_ref[...].astype(BF16), m_ref, l_ref, acc_ref, 0)

    @pl.when(p == pl.num_programs(1) - 1)
    def _():
        o_ref[0] = acc_ref[0] / l_ref[0]


def _dec_moba(qx, q32x, kmean, cache_k4, cache_v4, knew, vnew, page_table, layer):
    DB, n_pages = page_table.shape
    nb = kmean.shape[1]
    grid_spec = pltpu.PrefetchScalarGridSpec(
        num_scalar_prefetch=1, grid=(DB, n_pages),
        in_specs=[_head_spec(KV_W), _head_spec(KV_W), pl.BlockSpec((1, nb, KV_W), lambda b, p, pt: (b, 0, 0)),
                  _page_specs(layer, n_pages, False), _page_specs(layer, n_pages, False), _row_spec(KV_W), _row_spec(KV_W)],
        out_specs=_head_spec(KV_W),
        scratch_shapes=[pltpu.VMEM((N_HEADS, nb), F32)] + _softmax_scratch())
    return pl.pallas_call(
        functools.partial(_dec_moba_kernel, nb=nb), grid_spec=grid_spec,
        out_shape=jax.ShapeDtypeStruct((DB, N_HEADS, KV_W), F32),
        compiler_params=_cp("parallel", "arbitrary"), name="moba_s")(
            page_table, qx, q32x, kmean, cache_k4, cache_v4, knew, vnew)


def _dec_score_kernel(pt_ref, qi_ref, wi_ref, kidx_ref, kin_ref, o_ref, *, n_pages):
    p = pl.program_id(1)
    qi, wi = qi_ref[0], wi_ref[0]

    @pl.when(p == 0)
    def _():
        o_ref[...] = jnp.full(o_ref.shape, -jnp.inf, F32)
        dn = jnp.sum(qi * kin_ref[0], axis=1, keepdims=True)
        sn = jnp.sum(wi * jnp.maximum(dn, 0.0), axis=0, keepdims=True) + 0.0
        o_ref[0, n_pages:n_pages + 1, :] = jnp.where(_iota((1, PAGE), 1) == 0, sn, -jnp.inf)

    dots = _dot_nt_f32(qi, kidx_ref[...])
    o_ref[0, pl.ds(p, 1), :] = jnp.sum(wi * jnp.maximum(dots, 0.0), axis=0, keepdims=True) + 0.0


def _dec_scores(qi3, wi3, cache_kidx, kinew, page_table):
    DB, n_pages = page_table.shape
    assert n_pages < PAGE
    grid_spec = pltpu.PrefetchScalarGridSpec(
        num_scalar_prefetch=1, grid=(DB, n_pages),
        in_specs=[pl.BlockSpec((1, IDX_HEADS, IDX_DIM), lambda b, p, pt: (b, 0, 0)),
                  pl.BlockSpec((1, IDX_HEADS, 1), lambda b, p, pt: (b, 0, 0)),
                  pl.BlockSpec((None, PAGE, IDX_DIM), lambda b, p, pt: (pt[b, p], 0, 0)), _row_spec(IDX_DIM)],
        out_specs=pl.BlockSpec((1, PAGE, PAGE), lambda b, p, pt: (b, 0, 0)))
    return pl.pallas_call(
        functools.partial(_dec_score_kernel, n_pages=n_pages), grid_spec=grid_spec,
        out_shape=jax.ShapeDtypeStruct((DB, PAGE, PAGE), F32),
        compiler_params=_cp("parallel", "arbitrary"), name="dsa_score_s")(page_table, qi3, wi3, cache_kidx, kinew)


def _dec_select_kernel(sc_ref, o_ref, *, n_keys, n_keep):
    key = _order_key(sc_ref[0])

    def total(x):
        return jnp.sum(jnp.sum(x, axis=1, keepdims=True), axis=0, keepdims=True)

    thr = _kth_largest_key(lambda cand: total(jnp.where(key >= cand, 1.0, 0.0)), (1, 1), n_keep)
    need = n_keep - total(jnp.where(key > thr, 1.0, 0.0))
    tied = jnp.where(key == thr, 1.0, 0.0)
    in_row = _dot(tied.astype(BF16), _tri(PAGE, "lt"))
    row_tot = jnp.broadcast_to(jnp.sum(tied, axis=1, keepdims=True), tied.shape).astype(BF16)
    rank = in_row + _dot(_tri(PAGE, "gt"), row_tot)
    pos = _iota(key.shape, 0) * PAGE + _iota(key.shape, 1)
    keep = ((key > thr) | ((key == thr) & (rank < need))) & (pos < n_keys)
    o_ref[0] = jnp.where(keep, 1.0, 0.0)


def _dec_select(scores, n_keys):
    DB = scores.shape[0]
    spec = pl.BlockSpec((1, PAGE, PAGE), lambda b: (b, 0, 0))
    return pl.pallas_call(
        functools.partial(_dec_select_kernel, n_keys=n_keys, n_keep=min(IDX_TOPK, n_keys // 4)), grid=(DB,), in_specs=[spec], out_specs=spec,
        out_shape=jax.ShapeDtypeStruct(scores.shape, F32), compiler_params=_cp("parallel"), name="dsa_select_s")(scores)


def _dec_dsa_kernel(pt_ref, q_ref, k_ref, v_ref, sel_ref, kn_ref, vn_ref, o_ref, m_ref, l_ref, acc_ref, *, n_pages):
    p = pl.program_id(1)

    @pl.when(p == 0)
    def _():
        _init_softmax(m_ref, l_ref, acc_ref)
        q = q_ref[0]
        zn = jnp.sum(q.astype(F32) * kn_ref[0].astype(BF16).astype(F32), axis=1, keepdims=True)
        keep = jnp.broadcast_to(sel_ref[0, n_pages:n_pages + 1, 0:1] > 0.5, zn.shape)
        sn = jnp.where(keep, zn, NEG)
        m_ref[0] = sn
        l_ref[0] = jnp.where(keep, 1.0, 0.0)
        acc_ref[0] = jnp.where(keep, jnp.broadcast_to(vn_ref[0].astype(BF16).astype(F32), acc_ref.shape[1:]), 0.0)

    s = _dot_nt(q_ref[0], k_ref[...].astype(BF16))
    valid = jnp.broadcast_to(sel_ref[0, pl.ds(p, 1), :] > 0.5, s.shape)
    _online(s, valid, v_ref[...].astype(BF16), m_ref, l_ref, acc_ref, 0)

    @pl.when(p == pl.num_programs(1) - 1)
    def _():
        o_ref[0] = acc_ref[0] / l_ref[0]


def _dec_dsa(qx, cache_k4, cache_v4, sel, knew, vnew, page_table, layer):
    DB, n_pages = page_table.shape
    grid_spec = pltpu.PrefetchScalarGridSpec(
        num_scalar_prefetch=1, grid=(DB, n_pages),
        in_specs=[_head_spec(KV_W), _page_specs(layer, n_pages, False), _page_specs(layer, n_pages, False),
                  pl.BlockSpec((1, PAGE, PAGE), lambda b, p, pt: (b, 0, 0)), _row_spec(KV_W), _row_spec(KV_W)],
        out_specs=_head_spec(KV_W), scratch_shapes=_softmax_scratch())
    return pl.pallas_call(
        functools.partial(_dec_dsa_kernel, n_pages=n_pages), grid_spec=grid_spec,
        out_shape=jax.ShapeDtypeStruct((DB, N_HEADS, KV_W), F32),
        compiler_params=_cp("parallel", "arbitrary"), name="dsa_s")(page_table, qx, cache_k4, cache_v4, sel, knew, vnew)


def _rope_tables(pos):
    half = HD // 2
    inv_freq = jnp.power(ROPE_THETA, -jnp.arange(half, dtype=F32) * (2.0 / HD))
    ang = pos.astype(F32)[:, None] * inv_freq[None, :]
    cos, sin = jnp.cos(ang), jnp.sin(ang)
    return jnp.concatenate([cos] * 4, axis=1), jnp.concatenate([-sin, sin] * 2, axis=1)


def _pad_cols(w, mult=128):
    pad = (-w.shape[1]) % mult
    return jnp.pad(w, ((0, 0), (0, pad))) if pad else w


def _layer_weights(kind, w_qkv, fox_w_f, idx_w_q, idx_w_k, idx_w_w):
    cols = [w_qkv]
    if kind == FOX:
        cols.append(_pad_cols(fox_w_f))
    if kind == DSA:
        cols += [idx_w_q, _pad_cols(idx_w_k), _pad_cols(idx_w_w)]
    return jnp.concatenate(cols, axis=1).astype(BF16)


_HEAD_ONEHOT = np.arange(N_HEADS)[:, None] // GROUP == np.arange(N_KV)[None, :]


def _expand_heads(q, dtype):
    m = jnp.asarray(_HEAD_ONEHOT, q.dtype)
    return (q[:, :, None, :] * m[None, :, :, None]).reshape(q.shape[0], N_HEADS, KV_W).astype(dtype)


def _collapse_heads(ox):
    m = jnp.asarray(_HEAD_ONEHOT, ox.dtype)
    o = jnp.sum(ox.reshape(ox.shape[0], N_HEADS, N_KV, HD) * m[None, :, :, None], axis=2)
    return o.reshape(ox.shape[0], D_MODEL).astype(BF16)


def kernel(x_prompt, x_sample, cache_k, cache_v, cache_logf, cache_kidx, state_conv, page_table, norm_mix, norm_ffn, norm_final, w_qkv, w_o, fox_w_f, fox_b_f, idx_w_q, idx_w_k, idx_w_w, ffn_w_a, ffn_w_b, ffn_conv_w, ffn_conv_b, ffn_w_down):
    B, T, _ = x_prompt.shape
    DB = x_sample.shape[0]
    depth, n_pool = cache_k.shape[:2]
    n_pages = page_table.shape[1]
    past = n_pages * PAGE
    assert T % MOBA_BLOCK == 0 and past % MOBA_BLOCK == 0 and x_sample.shape[1] == 1
    cache_k4 = cache_k.reshape(depth, n_pool, PAGE, KV_W)
    cache_v4 = cache_v.reshape(depth, n_pool, PAGE, KV_W)
    rope_p = _rope_tables(jnp.arange(T, dtype=I32))
    rope_s = _rope_tables(jnp.full((DB,), past, I32))

    xp = x_prompt
    xs = x_sample.reshape(1, DB, D_MODEL)
    new_k_p, new_v_p, new_k_s, new_v_s, conv_p, conv_s = [], [], [], [], [], []
    logf_p = logf_s = kidx_p = kidx_s = None
    for i in range(depth):
        kind = i % 4
        rope = kind in (MOBA, DSA)
        w_all = _layer_weights(kind, w_qkv[i], fox_w_f, idx_w_q, idx_w_k, idx_w_w)
        wo_b = w_o[i].astype(BF16)
        wa_b, wb_b, wd_b = ffn_w_a[i].astype(BF16), ffn_w_b[i].astype(BF16), ffn_w_down[i].astype(BF16)

        outs = _proj(xp, norm_mix[i], w_all, kind, rope_p if rope else None, fox_b_f, True)
        qhm, k32, v32, khm, vhm = outs[:5]
        new_k_p.append(k32.reshape(B, T, N_KV, HD))
        new_v_p.append(v32.reshape(B, T, N_KV, HD))
        if kind == STICK:
            o = _stick_prompt(qhm, khm, vhm)
        elif kind == FOX:
            logf_p = outs[5]
            o = _fox_prompt(qhm, khm, vhm, _cumsum_time(logf_p))
        elif kind == MOBA:
            o = _moba_prompt(qhm, outs[5], _block_mean(k32), khm, vhm)
        else:
            qihm, kidx_p, wi = outs[5:8]
            o = _dsa_prompt(qhm, qihm, wi, kidx_p, khm, vhm)
        x2 = _out_proj(xp.reshape(B * T, D_MODEL), o.reshape(B * T, D_MODEL), wo_b)
        x2, rows = _ffn(x2, norm_ffn[i], wa_b, wb_b, ffn_conv_w[i], ffn_conv_b[i], wd_b, T)
        conv_p.append(rows)
        xp = x2.reshape(B, T, D_MODEL)

        outs = _proj(xs, norm_mix[i], w_all, kind, rope_s if rope else None, fox_b_f, False)
        qhm, k32, v32 = outs[:3]
        new_k_s.append(k32.reshape(DB, 1, N_KV, HD))
        new_v_s.append(v32.reshape(DB, 1, N_KV, HD))
        q_heads = qhm[0].transpose(1, 0, 2)
        qx = _expand_heads(q_heads, BF16)
        knew, vnew = k32.reshape(DB, 1, KV_W), v32.reshape(DB, 1, KV_W)
        if kind == STICK:
            ox = _dec_stick(qx, cache_k4, cache_v4, page_table, i)
        elif kind == FOX:
            lf = outs[5][0]
            logf_s = lf.reshape(DB, 1, N_HEADS)
            ox = _dec_fox(qx, cache_k4, cache_v4, cache_logf.transpose(0, 2, 1), knew, vnew,
                          lf.reshape(DB, N_HEADS, 1), page_table, i)
        elif kind == MOBA:
            q32x = _expand_heads(outs[5][0].transpose(1, 0, 2), F32)
            ox = _dec_moba(qx, q32x, _dec_kmean(cache_k4, page_table, i), cache_k4, cache_v4, knew, vnew, page_table, i)
        else:
            qihm, ki, wi = outs[5:8]
            kidx_s = ki.reshape(DB, 1, IDX_DIM)
            scores = _dec_scores(qihm[0].transpose(1, 0, 2), wi[0].reshape(DB, IDX_HEADS, 1), cache_kidx, kidx_s, page_table)
            ox = _dec_dsa(qx, cache_k4, cache_v4, _dec_select(scores, past + 1), knew, vnew, page_table, i)
        s2 = _out_proj(xs.reshape(DB, D_MODEL), _collapse_heads(ox), wo_b)
        st = state_conv[i]
        s2, a_new = _ffn(s2, norm_ffn[i], wa_b, wb_b, ffn_conv_w[i], ffn_conv_b[i], wd_b, 1, prev=(st[:, 0], st[:, 1]))
        conv_s.append(jnp.stack([st[:, 1], a_new], axis=1))
        xs = s2.reshape(1, DB, D_MODEL)

    y_p = _final_norm(xp.reshape(B * T, D_MODEL), norm_final).reshape(B, T, D_MODEL)
    y_s = _final_norm(xs.reshape(DB, D_MODEL), norm_final).reshape(DB, 1, D_MODEL)
    return (y_p, y_s, jnp.stack(new_k_p), jnp.stack(new_v_p), jnp.stack(new_k_s), jnp.stack(new_v_s),
            logf_p, logf_s, kidx_p, kidx_s, jnp.stack(conv_p), jnp.stack(conv_s))
```

```python
import functools
import math

import jax
import jax.numpy as jnp
import numpy as np
from jax import lax
from jax.experimental import pallas as pl
from jax.experimental.pallas import tpu as pltpu

F32, BF16, I32 = jnp.float32, jnp.bfloat16, jnp.int32

D_MODEL = 1024
N_HEADS = 16
N_KV = 4
GROUP = N_HEADS // N_KV
HD = 64
KV_W = N_KV * HD
D_FF = 2816
CONV_W = 3
PAGE = 128
MOBA_BLOCK = 256
MOBA_TOPK = 3
IDX_HEADS = 8
IDX_DIM = 64
IDX_TOPK = 256
ROPE_THETA = 10000.0
RMS_EPS = 1e-6
STICK, FOX, MOBA, DSA = 0, 1, 2, 3
NEG = -1e30
INT_MIN = -2147483648
VMEM_LIMIT = 48 * 1024 * 1024
FF_CHUNK = 256
PROJ_TM = 256
KV_TILE = 256
DEC_PAGES = 8
FOX_AUG = 3 * GROUP


def _cp(*sem):
    return pltpu.CompilerParams(dimension_semantics=sem, vmem_limit_bytes=VMEM_LIMIT)


def _dot(a, b):
    return jnp.dot(a, b, preferred_element_type=F32)


def _dot_nt(a, b):
    return lax.dot_general(a, b, (((1,), (1,)), ((), ())), preferred_element_type=F32)


def _split2(x):
    hi = x.astype(BF16)
    lo = (x - hi.astype(F32)).astype(BF16)
    return hi, lo


def _split3(x):
    hi = x.astype(BF16)
    r = x - hi.astype(F32)
    mid = r.astype(BF16)
    lo = (r - mid.astype(F32)).astype(BF16)
    return hi, mid, lo


def _dot_f32(a, b):
    ah, al = _split2(a)
    bh, bl = _split2(b)
    return _dot(ah, bh) + _dot(al, bh) + _dot(ah, bl)


def _dot_nt_f32(a, b):
    ah, al = _split2(a)
    bh, bl = _split2(b)
    return _dot_nt(ah, bh) + _dot_nt(al, bh) + _dot_nt(ah, bl)


def _log_sigmoid_pair(z):
    l1p = jnp.log(1.0 + jnp.exp(-jnp.abs(z)))
    return jnp.minimum(z, 0.0) - l1p, jnp.minimum(-z, 0.0) - l1p


def _iota(shape, dim):
    return lax.broadcasted_iota(I32, shape, dim)


def _tri(n, kind):
    r, c = _iota((n, n), 0), _iota((n, n), 1)
    m = {"gt": r > c, "lt": r < c, "ge": r >= c}[kind]
    return jnp.where(m, 1.0, 0.0).astype(BF16)


def _rms_bf16(x, g):
    y = x * lax.rsqrt(jnp.mean(x * x, axis=-1, keepdims=True) + RMS_EPS)
    return (y * g).astype(BF16)


def _rope128(xc, cos, sin):
    lane = _iota(xc.shape, 1)
    first = (lane % HD) < (HD // 2)
    rolled = jnp.where(first, pltpu.roll(xc, 128 - HD // 2, 1), pltpu.roll(xc, HD // 2, 1))
    return xc * cos + rolled * sin


def _halves(xc):
    return xc[:, :HD], pltpu.roll(xc, HD, 1)[:, :HD]


def _place(parts):
    r, c = _iota((HD, KV_W), 0), _iota((HD, KV_W), 1)
    out = None
    for j, pj in enumerate(parts):
        e = jnp.where(c == r + HD * j, 1.0, 0.0).astype(BF16)
        t = _dot(pj, e)
        out = t if out is None else out + t
    return out


def _stack_heads(q_ref):
    return jnp.concatenate([q_ref[0, j] for j in range(GROUP)], axis=0)


def _init_softmax(m_ref, l_ref, acc_ref):
    m_ref[...] = jnp.full(m_ref.shape, NEG, F32)
    l_ref[...] = jnp.zeros_like(l_ref)
    acc_ref[...] = jnp.zeros_like(acc_ref)


def _online_t(s, valid, vt, m_ref, l_ref, acc_ref, idx):
    if valid is not None:
        s = jnp.where(valid, s, NEG)
    m_old = m_ref[idx]
    m_new = jnp.maximum(m_old, jnp.max(s, axis=0, keepdims=True))
    alpha = jnp.exp(m_old - m_new)
    p = jnp.exp(s - m_new)
    if valid is not None:
        p = jnp.where(valid, p, 0.0)
    l_ref[idx] = alpha * l_ref[idx] + jnp.sum(p, axis=0, keepdims=True)
    acc_ref[idx] = alpha * acc_ref[idx] + _dot(vt, p.astype(BF16))
    m_ref[idx] = m_new


def _group_out_t(acc_t, l, tq):
    r = acc_t / l
    o_t = jnp.concatenate([r[:, j * tq:(j + 1) * tq] for j in range(GROUP)], axis=0)
    return o_t.T.astype(BF16)


def _topk_rank_select_t(gate_t, n_valid, k, nb):
    row = _iota(gate_t.shape, 0)
    gm = jnp.where(row < n_valid, gate_t, -jnp.inf)
    sel = jnp.zeros(gate_t.shape, F32)
    for n in range(nb):
        gn = gm[n:n + 1, :]
        beats = jnp.where(gm > gn, 1.0, jnp.where((gm == gn) & (row < n), 1.0, 0.0))
        rank = jnp.sum(beats, axis=0, keepdims=True)
        ok = jnp.where((rank < k) & (n < n_valid), 1.0, 0.0)
        sel = jnp.where(row == n, ok, sel)
    return sel


def _topk_rank_select(gate, n_valid, k, nb):
    lane = _iota(gate.shape, 1)
    gm = jnp.where(lane < n_valid, gate, -jnp.inf)
    sel = jnp.zeros(gate.shape, F32)
    for n in range(nb):
        gn = gm[:, n:n + 1]
        beats = jnp.where(gm > gn, 1.0, jnp.where((gm == gn) & (lane < n), 1.0, 0.0))
        rank = jnp.sum(beats, axis=1, keepdims=True)
        ok = jnp.where((rank < k) & (n < n_valid), 1.0, 0.0)
        sel = jnp.where(lane == n, ok, sel)
    return sel


def _order_key(score):
    bits = lax.bitcast_convert_type(score, I32)
    return jnp.where(bits < 0, bits ^ jnp.int32(0x7FFFFFFF), bits)


def _kth_largest_key(count_ge, shape, k):
    def body(it, v):
        cand = v + lax.shift_left(jnp.int32(1), 31 - it)
        return jnp.where(count_ge(cand) >= k, cand, v)

    return lax.fori_loop(0, 32, body, jnp.full(shape, INT_MIN, I32))


def _proj_kernel(*refs, kind, rope, prompt):
    it = iter(refs)
    x_ref, g_ref, w_ref = next(it), next(it), next(it)
    bf_ref = next(it) if kind == FOX else None
    cos_ref, sin_ref = (next(it), next(it)) if rope else (None, None)
    qhm_ref, k32_ref, v32_ref, khm_ref, vhm_ref = next(it), next(it), next(it), next(it), next(it)
    h = _rms_bf16(x_ref[0], g_ref[...])
    y = _dot(h, w_ref[...])
    cos = cos_ref[...] if rope else None
    sin = sin_ref[...] if rope else None

    def chunk(c, rot):
        xc = y[:, 128 * c:128 * (c + 1)]
        return _rope128(xc, cos, sin) if rot else xc

    q32hm_ref = next(it) if kind == MOBA else None
    for c in range(N_HEADS // 2):
        qc = chunk(c, rope)
        lo, hi = _halves(qc)
        qhm_ref[0, 2 * c] = (lo * 0.125).astype(BF16)
        qhm_ref[0, 2 * c + 1] = (hi * 0.125).astype(BF16)
        if kind == MOBA:
            q32hm_ref[0, 2 * c] = lo
            q32hm_ref[0, 2 * c + 1] = hi
    base = D_MODEL // 128
    for c in range(N_KV // 2):
        kc = chunk(base + c, rope)
        k32_ref[0, :, 128 * c:128 * (c + 1)] = kc
        vc = chunk(base + N_KV // 2 + c, False)
        v32_ref[0, :, 128 * c:128 * (c + 1)] = vc
        klo, khi = _halves(kc)
        khm_ref[0, 2 * c] = klo.astype(BF16)
        khm_ref[0, 2 * c + 1] = khi.astype(BF16)
        vlo, vhi = _halves(vc)
        vhm_ref[0, 2 * c] = vlo.astype(BF16)
        vhm_ref[0, 2 * c + 1] = vhi.astype(BF16)
    base = (D_MODEL + 2 * KV_W) // 128
    if kind == FOX:
        logf_ref = next(it)
        zf = chunk(base, False)[:, :N_HEADS] + bf_ref[...]
        logf_ref[0] = _log_sigmoid_pair(zf)[0]
    if kind == DSA:
        qihm_ref, ki_ref, wi_ref = next(it), next(it), next(it)
        for c in range(IDX_HEADS // 2):
            lo, hi = _halves(chunk(base + c, True))
            qihm_ref[0, 2 * c] = lo
            qihm_ref[0, 2 * c + 1] = hi
        ki_ref[0] = chunk(base + IDX_HEADS // 2, True)[:, :IDX_DIM]
        wi_ref[0] = chunk(base + IDX_HEADS // 2 + 1, False)[:, :IDX_HEADS] * (IDX_HEADS ** -0.5 * IDX_DIM ** -0.5)
    if prompt:
        vt_ref = next(it)
        vt_ref[0, 0] = y[:, D_MODEL + KV_W:D_MODEL + 2 * KV_W].T.astype(BF16)


def _proj(x, g, w_all, kind, rope_tabs, fox_b, prompt):
    B, T, _ = x.shape
    tm = min(PROJ_TM, T)
    n_cols = w_all.shape[1]
    rope = rope_tabs is not None
    row = lambda b, i: (b, i, 0)
    hm = lambda b, i: (b, 0, i, 0)
    in_specs = [pl.BlockSpec((1, tm, D_MODEL), row), pl.BlockSpec((1, D_MODEL), lambda b, i: (0, 0)),
                pl.BlockSpec((D_MODEL, n_cols), lambda b, i: (0, 0))]
    args = [x, g.reshape(1, D_MODEL), w_all]
    if kind == FOX:
        in_specs.append(pl.BlockSpec((1, N_HEADS), lambda b, i: (0, 0)))
        args.append(fox_b.reshape(1, N_HEADS))
    if rope:
        in_specs += [pl.BlockSpec((tm, 128), lambda b, i: (i, 0))] * 2
        args += list(rope_tabs)
    out_shape = [jax.ShapeDtypeStruct((B, N_HEADS, T, HD), BF16), jax.ShapeDtypeStruct((B, T, KV_W), F32),
                 jax.ShapeDtypeStruct((B, T, KV_W), F32), jax.ShapeDtypeStruct((B, N_KV, T, HD), BF16),
                 jax.ShapeDtypeStruct((B, N_KV, T, HD), BF16)]
    out_specs = [pl.BlockSpec((1, N_HEADS, tm, HD), hm), pl.BlockSpec((1, tm, KV_W), row),
                 pl.BlockSpec((1, tm, KV_W), row), pl.BlockSpec((1, N_KV, tm, HD), hm),
                 pl.BlockSpec((1, N_KV, tm, HD), hm)]
    if kind == MOBA:
        out_shape.append(jax.ShapeDtypeStruct((B, N_HEADS, T, HD), F32))
        out_specs.append(pl.BlockSpec((1, N_HEADS, tm, HD), hm))
    if kind == FOX:
        out_shape.append(jax.ShapeDtypeStruct((B, T, N_HEADS), F32))
        out_specs.append(pl.BlockSpec((1, tm, N_HEADS), row))
    if kind == DSA:
        out_shape += [jax.ShapeDtypeStruct((B, IDX_HEADS, T, IDX_DIM), F32),
                      jax.ShapeDtypeStruct((B, T, IDX_DIM), F32), jax.ShapeDtypeStruct((B, T, IDX_HEADS), F32)]
        out_specs += [pl.BlockSpec((1, IDX_HEADS, tm, IDX_DIM), hm), pl.BlockSpec((1, tm, IDX_DIM), row),
                      pl.BlockSpec((1, tm, IDX_HEADS), row)]
    if prompt:
        assert tm == KV_TILE
        out_shape.append(jax.ShapeDtypeStruct((B, T // tm, KV_W, tm), BF16))
        out_specs.append(pl.BlockSpec((1, 1, KV_W, tm), lambda b, i: (b, i, 0, 0)))
    return pl.pallas_call(
        functools.partial(_proj_kernel, kind=kind, rope=rope, prompt=prompt),
        grid=(B, T // tm), in_specs=in_specs, out_specs=out_specs, out_shape=out_shape,
        compiler_params=_cp("parallel", "arbitrary"), name=f"proj_{kind}_{'p' if prompt else 's'}")(*args)


def _out_proj_kernel(x_ref, o_ref, w_ref, y_ref):
    y_ref[...] = x_ref[...] + _dot(o_ref[...], w_ref[...])


def _out_proj(x2, o2, w_o):
    M = x2.shape[0]
    tm = min(512, M)
    return pl.pallas_call(
        _out_proj_kernel, grid=(M // tm,),
        in_specs=[pl.BlockSpec((tm, D_MODEL), lambda i: (i, 0)), pl.BlockSpec((tm, D_MODEL), lambda i: (i, 0)),
                  pl.BlockSpec((D_MODEL, D_MODEL), lambda i: (0, 0))],
        out_specs=pl.BlockSpec((tm, D_MODEL), lambda i: (i, 0)),
        out_shape=jax.ShapeDtypeStruct((M, D_MODEL), F32), compiler_params=_cp("parallel"), name="out_proj")(x2, o2, w_o)


def _gelu_tanh(x):
    return 0.5 * x * (1.0 + jnp.tanh(0.7978845608028654 * (x + 0.044715 * x * x * x)))


def _ffn_kernel(*refs, prompt, tiles_per_seq):
    if prompt:
        x_ref, g_ref, wa_ref, wb_ref, cw_ref, cb_ref, wd_ref, y_ref, conv_ref, h_ref, acc_ref, carry_ref = refs
    else:
        x_ref, g_ref, wa_ref, wb_ref, cw_ref, cb_ref, wd_ref, p2_ref, p1_ref, y_ref, a_ref, h_ref, acc_ref = refs
    i, c = pl.program_id(0), pl.program_id(1)

    @pl.when(c == 0)
    def _():
        h_ref[...] = _rms_bf16(x_ref[...], g_ref[...])
        acc_ref[...] = jnp.zeros_like(acc_ref)

    h = h_ref[...]
    a = _dot(h, wa_ref[...])
    gate = _dot(h, wb_ref[...])
    if prompt:
        @pl.when(i % tiles_per_seq == 0)
        def _():
            carry_ref[c] = jnp.zeros(carry_ref.shape[1:], F32)

        prev2, prev1 = carry_ref[c, 0:1, :], carry_ref[c, 1:2, :]
        row = _iota(a.shape, 0)
        back2 = pltpu.roll(a, 2, 0)
        a_m1 = jnp.where(row == 0, prev1, pltpu.roll(a, 1, 0))
        a_m2 = jnp.where(row == 0, prev2, jnp.where(row == 1, prev1, back2))
        carry_ref[c, 0:2, :] = back2[0:2, :]
        conv_ref[0] = back2[0:2, :]
    else:
        a_m2, a_m1 = p2_ref[...], p1_ref[...]
        a_ref[...] = a
    conv = cb_ref[...] + cw_ref[0:1, :] * a_m2 + cw_ref[1:2, :] * a_m1 + cw_ref[2:3, :] * a
    u = (_gelu_tanh(conv) * gate).astype(BF16)
    acc_ref[...] += _dot(u, wd_ref[...])

    @pl.when(c == pl.num_programs(1) - 1)
    def _():
        y_ref[...] = x_ref[...] + acc_ref[...]


def _ffn(x2, g, w_a, w_b, conv_w, conv_b, w_down, seq_len, prev=None):
    M = x2.shape[0]
    prompt = prev is None
    tm = min(1024, seq_len) if prompt else M
    n_c = D_FF // FF_CHUNK
    tps = seq_len // tm if prompt else 1
    xs = pl.BlockSpec((tm, D_MODEL), lambda i, c: (i, 0))
    in_specs = [xs, pl.BlockSpec((1, D_MODEL), lambda i, c: (0, 0)),
                pl.BlockSpec((D_MODEL, FF_CHUNK), lambda i, c: (0, c)), pl.BlockSpec((D_MODEL, FF_CHUNK), lambda i, c: (0, c)),
                pl.BlockSpec((CONV_W, FF_CHUNK), lambda i, c: (0, c)), pl.BlockSpec((1, FF_CHUNK), lambda i, c: (0, c)),
                pl.BlockSpec((FF_CHUNK, D_MODEL), lambda i, c: (c, 0))]
    args = [x2, g.reshape(1, D_MODEL), w_a, w_b, conv_w, conv_b.reshape(1, D_FF), w_down]
    scratch = [pltpu.VMEM((tm, D_MODEL), BF16), pltpu.VMEM((tm, D_MODEL), F32)]
    if prompt:
        out_shape = [jax.ShapeDtypeStruct((M, D_MODEL), F32), jax.ShapeDtypeStruct((M // tm, CONV_W - 1, D_FF), F32)]
        out_specs = [xs, pl.BlockSpec((1, CONV_W - 1, FF_CHUNK), lambda i, c: (i, 0, c))]
        scratch.append(pltpu.VMEM((n_c, 8, FF_CHUNK), F32))
    else:
        ps = pl.BlockSpec((tm, FF_CHUNK), lambda i, c: (i, c))
        in_specs += [ps, ps]
        args += list(prev)
        out_shape = [jax.ShapeDtypeStruct((M, D_MODEL), F32), jax.ShapeDtypeStruct((M, D_FF), F32)]
        out_specs = [xs, ps]
    y, extra = pl.pallas_call(
        functools.partial(_ffn_kernel, prompt=prompt, tiles_per_seq=tps), grid=(M // tm, n_c),
        in_specs=in_specs, out_specs=out_specs, out_shape=out_shape, scratch_shapes=scratch,
        compiler_params=_cp("arbitrary", "arbitrary"), name="ffn_p" if prompt else "ffn_s")(*args)
    return y, (extra[tps - 1::tps] if prompt else extra)


def _final_norm_kernel(x_ref, g_ref, y_ref):
    x = x_ref[...]
    y_ref[...] = x * lax.rsqrt(jnp.mean(x * x, axis=-1, keepdims=True) + RMS_EPS) * g_ref[...]


def _final_norm(x2, g):
    M = x2.shape[0]
    tm = min(1024, M)
    return pl.pallas_call(
        _final_norm_kernel, grid=(M // tm,),
        in_specs=[pl.BlockSpec((tm, D_MODEL), lambda i: (i, 0)), pl.BlockSpec((1, D_MODEL), lambda i: (0, 0))],
        out_specs=pl.BlockSpec((tm, D_MODEL), lambda i: (i, 0)),
        out_shape=jax.ShapeDtypeStruct((M, D_MODEL), F32), compiler_params=_cp("parallel"), name="final_norm")(x2, g.reshape(1, D_MODEL))


def _stick_kernel(q_ref, k_ref, v_ref, o_ref, r_ref, acc_ref, *, tq, tk):
    q0 = pl.program_id(2) * tq
    qs = _stack_heads(q_ref)
    m = GROUP * tq
    u = _tri(tk, "gt")
    qpos = q0 + _iota((m, 1), 0) % tq
    r_ref[...] = jnp.zeros_like(r_ref)
    acc_ref[...] = jnp.zeros_like(acc_ref)

    def tile(j, masked):
        ks = pl.multiple_of(j * tk, tk)
        kt, vt = k_ref[0, 0, pl.ds(ks, tk), :], v_ref[0, 0, pl.ds(ks, tk), :]
        z = _dot_nt(qs, kt)
        ls, lk = _log_sigmoid_pair(z)
        if masked:
            past = (ks + _iota((m, tk), 1)) < qpos
            lk = jnp.where(past, lk, 0.0)
        hi, lo = _split2(lk)
        cs = _dot(hi, u) + _dot(lo, u)
        w = jnp.exp(ls + cs + r_ref[...])
        if masked:
            w = jnp.where(past, w, 0.0)
        acc_ref[...] += _dot(w.astype(BF16), vt)
        r_ref[...] += jnp.sum(lk, axis=1, keepdims=True)

    jd = q0 // tk
    tile(jd, True)

    def body(n, carry):
        tile(jd - 1 - n, False)
        return carry

    lax.fori_loop(0, jd, body, 0)
    rows = acc_ref[...]
    o_ref[0] = _place([rows[j * tq:(j + 1) * tq].astype(BF16) for j in range(GROUP)]).astype(BF16)


def _stick_prompt(qhm, khm, vhm):
    B, _, T, _ = qhm.shape
    tq, tk = 128, KV_TILE
    m = GROUP * tq
    return pl.pallas_call(
        functools.partial(_stick_kernel, tq=tq, tk=tk), grid=(B, N_KV, T // tq),
        in_specs=[pl.BlockSpec((1, GROUP, tq, HD), lambda b, g, i: (b, g, i, 0)),
                  pl.BlockSpec((1, 1, T, HD), lambda b, g, i: (b, g, 0, 0)),
                  pl.BlockSpec((1, 1, T, HD), lambda b, g, i: (b, g, 0, 0))],
        out_specs=pl.BlockSpec((1, tq, KV_W), lambda b, g, i: (b, i, g)),
        out_shape=jax.ShapeDtypeStruct((B, T, D_MODEL), BF16),
        scratch_shapes=[pltpu.VMEM((m, 1), F32), pltpu.VMEM((m, HD), F32)],
        compiler_params=_cp("parallel", "parallel", "arbitrary"), name="stick_p")(qhm, khm, vhm)


def _cumsum_kernel(x_ref, hi_ref, mid_ref, lo_ref, carry_ref):
    @pl.when(pl.program_id(1) == 0)
    def _():
        carry_ref[...] = jnp.zeros_like(carry_ref)

    n = x_ref.shape[1]
    low = _tri(n, "ge")
    hi, mid, lo = _split3(x_ref[0])
    c = _dot(low, hi) + _dot(low, mid) + _dot(low, lo) + carry_ref[...]
    carry_ref[...] = c[n - 1:n, :]
    hi_ref[0], mid_ref[0], lo_ref[0] = _split3(c)


def _cumsum_time_parts(x):
    B, T, C = x.shape
    n = 256
    spec = pl.BlockSpec((1, n, C), lambda b, i: (b, i, 0))
    return pl.pallas_call(
        _cumsum_kernel, grid=(B, T // n), in_specs=[spec], out_specs=[spec] * 3,
        out_shape=[jax.ShapeDtypeStruct((B, T, C), BF16)] * 3, scratch_shapes=[pltpu.VMEM((1, C), F32)],
        compiler_params=_cp("parallel", "arbitrary"), name="cumsum_time")(x)


def _softmax_scratch_t(n, m):
    return [pltpu.VMEM((n, 1, m), F32), pltpu.VMEM((n, 1, m), F32), pltpu.VMEM((n, HD, m), F32)]


def _causal_kernel(q_ref, k_ref, vt_ref, o_ref, m_ref, l_ref, acc_ref, *, tq, tk):
    q0 = pl.program_id(2) * tq
    qs = _stack_heads(q_ref)
    m = GROUP * tq
    _init_softmax(m_ref, l_ref, acc_ref)
    qpos = q0 + _iota((1, m), 1) % tq

    def tile(j, masked):
        ks = pl.multiple_of(j * tk, tk)
        s = _dot_nt(k_ref[0, 0, pl.ds(ks, tk), :], qs)
        valid = (ks + _iota((tk, 1), 0)) <= qpos if masked else None
        _online_t(s, valid, vt_ref[0, j], m_ref, l_ref, acc_ref, 0)

    jd = q0 // tk
    tile(jd, True)

    def body(n, carry):
        tile(n, False)
        return carry

    lax.fori_loop(0, jd, body, 0)
    o_ref[0] = _group_out_t(acc_ref[0], l_ref[0], tq)


def _fox_prompt(qhm, khm, vt, cum_parts):
    B, _, T, _ = qhm.shape
    tq, tk = 256, KV_TILE
    d_aug = 2 * HD
    c = jnp.stack(cum_parts, axis=-1).reshape(B, T, N_KV, FOX_AUG).transpose(0, 2, 1, 3)
    k_aug = jnp.concatenate([khm, c, jnp.zeros((B, N_KV, T, d_aug - HD - FOX_AUG), BF16)], axis=-1)
    pat = np.zeros((N_HEADS, d_aug - HD), np.float32)
    for h in range(N_HEADS):
        pat[h, 3 * (h % GROUP):3 * (h % GROUP) + 3] = -1.0
    q_aug = jnp.concatenate([qhm, jnp.broadcast_to(jnp.asarray(pat, BF16)[None, :, None, :], (B, N_HEADS, T, d_aug - HD))], axis=-1)
    m = GROUP * tq
    return pl.pallas_call(
        functools.partial(_causal_kernel, tq=tq, tk=tk), grid=(B, N_KV, T // tq),
        in_specs=[pl.BlockSpec((1, GROUP, tq, d_aug), lambda b, g, i: (b, g, i, 0)),
                  pl.BlockSpec((1, 1, T, d_aug), lambda b, g, i: (b, g, 0, 0)),
                  pl.BlockSpec((1, T // tk, HD, tk), lambda b, g, i: (b, 0, g, 0))],
        out_specs=pl.BlockSpec((1, tq, KV_W), lambda b, g, i: (b, i, g)),
        out_shape=jax.ShapeDtypeStruct((B, T, D_MODEL), BF16), scratch_shapes=_softmax_scratch_t(1, m),
        compiler_params=_cp("parallel", "parallel", "arbitrary"), name="fox_p")(q_aug, k_aug, vt)


def _block_mean_kernel(k_ref, o_ref):
    k = k_ref[0]
    nb = k.shape[0] // MOBA_BLOCK
    o_ref[0] = jnp.sum(k.reshape(nb, MOBA_BLOCK, KV_W), axis=1) * (1.0 / MOBA_BLOCK)


def _block_mean(k32):
    B, T, _ = k32.shape
    nb = T // MOBA_BLOCK
    return pl.pallas_call(
        _block_mean_kernel, grid=(B,), in_specs=[pl.BlockSpec((1, T, KV_W), lambda b: (b, 0, 0))],
        out_specs=pl.BlockSpec((1, nb, KV_W), lambda b: (b, 0, 0)),
        out_shape=jax.ShapeDtypeStruct((B, nb, KV_W), F32), compiler_params=_cp("parallel"), name="block_mean")(k32)


def _moba_kernel(q_ref, q32_ref, km_ref, k_ref, vt_ref, o_ref, sel_ref, m_ref, l_ref, acc_ref, *, tq, nb):
    tk = MOBA_BLOCK
    q0 = pl.program_id(2) * tq
    own = q0 // tk
    qs = _stack_heads(q_ref)
    m = GROUP * tq
    _init_softmax(m_ref, l_ref, acc_ref)
    sel_ref[...] = _topk_rank_select_t(_dot_nt_f32(km_ref[0, 0], _stack_heads(q32_ref)), own, MOBA_TOPK, nb)
    qpos = q0 + _iota((1, m), 1) % tq
    ks = pl.multiple_of(own * tk, tk)
    s = _dot_nt(k_ref[0, 0, pl.ds(ks, tk), :], qs)
    _online_t(s, (ks + _iota((tk, 1), 0)) <= qpos, vt_ref[0, own], m_ref, l_ref, acc_ref, 0)

    def body(j, carry):
        ks = pl.multiple_of(j * tk, tk)
        bias = jnp.where(sel_ref[pl.ds(j, 1), :] > 0.5, 0.0, NEG)
        _online_t(_dot_nt(k_ref[0, 0, pl.ds(ks, tk), :], qs) + bias, None, vt_ref[0, j], m_ref, l_ref, acc_ref, 0)
        return carry

    lax.fori_loop(0, own, body, 0)
    o_ref[0] = _group_out_t(acc_ref[0], l_ref[0], tq)


def _moba_prompt(qhm, q32hm, kmean, khm, vt):
    B, _, T, _ = qhm.shape
    tq = 256
    nb = T // MOBA_BLOCK
    m = GROUP * tq
    km = kmean.reshape(B, nb, N_KV, HD).transpose(0, 2, 1, 3)
    qspec = pl.BlockSpec((1, GROUP, tq, HD), lambda b, g, i: (b, g, i, 0))
    return pl.pallas_call(
        functools.partial(_moba_kernel, tq=tq, nb=nb), grid=(B, N_KV, T // tq),
        in_specs=[qspec, qspec, pl.BlockSpec((1, 1, nb, HD), lambda b, g, i: (b, g, 0, 0)),
                  pl.BlockSpec((1, 1, T, HD), lambda b, g, i: (b, g, 0, 0)),
                  pl.BlockSpec((1, nb, HD, MOBA_BLOCK), lambda b, g, i: (b, 0, g, 0))],
        out_specs=pl.BlockSpec((1, tq, KV_W), lambda b, g, i: (b, i, g)),
        out_shape=jax.ShapeDtypeStruct((B, T, D_MODEL), BF16),
        scratch_shapes=[pltpu.VMEM((nb, m), F32)] + _softmax_scratch_t(1, m),
        compiler_params=_cp("parallel", "parallel", "arbitrary"), name="moba_p")(qhm, q32hm, km, khm, vt)


def _dsa_kernel(q_ref, qi_ref, wi_ref, ki_ref, k_ref, vt_ref, o_ref, key_ref, m_ref, l_ref, acc_ref, *, tq, n_keep):
    tk = KV_TILE
    q0 = pl.program_id(1) * tq
    n_proc = (q0 + tq + tk - 1) // tk
    qpos = q0 + _iota((1, tq), 1)

    def visible(c):
        return (c * tk + _iota((tk, 1), 0)) <= qpos

    qih, qil = _split2(jnp.concatenate([qi_ref[0, i] for i in range(IDX_HEADS)], axis=0))

    def score_tile(c, carry):
        ks = pl.multiple_of(c * tk, tk)
        kh, kl = _split2(ki_ref[0, pl.ds(ks, tk), :])
        dots = _dot_nt(kh, qih) + _dot_nt(kh, qil) + _dot_nt(kl, qih)
        sc = jnp.zeros((tk, tq), F32)
        for i in range(IDX_HEADS):
            sc = sc + wi_ref[0, i:i + 1, :] * jnp.maximum(dots[:, i * tq:(i + 1) * tq], 0.0)
        key_ref[c] = _order_key(jnp.where(visible(c), sc + 0.0, -jnp.inf))
        return carry

    lax.fori_loop(0, n_proc, score_tile, 0)

    def count(pred_of_tile):
        def body(c, part):
            return part + jnp.where(pred_of_tile(key_ref[c]), 1.0, 0.0)

        return jnp.sum(lax.fori_loop(0, n_proc, body, jnp.zeros((tk, tq), F32)), axis=0, keepdims=True)

    thr = _kth_largest_key(lambda cand: count(lambda k: k >= cand), (1, tq), n_keep)
    need = n_keep - count(lambda k: k > thr)
    _init_softmax(m_ref, l_ref, acc_ref)
    before = _tri(tk, "gt")
    qs = [jnp.concatenate([q_ref[0, GROUP * g + j] for j in range(GROUP)], axis=0) for g in range(N_KV)]

    def attend(c, ties_seen):
        ks = pl.multiple_of(c * tk, tk)
        key = key_ref[c]
        tied = jnp.where(key == thr, 1.0, 0.0)
        rank = _dot(before, tied.astype(BF16)) + ties_seen
        keep = (key > thr) | ((key == thr) & (rank < need))
        keep = jnp.where(keep & visible(c), 1.0, 0.0)
        valid = jnp.concatenate([keep] * GROUP, axis=1) > 0.5
        for g in range(N_KV):
            s = _dot_nt(k_ref[0, g, pl.ds(ks, tk), :], qs[g])
            _online_t(s, valid, vt_ref[0, c, HD * g:HD * (g + 1), :], m_ref, l_ref, acc_ref, g)
        return ties_seen + jnp.sum(tied, axis=0, keepdims=True)

    lax.fori_loop(0, n_proc, attend, jnp.zeros((1, tq), F32))
    for g in range(N_KV):
        o_ref[0, :, KV_W * g:KV_W * (g + 1)] = _group_out_t(acc_ref[g], l_ref[g], tq)


def _dsa_prompt(qhm, qihm, wi, ki, khm, vt):
    B, _, T, _ = qhm.shape
    tq = 128
    m = GROUP * tq
    return pl.pallas_call(
        functools.partial(_dsa_kernel, tq=tq, n_keep=min(IDX_TOPK, T // 4)), grid=(B, T // tq),
        in_specs=[pl.BlockSpec((1, N_HEADS, tq, HD), lambda b, i: (b, 0, i, 0)),
                  pl.BlockSpec((1, IDX_HEADS, tq, IDX_DIM), lambda b, i: (b, 0, i, 0)),
                  pl.BlockSpec((1, IDX_HEADS, tq), lambda b, i: (b, 0, i)),
                  pl.BlockSpec((1, T, IDX_DIM), lambda b, i: (b, 0, 0)),
                  pl.BlockSpec((1, N_KV, T, HD), lambda b, i: (b, 0, 0, 0)),
                  pl.BlockSpec((1, T // KV_TILE, KV_W, KV_TILE), lambda b, i: (b, 0, 0, 0))],
        out_specs=pl.BlockSpec((1, tq, D_MODEL), lambda b, i: (b, i, 0)),
        out_shape=jax.ShapeDtypeStruct((B, T, D_MODEL), BF16),
        scratch_shapes=[pltpu.VMEM((T // KV_TILE, KV_TILE, tq), I32)] + _softmax_scratch_t(N_KV, m),
        compiler_params=_cp("parallel", "arbitrary"), name="dsa_p")(qhm, qihm, wi.transpose(0, 2, 1), ki, khm, vt)


def _page_specs(layer, n_pages, nps, reverse):
    def spec(i):
        def idx(b, s, pt):
            p = s * nps + i
            return (layer, pt[b, n_pages - 1 - p if reverse else p], 0, 0)

        return pl.BlockSpec((None, None, KV_W, PAGE), idx)

    return [spec(i) for i in range(nps)]


def _pool_specs(rows, n_pages, nps, reverse):
    def spec(i):
        def idx(b, s, pt):
            p = s * nps + i
            return (pt[b, n_pages - 1 - p if reverse else p], 0, 0)

        return pl.BlockSpec((None, rows, PAGE), idx)

    return [spec(i) for i in range(nps)]


def _row_spec(width):
    return pl.BlockSpec((1, 1, width), lambda b, s, pt: (b, 0, 0))


def _head_spec(width):
    return pl.BlockSpec((1, N_HEADS, width), lambda b, s, pt: (b, 0, 0))


def _softmax_scratch():
    return [pltpu.VMEM((N_HEADS, 1), F32), pltpu.VMEM((N_HEADS, 1), F32), pltpu.VMEM((N_HEADS, KV_W), F32)]


def _new_key_start(q, knew, vnew, m_ref, l_ref, acc_ref):
    z = jnp.sum(q.astype(F32) * knew.astype(BF16).astype(F32), axis=1, keepdims=True)
    m_ref[...] = z
    l_ref[...] = jnp.ones_like(z)
    acc_ref[...] = jnp.broadcast_to(vnew.astype(BF16).astype(F32), acc_ref.shape)


def _online_pages(s_list, valid_list, v_refs, m_ref, l_ref, acc_ref):
    s = jnp.concatenate(s_list, axis=1)
    valid = None if valid_list is None else jnp.concatenate(valid_list, axis=1)
    if valid is not None:
        s = jnp.where(valid, s, NEG)
    m_old = m_ref[...]
    m_new = jnp.maximum(m_old, jnp.max(s, axis=1, keepdims=True))
    alpha = jnp.exp(m_old - m_new)
    p = jnp.exp(s - m_new)
    if valid is not None:
        p = jnp.where(valid, p, 0.0)
    l_ref[...] = alpha * l_ref[...] + jnp.sum(p, axis=1, keepdims=True)
    pv = None
    for i, v_ref in enumerate(v_refs):
        t = _dot_nt(p[:, PAGE * i:PAGE * (i + 1)].astype(BF16), v_ref[...].astype(BF16))
        pv = t if pv is None else pv + t
    acc_ref[...] = alpha * acc_ref[...] + pv
    m_ref[...] = m_new


def _dec_stick_kernel(pt_ref, q_ref, *refs, nps):
    k_refs, v_refs = refs[:nps], refs[nps:2 * nps]
    o_ref, r_ref, acc_ref = refs[2 * nps:]
    step = pl.program_id(1)

    @pl.when(step == 0)
    def _():
        r_ref[...] = jnp.zeros_like(r_ref)
        acc_ref[...] = jnp.zeros_like(acc_ref)

    q = q_ref[0]
    u = _tri(PAGE, "gt")
    r, acc = r_ref[...], acc_ref[...]
    for i in range(nps):
        ls, lk = _log_sigmoid_pair(_dot(q, k_refs[i][...].astype(BF16)))
        hi, lo = _split2(lk)
        w = jnp.exp(ls + _dot(hi, u) + _dot(lo, u) + r)
        acc = acc + _dot_nt(w.astype(BF16), v_refs[i][...].astype(BF16))
        r = r + jnp.sum(lk, axis=1, keepdims=True)
    r_ref[...], acc_ref[...] = r, acc

    @pl.when(step == pl.num_programs(1) - 1)
    def _():
        o_ref[0] = acc


def _dec_call(kernel, name, grid_spec, DB):
    return pl.pallas_call(kernel, grid_spec=grid_spec, out_shape=jax.ShapeDtypeStruct((DB, N_HEADS, KV_W), F32),
                          compiler_params=_cp("parallel", "arbitrary"), name=name)


def _dec_stick(qx, kt, vt, page_table, layer):
    DB, n_pages = page_table.shape
    nps = math.gcd(n_pages, DEC_PAGES)
    pages = _page_specs(layer, n_pages, nps, True)
    grid_spec = pltpu.PrefetchScalarGridSpec(
        num_scalar_prefetch=1, grid=(DB, n_pages // nps), in_specs=[_head_spec(KV_W)] + pages + pages,
        out_specs=_head_spec(KV_W), scratch_shapes=[pltpu.VMEM((N_HEADS, 1), F32), pltpu.VMEM((N_HEADS, KV_W), F32)])
    return _dec_call(functools.partial(_dec_stick_kernel, nps=nps), "stick_s", grid_spec, DB)(
        page_table, qx, *([kt] * nps), *([vt] * nps))


def _dec_fox_kernel(pt_ref, q_ref, *refs, nps):
    k_refs, v_refs, lf_refs = refs[:nps], refs[nps:2 * nps], refs[2 * nps:3 * nps]
    kn_ref, vn_ref, lfn_ref, o_ref, s_ref, m_ref, l_ref, acc_ref = refs[3 * nps:]
    step = pl.program_id(1)

    @pl.when(step == 0)
    def _():
        s_ref[...] = lfn_ref[0]
        _new_key_start(q_ref[0], kn_ref[0], vn_ref[0], m_ref, l_ref, acc_ref)

    q = q_ref[0]
    u = _tri(PAGE, "gt")
    later = s_ref[...]
    s_list = []
    for i in range(nps):
        lf = lf_refs[i][...]
        hi, mid, lo = _split3(lf)
        s_list.append(_dot(q, k_refs[i][...].astype(BF16)) + _dot(hi, u) + _dot(mid, u) + _dot(lo, u) + later)
        later = later + jnp.sum(lf, axis=1, keepdims=True)
    s_ref[...] = later
    _online_pages(s_list, None, v_refs, m_ref, l_ref, acc_ref)

    @pl.when(step == pl.num_programs(1) - 1)
    def _():
        o_ref[0] = acc_ref[...] / l_ref[...]


def _dec_fox(qx, kt, vt, logf_t, knew, vnew, lfnew, page_table, layer):
    DB, n_pages = page_table.shape
    nps = math.gcd(n_pages, DEC_PAGES)
    pages = _page_specs(layer, n_pages, nps, True)
    grid_spec = pltpu.PrefetchScalarGridSpec(
        num_scalar_prefetch=1, grid=(DB, n_pages // nps),
        in_specs=[_head_spec(KV_W)] + pages + pages + _pool_specs(N_HEADS, n_pages, nps, True)
        + [_row_spec(KV_W), _row_spec(KV_W), _head_spec(1)],
        out_specs=_head_spec(KV_W), scratch_shapes=[pltpu.VMEM((N_HEADS, 1), F32)] + _softmax_scratch())
    return _dec_call(functools.partial(_dec_fox_kernel, nps=nps), "fox_s", grid_spec, DB)(
        page_table, qx, *([kt] * nps), *([vt] * nps), *([logf_t] * nps), knew, vnew, lfnew)


def _dec_kmean_kernel(pt_ref, *refs, nps):
    k_refs, o_ref = refs[:nps], refs[nps]
    step = pl.program_id(1)

    @pl.when(step == 0)
    def _():
        o_ref[...] = jnp.zeros_like(o_ref)

    ones = jnp.ones((8, PAGE), BF16)
    for i in range(nps):
        hi, mid, lo = _split3(k_refs[i][...])
        tot = _dot_nt(ones, hi) + _dot_nt(ones, mid) + _dot_nt(ones, lo)
        blk = (step * nps + i) // (MOBA_BLOCK // PAGE)
        o_ref[0, pl.ds(blk, 1), :] = o_ref[0, pl.ds(blk, 1), :] + tot[0:1, :] * (1.0 / MOBA_BLOCK)


def _dec_kmean(kt, page_table, layer):
    DB, n_pages = page_table.shape
    nps = math.gcd(n_pages, DEC_PAGES)
    nb = n_pages * PAGE // MOBA_BLOCK
    grid_spec = pltpu.PrefetchScalarGridSpec(
        num_scalar_prefetch=1, grid=(DB, n_pages // nps), in_specs=_page_specs(layer, n_pages, nps, False),
        out_specs=pl.BlockSpec((1, nb, KV_W), lambda b, s, pt: (b, 0, 0)))
    return pl.pallas_call(
        functools.partial(_dec_kmean_kernel, nps=nps), grid_spec=grid_spec,
        out_shape=jax.ShapeDtypeStruct((DB, nb, KV_W), F32),
        compiler_params=_cp("parallel", "arbitrary"), name="kmean_s")(page_table, *([kt] * nps))


def _dec_moba_kernel(pt_ref, q_ref, q32_ref, km_ref, *refs, nps, nb):
    k_refs, v_refs = refs[:nps], refs[nps:2 * nps]
    kn_ref, vn_ref, o_ref, sel_ref, m_ref, l_ref, acc_ref = refs[2 * nps:]
    step = pl.program_id(1)

    @pl.when(step == 0)
    def _():
        sel_ref[...] = _topk_rank_select(_dot_nt_f32(q32_ref[0], km_ref[0]), nb, MOBA_TOPK, nb)
        _new_key_start(q_ref[0], kn_ref[0], vn_ref[0], m_ref, l_ref, acc_ref)

    q = q_ref[0]
    lane = _iota((N_HEADS, nb), 1)
    s_list = []
    for i in range(nps):
        blk = (step * nps + i) // (MOBA_BLOCK // PAGE)
        picked = jnp.sum(jnp.where(lane == blk, sel_ref[...], 0.0), axis=1, keepdims=True) > 0.5
        bias = jnp.where(picked, 0.0, NEG)
        s_list.append(_dot(q, k_refs[i][...].astype(BF16)) + bias)
    _online_pages(s_list, None, v_refs, m_ref, l_ref, acc_ref)

    @pl.when(step == pl.num_programs(1) - 1)
    def _():
        o_ref[0] = acc_ref[...] / l_ref[...]


def _dec_moba(qx, q32x, kmean, kt, vt, knew, vnew, page_table, layer):
    DB, n_pages = page_table.shape
    nps = math.gcd(n_pages, DEC_PAGES)
    nb = kmean.shape[1]
    pages = _page_specs(layer, n_pages, nps, False)
    grid_spec = pltpu.PrefetchScalarGridSpec(
        num_scalar_prefetch=1, grid=(DB, n_pages // nps),
        in_specs=[_head_spec(KV_W), _head_spec(KV_W), pl.BlockSpec((1, nb, KV_W), lambda b, s, pt: (b, 0, 0))]
        + pages + pages + [_row_spec(KV_W), _row_spec(KV_W)],
        out_specs=_head_spec(KV_W), scratch_shapes=[pltpu.VMEM((N_HEADS, nb), F32)] + _softmax_scratch())
    return _dec_call(functools.partial(_dec_moba_kernel, nps=nps, nb=nb), "moba_s", grid_spec, DB)(
        page_table, qx, q32x, kmean, *([kt] * nps), *([vt] * nps), knew, vnew)


def _dec_score_kernel(pt_ref, qi_ref, wi_ref, *refs, nps, n_pages):
    kidx_refs = refs[:nps]
    kin_ref, o_ref = refs[nps:]
    step = pl.program_id(1)
    qi, wi = qi_ref[0], wi_ref[0]

    @pl.when(step == 0)
    def _():
        o_ref[...] = jnp.full(o_ref.shape, -jnp.inf, F32)
        dn = jnp.sum(qi * kin_ref[0], axis=1, keepdims=True)
        sn = jnp.sum(wi * jnp.maximum(dn, 0.0), axis=0, keepdims=True) + 0.0
        o_ref[0, n_pages:n_pages + 1, :] = jnp.where(_iota((1, PAGE), 1) == 0, sn, -jnp.inf)

    for i in range(nps):
        dots = _dot_f32(qi, kidx_refs[i][...])
        o_ref[0, pl.ds(step * nps + i, 1), :] = jnp.sum(wi * jnp.maximum(dots, 0.0), axis=0, keepdims=True) + 0.0


def _dec_scores(qi3, wi3, kidx_t, kinew, page_table):
    DB, n_pages = page_table.shape
    assert n_pages < PAGE
    nps = math.gcd(n_pages, DEC_PAGES)
    grid_spec = pltpu.PrefetchScalarGridSpec(
        num_scalar_prefetch=1, grid=(DB, n_pages // nps),
        in_specs=[pl.BlockSpec((1, IDX_HEADS, IDX_DIM), lambda b, s, pt: (b, 0, 0)),
                  pl.BlockSpec((1, IDX_HEADS, 1), lambda b, s, pt: (b, 0, 0))]
        + _pool_specs(IDX_DIM, n_pages, nps, False) + [_row_spec(IDX_DIM)],
        out_specs=pl.BlockSpec((1, PAGE, PAGE), lambda b, s, pt: (b, 0, 0)))
    return pl.pallas_call(
        functools.partial(_dec_score_kernel, nps=nps, n_pages=n_pages), grid_spec=grid_spec,
        out_shape=jax.ShapeDtypeStruct((DB, PAGE, PAGE), F32),
        compiler_params=_cp("parallel", "arbitrary"), name="dsa_score_s")(page_table, qi3, wi3, *([kidx_t] * nps), kinew)


def _dec_select_kernel(sc_ref, o_ref, *, n_keys, n_keep):
    key = _order_key(sc_ref[0])

    def total(x):
        return jnp.sum(jnp.sum(x, axis=1, keepdims=True), axis=0, keepdims=True)

    thr = _kth_largest_key(lambda cand: total(jnp.where(key >= cand, 1.0, 0.0)), (1, 1), n_keep)
    need = n_keep - total(jnp.where(key > thr, 1.0, 0.0))
    tied = jnp.where(key == thr, 1.0, 0.0)
    in_row = _dot(tied.astype(BF16), _tri(PAGE, "lt"))
    row_tot = jnp.broadcast_to(jnp.sum(tied, axis=1, keepdims=True), tied.shape).astype(BF16)
    rank = in_row + _dot(_tri(PAGE, "gt"), row_tot)
    pos = _iota(key.shape, 0) * PAGE + _iota(key.shape, 1)
    keep = ((key > thr) | ((key == thr) & (rank < need))) & (pos < n_keys)
    o_ref[0] = jnp.where(keep, 1.0, 0.0)


def _dec_select(scores, n_keys):
    DB = scores.shape[0]
    spec = pl.BlockSpec((1, PAGE, PAGE), lambda b: (b, 0, 0))
    return pl.pallas_call(
        functools.partial(_dec_select_kernel, n_keys=n_keys, n_keep=min(IDX_TOPK, n_keys // 4)), grid=(DB,),
        in_specs=[spec], out_specs=spec, out_shape=jax.ShapeDtypeStruct(scores.shape, F32),
        compiler_params=_cp("parallel"), name="dsa_select_s")(scores)


def _dec_dsa_kernel(pt_ref, q_ref, *refs, nps, n_pages):
    k_refs, v_refs = refs[:nps], refs[nps:2 * nps]
    sel_ref, kn_ref, vn_ref, o_ref, m_ref, l_ref, acc_ref = refs[2 * nps:]
    step = pl.program_id(1)
    q = q_ref[0]

    @pl.when(step == 0)
    def _():
        zn = jnp.sum(q.astype(F32) * kn_ref[0].astype(BF16).astype(F32), axis=1, keepdims=True)
        keep = jnp.broadcast_to(sel_ref[0, n_pages:n_pages + 1, 0:1] > 0.5, zn.shape)
        m_ref[...] = jnp.where(keep, zn, NEG)
        l_ref[...] = jnp.where(keep, 1.0, 0.0)
        acc_ref[...] = jnp.where(keep, jnp.broadcast_to(vn_ref[0].astype(BF16).astype(F32), acc_ref.shape), 0.0)

    s_list, valid_list = [], []
    for i in range(nps):
        s_list.append(_dot(q, k_refs[i][...].astype(BF16)))
        valid_list.append(jnp.broadcast_to(sel_ref[0, pl.ds(step * nps + i, 1), :] > 0.5, (N_HEADS, PAGE)))
    _online_pages(s_list, valid_list, v_refs, m_ref, l_ref, acc_ref)

    @pl.when(step == pl.num_programs(1) - 1)
    def _():
        o_ref[0] = acc_ref[...] / l_ref[...]


def _dec_dsa(qx, kt, vt, sel, knew, vnew, page_table, layer):
    DB, n_pages = page_table.shape
    nps = math.gcd(n_pages, DEC_PAGES)
    pages = _page_specs(layer, n_pages, nps, False)
    grid_spec = pltpu.PrefetchScalarGridSpec(
        num_scalar_prefetch=1, grid=(DB, n_pages // nps),
        in_specs=[_head_spec(KV_W)] + pages + pages
        + [pl.BlockSpec((1, PAGE, PAGE), lambda b, s, pt: (b, 0, 0)), _row_spec(KV_W), _row_spec(KV_W)],
        out_specs=_head_spec(KV_W), scratch_shapes=_softmax_scratch())
    return _dec_call(functools.partial(_dec_dsa_kernel, nps=nps, n_pages=n_pages), "dsa_s", grid_spec, DB)(
        page_table, qx, *([kt] * nps), *([vt] * nps), sel, knew, vnew)


def _rope_tables(pos):
    half = HD // 2
    inv_freq = jnp.power(ROPE_THETA, -jnp.arange(half, dtype=F32) * (2.0 / HD))
    ang = pos.astype(F32)[:, None] * inv_freq[None, :]
    cos, sin = jnp.cos(ang), jnp.sin(ang)
    return jnp.concatenate([cos] * 4, axis=1), jnp.concatenate([-sin, sin] * 2, axis=1)


def _pad_cols(w, mult=128):
    pad = (-w.shape[1]) % mult
    return jnp.pad(w, ((0, 0), (0, pad))) if pad else w


def _layer_weights(kind, w_qkv, fox_w_f, idx_w_q, idx_w_k, idx_w_w):
    cols = [w_qkv]
    if kind == FOX:
        cols.append(_pad_cols(fox_w_f))
    if kind == DSA:
        cols += [idx_w_q, _pad_cols(idx_w_k), _pad_cols(idx_w_w)]
    return jnp.concatenate(cols, axis=1).astype(BF16)


_HEAD_ONEHOT = np.arange(N_HEADS)[:, None] // GROUP == np.arange(N_KV)[None, :]


def _expand_heads(q, dtype):
    m = jnp.asarray(_HEAD_ONEHOT, q.dtype)
    return (q[:, :, None, :] * m[None, :, :, None]).reshape(q.shape[0], N_HEADS, KV_W).astype(dtype)


def _collapse_heads(ox):
    m = jnp.asarray(_HEAD_ONEHOT, ox.dtype)
    o = jnp.sum(ox.reshape(ox.shape[0], N_HEADS, N_KV, HD) * m[None, :, :, None], axis=2)
    return o.reshape(ox.shape[0], D_MODEL).astype(BF16)


def _pages_t(cache):
    depth, n_pool = cache.shape[:2]
    return cache.transpose(0, 1, 3, 4, 2).reshape(depth, n_pool, KV_W, PAGE)


def kernel(x_prompt, x_sample, cache_k, cache_v, cache_logf, cache_kidx, state_conv, page_table, norm_mix, norm_ffn, norm_final, w_qkv, w_o, fox_w_f, fox_b_f, idx_w_q, idx_w_k, idx_w_w, ffn_w_a, ffn_w_b, ffn_conv_w, ffn_conv_b, ffn_w_down):
    B, T, _ = x_prompt.shape
    DB = x_sample.shape[0]
    depth = cache_k.shape[0]
    n_pages = page_table.shape[1]
    past = n_pages * PAGE
    assert T % MOBA_BLOCK == 0 and past % MOBA_BLOCK == 0 and x_sample.shape[1] == 1
    cache_kt, cache_vt = _pages_t(cache_k), _pages_t(cache_v)
    rope_p = _rope_tables(jnp.arange(T, dtype=I32))
    rope_s = _rope_tables(jnp.full((DB,), past, I32))

    xp = x_prompt
    xs = x_sample.reshape(1, DB, D_MODEL)
    new_k_p, new_v_p, new_k_s, new_v_s, conv_p, conv_s = [], [], [], [], [], []
    logf_p = logf_s = kidx_p = kidx_s = None
    for i in range(depth):
        kind = i % 4
        rope = kind in (MOBA, DSA)
        w_all = _layer_weights(kind, w_qkv[i], fox_w_f, idx_w_q, idx_w_k, idx_w_w)
        wo_b = w_o[i].astype(BF16)
        wa_b, wb_b, wd_b = ffn_w_a[i].astype(BF16), ffn_w_b[i].astype(BF16), ffn_w_down[i].astype(BF16)

        outs = _proj(xp, norm_mix[i], w_all, kind, rope_p if rope else None, fox_b_f, True)
        qhm, k32, v32, khm, vhm = outs[:5]
        vt = outs[-1]
        new_k_p.append(k32.reshape(B, T, N_KV, HD))
        new_v_p.append(v32.reshape(B, T, N_KV, HD))
        if kind == STICK:
            o = _stick_prompt(qhm, khm, vhm)
        elif kind == FOX:
            logf_p = outs[5]
            o = _fox_prompt(qhm, khm, vt, _cumsum_time_parts(logf_p))
        elif kind == MOBA:
            o = _moba_prompt(qhm, outs[5], _block_mean(k32), khm, vt)
        else:
            qihm, kidx_p, wi = outs[5:8]
            o = _dsa_prompt(qhm, qihm, wi, kidx_p, khm, vt)
        x2 = _out_proj(xp.reshape(B * T, D_MODEL), o.reshape(B * T, D_MODEL), wo_b)
        x2, rows = _ffn(x2, norm_ffn[i], wa_b, wb_b, ffn_conv_w[i], ffn_conv_b[i], wd_b, T)
        conv_p.append(rows)
        xp = x2.reshape(B, T, D_MODEL)

        outs = _proj(xs, norm_mix[i], w_all, kind, rope_s if rope else None, fox_b_f, False)
        qhm, k32, v32 = outs[:3]
        new_k_s.append(k32.reshape(DB, 1, N_KV, HD))
        new_v_s.append(v32.reshape(DB, 1, N_KV, HD))
        qx = _expand_heads(qhm[0].transpose(1, 0, 2), BF16)
        knew, vnew = k32.reshape(DB, 1, KV_W), v32.reshape(DB, 1, KV_W)
        if kind == STICK:
            ox = _dec_stick(qx, cache_kt, cache_vt, page_table, i)
        elif kind == FOX:
            lf = outs[5][0]
            logf_s = lf.reshape(DB, 1, N_HEADS)
            ox = _dec_fox(qx, cache_kt, cache_vt, cache_logf.transpose(0, 2, 1), knew, vnew,
                          lf.reshape(DB, N_HEADS, 1), page_table, i)
        elif kind == MOBA:
            q32x = _expand_heads(outs[5][0].transpose(1, 0, 2), F32)
            ox = _dec_moba(qx, q32x, _dec_kmean(cache_kt, page_table, i), cache_kt, cache_vt, knew, vnew, page_table, i)
        else:
            qihm, ki, wi = outs[5:8]
            kidx_s = ki.reshape(DB, 1, IDX_DIM)
            scores = _dec_scores(qihm[0].transpose(1, 0, 2), wi[0].reshape(DB, IDX_HEADS, 1),
                                 cache_kidx.transpose(0, 2, 1), kidx_s, page_table)
            ox = _dec_dsa(qx, cache_kt, cache_vt, _dec_select(scores, past + 1), knew, vnew, page_table, i)
        s2 = _out_proj(xs.reshape(DB, D_MODEL), _collapse_heads(ox), wo_b)
        st = state_conv[i]
        s2, a_new = _ffn(s2, norm_ffn[i], wa_b, wb_b, ffn_conv_w[i], ffn_conv_b[i], wd_b, 1, prev=(st[:, 0], st[:, 1]))
        conv_s.append(jnp.stack([st[:, 1], a_new], axis=1))
        xs = s2.reshape(1, DB, D_MODEL)

    y_p = _final_norm(xp.reshape(B * T, D_MODEL), norm_final).reshape(B, T, D_MODEL)
    y_s = _final_norm(xs.reshape(DB, D_MODEL), norm_final).reshape(DB, 1, D_MODEL)
    return (y_p, y_s, jnp.stack(new_k_p), jnp.stack(new_v_p), jnp.stack(new_k_s), jnp.stack(new_v_s),
            logf_p, logf_s, kidx_p, kidx_s, jnp.stack(conv_p), jnp.stack(conv_s))
```

```python
import functools
import math

import jax
import jax.numpy as jnp
import numpy as np
from jax import lax
from jax.experimental import pallas as pl
from jax.experimental.pallas import tpu as pltpu

F32, BF16, I32 = jnp.float32, jnp.bfloat16, jnp.int32

D_MODEL = 1024
N_HEADS = 16
N_KV = 4
GROUP = N_HEADS // N_KV
HD = 64
KV_W = N_KV * HD
D_FF = 2816
CONV_W = 3
PAGE = 128
MOBA_BLOCK = 256
MOBA_TOPK = 3
IDX_HEADS = 8
IDX_DIM = 64
IDX_TOPK = 256
ROPE_THETA = 10000.0
RMS_EPS = 1e-6
STICK, FOX, MOBA, DSA = 0, 1, 2, 3
NEG = -1e30
INT_MIN = -2147483648
VMEM_LIMIT = 48 * 1024 * 1024
FF_CHUNK = 256
FFN_TM = 512
PROJ_TM = 256
KV_TILE = 256
DEC_PAGES = 8
Q_CHUNK = 128
FOX_AUG = 3 * GROUP


def _cp(*sem):
    return pltpu.CompilerParams(dimension_semantics=sem, vmem_limit_bytes=VMEM_LIMIT)


def _dot(a, b):
    return jnp.dot(a, b, preferred_element_type=F32)


def _dot_nt(a, b):
    return lax.dot_general(a, b, (((1,), (1,)), ((), ())), preferred_element_type=F32)


def _split2(x):
    hi = x.astype(BF16)
    lo = (x - hi.astype(F32)).astype(BF16)
    return hi, lo


def _split3(x):
    hi = x.astype(BF16)
    r = x - hi.astype(F32)
    mid = r.astype(BF16)
    lo = (r - mid.astype(F32)).astype(BF16)
    return hi, mid, lo


def _dot_f32(a, b):
    ah, al = _split2(a)
    bh, bl = _split2(b)
    return _dot(ah, bh) + _dot(al, bh) + _dot(ah, bl)


def _dot_nt_f32(a, b):
    ah, al = _split2(a)
    bh, bl = _split2(b)
    return _dot_nt(ah, bh) + _dot_nt(al, bh) + _dot_nt(ah, bl)


def _log_sigmoid_pair(z):
    l1p = jnp.log(1.0 + jnp.exp(-jnp.abs(z)))
    return jnp.minimum(z, 0.0) - l1p, jnp.minimum(-z, 0.0) - l1p


def _iota(shape, dim):
    return lax.broadcasted_iota(I32, shape, dim)


def _tri(n, kind):
    r, c = _iota((n, n), 0), _iota((n, n), 1)
    m = {"gt": r > c, "lt": r < c, "ge": r >= c}[kind]
    return jnp.where(m, 1.0, 0.0).astype(BF16)


def _rms_bf16(x, g):
    y = x * lax.rsqrt(jnp.mean(x * x, axis=-1, keepdims=True) + RMS_EPS)
    return (y * g).astype(BF16)


def _rope128(xc, cos, sin):
    lane = _iota(xc.shape, 1)
    first = (lane % HD) < (HD // 2)
    rolled = jnp.where(first, pltpu.roll(xc, 128 - HD // 2, 1), pltpu.roll(xc, HD // 2, 1))
    return xc * cos + rolled * sin


def _halves(xc):
    return xc[:, :HD], pltpu.roll(xc, HD, 1)[:, :HD]


def _place(parts):
    r, c = _iota((HD, KV_W), 0), _iota((HD, KV_W), 1)
    out = None
    for j, pj in enumerate(parts):
        e = jnp.where(c == r + HD * j, 1.0, 0.0).astype(BF16)
        t = _dot(pj, e)
        out = t if out is None else out + t
    return out


def _stack_heads(q_ref):
    return jnp.concatenate([q_ref[0, j] for j in range(GROUP)], axis=0)


def _init_softmax(m_ref, l_ref, acc_ref):
    m_ref[...] = jnp.full(m_ref.shape, NEG, F32)
    l_ref[...] = jnp.zeros_like(l_ref)
    acc_ref[...] = jnp.zeros_like(acc_ref)


def _online_t(s, valid, vt, m_ref, l_ref, acc_ref, idx):
    m_all, l_all, acc_all = m_ref[idx], l_ref[idx], acc_ref[idx]
    m_out, l_out, acc_out = [], [], []
    for c0 in range(0, s.shape[1], Q_CHUNK):
        cols = slice(c0, c0 + Q_CHUNK)
        sc = s[:, cols]
        if valid is not None:
            vc = valid[:, cols]
            sc = jnp.where(vc, sc, NEG)
        m_old = m_all[:, cols]
        m_new = jnp.maximum(m_old, jnp.max(sc, axis=0, keepdims=True))
        alpha = jnp.exp(m_old - m_new)
        p = jnp.exp(sc - m_new)
        if valid is not None:
            p = jnp.where(vc, p, 0.0)
        l_out.append(alpha * l_all[:, cols] + jnp.sum(p, axis=0, keepdims=True))
        acc_out.append(alpha * acc_all[:, cols] + _dot(vt, p.astype(BF16)))
        m_out.append(m_new)
    m_ref[idx] = jnp.concatenate(m_out, axis=1)
    l_ref[idx] = jnp.concatenate(l_out, axis=1)
    acc_ref[idx] = jnp.concatenate(acc_out, axis=1)


def _group_out_t(acc_t, l, tq):
    r = acc_t if l is None else acc_t / l
    o_t = jnp.concatenate([r[:, j * tq:(j + 1) * tq] for j in range(GROUP)], axis=0)
    return o_t.T.astype(BF16)


def _topk_rank_select_t(gate_t, n_valid, k, nb):
    row = _iota(gate_t.shape, 0)
    gm = jnp.where(row < n_valid, gate_t, -jnp.inf)
    sel = jnp.zeros(gate_t.shape, F32)
    for n in range(nb):
        gn = gm[n:n + 1, :]
        beats = jnp.where(gm > gn, 1.0, jnp.where((gm == gn) & (row < n), 1.0, 0.0))
        rank = jnp.sum(beats, axis=0, keepdims=True)
        ok = jnp.where((rank < k) & (n < n_valid), 1.0, 0.0)
        sel = jnp.where(row == n, ok, sel)
    return sel


def _topk_rank_select(gate, n_valid, k, nb):
    lane = _iota(gate.shape, 1)
    gm = jnp.where(lane < n_valid, gate, -jnp.inf)
    sel = jnp.zeros(gate.shape, F32)
    for n in range(nb):
        gn = gm[:, n:n + 1]
        beats = jnp.where(gm > gn, 1.0, jnp.where((gm == gn) & (lane < n), 1.0, 0.0))
        rank = jnp.sum(beats, axis=1, keepdims=True)
        ok = jnp.where((rank < k) & (n < n_valid), 1.0, 0.0)
        sel = jnp.where(lane == n, ok, sel)
    return sel


def _order_key(score):
    bits = lax.bitcast_convert_type(score, I32)
    return jnp.where(bits < 0, bits ^ jnp.int32(0x7FFFFFFF), bits)


def _kth_largest_key(count_ge, shape, k):
    def body(it, v):
        cand = v + lax.shift_left(jnp.int32(1), 31 - it)
        return jnp.where(count_ge(cand) >= k, cand, v)

    return lax.fori_loop(0, 32, body, jnp.full(shape, INT_MIN, I32))


def _proj_kernel(*refs, kind, rope, prompt):
    it = iter(refs)
    x_ref, g_ref, w_ref = next(it), next(it), next(it)
    bf_ref = next(it) if kind == FOX else None
    cos_ref, sin_ref = (next(it), next(it)) if rope else (None, None)
    qhm_ref, k32_ref, v32_ref, khm_ref, vhm_ref = next(it), next(it), next(it), next(it), next(it)
    h = _rms_bf16(x_ref[0], g_ref[...])
    y = _dot(h, w_ref[...])
    cos = cos_ref[...] if rope else None
    sin = sin_ref[...] if rope else None

    def chunk(c, rot):
        xc = y[:, 128 * c:128 * (c + 1)]
        return _rope128(xc, cos, sin) if rot else xc

    q32hm_ref = next(it) if kind == MOBA else None
    for c in range(N_HEADS // 2):
        qc = chunk(c, rope)
        lo, hi = _halves(qc)
        qhm_ref[0, 2 * c] = (lo * 0.125).astype(BF16)
        qhm_ref[0, 2 * c + 1] = (hi * 0.125).astype(BF16)
        if kind == MOBA:
            q32hm_ref[0, 2 * c] = lo
            q32hm_ref[0, 2 * c + 1] = hi
    base = D_MODEL // 128
    for c in range(N_KV // 2):
        kc = chunk(base + c, rope)
        k32_ref[0, :, 128 * c:128 * (c + 1)] = kc
        vc = chunk(base + N_KV // 2 + c, False)
        v32_ref[0, :, 128 * c:128 * (c + 1)] = vc
        klo, khi = _halves(kc)
        khm_ref[0, 2 * c] = klo.astype(BF16)
        khm_ref[0, 2 * c + 1] = khi.astype(BF16)
        vlo, vhi = _halves(vc)
        vhm_ref[0, 2 * c] = vlo.astype(BF16)
        vhm_ref[0, 2 * c + 1] = vhi.astype(BF16)
    base = (D_MODEL + 2 * KV_W) // 128
    if kind == FOX:
        logf_ref = next(it)
        zf = chunk(base, False)[:, :N_HEADS] + bf_ref[...]
        logf_ref[0] = _log_sigmoid_pair(zf)[0]
    if kind == DSA:
        qihm_ref, ki_ref, wi_ref = next(it), next(it), next(it)
        for c in range(IDX_HEADS // 2):
            lo, hi = _halves(chunk(base + c, True))
            qihm_ref[0, 2 * c] = lo
            qihm_ref[0, 2 * c + 1] = hi
        kic = chunk(base + IDX_HEADS // 2, True)
        ki_ref[0] = kic[:, :IDX_DIM]
        wi_ref[0] = chunk(base + IDX_HEADS // 2 + 1, False)[:, :IDX_HEADS] * (IDX_HEADS ** -0.5 * IDX_DIM ** -0.5)
        if prompt:
            qi3_ref, ki3_ref = next(it), next(it)
            low = _iota(kic.shape, 1) < IDX_DIM
            for c in range(IDX_HEADS // 2):
                xc = chunk(base + c, True)
                for j, z in enumerate((jnp.where(low, xc, 0.0), jnp.where(low, pltpu.roll(xc, IDX_DIM, 1), 0.0))):
                    hi = z.astype(BF16).astype(F32)
                    qi3_ref[0, 2 * c + j] = jnp.concatenate([hi + pltpu.roll(z - hi, IDX_DIM, 1), hi], axis=1).astype(BF16)
            hi = kic.astype(BF16).astype(F32)
            ki3_ref[0] = jnp.concatenate([hi + pltpu.roll(hi, IDX_DIM, 1), kic - hi], axis=1).astype(BF16)
    if prompt:
        vt_ref = next(it)
        vt_ref[0, 0] = y[:, D_MODEL + KV_W:D_MODEL + 2 * KV_W].T.astype(BF16)


def _proj(x, g, w_all, kind, rope_tabs, fox_b, prompt):
    B, T, _ = x.shape
    tm = min(PROJ_TM, T)
    n_cols = w_all.shape[1]
    rope = rope_tabs is not None
    row = lambda b, i: (b, i, 0)
    hm = lambda b, i: (b, 0, i, 0)
    in_specs = [pl.BlockSpec((1, tm, D_MODEL), row), pl.BlockSpec((1, D_MODEL), lambda b, i: (0, 0)),
                pl.BlockSpec((D_MODEL, n_cols), lambda b, i: (0, 0))]
    args = [x, g.reshape(1, D_MODEL), w_all]
    if kind == FOX:
        in_specs.append(pl.BlockSpec((1, N_HEADS), lambda b, i: (0, 0)))
        args.append(fox_b.reshape(1, N_HEADS))
    if rope:
        in_specs += [pl.BlockSpec((tm, 128), lambda b, i: (i, 0))] * 2
        args += list(rope_tabs)
    out_shape = [jax.ShapeDtypeStruct((B, N_HEADS, T, HD), BF16), jax.ShapeDtypeStruct((B, T, KV_W), F32),
                 jax.ShapeDtypeStruct((B, T, KV_W), F32), jax.ShapeDtypeStruct((B, N_KV, T, HD), BF16),
                 jax.ShapeDtypeStruct((B, N_KV, T, HD), BF16)]
    out_specs = [pl.BlockSpec((1, N_HEADS, tm, HD), hm), pl.BlockSpec((1, tm, KV_W), row),
                 pl.BlockSpec((1, tm, KV_W), row), pl.BlockSpec((1, N_KV, tm, HD), hm),
                 pl.BlockSpec((1, N_KV, tm, HD), hm)]
    if kind == MOBA:
        out_shape.append(jax.ShapeDtypeStruct((B, N_HEADS, T, HD), F32))
        out_specs.append(pl.BlockSpec((1, N_HEADS, tm, HD), hm))
    if kind == FOX:
        out_shape.append(jax.ShapeDtypeStruct((B, T, N_HEADS), F32))
        out_specs.append(pl.BlockSpec((1, tm, N_HEADS), row))
    if kind == DSA:
        out_shape += [jax.ShapeDtypeStruct((B, IDX_HEADS, T, IDX_DIM), F32),
                      jax.ShapeDtypeStruct((B, T, IDX_DIM), F32), jax.ShapeDtypeStruct((B, T, IDX_HEADS), F32)]
        out_specs += [pl.BlockSpec((1, IDX_HEADS, tm, IDX_DIM), hm), pl.BlockSpec((1, tm, IDX_DIM), row),
                      pl.BlockSpec((1, tm, IDX_HEADS), row)]
        if prompt:
            out_shape += [jax.ShapeDtypeStruct((B, IDX_HEADS, T, 4 * IDX_DIM), BF16),
                          jax.ShapeDtypeStruct((B, T, 4 * IDX_DIM), BF16)]
            out_specs += [pl.BlockSpec((1, IDX_HEADS, tm, 4 * IDX_DIM), hm), pl.BlockSpec((1, tm, 4 * IDX_DIM), row)]
    if prompt:
        assert tm == KV_TILE
        out_shape.append(jax.ShapeDtypeStruct((B, T // tm, KV_W, tm), BF16))
        out_specs.append(pl.BlockSpec((1, 1, KV_W, tm), lambda b, i: (b, i, 0, 0)))
    return pl.pallas_call(
        functools.partial(_proj_kernel, kind=kind, rope=rope, prompt=prompt),
        grid=(B, T // tm), in_specs=in_specs, out_specs=out_specs, out_shape=out_shape,
        compiler_params=_cp("parallel", "arbitrary"), name=f"proj_{kind}_{'p' if prompt else 's'}")(*args)


def _out_proj_kernel(x_ref, o_ref, w_ref, y_ref):
    y_ref[...] = x_ref[...] + _dot(o_ref[...], w_ref[...])


def _out_proj(x2, o2, w_o):
    M = x2.shape[0]
    tm = min(512, M)
    return pl.pallas_call(
        _out_proj_kernel, grid=(M // tm,),
        in_specs=[pl.BlockSpec((tm, D_MODEL), lambda i: (i, 0)), pl.BlockSpec((tm, D_MODEL), lambda i: (i, 0)),
                  pl.BlockSpec((D_MODEL, D_MODEL), lambda i: (0, 0))],
        out_specs=pl.BlockSpec((tm, D_MODEL), lambda i: (i, 0)),
        out_shape=jax.ShapeDtypeStruct((M, D_MODEL), F32), compiler_params=_cp("parallel"), name="out_proj")(x2, o2, w_o)


def _gelu_tanh(x):
    return 0.5 * x * (1.0 + jnp.tanh(0.7978845608028654 * (x + 0.044715 * x * x * x)))


def _ffn_kernel(*refs, prompt, tiles_per_seq):
    if prompt:
        x_ref, g_ref, wa_ref, wb_ref, cw_ref, cb_ref, wd_ref, y_ref, conv_ref, u_ref, carry_ref = refs
    else:
        x_ref, g_ref, wa_ref, wb_ref, cw_ref, cb_ref, wd_ref, p2_ref, p1_ref, y_ref, a_ref, u_ref = refs
    x = x_ref[...]
    h = _rms_bf16(x, g_ref[...])
    if prompt:
        @pl.when(pl.program_id(0) % tiles_per_seq == 0)
        def _():
            carry_ref[...] = jnp.zeros_like(carry_ref)

        row = _iota((x.shape[0], FF_CHUNK), 0)
    for c in range(D_FF // FF_CHUNK):
        cols = slice(c * FF_CHUNK, (c + 1) * FF_CHUNK)
        a = _dot(h, wa_ref[:, cols])
        gate = _dot(h, wb_ref[:, cols])
        if prompt:
            prev2, prev1 = carry_ref[0:1, cols], carry_ref[1:2, cols]
            back2 = pltpu.roll(a, 2, 0)
            a_m1 = jnp.where(row == 0, prev1, pltpu.roll(a, 1, 0))
            a_m2 = jnp.where(row == 0, prev2, jnp.where(row == 1, prev1, back2))
            carry_ref[0:2, cols] = back2[0:2, :]
            conv_ref[0, :, cols] = back2[0:2, :]
        else:
            a_m2, a_m1 = p2_ref[:, cols], p1_ref[:, cols]
            a_ref[:, cols] = a
        conv = cb_ref[:, cols] + cw_ref[0:1, cols] * a_m2 + cw_ref[1:2, cols] * a_m1 + cw_ref[2:3, cols] * a
        u_ref[:, cols] = (_gelu_tanh(conv) * gate).astype(BF16)
    y_ref[...] = x + _dot(u_ref[...], wd_ref[...])


def _ffn(x2, g, w_a, w_b, conv_w, conv_b, w_down, seq_len, prev=None):
    M = x2.shape[0]
    prompt = prev is None
    tm = min(FFN_TM, seq_len) if prompt else M
    tps = seq_len // tm if prompt else 1
    xs = pl.BlockSpec((tm, D_MODEL), lambda i: (i, 0))
    whole = lambda shape: pl.BlockSpec(shape, lambda i: (0,) * len(shape), pipeline_mode=pl.Buffered(1))
    in_specs = [xs, whole((1, D_MODEL)), whole((D_MODEL, D_FF)), whole((D_MODEL, D_FF)), whole((CONV_W, D_FF)),
                whole((1, D_FF)), whole((D_FF, D_MODEL))]
    args = [x2, g.reshape(1, D_MODEL), w_a, w_b, conv_w, conv_b.reshape(1, D_FF), w_down]
    scratch = [pltpu.VMEM((tm, D_FF), BF16)]
    if prompt:
        out_shape = [jax.ShapeDtypeStruct((M, D_MODEL), F32), jax.ShapeDtypeStruct((M // tm, CONV_W - 1, D_FF), F32)]
        out_specs = [xs, pl.BlockSpec((1, CONV_W - 1, D_FF), lambda i: (i, 0, 0))]
        scratch.append(pltpu.VMEM((8, D_FF), F32))
    else:
        ps = pl.BlockSpec((tm, D_FF), lambda i: (i, 0))
        in_specs += [ps, ps]
        args += list(prev)
        out_shape = [jax.ShapeDtypeStruct((M, D_MODEL), F32), jax.ShapeDtypeStruct((M, D_FF), F32)]
        out_specs = [xs, ps]
    y, extra = pl.pallas_call(
        functools.partial(_ffn_kernel, prompt=prompt, tiles_per_seq=tps), grid=(M // tm,),
        in_specs=in_specs, out_specs=out_specs, out_shape=out_shape, scratch_shapes=scratch,
        compiler_params=_cp("arbitrary"), name="ffn_p" if prompt else "ffn_s")(*args)
    return y, (extra[tps - 1::tps] if prompt else extra)


def _final_norm_kernel(x_ref, g_ref, y_ref):
    x = x_ref[...]
    y_ref[...] = x * lax.rsqrt(jnp.mean(x * x, axis=-1, keepdims=True) + RMS_EPS) * g_ref[...]


def _final_norm(x2, g):
    M = x2.shape[0]
    tm = min(1024, M)
    return pl.pallas_call(
        _final_norm_kernel, grid=(M // tm,),
        in_specs=[pl.BlockSpec((tm, D_MODEL), lambda i: (i, 0)), pl.BlockSpec((1, D_MODEL), lambda i: (0, 0))],
        out_specs=pl.BlockSpec((tm, D_MODEL), lambda i: (i, 0)),
        out_shape=jax.ShapeDtypeStruct((M, D_MODEL), F32), compiler_params=_cp("parallel"), name="final_norm")(x2, g.reshape(1, D_MODEL))


def _stick_kernel(q_ref, k_ref, vt_ref, o_ref, r_ref, acc_ref, *, tq, tk):
    q0 = pl.program_id(2) * tq
    qs = _stack_heads(q_ref)
    m = GROUP * tq
    later = _tri(tk, "lt")
    qpos = q0 + _iota((1, m), 1) % tq
    r_ref[...] = jnp.zeros_like(r_ref)
    acc_ref[...] = jnp.zeros_like(acc_ref)

    def scores(j):
        return _dot_nt(k_ref[0, 0, pl.ds(pl.multiple_of(j * tk, tk), tk), :], qs)

    def tile(j, z, masked):
        vt = vt_ref[0, j]
        kpos = j * tk + _iota((tk, 1), 0)
        r_all, acc_all = r_ref[...], acc_ref[...]
        chunks = [slice(c0, c0 + Q_CHUNK) for c0 in range(0, m, Q_CHUNK)]
        staged, r_out, acc_out = [], [], []
        for cols in chunks:
            zc = z[:, cols]
            ls = jnp.minimum(zc, 0.0) - jnp.log(1.0 + jnp.exp(-jnp.abs(zc)))
            lk = ls - zc
            past = kpos < qpos[:, cols] if masked else None
            if masked:
                lk = jnp.where(past, lk, 0.0)
            parts = _dot(later, jnp.concatenate(_split2(lk), axis=1))
            staged.append((ls, parts, past))
            r_out.append(r_all[:, cols] + jnp.sum(lk, axis=0, keepdims=True))
        for cols, (ls, parts, past) in zip(chunks, staged):
            w = jnp.exp(ls + parts[:, :Q_CHUNK] + parts[:, Q_CHUNK:] + r_all[:, cols])
            if masked:
                w = jnp.where(past, w, 0.0)
            acc_out.append(acc_all[:, cols] + _dot(vt, w.astype(BF16)))
        r_ref[...] = jnp.concatenate(r_out, axis=1)
        acc_ref[...] = jnp.concatenate(acc_out, axis=1)

    jd = q0 // tk
    tile(jd, scores(jd), True)

    def body(n, carry):
        tile(jd - 1 - n, scores(jd - 1 - n), False)
        return carry

    lax.fori_loop(0, jd, body, 0)
    o_ref[0] = _group_out_t(acc_ref[...], None, tq)


def _stick_prompt(qhm, khm, vt):
    B, _, T, _ = qhm.shape
    tq, tk = 256, KV_TILE
    m = GROUP * tq
    return pl.pallas_call(
        functools.partial(_stick_kernel, tq=tq, tk=tk), grid=(B, N_KV, T // tq),
        in_specs=[pl.BlockSpec((1, GROUP, tq, HD), lambda b, g, i: (b, g, i, 0)),
                  pl.BlockSpec((1, 1, T, HD), lambda b, g, i: (b, g, 0, 0)),
                  pl.BlockSpec((1, T // tk, HD, tk), lambda b, g, i: (b, 0, g, 0))],
        out_specs=pl.BlockSpec((1, tq, KV_W), lambda b, g, i: (b, i, g)),
        out_shape=jax.ShapeDtypeStruct((B, T, D_MODEL), BF16),
        scratch_shapes=[pltpu.VMEM((1, m), F32), pltpu.VMEM((HD, m), F32)],
        compiler_params=_cp("parallel", "parallel", "arbitrary"), name="stick_p")(qhm, khm, vt)


def _cumsum_kernel(x_ref, hi_ref, mid_ref, lo_ref, carry_ref):
    @pl.when(pl.program_id(1) == 0)
    def _():
        carry_ref[...] = jnp.zeros_like(carry_ref)

    n = x_ref.shape[1]
    low = _tri(n, "ge")
    hi, mid, lo = _split3(x_ref[0])
    c = _dot(low, hi) + _dot(low, mid) + _dot(low, lo) + carry_ref[...]
    carry_ref[...] = c[n - 1:n, :]
    hi_ref[0], mid_ref[0], lo_ref[0] = _split3(c)


def _cumsum_time_parts(x):
    B, T, C = x.shape
    n = 256
    spec = pl.BlockSpec((1, n, C), lambda b, i: (b, i, 0))
    return pl.pallas_call(
        _cumsum_kernel, grid=(B, T // n), in_specs=[spec], out_specs=[spec] * 3,
        out_shape=[jax.ShapeDtypeStruct((B, T, C), BF16)] * 3, scratch_shapes=[pltpu.VMEM((1, C), F32)],
        compiler_params=_cp("parallel", "arbitrary"), name="cumsum_time")(x)


def _softmax_scratch_t(n, m):
    return [pltpu.VMEM((n, 1, m), F32), pltpu.VMEM((n, 1, m), F32), pltpu.VMEM((n, HD, m), F32)]


def _causal_kernel(q_ref, k_ref, vt_ref, o_ref, m_ref, l_ref, acc_ref, *, tq, tk):
    q0 = pl.program_id(2) * tq
    qs = _stack_heads(q_ref)
    m = GROUP * tq
    _init_softmax(m_ref, l_ref, acc_ref)
    qpos = q0 + _iota((1, m), 1) % tq

    def scores(j):
        return _dot_nt(k_ref[0, 0, pl.ds(pl.multiple_of(j * tk, tk), tk), :], qs)

    jd = q0 // tk
    _online_t(scores(jd), (jd * tk + _iota((tk, 1), 0)) <= qpos, vt_ref[0, jd], m_ref, l_ref, acc_ref, 0)

    def body(n, carry):
        _online_t(scores(n), None, vt_ref[0, n], m_ref, l_ref, acc_ref, 0)
        return carry

    lax.fori_loop(0, jd, body, 0)
    o_ref[0] = _group_out_t(acc_ref[0], l_ref[0], tq)


def _fox_prompt(qhm, khm, vt, cum_parts):
    B, _, T, _ = qhm.shape
    tq, tk = 256, KV_TILE
    d_aug = 2 * HD
    c = jnp.stack(cum_parts, axis=-1).reshape(B, T, N_KV, FOX_AUG).transpose(0, 2, 1, 3)
    k_aug = jnp.concatenate([khm, c, jnp.zeros((B, N_KV, T, d_aug - HD - FOX_AUG), BF16)], axis=-1)
    pat = np.zeros((N_HEADS, d_aug - HD), np.float32)
    for h in range(N_HEADS):
        pat[h, 3 * (h % GROUP):3 * (h % GROUP) + 3] = -1.0
    q_aug = jnp.concatenate([qhm, jnp.broadcast_to(jnp.asarray(pat, BF16)[None, :, None, :], (B, N_HEADS, T, d_aug - HD))], axis=-1)
    m = GROUP * tq
    return pl.pallas_call(
        functools.partial(_causal_kernel, tq=tq, tk=tk), grid=(B, N_KV, T // tq),
        in_specs=[pl.BlockSpec((1, GROUP, tq, d_aug), lambda b, g, i: (b, g, i, 0)),
                  pl.BlockSpec((1, 1, T, d_aug), lambda b, g, i: (b, g, 0, 0)),
                  pl.BlockSpec((1, T // tk, HD, tk), lambda b, g, i: (b, 0, g, 0))],
        out_specs=pl.BlockSpec((1, tq, KV_W), lambda b, g, i: (b, i, g)),
        out_shape=jax.ShapeDtypeStruct((B, T, D_MODEL), BF16),
        scratch_shapes=_softmax_scratch_t(1, m),
        compiler_params=_cp("parallel", "parallel", "arbitrary"), name="fox_p")(q_aug, k_aug, vt)


def _block_mean_kernel(k_ref, o_ref):
    k = k_ref[0]
    nb = k.shape[0] // MOBA_BLOCK
    o_ref[0] = jnp.sum(k.reshape(nb, MOBA_BLOCK, KV_W), axis=1) * (1.0 / MOBA_BLOCK)


def _block_mean(k32):
    B, T, _ = k32.shape
    nb = T // MOBA_BLOCK
    return pl.pallas_call(
        _block_mean_kernel, grid=(B,), in_specs=[pl.BlockSpec((1, T, KV_W), lambda b: (b, 0, 0))],
        out_specs=pl.BlockSpec((1, nb, KV_W), lambda b: (b, 0, 0)),
        out_shape=jax.ShapeDtypeStruct((B, nb, KV_W), F32), compiler_params=_cp("parallel"), name="block_mean")(k32)


def _moba_kernel(q_ref, q32_ref, km_ref, k_ref, vt_ref, o_ref, sel_ref, m_ref, l_ref, acc_ref, *, tq, nb):
    tk = MOBA_BLOCK
    q0 = pl.program_id(2) * tq
    own = q0 // tk
    qs = _stack_heads(q_ref)
    m = GROUP * tq
    _init_softmax(m_ref, l_ref, acc_ref)
    sel_ref[...] = _topk_rank_select_t(_dot_nt_f32(km_ref[0, 0], _stack_heads(q32_ref)), own, MOBA_TOPK, nb)
    qpos = q0 + _iota((1, m), 1) % tq

    def scores(j):
        return _dot_nt(k_ref[0, 0, pl.ds(pl.multiple_of(j * tk, tk), tk), :], qs)

    _online_t(scores(own), (own * tk + _iota((tk, 1), 0)) <= qpos, vt_ref[0, own], m_ref, l_ref, acc_ref, 0)

    def body(j, carry):
        bias = jnp.where(sel_ref[pl.ds(j, 1), :] > 0.5, 0.0, NEG)
        _online_t(scores(j) + bias, None, vt_ref[0, j], m_ref, l_ref, acc_ref, 0)
        return carry

    lax.fori_loop(0, own, body, 0)
    o_ref[0] = _group_out_t(acc_ref[0], l_ref[0], tq)


def _moba_prompt(qhm, q32hm, kmean, khm, vt):
    B, _, T, _ = qhm.shape
    tq = 256
    nb = T // MOBA_BLOCK
    m = GROUP * tq
    km = kmean.reshape(B, nb, N_KV, HD).transpose(0, 2, 1, 3)
    qspec = pl.BlockSpec((1, GROUP, tq, HD), lambda b, g, i: (b, g, i, 0))
    return pl.pallas_call(
        functools.partial(_moba_kernel, tq=tq, nb=nb), grid=(B, N_KV, T // tq),
        in_specs=[qspec, qspec, pl.BlockSpec((1, 1, nb, HD), lambda b, g, i: (b, g, 0, 0)),
                  pl.BlockSpec((1, 1, T, HD), lambda b, g, i: (b, g, 0, 0)),
                  pl.BlockSpec((1, nb, HD, MOBA_BLOCK), lambda b, g, i: (b, 0, g, 0))],
        out_specs=pl.BlockSpec((1, tq, KV_W), lambda b, g, i: (b, i, g)),
        out_shape=jax.ShapeDtypeStruct((B, T, D_MODEL), BF16),
        scratch_shapes=[pltpu.VMEM((nb, m), F32)] + _softmax_scratch_t(1, m),
        compiler_params=_cp("parallel", "parallel", "arbitrary"), name="moba_p")(qhm, q32hm, km, khm, vt)


def _dsa_kernel(q_ref, qi_ref, wi_ref, ki_ref, k_ref, vt_ref, o_ref, key_ref, m_ref, l_ref, acc_ref, *, tq, n_keep):
    tk = KV_TILE
    q0 = pl.program_id(1) * tq
    n_proc = (q0 + tq + tk - 1) // tk
    qpos = q0 + _iota((1, tq), 1)

    def visible(c):
        return (c * tk + _iota((tk, 1), 0)) <= qpos

    qi3 = jnp.concatenate([qi_ref[0, i] for i in range(IDX_HEADS)], axis=0)

    def score_tile(c, carry):
        ks = pl.multiple_of(c * tk, tk)
        dots = _dot_nt(ki_ref[0, pl.ds(ks, tk), :], qi3)
        sc = jnp.zeros((tk, tq), F32)
        for i in range(IDX_HEADS):
            sc = sc + wi_ref[0, i:i + 1, :] * jnp.maximum(dots[:, i * tq:(i + 1) * tq], 0.0)
        key_ref[c] = _order_key(jnp.where(visible(c), sc + 0.0, -jnp.inf))
        return carry

    lax.fori_loop(0, n_proc, score_tile, 0)

    def count(pred_of_tile):
        def body(c, part):
            return part + jnp.where(pred_of_tile(key_ref[c]), 1.0, 0.0)

        return jnp.sum(lax.fori_loop(0, n_proc, body, jnp.zeros((tk, tq), F32)), axis=0, keepdims=True)

    thr = _kth_largest_key(lambda cand: count(lambda k: k >= cand), (1, tq), n_keep)
    need = n_keep - count(lambda k: k > thr)
    _init_softmax(m_ref, l_ref, acc_ref)
    before = _tri(tk, "gt")
    qs = [jnp.concatenate([q_ref[0, GROUP * g + j] for j in range(GROUP)], axis=0) for g in range(N_KV)]

    def attend(c, ties_seen):
        ks = pl.multiple_of(c * tk, tk)
        key = key_ref[c]
        tied = jnp.where(key == thr, 1.0, 0.0)
        rank = _dot(before, tied.astype(BF16)) + ties_seen
        keep = (key > thr) | ((key == thr) & (rank < need))
        keep = jnp.where(keep & visible(c), 1.0, 0.0)
        valid = jnp.concatenate([keep] * GROUP, axis=1) > 0.5
        for g in range(N_KV):
            s = _dot_nt(k_ref[0, g, pl.ds(ks, tk), :], qs[g])
            _online_t(s, valid, vt_ref[0, c, HD * g:HD * (g + 1), :], m_ref, l_ref, acc_ref, g)
        return ties_seen + jnp.sum(tied, axis=0, keepdims=True)

    lax.fori_loop(0, n_proc, attend, jnp.zeros((1, tq), F32))
    for g in range(N_KV):
        o_ref[0, :, KV_W * g:KV_W * (g + 1)] = _group_out_t(acc_ref[g], l_ref[g], tq)


def _dsa_prompt(qhm, qi3, wi, ki3, khm, vt):
    B, _, T, _ = qhm.shape
    tq = 128
    m = GROUP * tq
    return pl.pallas_call(
        functools.partial(_dsa_kernel, tq=tq, n_keep=min(IDX_TOPK, T // 4)), grid=(B, T // tq),
        in_specs=[pl.BlockSpec((1, N_HEADS, tq, HD), lambda b, i: (b, 0, i, 0)),
                  pl.BlockSpec((1, IDX_HEADS, tq, 4 * IDX_DIM), lambda b, i: (b, 0, i, 0)),
                  pl.BlockSpec((1, IDX_HEADS, tq), lambda b, i: (b, 0, i)),
                  pl.BlockSpec((1, T, 4 * IDX_DIM), lambda b, i: (b, 0, 0)),
                  pl.BlockSpec((1, N_KV, T, HD), lambda b, i: (b, 0, 0, 0)),
                  pl.BlockSpec((1, T // KV_TILE, KV_W, KV_TILE), lambda b, i: (b, 0, 0, 0))],
        out_specs=pl.BlockSpec((1, tq, D_MODEL), lambda b, i: (b, i, 0)),
        out_shape=jax.ShapeDtypeStruct((B, T, D_MODEL), BF16),
        scratch_shapes=[pltpu.VMEM((T // KV_TILE, KV_TILE, tq), I32)] + _softmax_scratch_t(N_KV, m),
        compiler_params=_cp("parallel", "arbitrary"), name="dsa_p")(qhm, qi3, wi.transpose(0, 2, 1), ki3, khm, vt)


def _page_specs(layer, n_pages, nps, reverse):
    def spec(i):
        def idx(b, s, pt):
            p = s * nps + i
            return (layer, pt[b, n_pages - 1 - p if reverse else p], 0, 0)

        return pl.BlockSpec((None, None, KV_W, PAGE), idx)

    return [spec(i) for i in range(nps)]


def _pool_specs(rows, n_pages, nps, reverse):
    def spec(i):
        def idx(b, s, pt):
            p = s * nps + i
            return (pt[b, n_pages - 1 - p if reverse else p], 0, 0)

        return pl.BlockSpec((None, rows, PAGE), idx)

    return [spec(i) for i in range(nps)]


def _row_spec(width):
    return pl.BlockSpec((1, 1, width), lambda b, s, pt: (b, 0, 0))


def _head_spec(width):
    return pl.BlockSpec((1, N_HEADS, width), lambda b, s, pt: (b, 0, 0))


def _softmax_scratch():
    return [pltpu.VMEM((N_HEADS, 1), F32), pltpu.VMEM((N_HEADS, 1), F32), pltpu.VMEM((N_HEADS, KV_W), F32)]


def _new_key_start(q, knew, vnew, m_ref, l_ref, acc_ref):
    z = jnp.sum(q.astype(F32) * knew.astype(BF16).astype(F32), axis=1, keepdims=True)
    m_ref[...] = z
    l_ref[...] = jnp.ones_like(z)
    acc_ref[...] = jnp.broadcast_to(vnew.astype(BF16).astype(F32), acc_ref.shape)


def _cat_pages(refs):
    return jnp.concatenate([r[...].astype(BF16) for r in refs], axis=1)


def _later_sums(x, parts):
    n = x.shape[1] // PAGE
    stack = jnp.concatenate([p[:, PAGE * i:PAGE * (i + 1)] for p in parts(x) for i in range(n)], axis=0)
    out = _dot(stack, _tri(PAGE, "gt"))
    rows = x.shape[0]
    tot = None
    for k in range(out.shape[0] // (n * rows)):
        part = jnp.concatenate([out[(k * n + i) * rows:(k * n + i + 1) * rows] for i in range(n)], axis=1)
        tot = part if tot is None else tot + part
    return tot


def _page_totals(x):
    return [jnp.sum(x[:, PAGE * i:PAGE * (i + 1)], axis=1, keepdims=True) for i in range(x.shape[1] // PAGE)]


def _per_page(cols):
    return jnp.concatenate([jnp.broadcast_to(c, (c.shape[0], PAGE)) for c in cols], axis=1)


def _online_pages(s, valid, vcat, m_ref, l_ref, acc_ref):
    if valid is not None:
        s = jnp.where(valid, s, NEG)
    m_old = m_ref[...]
    m_new = jnp.maximum(m_old, jnp.max(s, axis=1, keepdims=True))
    alpha = jnp.exp(m_old - m_new)
    p = jnp.exp(s - m_new)
    if valid is not None:
        p = jnp.where(valid, p, 0.0)
    l_ref[...] = alpha * l_ref[...] + jnp.sum(p, axis=1, keepdims=True)
    acc_ref[...] = alpha * acc_ref[...] + _dot_nt(p.astype(BF16), vcat)
    m_ref[...] = m_new


def _dec_stick_kernel(pt_ref, q_ref, *refs, nps):
    k_refs, v_refs = refs[:nps], refs[nps:2 * nps]
    o_ref, r_ref, acc_ref = refs[2 * nps:]
    step = pl.program_id(1)

    @pl.when(step == 0)
    def _():
        r_ref[...] = jnp.zeros_like(r_ref)
        acc_ref[...] = jnp.zeros_like(acc_ref)

    ls, lk = _log_sigmoid_pair(_dot(q_ref[0], _cat_pages(k_refs)))
    after, run = [], r_ref[...]
    for tot in _page_totals(lk):
        after.append(run)
        run = run + tot
    w = jnp.exp(ls + _later_sums(lk, _split2) + _per_page(after))
    acc = acc_ref[...] + _dot_nt(w.astype(BF16), _cat_pages(v_refs))
    r_ref[...], acc_ref[...] = run, acc

    @pl.when(step == pl.num_programs(1) - 1)
    def _():
        o_ref[0] = acc


def _dec_call(kernel, name, grid_spec, DB):
    return pl.pallas_call(kernel, grid_spec=grid_spec, out_shape=jax.ShapeDtypeStruct((DB, N_HEADS, KV_W), F32),
                          compiler_params=_cp("parallel", "arbitrary"), name=name)


def _dec_stick(qx, kt, vt, page_table, layer):
    DB, n_pages = page_table.shape
    nps = math.gcd(n_pages, DEC_PAGES)
    pages = _page_specs(layer, n_pages, nps, True)
    grid_spec = pltpu.PrefetchScalarGridSpec(
        num_scalar_prefetch=1, grid=(DB, n_pages // nps), in_specs=[_head_spec(KV_W)] + pages + pages,
        out_specs=_head_spec(KV_W), scratch_shapes=[pltpu.VMEM((N_HEADS, 1), F32), pltpu.VMEM((N_HEADS, KV_W), F32)])
    return _dec_call(functools.partial(_dec_stick_kernel, nps=nps), "stick_s", grid_spec, DB)(
        page_table, qx, *([kt] * nps), *([vt] * nps))


def _dec_fox_kernel(pt_ref, q_ref, *refs, nps):
    k_refs, v_refs, lf_refs = refs[:nps], refs[nps:2 * nps], refs[2 * nps:3 * nps]
    kn_ref, vn_ref, lfn_ref, o_ref, s_ref, m_ref, l_ref, acc_ref = refs[3 * nps:]
    step = pl.program_id(1)

    @pl.when(step == 0)
    def _():
        s_ref[...] = lfn_ref[0]
        _new_key_start(q_ref[0], kn_ref[0], vn_ref[0], m_ref, l_ref, acc_ref)

    lf = jnp.concatenate([r[...] for r in lf_refs], axis=1)
    after, run = [], s_ref[...]
    for tot in _page_totals(lf):
        after.append(run)
        run = run + tot
    s_ref[...] = run
    s = _dot(q_ref[0], _cat_pages(k_refs)) + _later_sums(lf, _split3) + _per_page(after)
    _online_pages(s, None, _cat_pages(v_refs), m_ref, l_ref, acc_ref)

    @pl.when(step == pl.num_programs(1) - 1)
    def _():
        o_ref[0] = acc_ref[...] / l_ref[...]


def _dec_fox(qx, kt, vt, logf_t, knew, vnew, lfnew, page_table, layer):
    DB, n_pages = page_table.shape
    nps = math.gcd(n_pages, DEC_PAGES)
    pages = _page_specs(layer, n_pages, nps, True)
    grid_spec = pltpu.PrefetchScalarGridSpec(
        num_scalar_prefetch=1, grid=(DB, n_pages // nps),
        in_specs=[_head_spec(KV_W)] + pages + pages + _pool_specs(N_HEADS, n_pages, nps, True)
        + [_row_spec(KV_W), _row_spec(KV_W), _head_spec(1)],
        out_specs=_head_spec(KV_W), scratch_shapes=[pltpu.VMEM((N_HEADS, 1), F32)] + _softmax_scratch())
    return _dec_call(functools.partial(_dec_fox_kernel, nps=nps), "fox_s", grid_spec, DB)(
        page_table, qx, *([kt] * nps), *([vt] * nps), *([logf_t] * nps), knew, vnew, lfnew)


def _dec_kmean_kernel(pt_ref, *refs, nps):
    k_refs, o_ref = refs[:nps], refs[nps]
    step = pl.program_id(1)

    @pl.when(step == 0)
    def _():
        o_ref[...] = jnp.zeros_like(o_ref)

    ones = jnp.ones((8, PAGE), BF16)
    for i in range(nps):
        hi, mid, lo = _split3(k_refs[i][...])
        tot = _dot_nt(ones, hi) + _dot_nt(ones, mid) + _dot_nt(ones, lo)
        blk = (step * nps + i) // (MOBA_BLOCK // PAGE)
        o_ref[0, pl.ds(blk, 1), :] = o_ref[0, pl.ds(blk, 1), :] + tot[0:1, :] * (1.0 / MOBA_BLOCK)


def _dec_kmean(kt, page_table, layer):
    DB, n_pages = page_table.shape
    nps = math.gcd(n_pages, DEC_PAGES)
    nb = n_pages * PAGE // MOBA_BLOCK
    grid_spec = pltpu.PrefetchScalarGridSpec(
        num_scalar_prefetch=1, grid=(DB, n_pages // nps), in_specs=_page_specs(layer, n_pages, nps, False),
        out_specs=pl.BlockSpec((1, nb, KV_W), lambda b, s, pt: (b, 0, 0)))
    return pl.pallas_call(
        functools.partial(_dec_kmean_kernel, nps=nps), grid_spec=grid_spec,
        out_shape=jax.ShapeDtypeStruct((DB, nb, KV_W), F32),
        compiler_params=_cp("parallel", "arbitrary"), name="kmean_s")(page_table, *([kt] * nps))


def _dec_moba_kernel(pt_ref, q_ref, q32_ref, km_ref, *refs, nps, nb):
    k_refs, v_refs = refs[:nps], refs[nps:2 * nps]
    kn_ref, vn_ref, o_ref, sel_ref, m_ref, l_ref, acc_ref = refs[2 * nps:]
    step = pl.program_id(1)

    @pl.when(step == 0)
    def _():
        sel_ref[...] = _topk_rank_select(_dot_nt_f32(q32_ref[0], km_ref[0]), nb, MOBA_TOPK, nb)
        _new_key_start(q_ref[0], kn_ref[0], vn_ref[0], m_ref, l_ref, acc_ref)

    lane = _iota((N_HEADS, nb), 1)
    sel = sel_ref[...]
    bias = []
    for i in range(nps):
        blk = (step * nps + i) // (MOBA_BLOCK // PAGE)
        picked = jnp.sum(jnp.where(lane == blk, sel, 0.0), axis=1, keepdims=True) > 0.5
        bias.append(jnp.where(picked, 0.0, NEG))
    s = _dot(q_ref[0], _cat_pages(k_refs)) + _per_page(bias)
    _online_pages(s, None, _cat_pages(v_refs), m_ref, l_ref, acc_ref)

    @pl.when(step == pl.num_programs(1) - 1)
    def _():
        o_ref[0] = acc_ref[...] / l_ref[...]


def _dec_moba(qx, q32x, kmean, kt, vt, knew, vnew, page_table, layer):
    DB, n_pages = page_table.shape
    nps = math.gcd(n_pages, DEC_PAGES)
    nb = kmean.shape[1]
    pages = _page_specs(layer, n_pages, nps, False)
    grid_spec = pltpu.PrefetchScalarGridSpec(
        num_scalar_prefetch=1, grid=(DB, n_pages // nps),
        in_specs=[_head_spec(KV_W), _head_spec(KV_W), pl.BlockSpec((1, nb, KV_W), lambda b, s, pt: (b, 0, 0))]
        + pages + pages + [_row_spec(KV_W), _row_spec(KV_W)],
        out_specs=_head_spec(KV_W), scratch_shapes=[pltpu.VMEM((N_HEADS, nb), F32)] + _softmax_scratch())
    return _dec_call(functools.partial(_dec_moba_kernel, nps=nps, nb=nb), "moba_s", grid_spec, DB)(
        page_table, qx, q32x, kmean, *([kt] * nps), *([vt] * nps), knew, vnew)


def _dec_score_kernel(pt_ref, qi_ref, wi_ref, *refs, nps, n_pages):
    kidx_refs = refs[:nps]
    kin_ref, o_ref = refs[nps:]
    step = pl.program_id(1)
    qi, wi = qi_ref[0], wi_ref[0]

    @pl.when(step == 0)
    def _():
        o_ref[...] = jnp.full(o_ref.shape, -jnp.inf, F32)
        dn = jnp.sum(qi * kin_ref[0], axis=1, keepdims=True)
        sn = jnp.sum(wi * jnp.maximum(dn, 0.0), axis=0, keepdims=True) + 0.0
        o_ref[0, n_pages:n_pages + 1, :] = jnp.where(_iota((1, PAGE), 1) == 0, sn, -jnp.inf)

    for i in range(nps):
        dots = _dot_f32(qi, kidx_refs[i][...])
        o_ref[0, pl.ds(step * nps + i, 1), :] = jnp.sum(wi * jnp.maximum(dots, 0.0), axis=0, keepdims=True) + 0.0


def _dec_scores(qi3, wi3, kidx_t, kinew, page_table):
    DB, n_pages = page_table.shape
    assert n_pages < PAGE
    nps = math.gcd(n_pages, DEC_PAGES)
    grid_spec = pltpu.PrefetchScalarGridSpec(
        num_scalar_prefetch=1, grid=(DB, n_pages // nps),
        in_specs=[pl.BlockSpec((1, IDX_HEADS, IDX_DIM), lambda b, s, pt: (b, 0, 0)),
                  pl.BlockSpec((1, IDX_HEADS, 1), lambda b, s, pt: (b, 0, 0))]
        + _pool_specs(IDX_DIM, n_pages, nps, False) + [_row_spec(IDX_DIM)],
        out_specs=pl.BlockSpec((1, PAGE, PAGE), lambda b, s, pt: (b, 0, 0)))
    return pl.pallas_call(
        functools.partial(_dec_score_kernel, nps=nps, n_pages=n_pages), grid_spec=grid_spec,
        out_shape=jax.ShapeDtypeStruct((DB, PAGE, PAGE), F32),
        compiler_params=_cp("parallel", "arbitrary"), name="dsa_score_s")(page_table, qi3, wi3, *([kidx_t] * nps), kinew)


def _dec_select_kernel(sc_ref, o_ref, *, n_keys, n_keep):
    key = _order_key(sc_ref[0])

    def total(x):
        return jnp.sum(jnp.sum(x, axis=1, keepdims=True), axis=0, keepdims=True)

    thr = _kth_largest_key(lambda cand: total(jnp.where(key >= cand, 1.0, 0.0)), (1, 1), n_keep)
    need = n_keep - total(jnp.where(key > thr, 1.0, 0.0))
    tied = jnp.where(key == thr, 1.0, 0.0)
    in_row = _dot(tied.astype(BF16), _tri(PAGE, "lt"))
    row_tot = jnp.broadcast_to(jnp.sum(tied, axis=1, keepdims=True), tied.shape).astype(BF16)
    rank = in_row + _dot(_tri(PAGE, "gt"), row_tot)
    pos = _iota(key.shape, 0) * PAGE + _iota(key.shape, 1)
    keep = ((key > thr) | ((key == thr) & (rank < need))) & (pos < n_keys)
    o_ref[0] = jnp.where(keep, 1.0, 0.0)


def _dec_select(scores, n_keys):
    DB = scores.shape[0]
    spec = pl.BlockSpec((1, PAGE, PAGE), lambda b: (b, 0, 0))
    return pl.pallas_call(
        functools.partial(_dec_select_kernel, n_keys=n_keys, n_keep=min(IDX_TOPK, n_keys // 4)), grid=(DB,),
        in_specs=[spec], out_specs=spec, out_shape=jax.ShapeDtypeStruct(scores.shape, F32),
        compiler_params=_cp("parallel"), name="dsa_select_s")(scores)


def _dec_dsa_kernel(pt_ref, q_ref, *refs, nps, n_pages):
    k_refs, v_refs = refs[:nps], refs[nps:2 * nps]
    sel_ref, kn_ref, vn_ref, o_ref, m_ref, l_ref, acc_ref = refs[2 * nps:]
    step = pl.program_id(1)
    q = q_ref[0]

    @pl.when(step == 0)
    def _():
        zn = jnp.sum(q.astype(F32) * kn_ref[0].astype(BF16).astype(F32), axis=1, keepdims=True)
        keep = jnp.broadcast_to(sel_ref[0, n_pages:n_pages + 1, 0:1] > 0.5, zn.shape)
        m_ref[...] = jnp.where(keep, zn, NEG)
        l_ref[...] = jnp.where(keep, 1.0, 0.0)
        acc_ref[...] = jnp.where(keep, jnp.broadcast_to(vn_ref[0].astype(BF16).astype(F32), acc_ref.shape), 0.0)

    keep = jnp.concatenate([sel_ref[0, pl.ds(step * nps + i, 1), :] for i in range(nps)], axis=1)
    s = _dot(q, _cat_pages(k_refs))
    _online_pages(s, jnp.broadcast_to(keep > 0.5, s.shape), _cat_pages(v_refs), m_ref, l_ref, acc_ref)

    @pl.when(step == pl.num_programs(1) - 1)
    def _():
        o_ref[0] = acc_ref[...] / l_ref[...]


def _dec_dsa(qx, kt, vt, sel, knew, vnew, page_table, layer):
    DB, n_pages = page_table.shape
    nps = math.gcd(n_pages, DEC_PAGES)
    pages = _page_specs(layer, n_pages, nps, False)
    grid_spec = pltpu.PrefetchScalarGridSpec(
        num_scalar_prefetch=1, grid=(DB, n_pages // nps),
        in_specs=[_head_spec(KV_W)] + pages + pages
        + [pl.BlockSpec((1, PAGE, PAGE), lambda b, s, pt: (b, 0, 0)), _row_spec(KV_W), _row_spec(KV_W)],
        out_specs=_head_spec(KV_W), scratch_shapes=_softmax_scratch())
    return _dec_call(functools.partial(_dec_dsa_kernel, nps=nps, n_pages=n_pages), "dsa_s", grid_spec, DB)(
        page_table, qx, *([kt] * nps), *([vt] * nps), sel, knew, vnew)


def _rope_tables(pos):
    half = HD // 2
    inv_freq = jnp.power(ROPE_THETA, -jnp.arange(half, dtype=F32) * (2.0 / HD))
    ang = pos.astype(F32)[:, None] * inv_freq[None, :]
    cos, sin = jnp.cos(ang), jnp.sin(ang)
    return jnp.concatenate([cos] * 4, axis=1), jnp.concatenate([-sin, sin] * 2, axis=1)


def _pad_cols(w, mult=128):
    pad = (-w.shape[1]) % mult
    return jnp.pad(w, ((0, 0), (0, pad))) if pad else w


def _layer_weights(kind, w_qkv, fox_w_f, idx_w_q, idx_w_k, idx_w_w):
    cols = [w_qkv]
    if kind == FOX:
        cols.append(_pad_cols(fox_w_f))
    if kind == DSA:
        cols += [idx_w_q, _pad_cols(idx_w_k), _pad_cols(idx_w_w)]
    return jnp.concatenate(cols, axis=1).astype(BF16)


_HEAD_ONEHOT = np.arange(N_HEADS)[:, None] // GROUP == np.arange(N_KV)[None, :]


def _expand_heads(q, dtype):
    m = jnp.asarray(_HEAD_ONEHOT, q.dtype)
    return (q[:, :, None, :] * m[None, :, :, None]).reshape(q.shape[0], N_HEADS, KV_W).astype(dtype)


def _collapse_heads(ox):
    m = jnp.asarray(_HEAD_ONEHOT, ox.dtype)
    o = jnp.sum(ox.reshape(ox.shape[0], N_HEADS, N_KV, HD) * m[None, :, :, None], axis=2)
    return o.reshape(ox.shape[0], D_MODEL).astype(BF16)


def _pages_t(cache):
    depth, n_pool = cache.shape[:2]
    return cache.transpose(0, 1, 3, 4, 2).reshape(depth, n_pool, KV_W, PAGE)


def kernel(x_prompt, x_sample, cache_k, cache_v, cache_logf, cache_kidx, state_conv, page_table, norm_mix, norm_ffn, norm_final, w_qkv, w_o, fox_w_f, fox_b_f, idx_w_q, idx_w_k, idx_w_w, ffn_w_a, ffn_w_b, ffn_conv_w, ffn_conv_b, ffn_w_down):
    B, T, _ = x_prompt.shape
    DB = x_sample.shape[0]
    depth = cache_k.shape[0]
    n_pages = page_table.shape[1]
    past = n_pages * PAGE
    assert T % MOBA_BLOCK == 0 and past % MOBA_BLOCK == 0 and x_sample.shape[1] == 1
    cache_kt, cache_vt = _pages_t(cache_k), _pages_t(cache_v)
    rope_p = _rope_tables(jnp.arange(T, dtype=I32))
    rope_s = _rope_tables(jnp.full((DB,), past, I32))

    xp = x_prompt
    xs = x_sample.reshape(1, DB, D_MODEL)
    new_k_p, new_v_p, new_k_s, new_v_s, conv_p, conv_s = [], [], [], [], [], []
    logf_p = logf_s = kidx_p = kidx_s = None
    for i in range(depth):
        kind = i % 4
        rope = kind in (MOBA, DSA)
        w_all = _layer_weights(kind, w_qkv[i], fox_w_f, idx_w_q, idx_w_k, idx_w_w)
        wo_b = w_o[i].astype(BF16)
        wa_b, wb_b, wd_b = ffn_w_a[i].astype(BF16), ffn_w_b[i].astype(BF16), ffn_w_down[i].astype(BF16)

        outs = _proj(xp, norm_mix[i], w_all, kind, rope_p if rope else None, fox_b_f, True)
        qhm, k32, v32, khm, vhm = outs[:5]
        vt = outs[-1]
        new_k_p.append(k32.reshape(B, T, N_KV, HD))
        new_v_p.append(v32.reshape(B, T, N_KV, HD))
        if kind == STICK:
            o = _stick_prompt(qhm, khm, vt)
        elif kind == FOX:
            logf_p = outs[5]
            o = _fox_prompt(qhm, khm, vt, _cumsum_time_parts(logf_p))
        elif kind == MOBA:
            o = _moba_prompt(qhm, outs[5], _block_mean(k32), khm, vt)
        else:
            kidx_p, wi, qi3, ki3 = outs[6:10]
            o = _dsa_prompt(qhm, qi3, wi, ki3, khm, vt)
        x2 = _out_proj(xp.reshape(B * T, D_MODEL), o.reshape(B * T, D_MODEL), wo_b)
        x2, rows = _ffn(x2, norm_ffn[i], wa_b, wb_b, ffn_conv_w[i], ffn_conv_b[i], wd_b, T)
        conv_p.append(rows)
        xp = x2.reshape(B, T, D_MODEL)

        outs = _proj(xs, norm_mix[i], w_all, kind, rope_s if rope else None, fox_b_f, False)
        qhm, k32, v32 = outs[:3]
        new_k_s.append(k32.reshape(DB, 1, N_KV, HD))
        new_v_s.append(v32.reshape(DB, 1, N_KV, HD))
        qx = _expand_heads(qhm[0].transpose(1, 0, 2), BF16)
        knew, vnew = k32.reshape(DB, 1, KV_W), v32.reshape(DB, 1, KV_W)
        if kind == STICK:
            ox = _dec_stick(qx, cache_kt, cache_vt, page_table, i)
        elif kind == FOX:
            lf = outs[5][0]
            logf_s = lf.reshape(DB, 1, N_HEADS)
            ox = _dec_fox(qx, cache_kt, cache_vt, cache_logf.transpose(0, 2, 1), knew, vnew,
                          lf.reshape(DB, N_HEADS, 1), page_table, i)
        elif kind == MOBA:
            q32x = _expand_heads(outs[5][0].transpose(1, 0, 2), F32)
            ox = _dec_moba(qx, q32x, _dec_kmean(cache_kt, page_table, i), cache_kt, cache_vt, knew, vnew, page_table, i)
        else:
            qihm, ki, wi = outs[5:8]
            kidx_s = ki.reshape(DB, 1, IDX_DIM)
            scores = _dec_scores(qihm[0].transpose(1, 0, 2), wi[0].reshape(DB, IDX_HEADS, 1),
                                 cache_kidx.transpose(0, 2, 1), kidx_s, page_table)
            ox = _dec_dsa(qx, cache_kt, cache_vt, _dec_select(scores, past + 1), knew, vnew, page_table, i)
        s2 = _out_proj(xs.reshape(DB, D_MODEL), _collapse_heads(ox), wo_b)
        st = state_conv[i]
        s2, a_new = _ffn(s2, norm_ffn[i], wa_b, wb_b, ffn_conv_w[i], ffn_conv_b[i], wd_b, 1, prev=(st[:, 0], st[:, 1]))
        conv_s.append(jnp.stack([st[:, 1], a_new], axis=1))
        xs = s2.reshape(1, DB, D_MODEL)

    y_p = _final_norm(xp.reshape(B * T, D_MODEL), norm_final).reshape(B, T, D_MODEL)
    y_s = _final_norm(xs.reshape(DB, D_MODEL), norm_final).reshape(DB, 1, D_MODEL)
    return (y_p, y_s, jnp.stack(new_k_p), jnp.stack(new_v_p), jnp.stack(new_k_s), jnp.stack(new_v_s),
            logf_p, logf_s, kidx_p, kidx_s, jnp.stack(conv_p), jnp.stack(conv_s))
```

```python
import functools
import math

import jax
import jax.numpy as jnp
import numpy as np
from jax import lax
from jax.experimental import pallas as pl
from jax.experimental.pallas import tpu as pltpu

F32, BF16, I32 = jnp.float32, jnp.bfloat16, jnp.int32

D_MODEL = 1024
N_HEADS = 16
N_KV = 4
GROUP = N_HEADS // N_KV
HD = 64
KV_W = N_KV * HD
D_FF = 2816
CONV_W = 3
PAGE = 128
MOBA_BLOCK = 256
MOBA_TOPK = 3
IDX_HEADS = 8
IDX_DIM = 64
IDX_TOPK = 256
ROPE_THETA = 10000.0
RMS_EPS = 1e-6
STICK, FOX, MOBA, DSA = 0, 1, 2, 3
NEG = -1e30
M_FLOOR = -1e29
INT_MIN = -2147483648
VMEM_LIMIT = 48 * 1024 * 1024
FF_CHUNK = 256
FFN_TM = 512
PROJ_TM = 256
KV_TILE = 256
DEC_PAGES = 16
KV_PER_STEP = 4
Q_CHUNK = 128
FOX_AUG = 3 * GROUP


def _cp(*sem):
    return pltpu.CompilerParams(dimension_semantics=sem, vmem_limit_bytes=VMEM_LIMIT)


def _dot(a, b):
    return jnp.dot(a, b, preferred_element_type=F32)


def _dot_nt(a, b):
    return lax.dot_general(a, b, (((1,), (1,)), ((), ())), preferred_element_type=F32)


def _split2(x):
    hi = x.astype(BF16)
    lo = (x - hi.astype(F32)).astype(BF16)
    return hi, lo


def _split3(x):
    hi = x.astype(BF16)
    r = x - hi.astype(F32)
    mid = r.astype(BF16)
    lo = (r - mid.astype(F32)).astype(BF16)
    return hi, mid, lo


def _dot_f32(a, b):
    ah, al = _split2(a)
    bh, bl = _split2(b)
    return _dot(ah, bh) + _dot(al, bh) + _dot(ah, bl)


def _dot_nt_f32(a, b):
    ah, al = _split2(a)
    bh, bl = _split2(b)
    return _dot_nt(ah, bh) + _dot_nt(al, bh) + _dot_nt(ah, bl)


def _log_sigmoid_pair(z):
    l1p = jnp.log(1.0 + jnp.exp(-jnp.abs(z)))
    return jnp.minimum(z, 0.0) - l1p, jnp.minimum(-z, 0.0) - l1p


def _iota(shape, dim):
    return lax.broadcasted_iota(I32, shape, dim)


def _tri(n, kind):
    r, c = _iota((n, n), 0), _iota((n, n), 1)
    m = {"gt": r > c, "lt": r < c, "ge": r >= c}[kind]
    return jnp.where(m, 1.0, 0.0).astype(BF16)


def _rms_bf16(x, g):
    y = x * lax.rsqrt(jnp.mean(x * x, axis=-1, keepdims=True) + RMS_EPS)
    return (y * g).astype(BF16)


def _rope128(xc, cos, sin):
    lane = _iota(xc.shape, 1)
    first = (lane % HD) < (HD // 2)
    rolled = jnp.where(first, pltpu.roll(xc, 128 - HD // 2, 1), pltpu.roll(xc, HD // 2, 1))
    return xc * cos + rolled * sin


def _halves(xc):
    return xc[:, :HD], pltpu.roll(xc, HD, 1)[:, :HD]


def _place(parts):
    r, c = _iota((HD, KV_W), 0), _iota((HD, KV_W), 1)
    out = None
    for j, pj in enumerate(parts):
        e = jnp.where(c == r + HD * j, 1.0, 0.0).astype(BF16)
        t = _dot(pj, e)
        out = t if out is None else out + t
    return out


def _stack_heads(q_ref):
    return jnp.concatenate([q_ref[0, j] for j in range(GROUP)], axis=0)


def _init_softmax(m_ref, l_ref, acc_ref):
    m_ref[...] = jnp.full(m_ref.shape, M_FLOOR, F32)
    l_ref[...] = jnp.zeros_like(l_ref)
    acc_ref[...] = jnp.zeros_like(acc_ref)


def _online_t(s, valid, vt, m_ref, l_ref, acc_ref, idx, bias=None):
    m_all, l_all, acc_all = m_ref[idx], l_ref[idx], acc_ref[idx]
    m_out, l_out, acc_out = [], [], []
    for c0 in range(0, s.shape[1], Q_CHUNK):
        cols = slice(c0, c0 + Q_CHUNK)
        sc = s[:, cols]
        if bias is not None:
            sc = sc + bias[:, cols]
        if valid is not None:
            vc = valid[:, cols]
            sc = jnp.where(vc, sc, NEG)
        m_old = m_all[:, cols]
        m_new = jnp.maximum(m_old, jnp.max(sc, axis=0, keepdims=True))
        alpha = jnp.exp(m_old - m_new)
        p = jnp.exp(sc - m_new)
        if valid is not None:
            p = jnp.where(vc, p, 0.0)
        l_out.append(alpha * l_all[:, cols] + jnp.sum(p, axis=0, keepdims=True))
        acc_out.append(alpha * acc_all[:, cols] + _dot(vt, p.astype(BF16)))
        m_out.append(m_new)
    m_ref[idx] = jnp.concatenate(m_out, axis=1)
    l_ref[idx] = jnp.concatenate(l_out, axis=1)
    acc_ref[idx] = jnp.concatenate(acc_out, axis=1)


def _group_out_t(acc_t, l, tq):
    r = acc_t if l is None else acc_t / l
    o_t = jnp.concatenate([r[:, j * tq:(j + 1) * tq] for j in range(GROUP)], axis=0)
    return o_t.T.astype(BF16)


def _topk_rank_select_t(gate_t, n_valid, k, nb):
    row = _iota(gate_t.shape, 0)
    gm = jnp.where(row < n_valid, gate_t, -jnp.inf)
    sel = jnp.zeros(gate_t.shape, F32)
    for n in range(nb):
        gn = gm[n:n + 1, :]
        beats = jnp.where(gm > gn, 1.0, jnp.where((gm == gn) & (row < n), 1.0, 0.0))
        rank = jnp.sum(beats, axis=0, keepdims=True)
        ok = jnp.where((rank < k) & (n < n_valid), 1.0, 0.0)
        sel = jnp.where(row == n, ok, sel)
    return sel


def _topk_rank_select(gate, n_valid, k, nb):
    lane = _iota(gate.shape, 1)
    gm = jnp.where(lane < n_valid, gate, -jnp.inf)
    sel = jnp.zeros(gate.shape, F32)
    for n in range(nb):
        gn = gm[:, n:n + 1]
        beats = jnp.where(gm > gn, 1.0, jnp.where((gm == gn) & (lane < n), 1.0, 0.0))
        rank = jnp.sum(beats, axis=1, keepdims=True)
        ok = jnp.where((rank < k) & (n < n_valid), 1.0, 0.0)
        sel = jnp.where(lane == n, ok, sel)
    return sel


def _order_key(score):
    bits = lax.bitcast_convert_type(score, I32)
    return jnp.where(bits < 0, bits ^ jnp.int32(0x7FFFFFFF), bits)


def _kth_largest_key(count_ge, shape, k):
    def body(it, v):
        cand = v + lax.shift_left(jnp.int32(1), 31 - it)
        return jnp.where(count_ge(cand) >= k, cand, v)

    return lax.fori_loop(0, 32, body, jnp.full(shape, INT_MIN, I32))


def _proj_kernel(*refs, kind, rope, prompt):
    it = iter(refs)
    x_ref, g_ref, w_ref = next(it), next(it), next(it)
    bf_ref = next(it) if kind == FOX else None
    cos_ref, sin_ref = (next(it), next(it)) if rope else (None, None)
    qhm_ref, k32_ref, v32_ref, khm_ref, vhm_ref = next(it), next(it), next(it), next(it), next(it)
    h = _rms_bf16(x_ref[0], g_ref[...])
    y = _dot(h, w_ref[...])
    cos = cos_ref[...] if rope else None
    sin = sin_ref[...] if rope else None

    def chunk(c, rot):
        xc = y[:, 128 * c:128 * (c + 1)]
        return _rope128(xc, cos, sin) if rot else xc

    q32hm_ref = next(it) if kind == MOBA else None
    for c in range(N_HEADS // 2):
        qc = chunk(c, rope)
        lo, hi = _halves(qc)
        qhm_ref[0, 2 * c] = (lo * 0.125).astype(BF16)
        qhm_ref[0, 2 * c + 1] = (hi * 0.125).astype(BF16)
        if kind == MOBA:
            q32hm_ref[0, 2 * c] = lo
            q32hm_ref[0, 2 * c + 1] = hi
    base = D_MODEL // 128
    for c in range(N_KV // 2):
        kc = chunk(base + c, rope)
        k32_ref[0, :, 128 * c:128 * (c + 1)] = kc
        vc = chunk(base + N_KV // 2 + c, False)
        v32_ref[0, :, 128 * c:128 * (c + 1)] = vc
        klo, khi = _halves(kc)
        khm_ref[0, 2 * c] = klo.astype(BF16)
        khm_ref[0, 2 * c + 1] = khi.astype(BF16)
        vlo, vhi = _halves(vc)
        vhm_ref[0, 2 * c] = vlo.astype(BF16)
        vhm_ref[0, 2 * c + 1] = vhi.astype(BF16)
    base = (D_MODEL + 2 * KV_W) // 128
    if kind == FOX:
        logf_ref = next(it)
        zf = chunk(base, False)[:, :N_HEADS] + bf_ref[...]
        logf_ref[0] = _log_sigmoid_pair(zf)[0]
    if kind == DSA:
        qihm_ref, ki_ref, wi_ref = next(it), next(it), next(it)
        for c in range(IDX_HEADS // 2):
            lo, hi = _halves(chunk(base + c, True))
            qihm_ref[0, 2 * c] = lo
            qihm_ref[0, 2 * c + 1] = hi
        kic = chunk(base + IDX_HEADS // 2, True)
        ki_ref[0] = kic[:, :IDX_DIM]
        wi_ref[0] = chunk(base + IDX_HEADS // 2 + 1, False)[:, :IDX_HEADS] * (IDX_HEADS ** -0.5 * IDX_DIM ** -0.5)
        if prompt:
            qi3_ref, ki3_ref = next(it), next(it)
            low = _iota(kic.shape, 1) < IDX_DIM
            for c in range(IDX_HEADS // 2):
                xc = chunk(base + c, True)
                for j, z in enumerate((jnp.where(low, xc, 0.0), jnp.where(low, pltpu.roll(xc, IDX_DIM, 1), 0.0))):
                    hi = z.astype(BF16).astype(F32)
                    qi3_ref[0, 2 * c + j] = jnp.concatenate([hi + pltpu.roll(z - hi, IDX_DIM, 1), hi], axis=1).astype(BF16)
            hi = kic.astype(BF16).astype(F32)
            ki3_ref[0] = jnp.concatenate([hi + pltpu.roll(hi, IDX_DIM, 1), kic - hi], axis=1).astype(BF16)
    if prompt:
        vt_ref = next(it)
        vt_ref[0, 0] = y[:, D_MODEL + KV_W:D_MODEL + 2 * KV_W].T.astype(BF16)


def _proj(x, g, w_all, kind, rope_tabs, fox_b, prompt):
    B, T, _ = x.shape
    tm = min(PROJ_TM, T)
    n_cols = w_all.shape[1]
    rope = rope_tabs is not None
    row = lambda b, i: (b, i, 0)
    hm = lambda b, i: (b, 0, i, 0)
    in_specs = [pl.BlockSpec((1, tm, D_MODEL), row), pl.BlockSpec((1, D_MODEL), lambda b, i: (0, 0)),
                pl.BlockSpec((D_MODEL, n_cols), lambda b, i: (0, 0))]
    args = [x, g.reshape(1, D_MODEL), w_all]
    if kind == FOX:
        in_specs.append(pl.BlockSpec((1, N_HEADS), lambda b, i: (0, 0)))
        args.append(fox_b.reshape(1, N_HEADS))
    if rope:
        in_specs += [pl.BlockSpec((tm, 128), lambda b, i: (i, 0))] * 2
        args += list(rope_tabs)
    out_shape = [jax.ShapeDtypeStruct((B, N_HEADS, T, HD), BF16), jax.ShapeDtypeStruct((B, T, KV_W), F32),
                 jax.ShapeDtypeStruct((B, T, KV_W), F32), jax.ShapeDtypeStruct((B, N_KV, T, HD), BF16),
                 jax.ShapeDtypeStruct((B, N_KV, T, HD), BF16)]
    out_specs = [pl.BlockSpec((1, N_HEADS, tm, HD), hm), pl.BlockSpec((1, tm, KV_W), row),
                 pl.BlockSpec((1, tm, KV_W), row), pl.BlockSpec((1, N_KV, tm, HD), hm),
                 pl.BlockSpec((1, N_KV, tm, HD), hm)]
    if kind == MOBA:
        out_shape.append(jax.ShapeDtypeStruct((B, N_HEADS, T, HD), F32))
        out_specs.append(pl.BlockSpec((1, N_HEADS, tm, HD), hm))
    if kind == FOX:
        out_shape.append(jax.ShapeDtypeStruct((B, T, N_HEADS), F32))
        out_specs.append(pl.BlockSpec((1, tm, N_HEADS), row))
    if kind == DSA:
        out_shape += [jax.ShapeDtypeStruct((B, IDX_HEADS, T, IDX_DIM), F32),
                      jax.ShapeDtypeStruct((B, T, IDX_DIM), F32), jax.ShapeDtypeStruct((B, T, IDX_HEADS), F32)]
        out_specs += [pl.BlockSpec((1, IDX_HEADS, tm, IDX_DIM), hm), pl.BlockSpec((1, tm, IDX_DIM), row),
                      pl.BlockSpec((1, tm, IDX_HEADS), row)]
        if prompt:
            out_shape += [jax.ShapeDtypeStruct((B, IDX_HEADS, T, 4 * IDX_DIM), BF16),
                          jax.ShapeDtypeStruct((B, T, 4 * IDX_DIM), BF16)]
            out_specs += [pl.BlockSpec((1, IDX_HEADS, tm, 4 * IDX_DIM), hm), pl.BlockSpec((1, tm, 4 * IDX_DIM), row)]
    if prompt:
        assert tm == KV_TILE
        out_shape.append(jax.ShapeDtypeStruct((B, T // tm, KV_W, tm), BF16))
        out_specs.append(pl.BlockSpec((1, 1, KV_W, tm), lambda b, i: (b, i, 0, 0)))
    return pl.pallas_call(
        functools.partial(_proj_kernel, kind=kind, rope=rope, prompt=prompt),
        grid=(B, T // tm), in_specs=in_specs, out_specs=out_specs, out_shape=out_shape,
        compiler_params=_cp("parallel", "arbitrary"), name=f"proj_{kind}_{'p' if prompt else 's'}")(*args)


def _gelu_tanh(x):
    return 0.5 * x * (1.0 + jnp.tanh(0.7978845608028654 * (x + 0.044715 * x * x * x)))


def _ffn_kernel(*refs, prompt, tiles_per_seq):
    if prompt:
        x_ref, o_ref, wo_ref, g_ref, wa_ref, wb_ref, cw_ref, cb_ref, wd_ref, y_ref, conv_ref, u_ref, carry_ref = refs
    else:
        x_ref, o_ref, wo_ref, g_ref, wa_ref, wb_ref, cw_ref, cb_ref, wd_ref, p2_ref, p1_ref, y_ref, a_ref, u_ref = refs
    x = x_ref[...] + _dot(o_ref[...], wo_ref[...])
    h = _rms_bf16(x, g_ref[...])
    if prompt:
        @pl.when(pl.program_id(0) % tiles_per_seq == 0)
        def _():
            carry_ref[...] = jnp.zeros_like(carry_ref)

        row = _iota((x.shape[0], FF_CHUNK), 0)
    for c in range(D_FF // FF_CHUNK):
        cols = slice(c * FF_CHUNK, (c + 1) * FF_CHUNK)
        a = _dot(h, wa_ref[:, cols])
        gate = _dot(h, wb_ref[:, cols])
        if prompt:
            prev2, prev1 = carry_ref[0:1, cols], carry_ref[1:2, cols]
            back2 = pltpu.roll(a, 2, 0)
            a_m1 = jnp.where(row == 0, prev1, pltpu.roll(a, 1, 0))
            a_m2 = jnp.where(row == 0, prev2, jnp.where(row == 1, prev1, back2))
            carry_ref[0:2, cols] = back2[0:2, :]
            conv_ref[0, :, cols] = back2[0:2, :]
        else:
            a_m2, a_m1 = p2_ref[:, cols], p1_ref[:, cols]
            a_ref[:, cols] = a
        conv = cb_ref[:, cols] + cw_ref[0:1, cols] * a_m2 + cw_ref[1:2, cols] * a_m1 + cw_ref[2:3, cols] * a
        u_ref[:, cols] = (_gelu_tanh(conv) * gate).astype(BF16)
    y_ref[...] = x + _dot(u_ref[...], wd_ref[...])


def _mix_ffn(x2, o2, w_o, g, w_a, w_b, conv_w, conv_b, w_down, seq_len, prev=None):
    M = x2.shape[0]
    prompt = prev is None
    tm = min(FFN_TM, seq_len) if prompt else M
    tps = seq_len // tm if prompt else 1
    xs = pl.BlockSpec((tm, D_MODEL), lambda i: (i, 0))
    whole = lambda shape: pl.BlockSpec(shape, lambda i: (0,) * len(shape), pipeline_mode=pl.Buffered(1))
    in_specs = [xs, xs, whole((D_MODEL, D_MODEL)), whole((1, D_MODEL)), whole((D_MODEL, D_FF)), whole((D_MODEL, D_FF)),
                whole((CONV_W, D_FF)), whole((1, D_FF)), whole((D_FF, D_MODEL))]
    args = [x2, o2, w_o, g.reshape(1, D_MODEL), w_a, w_b, conv_w, conv_b.reshape(1, D_FF), w_down]
    scratch = [pltpu.VMEM((tm, D_FF), BF16)]
    if prompt:
        out_shape = [jax.ShapeDtypeStruct((M, D_MODEL), F32), jax.ShapeDtypeStruct((M // tm, CONV_W - 1, D_FF), F32)]
        out_specs = [xs, pl.BlockSpec((1, CONV_W - 1, D_FF), lambda i: (i, 0, 0))]
        scratch.append(pltpu.VMEM((8, D_FF), F32))
    else:
        ps = pl.BlockSpec((tm, D_FF), lambda i: (i, 0))
        in_specs += [ps, ps]
        args += list(prev)
        out_shape = [jax.ShapeDtypeStruct((M, D_MODEL), F32), jax.ShapeDtypeStruct((M, D_FF), F32)]
        out_specs = [xs, ps]
    y, extra = pl.pallas_call(
        functools.partial(_ffn_kernel, prompt=prompt, tiles_per_seq=tps), grid=(M // tm,),
        in_specs=in_specs, out_specs=out_specs, out_shape=out_shape, scratch_shapes=scratch,
        compiler_params=_cp("arbitrary"), name="ffn_p" if prompt else "ffn_s")(*args)
    return y, (extra[tps - 1::tps] if prompt else extra)


def _final_norm_kernel(x_ref, g_ref, y_ref):
    x = x_ref[...]
    y_ref[...] = x * lax.rsqrt(jnp.mean(x * x, axis=-1, keepdims=True) + RMS_EPS) * g_ref[...]


def _final_norm(x2, g):
    M = x2.shape[0]
    tm = min(1024, M)
    return pl.pallas_call(
        _final_norm_kernel, grid=(M // tm,),
        in_specs=[pl.BlockSpec((tm, D_MODEL), lambda i: (i, 0)), pl.BlockSpec((1, D_MODEL), lambda i: (0, 0))],
        out_specs=pl.BlockSpec((tm, D_MODEL), lambda i: (i, 0)),
        out_shape=jax.ShapeDtypeStruct((M, D_MODEL), F32), compiler_params=_cp("parallel"), name="final_norm")(x2, g.reshape(1, D_MODEL))


def _stick_kernel(q_ref, k_ref, vt_ref, o_ref, r_ref, acc_ref, *, tq, tk, nh):
    q0 = pl.program_id(2) * tq
    qs = [jnp.concatenate([q_ref[0, GROUP * h + j] for j in range(GROUP)], axis=0) for h in range(nh)]
    m = GROUP * tq
    later = _tri(tk, "lt")
    qpos = q0 + _iota((1, m), 1) % tq
    r_ref[...] = jnp.zeros_like(r_ref)
    acc_ref[...] = jnp.zeros_like(acc_ref)

    def tile(j, masked):
        ks = pl.multiple_of(j * tk, tk)
        zs = [_dot_nt(k_ref[0, h, pl.ds(ks, tk), :], qs[h]) for h in range(nh)]
        for h in range(nh):
            head_tile(j, h, zs[h], masked)

    def head_tile(j, h, z, masked):
        vt = vt_ref[0, j, HD * h:HD * (h + 1), :]
        kpos = j * tk + _iota((tk, 1), 0)
        r_all, acc_all = r_ref[h], acc_ref[h]
        chunks = [slice(c0, c0 + Q_CHUNK) for c0 in range(0, m, Q_CHUNK)]
        staged, r_out, acc_out = [], [], []
        for cols in chunks:
            zc = z[:, cols]
            ls = jnp.minimum(zc, 0.0) - jnp.log(1.0 + jnp.exp(-jnp.abs(zc)))
            lk = ls - zc
            past = kpos < qpos[:, cols] if masked else None
            if masked:
                lk = jnp.where(past, lk, 0.0)
            parts = _dot(later, jnp.concatenate(_split2(lk), axis=1))
            staged.append((ls, parts, past))
            r_out.append(r_all[:, cols] + jnp.sum(lk, axis=0, keepdims=True))
        for cols, (ls, parts, past) in zip(chunks, staged):
            w = jnp.exp(ls + parts[:, :Q_CHUNK] + parts[:, Q_CHUNK:] + r_all[:, cols])
            if masked:
                w = jnp.where(past, w, 0.0)
            acc_out.append(acc_all[:, cols] + _dot(vt, w.astype(BF16)))
        r_ref[h] = jnp.concatenate(r_out, axis=1)
        acc_ref[h] = jnp.concatenate(acc_out, axis=1)

    jd = q0 // tk
    tile(jd, True)

    def body(n, carry):
        tile(jd - 1 - n, False)
        return carry

    lax.fori_loop(0, jd, body, 0)
    for h in range(nh):
        o_ref[0, :, KV_W * h:KV_W * (h + 1)] = _group_out_t(acc_ref[h], None, tq)


def _stick_prompt(qhm, khm, vt):
    B, _, T, _ = qhm.shape
    tq, tk, nh = 256, KV_TILE, KV_PER_STEP
    m = GROUP * tq
    return pl.pallas_call(
        functools.partial(_stick_kernel, tq=tq, tk=tk, nh=nh), grid=(B, N_KV // nh, T // tq),
        in_specs=[pl.BlockSpec((1, nh * GROUP, tq, HD), lambda b, g, i: (b, g, i, 0)),
                  pl.BlockSpec((1, nh, T, HD), lambda b, g, i: (b, g, 0, 0)),
                  pl.BlockSpec((1, T // tk, nh * HD, tk), lambda b, g, i: (b, 0, g, 0))],
        out_specs=pl.BlockSpec((1, tq, nh * KV_W), lambda b, g, i: (b, i, g)),
        out_shape=jax.ShapeDtypeStruct((B, T, D_MODEL), BF16),
        scratch_shapes=[pltpu.VMEM((nh, 1, m), F32), pltpu.VMEM((nh, HD, m), F32)],
        compiler_params=_cp("parallel", "parallel", "arbitrary"), name="stick_p")(qhm, khm, vt)


def _cumsum_kernel(x_ref, hi_ref, mid_ref, lo_ref, carry_ref):
    @pl.when(pl.program_id(1) == 0)
    def _():
        carry_ref[...] = jnp.zeros_like(carry_ref)

    n = x_ref.shape[1]
    low = _tri(n, "ge")
    hi, mid, lo = _split3(x_ref[0])
    c = _dot(low, hi) + _dot(low, mid) + _dot(low, lo) + carry_ref[...]
    carry_ref[...] = c[n - 1:n, :]
    hi_ref[0], mid_ref[0], lo_ref[0] = _split3(c)


def _cumsum_time_parts(x):
    B, T, C = x.shape
    n = 256
    spec = pl.BlockSpec((1, n, C), lambda b, i: (b, i, 0))
    return pl.pallas_call(
        _cumsum_kernel, grid=(B, T // n), in_specs=[spec], out_specs=[spec] * 3,
        out_shape=[jax.ShapeDtypeStruct((B, T, C), BF16)] * 3, scratch_shapes=[pltpu.VMEM((1, C), F32)],
        compiler_params=_cp("parallel", "arbitrary"), name="cumsum_time")(x)


def _softmax_scratch_t(n, m):
    return [pltpu.VMEM((n, 1, m), F32), pltpu.VMEM((n, 1, m), F32), pltpu.VMEM((n, HD, m), F32)]


def _causal_kernel(q_ref, k_ref, vt_ref, o_ref, m_ref, l_ref, acc_ref, *, tq, tk, nh):
    q0 = pl.program_id(2) * tq
    qs = [jnp.concatenate([q_ref[0, GROUP * h + j] for j in range(GROUP)], axis=0) for h in range(nh)]
    m = GROUP * tq
    _init_softmax(m_ref, l_ref, acc_ref)
    qpos = q0 + _iota((1, m), 1) % tq

    def tile(j, valid):
        ks = pl.multiple_of(j * tk, tk)
        s = [_dot_nt(k_ref[0, h, pl.ds(ks, tk), :], qs[h]) for h in range(nh)]
        for h in range(nh):
            _online_t(s[h], valid, vt_ref[0, j, HD * h:HD * (h + 1), :], m_ref, l_ref, acc_ref, h)

    jd = q0 // tk
    tile(jd, (jd * tk + _iota((tk, 1), 0)) <= qpos)

    def body(n, carry):
        tile(n, None)
        return carry

    lax.fori_loop(0, jd, body, 0)
    for h in range(nh):
        o_ref[0, :, KV_W * h:KV_W * (h + 1)] = _group_out_t(acc_ref[h], l_ref[h], tq)


def _fox_prompt(qhm, khm, vt, cum_parts):
    B, _, T, _ = qhm.shape
    tq, tk = 256, KV_TILE
    d_aug = 2 * HD
    c = jnp.stack(cum_parts, axis=-1).reshape(B, T, N_KV, FOX_AUG).transpose(0, 2, 1, 3)
    k_aug = jnp.concatenate([khm, c, jnp.zeros((B, N_KV, T, d_aug - HD - FOX_AUG), BF16)], axis=-1)
    pat = np.zeros((N_HEADS, d_aug - HD), np.float32)
    for h in range(N_HEADS):
        pat[h, 3 * (h % GROUP):3 * (h % GROUP) + 3] = -1.0
    q_aug = jnp.concatenate([qhm, jnp.broadcast_to(jnp.asarray(pat, BF16)[None, :, None, :], (B, N_HEADS, T, d_aug - HD))], axis=-1)
    m = GROUP * tq
    return pl.pallas_call(
        functools.partial(_causal_kernel, tq=tq, tk=tk, nh=KV_PER_STEP), grid=(B, N_KV // KV_PER_STEP, T // tq),
        in_specs=[pl.BlockSpec((1, KV_PER_STEP * GROUP, tq, d_aug), lambda b, g, i: (b, g, i, 0)),
                  pl.BlockSpec((1, KV_PER_STEP, T, d_aug), lambda b, g, i: (b, g, 0, 0)),
                  pl.BlockSpec((1, T // tk, KV_PER_STEP * HD, tk), lambda b, g, i: (b, 0, g, 0))],
        out_specs=pl.BlockSpec((1, tq, KV_PER_STEP * KV_W), lambda b, g, i: (b, i, g)),
        out_shape=jax.ShapeDtypeStruct((B, T, D_MODEL), BF16),
        scratch_shapes=_softmax_scratch_t(KV_PER_STEP, m),
        compiler_params=_cp("parallel", "parallel", "arbitrary"), name="fox_p")(q_aug, k_aug, vt)


def _block_mean_kernel(k_ref, o_ref):
    k = k_ref[0]
    nb = k.shape[0] // MOBA_BLOCK
    o_ref[0] = jnp.sum(k.reshape(nb, MOBA_BLOCK, KV_W), axis=1) * (1.0 / MOBA_BLOCK)


def _block_mean(k32):
    B, T, _ = k32.shape
    nb = T // MOBA_BLOCK
    return pl.pallas_call(
        _block_mean_kernel, grid=(B,), in_specs=[pl.BlockSpec((1, T, KV_W), lambda b: (b, 0, 0))],
        out_specs=pl.BlockSpec((1, nb, KV_W), lambda b: (b, 0, 0)),
        out_shape=jax.ShapeDtypeStruct((B, nb, KV_W), F32), compiler_params=_cp("parallel"), name="block_mean")(k32)


def _moba_kernel(q_ref, q32_ref, km_ref, k_ref, vt_ref, o_ref, sel_ref, m_ref, l_ref, acc_ref, *, tq, nb, nh):
    tk = MOBA_BLOCK
    q0 = pl.program_id(2) * tq
    own = q0 // tk
    stack = lambda ref, h: jnp.concatenate([ref[0, GROUP * h + j] for j in range(GROUP)], axis=0)
    qs = [stack(q_ref, h) for h in range(nh)]
    m = GROUP * tq
    _init_softmax(m_ref, l_ref, acc_ref)
    for h in range(nh):
        sel_ref[h] = _topk_rank_select_t(_dot_nt_f32(km_ref[0, h], stack(q32_ref, h)), own, MOBA_TOPK, nb)
    qpos = q0 + _iota((1, m), 1) % tq

    def tile(j, valid, biased):
        ks = pl.multiple_of(j * tk, tk)
        s = [_dot_nt(k_ref[0, h, pl.ds(ks, tk), :], qs[h]) for h in range(nh)]
        for h in range(nh):
            sh = s[h]
            if biased:
                sh = sh + jnp.where(sel_ref[h, pl.ds(j, 1), :] > 0.5, 0.0, NEG)
            _online_t(sh, valid, vt_ref[0, j, HD * h:HD * (h + 1), :], m_ref, l_ref, acc_ref, h)

    tile(own, (own * tk + _iota((tk, 1), 0)) <= qpos, False)

    def body(j, carry):
        tile(j, None, True)
        return carry

    lax.fori_loop(0, own, body, 0)
    for h in range(nh):
        o_ref[0, :, KV_W * h:KV_W * (h + 1)] = _group_out_t(acc_ref[h], l_ref[h], tq)


def _moba_prompt(qhm, q32hm, kmean, khm, vt):
    B, _, T, _ = qhm.shape
    tq, nh = 256, KV_PER_STEP
    nb = T // MOBA_BLOCK
    m = GROUP * tq
    km = kmean.reshape(B, nb, N_KV, HD).transpose(0, 2, 1, 3)
    qspec = pl.BlockSpec((1, nh * GROUP, tq, HD), lambda b, g, i: (b, g, i, 0))
    return pl.pallas_call(
        functools.partial(_moba_kernel, tq=tq, nb=nb, nh=nh), grid=(B, N_KV // nh, T // tq),
        in_specs=[qspec, qspec, pl.BlockSpec((1, nh, nb, HD), lambda b, g, i: (b, g, 0, 0)),
                  pl.BlockSpec((1, nh, T, HD), lambda b, g, i: (b, g, 0, 0)),
                  pl.BlockSpec((1, nb, nh * HD, MOBA_BLOCK), lambda b, g, i: (b, 0, g, 0))],
        out_specs=pl.BlockSpec((1, tq, nh * KV_W), lambda b, g, i: (b, i, g)),
        out_shape=jax.ShapeDtypeStruct((B, T, D_MODEL), BF16),
        scratch_shapes=[pltpu.VMEM((nh, nb, m), F32)] + _softmax_scratch_t(nh, m),
        compiler_params=_cp("parallel", "parallel", "arbitrary"), name="moba_p")(qhm, q32hm, km, khm, vt)


def _dsa_kernel(q_ref, qi_ref, wi_ref, ki_ref, k_ref, vt_ref, o_ref, key_ref, m_ref, l_ref, acc_ref, *, tq, n_keep):
    tk = KV_TILE
    q0 = pl.program_id(1) * tq
    n_proc = (q0 + tq + tk - 1) // tk
    qpos = q0 + _iota((1, tq), 1)

    def visible(c):
        return (c * tk + _iota((tk, 1), 0)) <= qpos

    qi3 = jnp.concatenate([qi_ref[0, i] for i in range(IDX_HEADS)], axis=0)

    def score_tile(c, carry):
        ks = pl.multiple_of(c * tk, tk)
        dots = _dot_nt(ki_ref[0, pl.ds(ks, tk), :], qi3)
        sc = jnp.zeros((tk, tq), F32)
        for i in range(IDX_HEADS):
            sc = sc + wi_ref[0, i:i + 1, :] * jnp.maximum(dots[:, i * tq:(i + 1) * tq], 0.0)
        key_ref[c] = _order_key(jnp.where(visible(c), sc + 0.0, -jnp.inf))
        return carry

    lax.fori_loop(0, n_proc, score_tile, 0)

    def count(pred_of_tile):
        def body(c, part):
            return part + jnp.where(pred_of_tile(key_ref[c]), 1.0, 0.0)

        return jnp.sum(lax.fori_loop(0, n_proc, body, jnp.zeros((tk, tq), F32)), axis=0, keepdims=True)

    thr = _kth_largest_key(lambda cand: count(lambda k: k >= cand), (1, tq), n_keep)
    need = n_keep - count(lambda k: k > thr)
    _init_softmax(m_ref, l_ref, acc_ref)
    before = _tri(tk, "gt")
    qs = [jnp.concatenate([q_ref[0, GROUP * g + j] for j in range(GROUP)], axis=0) for g in range(N_KV)]

    def attend(c, ties_seen):
        ks = pl.multiple_of(c * tk, tk)
        key = key_ref[c]
        tied = jnp.where(key == thr, 1.0, 0.0)
        rank = _dot(before, tied.astype(BF16)) + ties_seen
        keep = (key > thr) | ((key == thr) & (rank < need))
        bias = jnp.where(keep & visible(c), 0.0, NEG)
        bias = jnp.concatenate([bias] * GROUP, axis=1)
        s = [_dot_nt(k_ref[0, g, pl.ds(ks, tk), :], qs[g]) for g in range(N_KV)]
        for g in range(N_KV):
            _online_t(s[g], None, vt_ref[0, c, HD * g:HD * (g + 1), :], m_ref, l_ref, acc_ref, g, bias=bias)
        return ties_seen + jnp.sum(tied, axis=0, keepdims=True)

    lax.fori_loop(0, n_proc, attend, jnp.zeros((1, tq), F32))
    for g in range(N_KV):
        o_ref[0, :, KV_W * g:KV_W * (g + 1)] = _group_out_t(acc_ref[g], l_ref[g], tq)


def _dsa_prompt(qhm, qi3, wi, ki3, khm, vt):
    B, _, T, _ = qhm.shape
    tq = 128
    m = GROUP * tq
    return pl.pallas_call(
        functools.partial(_dsa_kernel, tq=tq, n_keep=min(IDX_TOPK, T // 4)), grid=(B, T // tq),
        in_specs=[pl.BlockSpec((1, N_HEADS, tq, HD), lambda b, i: (b, 0, i, 0)),
                  pl.BlockSpec((1, IDX_HEADS, tq, 4 * IDX_DIM), lambda b, i: (b, 0, i, 0)),
                  pl.BlockSpec((1, IDX_HEADS, tq), lambda b, i: (b, 0, i)),
                  pl.BlockSpec((1, T, 4 * IDX_DIM), lambda b, i: (b, 0, 0)),
                  pl.BlockSpec((1, N_KV, T, HD), lambda b, i: (b, 0, 0, 0)),
                  pl.BlockSpec((1, T // KV_TILE, KV_W, KV_TILE), lambda b, i: (b, 0, 0, 0))],
        out_specs=pl.BlockSpec((1, tq, D_MODEL), lambda b, i: (b, i, 0)),
        out_shape=jax.ShapeDtypeStruct((B, T, D_MODEL), BF16),
        scratch_shapes=[pltpu.VMEM((T // KV_TILE, KV_TILE, tq), I32)] + _softmax_scratch_t(N_KV, m),
        compiler_params=_cp("parallel", "arbitrary"), name="dsa_p")(qhm, qi3, wi.transpose(0, 2, 1), ki3, khm, vt)


def _page_specs(layer, n_pages, nps, reverse):
    def spec(i):
        def idx(b, s, pt):
            p = s * nps + i
            return (layer, pt[b, n_pages - 1 - p if reverse else p], 0, 0)

        return pl.BlockSpec((None, None, KV_W, PAGE), idx)

    return [spec(i) for i in range(nps)]


def _pool_specs(rows, n_pages, nps, reverse):
    def spec(i):
        def idx(b, s, pt):
            p = s * nps + i
            return (pt[b, n_pages - 1 - p if reverse else p], 0, 0)

        return pl.BlockSpec((None, rows, PAGE), idx)

    return [spec(i) for i in range(nps)]


def _row_spec(width):
    return pl.BlockSpec((1, 1, width), lambda b, s, pt: (b, 0, 0))


def _head_spec(width):
    return pl.BlockSpec((1, N_HEADS, width), lambda b, s, pt: (b, 0, 0))


def _softmax_scratch():
    return [pltpu.VMEM((N_HEADS, 1), F32), pltpu.VMEM((N_HEADS, 1), F32), pltpu.VMEM((N_HEADS, KV_W), F32)]


def _new_key_start(q, knew, vnew, m_ref, l_ref, acc_ref):
    z = jnp.sum(q.astype(F32) * knew.astype(BF16).astype(F32), axis=1, keepdims=True)
    m_ref[...] = z
    l_ref[...] = jnp.ones_like(z)
    acc_ref[...] = jnp.broadcast_to(vnew.astype(BF16).astype(F32), acc_ref.shape)


def _cat_pages(refs):
    return jnp.concatenate([r[...].astype(BF16) for r in refs], axis=1)


def _later_sums(x, parts):
    n = x.shape[1] // PAGE
    stack = jnp.concatenate([p[:, PAGE * i:PAGE * (i + 1)] for p in parts(x) for i in range(n)], axis=0)
    out = _dot(stack, _tri(PAGE, "gt"))
    rows = x.shape[0]
    tot = None
    for k in range(out.shape[0] // (n * rows)):
        part = jnp.concatenate([out[(k * n + i) * rows:(k * n + i + 1) * rows] for i in range(n)], axis=1)
        tot = part if tot is None else tot + part
    return tot


def _page_totals(x):
    return [jnp.sum(x[:, PAGE * i:PAGE * (i + 1)], axis=1, keepdims=True) for i in range(x.shape[1] // PAGE)]


def _per_page(cols):
    return jnp.concatenate([jnp.broadcast_to(c, (c.shape[0], PAGE)) for c in cols], axis=1)


def _online_pages(s, valid, vcat, m_ref, l_ref, acc_ref):
    if valid is not None:
        s = jnp.where(valid, s, NEG)
    m_old = m_ref[...]
    m_new = jnp.maximum(m_old, jnp.max(s, axis=1, keepdims=True))
    alpha = jnp.exp(m_old - m_new)
    p = jnp.exp(s - m_new)
    if valid is not None:
        p = jnp.where(valid, p, 0.0)
    l_ref[...] = alpha * l_ref[...] + jnp.sum(p, axis=1, keepdims=True)
    acc_ref[...] = alpha * acc_ref[...] + _dot_nt(p.astype(BF16), vcat)
    m_ref[...] = m_new


def _dec_stick_kernel(pt_ref, q_ref, *refs, nps):
    k_refs, v_refs = refs[:nps], refs[nps:2 * nps]
    o_ref, r_ref, acc_ref = refs[2 * nps:]
    step = pl.program_id(1)

    @pl.when(step == 0)
    def _():
        r_ref[...] = jnp.zeros_like(r_ref)
        acc_ref[...] = jnp.zeros_like(acc_ref)

    ls, lk = _log_sigmoid_pair(_dot(q_ref[0], _cat_pages(k_refs)))
    after, run = [], r_ref[...]
    for tot in _page_totals(lk):
        after.append(run)
        run = run + tot
    w = jnp.exp(ls + _later_sums(lk, _split2) + _per_page(after))
    acc = acc_ref[...] + _dot_nt(w.astype(BF16), _cat_pages(v_refs))
    r_ref[...], acc_ref[...] = run, acc

    @pl.when(step == pl.num_programs(1) - 1)
    def _():
        o_ref[0] = acc


def _dec_call(kernel, name, grid_spec, DB):
    return pl.pallas_call(kernel, grid_spec=grid_spec, out_shape=jax.ShapeDtypeStruct((DB, N_HEADS, KV_W), F32),
                          compiler_params=_cp("parallel", "arbitrary"), name=name)


def _dec_stick(qx, kt, vt, page_table, layer):
    DB, n_pages = page_table.shape
    nps = math.gcd(n_pages, DEC_PAGES)
    pages = _page_specs(layer, n_pages, nps, True)
    grid_spec = pltpu.PrefetchScalarGridSpec(
        num_scalar_prefetch=1, grid=(DB, n_pages // nps), in_specs=[_head_spec(KV_W)] + pages + pages,
        out_specs=_head_spec(KV_W), scratch_shapes=[pltpu.VMEM((N_HEADS, 1), F32), pltpu.VMEM((N_HEADS, KV_W), F32)])
    return _dec_call(functools.partial(_dec_stick_kernel, nps=nps), "stick_s", grid_spec, DB)(
        page_table, qx, *([kt] * nps), *([vt] * nps))


def _dec_fox_kernel(pt_ref, q_ref, *refs, nps):
    k_refs, v_refs, lf_refs = refs[:nps], refs[nps:2 * nps], refs[2 * nps:3 * nps]
    kn_ref, vn_ref, lfn_ref, o_ref, s_ref, m_ref, l_ref, acc_ref = refs[3 * nps:]
    step = pl.program_id(1)

    @pl.when(step == 0)
    def _():
        s_ref[...] = lfn_ref[0]
        _new_key_start(q_ref[0], kn_ref[0], vn_ref[0], m_ref, l_ref, acc_ref)

    lf = jnp.concatenate([r[...] for r in lf_refs], axis=1)
    after, run = [], s_ref[...]
    for tot in _page_totals(lf):
        after.append(run)
        run = run + tot
    s_ref[...] = run
    s = _dot(q_ref[0], _cat_pages(k_refs)) + _later_sums(lf, _split3) + _per_page(after)
    _online_pages(s, None, _cat_pages(v_refs), m_ref, l_ref, acc_ref)

    @pl.when(step == pl.num_programs(1) - 1)
    def _():
        o_ref[0] = acc_ref[...] / l_ref[...]


def _dec_fox(qx, kt, vt, logf_t, knew, vnew, lfnew, page_table, layer):
    DB, n_pages = page_table.shape
    nps = math.gcd(n_pages, DEC_PAGES)
    pages = _page_specs(layer, n_pages, nps, True)
    grid_spec = pltpu.PrefetchScalarGridSpec(
        num_scalar_prefetch=1, grid=(DB, n_pages // nps),
        in_specs=[_head_spec(KV_W)] + pages + pages + _pool_specs(N_HEADS, n_pages, nps, True)
        + [_row_spec(KV_W), _row_spec(KV_W), _head_spec(1)],
        out_specs=_head_spec(KV_W), scratch_shapes=[pltpu.VMEM((N_HEADS, 1), F32)] + _softmax_scratch())
    return _dec_call(functools.partial(_dec_fox_kernel, nps=nps), "fox_s", grid_spec, DB)(
        page_table, qx, *([kt] * nps), *([vt] * nps), *([logf_t] * nps), knew, vnew, lfnew)


def _dec_kmean_kernel(pt_ref, *refs, nps):
    k_refs, o_ref = refs[:nps], refs[nps]
    step = pl.program_id(1)

    @pl.when(step == 0)
    def _():
        o_ref[...] = jnp.zeros_like(o_ref)

    ones = jnp.ones((8, PAGE), BF16)
    for i in range(nps):
        hi, mid, lo = _split3(k_refs[i][...])
        tot = _dot_nt(ones, hi) + _dot_nt(ones, mid) + _dot_nt(ones, lo)
        blk = (step * nps + i) // (MOBA_BLOCK // PAGE)
        o_ref[0, pl.ds(blk, 1), :] = o_ref[0, pl.ds(blk, 1), :] + tot[0:1, :] * (1.0 / MOBA_BLOCK)


def _dec_kmean(kt, page_table, layer):
    DB, n_pages = page_table.shape
    nps = math.gcd(n_pages, DEC_PAGES)
    nb = n_pages * PAGE // MOBA_BLOCK
    grid_spec = pltpu.PrefetchScalarGridSpec(
        num_scalar_prefetch=1, grid=(DB, n_pages // nps), in_specs=_page_specs(layer, n_pages, nps, False),
        out_specs=pl.BlockSpec((1, nb, KV_W), lambda b, s, pt: (b, 0, 0)))
    return pl.pallas_call(
        functools.partial(_dec_kmean_kernel, nps=nps), grid_spec=grid_spec,
        out_shape=jax.ShapeDtypeStruct((DB, nb, KV_W), F32),
        compiler_params=_cp("parallel", "arbitrary"), name="kmean_s")(page_table, *([kt] * nps))


def _dec_moba_kernel(pt_ref, q_ref, q32_ref, km_ref, *refs, nps, nb):
    k_refs, v_refs = refs[:nps], refs[nps:2 * nps]
    kn_ref, vn_ref, o_ref, sel_ref, m_ref, l_ref, acc_ref = refs[2 * nps:]
    step = pl.program_id(1)

    @pl.when(step == 0)
    def _():
        sel_ref[...] = _topk_rank_select(_dot_nt_f32(q32_ref[0], km_ref[0]), nb, MOBA_TOPK, nb)
        _new_key_start(q_ref[0], kn_ref[0], vn_ref[0], m_ref, l_ref, acc_ref)

    lane = _iota((N_HEADS, nb), 1)
    sel = sel_ref[...]
    bias = []
    for i in range(nps):
        blk = (step * nps + i) // (MOBA_BLOCK // PAGE)
        picked = jnp.sum(jnp.where(lane == blk, sel, 0.0), axis=1, keepdims=True) > 0.5
        bias.append(jnp.where(picked, 0.0, NEG))
    s = _dot(q_ref[0], _cat_pages(k_refs)) + _per_page(bias)
    _online_pages(s, None, _cat_pages(v_refs), m_ref, l_ref, acc_ref)

    @pl.when(step == pl.num_programs(1) - 1)
    def _():
        o_ref[0] = acc_ref[...] / l_ref[...]


def _dec_moba(qx, q32x, kmean, kt, vt, knew, vnew, page_table, layer):
    DB, n_pages = page_table.shape
    nps = math.gcd(n_pages, DEC_PAGES)
    nb = kmean.shape[1]
    pages = _page_specs(layer, n_pages, nps, False)
    grid_spec = pltpu.PrefetchScalarGridSpec(
        num_scalar_prefetch=1, grid=(DB, n_pages // nps),
        in_specs=[_head_spec(KV_W), _head_spec(KV_W), pl.BlockSpec((1, nb, KV_W), lambda b, s, pt: (b, 0, 0))]
        + pages + pages + [_row_spec(KV_W), _row_spec(KV_W)],
        out_specs=_head_spec(KV_W), scratch_shapes=[pltpu.VMEM((N_HEADS, nb), F32)] + _softmax_scratch())
    return _dec_call(functools.partial(_dec_moba_kernel, nps=nps, nb=nb), "moba_s", grid_spec, DB)(
        page_table, qx, q32x, kmean, *([kt] * nps), *([vt] * nps), knew, vnew)


def _dec_score_kernel(pt_ref, qi_ref, wi_ref, *refs, nps, n_pages):
    kidx_refs = refs[:nps]
    kin_ref, o_ref = refs[nps:]
    step = pl.program_id(1)
    qi, wi = qi_ref[0], wi_ref[0]

    @pl.when(step == 0)
    def _():
        o_ref[...] = jnp.full(o_ref.shape, -jnp.inf, F32)
        dn = jnp.sum(qi * kin_ref[0], axis=1, keepdims=True)
        sn = jnp.sum(wi * jnp.maximum(dn, 0.0), axis=0, keepdims=True) + 0.0
        o_ref[0, n_pages:n_pages + 1, :] = jnp.where(_iota((1, PAGE), 1) == 0, sn, -jnp.inf)

    for i in range(nps):
        dots = _dot_f32(qi, kidx_refs[i][...])
        o_ref[0, pl.ds(step * nps + i, 1), :] = jnp.sum(wi * jnp.maximum(dots, 0.0), axis=0, keepdims=True) + 0.0


def _dec_scores(qi3, wi3, kidx_t, kinew, page_table):
    DB, n_pages = page_table.shape
    assert n_pages < PAGE
    nps = math.gcd(n_pages, DEC_PAGES)
    grid_spec = pltpu.PrefetchScalarGridSpec(
        num_scalar_prefetch=1, grid=(DB, n_pages // nps),
        in_specs=[pl.BlockSpec((1, IDX_HEADS, IDX_DIM), lambda b, s, pt: (b, 0, 0)),
                  pl.BlockSpec((1, IDX_HEADS, 1), lambda b, s, pt: (b, 0, 0))]
        + _pool_specs(IDX_DIM, n_pages, nps, False) + [_row_spec(IDX_DIM)],
        out_specs=pl.BlockSpec((1, PAGE, PAGE), lambda b, s, pt: (b, 0, 0)))
    return pl.pallas_call(
        functools.partial(_dec_score_kernel, nps=nps, n_pages=n_pages), grid_spec=grid_spec,
        out_shape=jax.ShapeDtypeStruct((DB, PAGE, PAGE), F32),
        compiler_params=_cp("parallel", "arbitrary"), name="dsa_score_s")(page_table, qi3, wi3, *([kidx_t] * nps), kinew)


def _dec_select_kernel(sc_ref, o_ref, *, n_keys, n_keep):
    key = _order_key(sc_ref[0])

    def total(x):
        return jnp.sum(jnp.sum(x, axis=1, keepdims=True), axis=0, keepdims=True)

    thr = _kth_largest_key(lambda cand: total(jnp.where(key >= cand, 1.0, 0.0)), (1, 1), n_keep)
    need = n_keep - total(jnp.where(key > thr, 1.0, 0.0))
    tied = jnp.where(key == thr, 1.0, 0.0)
    in_row = _dot(tied.astype(BF16), _tri(PAGE, "lt"))
    row_tot = jnp.broadcast_to(jnp.sum(tied, axis=1, keepdims=True), tied.shape).astype(BF16)
    rank = in_row + _dot(_tri(PAGE, "gt"), row_tot)
    pos = _iota(key.shape, 0) * PAGE + _iota(key.shape, 1)
    keep = ((key > thr) | ((key == thr) & (rank < need))) & (pos < n_keys)
    o_ref[0] = jnp.where(keep, 1.0, 0.0)


def _dec_select(scores, n_keys):
    DB = scores.shape[0]
    spec = pl.BlockSpec((1, PAGE, PAGE), lambda b: (b, 0, 0))
    return pl.pallas_call(
        functools.partial(_dec_select_kernel, n_keys=n_keys, n_keep=min(IDX_TOPK, n_keys // 4)), grid=(DB,),
        in_specs=[spec], out_specs=spec, out_shape=jax.ShapeDtypeStruct(scores.shape, F32),
        compiler_params=_cp("parallel"), name="dsa_select_s")(scores)


def _dec_dsa_kernel(pt_ref, q_ref, *refs, nps, n_pages):
    k_refs, v_refs = refs[:nps], refs[nps:2 * nps]
    sel_ref, kn_ref, vn_ref, o_ref, m_ref, l_ref, acc_ref = refs[2 * nps:]
    step = pl.program_id(1)
    q = q_ref[0]

    @pl.when(step == 0)
    def _():
        zn = jnp.sum(q.astype(F32) * kn_ref[0].astype(BF16).astype(F32), axis=1, keepdims=True)
        keep = jnp.broadcast_to(sel_ref[0, n_pages:n_pages + 1, 0:1] > 0.5, zn.shape)
        m_ref[...] = jnp.where(keep, zn, NEG)
        l_ref[...] = jnp.where(keep, 1.0, 0.0)
        acc_ref[...] = jnp.where(keep, jnp.broadcast_to(vn_ref[0].astype(BF16).astype(F32), acc_ref.shape), 0.0)

    keep = jnp.concatenate([sel_ref[0, pl.ds(step * nps + i, 1), :] for i in range(nps)], axis=1)
    s = _dot(q, _cat_pages(k_refs))
    _online_pages(s, jnp.broadcast_to(keep > 0.5, s.shape), _cat_pages(v_refs), m_ref, l_ref, acc_ref)

    @pl.when(step == pl.num_programs(1) - 1)
    def _():
        o_ref[0] = acc_ref[...] / l_ref[...]


def _dec_dsa(qx, kt, vt, sel, knew, vnew, page_table, layer):
    DB, n_pages = page_table.shape
    nps = math.gcd(n_pages, DEC_PAGES)
    pages = _page_specs(layer, n_pages, nps, False)
    grid_spec = pltpu.PrefetchScalarGridSpec(
        num_scalar_prefetch=1, grid=(DB, n_pages // nps),
        in_specs=[_head_spec(KV_W)] + pages + pages
        + [pl.BlockSpec((1, PAGE, PAGE), lambda b, s, pt: (b, 0, 0)), _row_spec(KV_W), _row_spec(KV_W)],
        out_specs=_head_spec(KV_W), scratch_shapes=_softmax_scratch())
    return _dec_call(functools.partial(_dec_dsa_kernel, nps=nps, n_pages=n_pages), "dsa_s", grid_spec, DB)(
        page_table, qx, *([kt] * nps), *([vt] * nps), sel, knew, vnew)


def _rope_tables(pos):
    half = HD // 2
    inv_freq = jnp.power(ROPE_THETA, -jnp.arange(half, dtype=F32) * (2.0 / HD))
    ang = pos.astype(F32)[:, None] * inv_freq[None, :]
    cos, sin = jnp.cos(ang), jnp.sin(ang)
    return jnp.concatenate([cos] * 4, axis=1), jnp.concatenate([-sin, sin] * 2, axis=1)


def _pad_cols(w, mult=128):
    pad = (-w.shape[1]) % mult
    return jnp.pad(w, ((0, 0), (0, pad))) if pad else w


def _layer_weights(kind, w_qkv, fox_w_f, idx_w_q, idx_w_k, idx_w_w):
    cols = [w_qkv]
    if kind == FOX:
        cols.append(_pad_cols(fox_w_f))
    if kind == DSA:
        cols += [idx_w_q, _pad_cols(idx_w_k), _pad_cols(idx_w_w)]
    return jnp.concatenate(cols, axis=1).astype(BF16)


_HEAD_ONEHOT = np.arange(N_HEADS)[:, None] // GROUP == np.arange(N_KV)[None, :]


def _expand_heads(q, dtype):
    m = jnp.asarray(_HEAD_ONEHOT, q.dtype)
    return (q[:, :, None, :] * m[None, :, :, None]).reshape(q.shape[0], N_HEADS, KV_W).astype(dtype)


def _collapse_heads(ox):
    m = jnp.asarray(_HEAD_ONEHOT, ox.dtype)
    o = jnp.sum(ox.reshape(ox.shape[0], N_HEADS, N_KV, HD) * m[None, :, :, None], axis=2)
    return o.reshape(ox.shape[0], D_MODEL).astype(BF16)


def _pages_t(cache):
    depth, n_pool = cache.shape[:2]
    return cache.transpose(0, 1, 3, 4, 2).reshape(depth, n_pool, KV_W, PAGE)


def kernel(x_prompt, x_sample, cache_k, cache_v, cache_logf, cache_kidx, state_conv, page_table, norm_mix, norm_ffn, norm_final, w_qkv, w_o, fox_w_f, fox_b_f, idx_w_q, idx_w_k, idx_w_w, ffn_w_a, ffn_w_b, ffn_conv_w, ffn_conv_b, ffn_w_down):
    B, T, _ = x_prompt.shape
    DB = x_sample.shape[0]
    depth = cache_k.shape[0]
    n_pages = page_table.shape[1]
    past = n_pages * PAGE
    assert T % MOBA_BLOCK == 0 and past % MOBA_BLOCK == 0 and x_sample.shape[1] == 1
    cache_kt, cache_vt = _pages_t(cache_k), _pages_t(cache_v)
    rope_p = _rope_tables(jnp.arange(T, dtype=I32))
    rope_s = _rope_tables(jnp.full((DB,), past, I32))

    xp = x_prompt
    xs = x_sample.reshape(1, DB, D_MODEL)
    new_k_p, new_v_p, new_k_s, new_v_s, conv_p, conv_s = [], [], [], [], [], []
    logf_p = logf_s = kidx_p = kidx_s = None
    for i in range(depth):
        kind = i % 4
        rope = kind in (MOBA, DSA)
        w_all = _layer_weights(kind, w_qkv[i], fox_w_f, idx_w_q, idx_w_k, idx_w_w)
        wo_b = w_o[i].astype(BF16)
        wa_b, wb_b, wd_b = ffn_w_a[i].astype(BF16), ffn_w_b[i].astype(BF16), ffn_w_down[i].astype(BF16)

        outs = _proj(xp, norm_mix[i], w_all, kind, rope_p if rope else None, fox_b_f, True)
        qhm, k32, v32, khm, vhm = outs[:5]
        vt = outs[-1]
        new_k_p.append(k32.reshape(B, T, N_KV, HD))
        new_v_p.append(v32.reshape(B, T, N_KV, HD))
        if kind == STICK:
            o = _stick_prompt(qhm, khm, vt)
        elif kind == FOX:
            logf_p = outs[5]
            o = _fox_prompt(qhm, khm, vt, _cumsum_time_parts(logf_p))
        elif kind == MOBA:
            o = _moba_prompt(qhm, outs[5], _block_mean(k32), khm, vt)
        else:
            kidx_p, wi, qi3, ki3 = outs[6:10]
            o = _dsa_prompt(qhm, qi3, wi, ki3, khm, vt)
        x2, rows = _mix_ffn(xp.reshape(B * T, D_MODEL), o.reshape(B * T, D_MODEL), wo_b, norm_ffn[i], wa_b, wb_b,
                            ffn_conv_w[i], ffn_conv_b[i], wd_b, T)
        conv_p.append(rows)
        xp = x2.reshape(B, T, D_MODEL)

        outs = _proj(xs, norm_mix[i], w_all, kind, rope_s if rope else None, fox_b_f, False)
        qhm, k32, v32 = outs[:3]
        new_k_s.append(k32.reshape(DB, 1, N_KV, HD))
        new_v_s.append(v32.reshape(DB, 1, N_KV, HD))
        qx = _expand_heads(qhm[0].transpose(1, 0, 2), BF16)
        knew, vnew = k32.reshape(DB, 1, KV_W), v32.reshape(DB, 1, KV_W)
        if kind == STICK:
            ox = _dec_stick(qx, cache_kt, cache_vt, page_table, i)
        elif kind == FOX:
            lf = outs[5][0]
            logf_s = lf.reshape(DB, 1, N_HEADS)
            ox = _dec_fox(qx, cache_kt, cache_vt, cache_logf.transpose(0, 2, 1), knew, vnew,
                          lf.reshape(DB, N_HEADS, 1), page_table, i)
        elif kind == MOBA:
            q32x = _expand_heads(outs[5][0].transpose(1, 0, 2), F32)
            ox = _dec_moba(qx, q32x, _dec_kmean(cache_kt, page_table, i), cache_kt, cache_vt, knew, vnew, page_table, i)
        else:
            qihm, ki, wi = outs[5:8]
            kidx_s = ki.reshape(DB, 1, IDX_DIM)
            scores = _dec_scores(qihm[0].transpose(1, 0, 2), wi[0].reshape(DB, IDX_HEADS, 1),
                                 cache_kidx.transpose(0, 2, 1), kidx_s, page_table)
            ox = _dec_dsa(qx, cache_kt, cache_vt, _dec_select(scores, past + 1), knew, vnew, page_table, i)
        st = state_conv[i]
        s2, a_new = _mix_ffn(xs.reshape(DB, D_MODEL), _collapse_heads(ox), wo_b, norm_ffn[i], wa_b, wb_b,
                             ffn_conv_w[i], ffn_conv_b[i], wd_b, 1, prev=(st[:, 0], st[:, 1]))
        conv_s.append(jnp.stack([st[:, 1], a_new], axis=1))
        xs = s2.reshape(1, DB, D_MODEL)

    y_p = _final_norm(xp.reshape(B * T, D_MODEL), norm_final).reshape(B, T, D_MODEL)
    y_s = _final_norm(xs.reshape(DB, D_MODEL), norm_final).reshape(DB, 1, D_MODEL)
    return (y_p, y_s, jnp.stack(new_k_p), jnp.stack(new_v_p), jnp.stack(new_k_s), jnp.stack(new_v_s),
            logf_p, logf_s, kidx_p, kidx_s, jnp.stack(conv_p), jnp.stack(conv_s))
```

```python
import functools
import math

import jax
import jax.numpy as jnp
import numpy as np
from jax import lax
from jax.experimental import pallas as pl
from jax.experimental.pallas import tpu as pltpu

F32, BF16, I32 = jnp.float32, jnp.bfloat16, jnp.int32

D_MODEL = 1024
N_HEADS = 16
N_KV = 4
GROUP = N_HEADS // N_KV
HD = 64
KV_W = N_KV * HD
D_FF = 2816
CONV_W = 3
PAGE = 128
MOBA_BLOCK = 256
MOBA_TOPK = 3
IDX_HEADS = 8
IDX_DIM = 64
IDX_TOPK = 256
ROPE_THETA = 10000.0
RMS_EPS = 1e-6
STICK, FOX, MOBA, DSA = 0, 1, 2, 3
NEG = -1e30
M_FLOOR = -1e29
DEAD_LOG = -104.0
INT_MIN = -2147483648
VMEM_LIMIT = 48 * 1024 * 1024
FF_CHUNK = 256
FFN_TM = 512
PROJ_TM = 256
KV_TILE = 256
DEC_PAGES = 32
KV_PER_STEP = 4
Q_CHUNK = 128
FOX_AUG = 3 * GROUP


def _cp(*sem):
    return pltpu.CompilerParams(dimension_semantics=sem, vmem_limit_bytes=VMEM_LIMIT)


def _dot(a, b):
    return jnp.dot(a, b, preferred_element_type=F32)


def _dot_nt(a, b):
    return lax.dot_general(a, b, (((1,), (1,)), ((), ())), preferred_element_type=F32)


def _split2(x):
    hi = x.astype(BF16)
    lo = (x - hi.astype(F32)).astype(BF16)
    return hi, lo


def _split3(x):
    hi = x.astype(BF16)
    r = x - hi.astype(F32)
    mid = r.astype(BF16)
    lo = (r - mid.astype(F32)).astype(BF16)
    return hi, mid, lo


def _dot_f32(a, b):
    ah, al = _split2(a)
    bh, bl = _split2(b)
    return _dot(ah, bh) + _dot(al, bh) + _dot(ah, bl)


def _dot_nt_f32(a, b):
    ah, al = _split2(a)
    bh, bl = _split2(b)
    return _dot_nt(ah, bh) + _dot_nt(al, bh) + _dot_nt(ah, bl)


def _log_sigmoid_pair(z):
    l1p = jnp.log(1.0 + jnp.exp(-jnp.abs(z)))
    return jnp.minimum(z, 0.0) - l1p, jnp.minimum(-z, 0.0) - l1p


def _iota(shape, dim):
    return lax.broadcasted_iota(I32, shape, dim)


def _tri(n, kind):
    r, c = _iota((n, n), 0), _iota((n, n), 1)
    m = {"gt": r > c, "lt": r < c, "ge": r >= c}[kind]
    return jnp.where(m, 1.0, 0.0).astype(BF16)


def _rms_bf16(x, g):
    y = x * lax.rsqrt(jnp.mean(x * x, axis=-1, keepdims=True) + RMS_EPS)
    return (y * g).astype(BF16)


def _rope128(xc, cos, sin):
    lane = _iota(xc.shape, 1)
    first = (lane % HD) < (HD // 2)
    rolled = jnp.where(first, pltpu.roll(xc, 128 - HD // 2, 1), pltpu.roll(xc, HD // 2, 1))
    return xc * cos + rolled * sin


def _halves(xc):
    return xc[:, :HD], pltpu.roll(xc, HD, 1)[:, :HD]


def _place(parts):
    r, c = _iota((HD, KV_W), 0), _iota((HD, KV_W), 1)
    out = None
    for j, pj in enumerate(parts):
        e = jnp.where(c == r + HD * j, 1.0, 0.0).astype(BF16)
        t = _dot(pj, e)
        out = t if out is None else out + t
    return out


def _stack_heads(q_ref):
    return jnp.concatenate([q_ref[0, j] for j in range(GROUP)], axis=0)


def _init_softmax(m_ref, l_ref, acc_ref):
    m_ref[...] = jnp.full(m_ref.shape, M_FLOOR, F32)
    l_ref[...] = jnp.zeros_like(l_ref)
    acc_ref[...] = jnp.zeros_like(acc_ref)


def _online_t(s, valid, vt, m_ref, l_ref, acc_ref, idx, bias=None):
    m_all, l_all, acc_all = m_ref[idx], l_ref[idx], acc_ref[idx]
    m_out, l_out, acc_out = [], [], []
    for c0 in range(0, s.shape[1], Q_CHUNK):
        cols = slice(c0, c0 + Q_CHUNK)
        sc = s[:, cols]
        if bias is not None:
            sc = sc + bias[:, cols]
        if valid is not None:
            vc = valid[:, cols]
            sc = jnp.where(vc, sc, NEG)
        m_old = m_all[:, cols]
        m_new = jnp.maximum(m_old, jnp.max(sc, axis=0, keepdims=True))
        alpha = jnp.exp(m_old - m_new)
        p = jnp.exp(sc - m_new)
        if valid is not None:
            p = jnp.where(vc, p, 0.0)
        l_out.append(alpha * l_all[:, cols] + jnp.sum(p, axis=0, keepdims=True))
        acc_out.append(alpha * acc_all[:, cols] + _dot(vt, p.astype(BF16)))
        m_out.append(m_new)
    m_ref[idx] = jnp.concatenate(m_out, axis=1)
    l_ref[idx] = jnp.concatenate(l_out, axis=1)
    acc_ref[idx] = jnp.concatenate(acc_out, axis=1)


def _group_out_t(acc_t, l, tq):
    r = acc_t if l is None else acc_t / l
    o_t = jnp.concatenate([r[:, j * tq:(j + 1) * tq] for j in range(GROUP)], axis=0)
    return o_t.T.astype(BF16)


def _topk_rank_select_t(gate_t, n_valid, k, nb):
    row = _iota(gate_t.shape, 0)
    gm = jnp.where(row < n_valid, gate_t, -jnp.inf)
    sel = jnp.zeros(gate_t.shape, F32)
    for n in range(nb):
        gn = gm[n:n + 1, :]
        beats = jnp.where(gm > gn, 1.0, jnp.where((gm == gn) & (row < n), 1.0, 0.0))
        rank = jnp.sum(beats, axis=0, keepdims=True)
        ok = jnp.where((rank < k) & (n < n_valid), 1.0, 0.0)
        sel = jnp.where(row == n, ok, sel)
    return sel


def _topk_rank_select(gate, n_valid, k, nb):
    lane = _iota(gate.shape, 1)
    gm = jnp.where(lane < n_valid, gate, -jnp.inf)
    sel = jnp.zeros(gate.shape, F32)
    for n in range(nb):
        gn = gm[:, n:n + 1]
        beats = jnp.where(gm > gn, 1.0, jnp.where((gm == gn) & (lane < n), 1.0, 0.0))
        rank = jnp.sum(beats, axis=1, keepdims=True)
        ok = jnp.where((rank < k) & (n < n_valid), 1.0, 0.0)
        sel = jnp.where(lane == n, ok, sel)
    return sel


def _order_key(score):
    bits = lax.bitcast_convert_type(score, I32)
    return jnp.where(bits < 0, bits ^ jnp.int32(0x7FFFFFFF), bits)


def _kth_largest_key(count_ge, shape, k):
    def body(it, v):
        cand = v + lax.shift_left(jnp.int32(1), 31 - it)
        return jnp.where(count_ge(cand) >= k, cand, v)

    return lax.fori_loop(0, 32, body, jnp.full(shape, INT_MIN, I32))


def _proj_kernel(*refs, kind, rope, prompt):
    it = iter(refs)
    x_ref, g_ref, w_ref = next(it), next(it), next(it)
    bf_ref = next(it) if kind == FOX else None
    cos_ref, sin_ref = (next(it), next(it)) if rope else (None, None)
    qhm_ref, k32_ref, v32_ref, khm_ref, vhm_ref = next(it), next(it), next(it), next(it), next(it)
    h = _rms_bf16(x_ref[0], g_ref[...])
    y = _dot(h, w_ref[...])
    cos = cos_ref[...] if rope else None
    sin = sin_ref[...] if rope else None

    def chunk(c, rot):
        xc = y[:, 128 * c:128 * (c + 1)]
        return _rope128(xc, cos, sin) if rot else xc

    q32hm_ref = next(it) if kind == MOBA else None
    for c in range(N_HEADS // 2):
        qc = chunk(c, rope)
        lo, hi = _halves(qc)
        qhm_ref[0, 2 * c] = (lo * 0.125).astype(BF16)
        qhm_ref[0, 2 * c + 1] = (hi * 0.125).astype(BF16)
        if kind == MOBA:
            q32hm_ref[0, 2 * c] = lo
            q32hm_ref[0, 2 * c + 1] = hi
    base = D_MODEL // 128
    for c in range(N_KV // 2):
        kc = chunk(base + c, rope)
        k32_ref[0, :, 128 * c:128 * (c + 1)] = kc
        vc = chunk(base + N_KV // 2 + c, False)
        v32_ref[0, :, 128 * c:128 * (c + 1)] = vc
        klo, khi = _halves(kc)
        khm_ref[0, 2 * c] = klo.astype(BF16)
        khm_ref[0, 2 * c + 1] = khi.astype(BF16)
        vlo, vhi = _halves(vc)
        vhm_ref[0, 2 * c] = vlo.astype(BF16)
        vhm_ref[0, 2 * c + 1] = vhi.astype(BF16)
    base = (D_MODEL + 2 * KV_W) // 128
    if kind == FOX:
        logf_ref = next(it)
        zf = chunk(base, False)[:, :N_HEADS] + bf_ref[...]
        logf_ref[0] = _log_sigmoid_pair(zf)[0]
    if kind == DSA:
        qihm_ref, ki_ref, wi_ref = next(it), next(it), next(it)
        for c in range(IDX_HEADS // 2):
            lo, hi = _halves(chunk(base + c, True))
            qihm_ref[0, 2 * c] = lo
            qihm_ref[0, 2 * c + 1] = hi
        kic = chunk(base + IDX_HEADS // 2, True)
        ki_ref[0] = kic[:, :IDX_DIM]
        wi_ref[0] = chunk(base + IDX_HEADS // 2 + 1, False)[:, :IDX_HEADS] * (IDX_HEADS ** -0.5 * IDX_DIM ** -0.5)
        if prompt:
            qi3_ref, ki3_ref = next(it), next(it)
            low = _iota(kic.shape, 1) < IDX_DIM
            for c in range(IDX_HEADS // 2):
                xc = chunk(base + c, True)
                for j, z in enumerate((jnp.where(low, xc, 0.0), jnp.where(low, pltpu.roll(xc, IDX_DIM, 1), 0.0))):
                    hi = z.astype(BF16).astype(F32)
                    qi3_ref[0, 2 * c + j] = jnp.concatenate([hi + pltpu.roll(z - hi, IDX_DIM, 1), hi], axis=1).astype(BF16)
            hi = kic.astype(BF16).astype(F32)
            ki3_ref[0] = jnp.concatenate([hi + pltpu.roll(hi, IDX_DIM, 1), kic - hi], axis=1).astype(BF16)
    if prompt:
        vt_ref = next(it)
        vt_ref[0, 0] = y[:, D_MODEL + KV_W:D_MODEL + 2 * KV_W].T.astype(BF16)


def _proj(x, g, w_all, kind, rope_tabs, fox_b, prompt):
    B, T, _ = x.shape
    tm = min(PROJ_TM, T)
    n_cols = w_all.shape[1]
    rope = rope_tabs is not None
    row = lambda b, i: (b, i, 0)
    hm = lambda b, i: (b, 0, i, 0)
    in_specs = [pl.BlockSpec((1, tm, D_MODEL), row), pl.BlockSpec((1, D_MODEL), lambda b, i: (0, 0)),
                pl.BlockSpec((D_MODEL, n_cols), lambda b, i: (0, 0))]
    args = [x, g.reshape(1, D_MODEL), w_all]
    if kind == FOX:
        in_specs.append(pl.BlockSpec((1, N_HEADS), lambda b, i: (0, 0)))
        args.append(fox_b.reshape(1, N_HEADS))
    if rope:
        in_specs += [pl.BlockSpec((tm, 128), lambda b, i: (i, 0))] * 2
        args += list(rope_tabs)
    out_shape = [jax.ShapeDtypeStruct((B, N_HEADS, T, HD), BF16), jax.ShapeDtypeStruct((B, T, KV_W), F32),
                 jax.ShapeDtypeStruct((B, T, KV_W), F32), jax.ShapeDtypeStruct((B, N_KV, T, HD), BF16),
                 jax.ShapeDtypeStruct((B, N_KV, T, HD), BF16)]
    out_specs = [pl.BlockSpec((1, N_HEADS, tm, HD), hm), pl.BlockSpec((1, tm, KV_W), row),
                 pl.BlockSpec((1, tm, KV_W), row), pl.BlockSpec((1, N_KV, tm, HD), hm),
                 pl.BlockSpec((1, N_KV, tm, HD), hm)]
    if kind == MOBA:
        out_shape.append(jax.ShapeDtypeStruct((B, N_HEADS, T, HD), F32))
        out_specs.append(pl.BlockSpec((1, N_HEADS, tm, HD), hm))
    if kind == FOX:
        out_shape.append(jax.ShapeDtypeStruct((B, T, N_HEADS), F32))
        out_specs.append(pl.BlockSpec((1, tm, N_HEADS), row))
    if kind == DSA:
        out_shape += [jax.ShapeDtypeStruct((B, IDX_HEADS, T, IDX_DIM), F32),
                      jax.ShapeDtypeStruct((B, T, IDX_DIM), F32), jax.ShapeDtypeStruct((B, T, IDX_HEADS), F32)]
        out_specs += [pl.BlockSpec((1, IDX_HEADS, tm, IDX_DIM), hm), pl.BlockSpec((1, tm, IDX_DIM), row),
                      pl.BlockSpec((1, tm, IDX_HEADS), row)]
        if prompt:
            out_shape += [jax.ShapeDtypeStruct((B, IDX_HEADS, T, 4 * IDX_DIM), BF16),
                          jax.ShapeDtypeStruct((B, T, 4 * IDX_DIM), BF16)]
            out_specs += [pl.BlockSpec((1, IDX_HEADS, tm, 4 * IDX_DIM), hm), pl.BlockSpec((1, tm, 4 * IDX_DIM), row)]
    if prompt:
        assert tm == KV_TILE
        out_shape.append(jax.ShapeDtypeStruct((B, T // tm, KV_W, tm), BF16))
        out_specs.append(pl.BlockSpec((1, 1, KV_W, tm), lambda b, i: (b, i, 0, 0)))
    return pl.pallas_call(
        functools.partial(_proj_kernel, kind=kind, rope=rope, prompt=prompt),
        grid=(B, T // tm), in_specs=in_specs, out_specs=out_specs, out_shape=out_shape,
        compiler_params=_cp("parallel", "arbitrary"), name=f"proj_{kind}_{'p' if prompt else 's'}")(*args)


def _gelu_tanh(x):
    return 0.5 * x * (1.0 + jnp.tanh(0.7978845608028654 * (x + 0.044715 * x * x * x)))


def _ffn_kernel(*refs, prompt, tiles_per_seq):
    if prompt:
        x_ref, o_ref, wo_ref, g_ref, wa_ref, wb_ref, cw_ref, cb_ref, wd_ref, y_ref, conv_ref, u_ref, carry_ref = refs
    else:
        x_ref, o_ref, wo_ref, g_ref, wa_ref, wb_ref, cw_ref, cb_ref, wd_ref, p2_ref, p1_ref, y_ref, a_ref, u_ref = refs
    x = x_ref[...] + _dot(o_ref[...], wo_ref[...])
    h = _rms_bf16(x, g_ref[...])
    if prompt:
        @pl.when(pl.program_id(0) % tiles_per_seq == 0)
        def _():
            carry_ref[...] = jnp.zeros_like(carry_ref)

        row = _iota((x.shape[0], FF_CHUNK), 0)
    for c in range(D_FF // FF_CHUNK):
        cols = slice(c * FF_CHUNK, (c + 1) * FF_CHUNK)
        a = _dot(h, wa_ref[:, cols])
        gate = _dot(h, wb_ref[:, cols])
        if prompt:
            prev2, prev1 = carry_ref[0:1, cols], carry_ref[1:2, cols]
            back2 = pltpu.roll(a, 2, 0)
            a_m1 = jnp.where(row == 0, prev1, pltpu.roll(a, 1, 0))
            a_m2 = jnp.where(row == 0, prev2, jnp.where(row == 1, prev1, back2))
            carry_ref[0:2, cols] = back2[0:2, :]
            conv_ref[0, :, cols] = back2[0:2, :]
        else:
            a_m2, a_m1 = p2_ref[:, cols], p1_ref[:, cols]
            a_ref[:, cols] = a
        conv = cb_ref[:, cols] + cw_ref[0:1, cols] * a_m2 + cw_ref[1:2, cols] * a_m1 + cw_ref[2:3, cols] * a
        u_ref[:, cols] = (_gelu_tanh(conv) * gate).astype(BF16)
    y_ref[...] = x + _dot(u_ref[...], wd_ref[...])


def _mix_ffn(x2, o2, w_o, g, w_a, w_b, conv_w, conv_b, w_down, seq_len, prev=None):
    M = x2.shape[0]
    prompt = prev is None
    tm = min(FFN_TM, seq_len) if prompt else M
    tps = seq_len // tm if prompt else 1
    xs = pl.BlockSpec((tm, D_MODEL), lambda i: (i, 0))
    whole = lambda shape: pl.BlockSpec(shape, lambda i: (0,) * len(shape), pipeline_mode=pl.Buffered(1))
    in_specs = [xs, xs, whole((D_MODEL, D_MODEL)), whole((1, D_MODEL)), whole((D_MODEL, D_FF)), whole((D_MODEL, D_FF)),
                whole((CONV_W, D_FF)), whole((1, D_FF)), whole((D_FF, D_MODEL))]
    args = [x2, o2, w_o, g.reshape(1, D_MODEL), w_a, w_b, conv_w, conv_b.reshape(1, D_FF), w_down]
    scratch = [pltpu.VMEM((tm, D_FF), BF16)]
    if prompt:
        out_shape = [jax.ShapeDtypeStruct((M, D_MODEL), F32), jax.ShapeDtypeStruct((M // tm, CONV_W - 1, D_FF), F32)]
        out_specs = [xs, pl.BlockSpec((1, CONV_W - 1, D_FF), lambda i: (i, 0, 0))]
        scratch.append(pltpu.VMEM((8, D_FF), F32))
    else:
        ps = pl.BlockSpec((tm, D_FF), lambda i: (i, 0))
        in_specs += [ps, ps]
        args += list(prev)
        out_shape = [jax.ShapeDtypeStruct((M, D_MODEL), F32), jax.ShapeDtypeStruct((M, D_FF), F32)]
        out_specs = [xs, ps]
    y, extra = pl.pallas_call(
        functools.partial(_ffn_kernel, prompt=prompt, tiles_per_seq=tps), grid=(M // tm,),
        in_specs=in_specs, out_specs=out_specs, out_shape=out_shape, scratch_shapes=scratch,
        compiler_params=_cp("arbitrary"), name="ffn_p" if prompt else "ffn_s")(*args)
    return y, (extra[tps - 1::tps] if prompt else extra)


def _final_norm_kernel(x_ref, g_ref, y_ref):
    x = x_ref[...]
    y_ref[...] = x * lax.rsqrt(jnp.mean(x * x, axis=-1, keepdims=True) + RMS_EPS) * g_ref[...]


def _final_norm(x2, g):
    M = x2.shape[0]
    tm = min(1024, M)
    return pl.pallas_call(
        _final_norm_kernel, grid=(M // tm,),
        in_specs=[pl.BlockSpec((tm, D_MODEL), lambda i: (i, 0)), pl.BlockSpec((1, D_MODEL), lambda i: (0, 0))],
        out_specs=pl.BlockSpec((tm, D_MODEL), lambda i: (i, 0)),
        out_shape=jax.ShapeDtypeStruct((M, D_MODEL), F32), compiler_params=_cp("parallel"), name="final_norm")(x2, g.reshape(1, D_MODEL))


def _stick_kernel(q_ref, k_ref, vt_ref, o_ref, r_ref, acc_ref, *, tq, tk, nh):
    q0 = pl.program_id(2) * tq
    qs = [jnp.concatenate([q_ref[0, GROUP * h + j] for j in range(GROUP)], axis=0) for h in range(nh)]
    m = GROUP * tq
    later = _tri(tk, "lt")
    qpos = q0 + _iota((1, m), 1) % tq
    r_ref[...] = jnp.zeros_like(r_ref)
    acc_ref[...] = jnp.zeros_like(acc_ref)

    def tile(j, masked):
        ks = pl.multiple_of(j * tk, tk)
        zs = [_dot_nt(k_ref[0, h, pl.ds(ks, tk), :], qs[h]) for h in range(nh)]
        for h in range(nh):
            head_tile(j, h, zs[h], masked)

    def head_tile(j, h, z, masked):
        vt = vt_ref[0, j, HD * h:HD * (h + 1), :]
        kpos = j * tk + _iota((tk, 1), 0)
        r_all, acc_all = r_ref[h], acc_ref[h]
        chunks = [slice(c0, c0 + Q_CHUNK) for c0 in range(0, m, Q_CHUNK)]
        staged, r_out, acc_out = [], [], []
        for cols in chunks:
            zc = z[:, cols]
            ls = jnp.minimum(zc, 0.0) - jnp.log(1.0 + jnp.exp(-jnp.abs(zc)))
            lk = ls - zc
            past = kpos < qpos[:, cols] if masked else None
            if masked:
                lk = jnp.where(past, lk, 0.0)
            parts = _dot(later, jnp.concatenate(_split2(lk), axis=1))
            staged.append((ls, parts, past))
            r_out.append(r_all[:, cols] + jnp.sum(lk, axis=0, keepdims=True))
        for cols, (ls, parts, past) in zip(chunks, staged):
            w = jnp.exp(ls + parts[:, :Q_CHUNK] + parts[:, Q_CHUNK:] + r_all[:, cols])
            if masked:
                w = jnp.where(past, w, 0.0)
            acc_out.append(acc_all[:, cols] + _dot(vt, w.astype(BF16)))
        r_ref[h] = jnp.concatenate(r_out, axis=1)
        acc_ref[h] = jnp.concatenate(acc_out, axis=1)

    jd = q0 // tk
    tile(jd, True)

    def live():
        return jnp.max(r_ref[...]) > DEAD_LOG

    def body(carry):
        n, _ = carry
        tile(jd - 1 - n, False)
        return n + 1, live()

    lax.while_loop(lambda c: (c[0] < jd) & c[1], body, (jnp.int32(0), live()))
    for h in range(nh):
        o_ref[0, :, KV_W * h:KV_W * (h + 1)] = _group_out_t(acc_ref[h], None, tq)


def _stick_prompt(qhm, khm, vt):
    B, _, T, _ = qhm.shape
    tq, tk, nh = 256, KV_TILE, KV_PER_STEP
    m = GROUP * tq
    return pl.pallas_call(
        functools.partial(_stick_kernel, tq=tq, tk=tk, nh=nh), grid=(B, N_KV // nh, T // tq),
        in_specs=[pl.BlockSpec((1, nh * GROUP, tq, HD), lambda b, g, i: (b, g, i, 0)),
                  pl.BlockSpec((1, nh, T, HD), lambda b, g, i: (b, g, 0, 0)),
                  pl.BlockSpec((1, T // tk, nh * HD, tk), lambda b, g, i: (b, 0, g, 0))],
        out_specs=pl.BlockSpec((1, tq, nh * KV_W), lambda b, g, i: (b, i, g)),
        out_shape=jax.ShapeDtypeStruct((B, T, D_MODEL), BF16),
        scratch_shapes=[pltpu.VMEM((nh, 1, m), F32), pltpu.VMEM((nh, HD, m), F32)],
        compiler_params=_cp("parallel", "parallel", "arbitrary"), name="stick_p")(qhm, khm, vt)


def _cumsum_kernel(x_ref, hi_ref, mid_ref, lo_ref, carry_ref):
    @pl.when(pl.program_id(1) == 0)
    def _():
        carry_ref[...] = jnp.zeros_like(carry_ref)

    n = x_ref.shape[1]
    low = _tri(n, "ge")
    hi, mid, lo = _split3(x_ref[0])
    c = _dot(low, hi) + _dot(low, mid) + _dot(low, lo) + carry_ref[...]
    carry_ref[...] = c[n - 1:n, :]
    hi_ref[0], mid_ref[0], lo_ref[0] = _split3(c)


def _cumsum_time_parts(x):
    B, T, C = x.shape
    n = 256
    spec = pl.BlockSpec((1, n, C), lambda b, i: (b, i, 0))
    return pl.pallas_call(
        _cumsum_kernel, grid=(B, T // n), in_specs=[spec], out_specs=[spec] * 3,
        out_shape=[jax.ShapeDtypeStruct((B, T, C), BF16)] * 3, scratch_shapes=[pltpu.VMEM((1, C), F32)],
        compiler_params=_cp("parallel", "arbitrary"), name="cumsum_time")(x)


def _softmax_scratch_t(n, m):
    return [pltpu.VMEM((n, 1, m), F32), pltpu.VMEM((n, 1, m), F32), pltpu.VMEM((n, HD, m), F32)]


def _causal_kernel(q_ref, k_ref, vt_ref, o_ref, m_ref, l_ref, acc_ref, *, tq, tk, nh):
    q0 = pl.program_id(2) * tq
    qs = [jnp.concatenate([q_ref[0, GROUP * h + j] for j in range(GROUP)], axis=0) for h in range(nh)]
    m = GROUP * tq
    _init_softmax(m_ref, l_ref, acc_ref)
    qpos = q0 + _iota((1, m), 1) % tq

    def tile(j, valid):
        ks = pl.multiple_of(j * tk, tk)
        s = [_dot_nt(k_ref[0, h, pl.ds(ks, tk), :], qs[h]) for h in range(nh)]
        for h in range(nh):
            _online_t(s[h], valid, vt_ref[0, j, HD * h:HD * (h + 1), :], m_ref, l_ref, acc_ref, h)

    jd = q0 // tk
    tile(jd, (jd * tk + _iota((tk, 1), 0)) <= qpos)

    def body(n, carry):
        tile(n, None)
        return carry

    lax.fori_loop(0, jd, body, 0)
    for h in range(nh):
        o_ref[0, :, KV_W * h:KV_W * (h + 1)] = _group_out_t(acc_ref[h], l_ref[h], tq)


def _fox_prompt(qhm, khm, vt, cum_parts):
    B, _, T, _ = qhm.shape
    tq, tk = 256, KV_TILE
    d_aug = 2 * HD
    c = jnp.stack(cum_parts, axis=-1).reshape(B, T, N_KV, FOX_AUG).transpose(0, 2, 1, 3)
    k_aug = jnp.concatenate([khm, c, jnp.zeros((B, N_KV, T, d_aug - HD - FOX_AUG), BF16)], axis=-1)
    pat = np.zeros((N_HEADS, d_aug - HD), np.float32)
    for h in range(N_HEADS):
        pat[h, 3 * (h % GROUP):3 * (h % GROUP) + 3] = -1.0
    q_aug = jnp.concatenate([qhm, jnp.broadcast_to(jnp.asarray(pat, BF16)[None, :, None, :], (B, N_HEADS, T, d_aug - HD))], axis=-1)
    m = GROUP * tq
    return pl.pallas_call(
        functools.partial(_causal_kernel, tq=tq, tk=tk, nh=KV_PER_STEP), grid=(B, N_KV // KV_PER_STEP, T // tq),
        in_specs=[pl.BlockSpec((1, KV_PER_STEP * GROUP, tq, d_aug), lambda b, g, i: (b, g, i, 0)),
                  pl.BlockSpec((1, KV_PER_STEP, T, d_aug), lambda b, g, i: (b, g, 0, 0)),
                  pl.BlockSpec((1, T // tk, KV_PER_STEP * HD, tk), lambda b, g, i: (b, 0, g, 0))],
        out_specs=pl.BlockSpec((1, tq, KV_PER_STEP * KV_W), lambda b, g, i: (b, i, g)),
        out_shape=jax.ShapeDtypeStruct((B, T, D_MODEL), BF16),
        scratch_shapes=_softmax_scratch_t(KV_PER_STEP, m),
        compiler_params=_cp("parallel", "parallel", "arbitrary"), name="fox_p")(q_aug, k_aug, vt)


def _block_mean_kernel(k_ref, o_ref):
    k = k_ref[0]
    nb = k.shape[0] // MOBA_BLOCK
    o_ref[0] = jnp.sum(k.reshape(nb, MOBA_BLOCK, KV_W), axis=1) * (1.0 / MOBA_BLOCK)


def _block_mean(k32):
    B, T, _ = k32.shape
    nb = T // MOBA_BLOCK
    return pl.pallas_call(
        _block_mean_kernel, grid=(B,), in_specs=[pl.BlockSpec((1, T, KV_W), lambda b: (b, 0, 0))],
        out_specs=pl.BlockSpec((1, nb, KV_W), lambda b: (b, 0, 0)),
        out_shape=jax.ShapeDtypeStruct((B, nb, KV_W), F32), compiler_params=_cp("parallel"), name="block_mean")(k32)


def _moba_kernel(q_ref, q32_ref, km_ref, k_ref, vt_ref, o_ref, sel_ref, m_ref, l_ref, acc_ref, *, tq, nb, nh):
    tk = MOBA_BLOCK
    q0 = pl.program_id(2) * tq
    own = q0 // tk
    stack = lambda ref, h: jnp.concatenate([ref[0, GROUP * h + j] for j in range(GROUP)], axis=0)
    qs = [stack(q_ref, h) for h in range(nh)]
    m = GROUP * tq
    _init_softmax(m_ref, l_ref, acc_ref)
    for h in range(nh):
        sel_ref[h] = _topk_rank_select_t(_dot_nt_f32(km_ref[0, h], stack(q32_ref, h)), own, MOBA_TOPK, nb)
    qpos = q0 + _iota((1, m), 1) % tq

    def tile(j, valid, biased):
        ks = pl.multiple_of(j * tk, tk)
        s = [_dot_nt(k_ref[0, h, pl.ds(ks, tk), :], qs[h]) for h in range(nh)]
        for h in range(nh):
            sh = s[h]
            if biased:
                sh = sh + jnp.where(sel_ref[h, pl.ds(j, 1), :] > 0.5, 0.0, NEG)
            _online_t(sh, valid, vt_ref[0, j, HD * h:HD * (h + 1), :], m_ref, l_ref, acc_ref, h)

    tile(own, (own * tk + _iota((tk, 1), 0)) <= qpos, False)

    def body(j, carry):
        tile(j, None, True)
        return carry

    lax.fori_loop(0, own, body, 0)
    for h in range(nh):
        o_ref[0, :, KV_W * h:KV_W * (h + 1)] = _group_out_t(acc_ref[h], l_ref[h], tq)


def _moba_prompt(qhm, q32hm, kmean, khm, vt):
    B, _, T, _ = qhm.shape
    tq, nh = 256, KV_PER_STEP
    nb = T // MOBA_BLOCK
    m = GROUP * tq
    km = kmean.reshape(B, nb, N_KV, HD).transpose(0, 2, 1, 3)
    qspec = pl.BlockSpec((1, nh * GROUP, tq, HD), lambda b, g, i: (b, g, i, 0))
    return pl.pallas_call(
        functools.partial(_moba_kernel, tq=tq, nb=nb, nh=nh), grid=(B, N_KV // nh, T // tq),
        in_specs=[qspec, qspec, pl.BlockSpec((1, nh, nb, HD), lambda b, g, i: (b, g, 0, 0)),
                  pl.BlockSpec((1, nh, T, HD), lambda b, g, i: (b, g, 0, 0)),
                  pl.BlockSpec((1, nb, nh * HD, MOBA_BLOCK), lambda b, g, i: (b, 0, g, 0))],
        out_specs=pl.BlockSpec((1, tq, nh * KV_W), lambda b, g, i: (b, i, g)),
        out_shape=jax.ShapeDtypeStruct((B, T, D_MODEL), BF16),
        scratch_shapes=[pltpu.VMEM((nh, nb, m), F32)] + _softmax_scratch_t(nh, m),
        compiler_params=_cp("parallel", "parallel", "arbitrary"), name="moba_p")(qhm, q32hm, km, khm, vt)


def _dsa_kernel(q_ref, qi_ref, wi_ref, ki_ref, k_ref, vt_ref, o_ref, key_ref, m_ref, l_ref, acc_ref, *, tq, n_keep):
    tk = KV_TILE
    q0 = pl.program_id(1) * tq
    n_proc = (q0 + tq + tk - 1) // tk
    qpos = q0 + _iota((1, tq), 1)

    def visible(c):
        return (c * tk + _iota((tk, 1), 0)) <= qpos

    qi3 = jnp.concatenate([qi_ref[0, i] for i in range(IDX_HEADS)], axis=0)

    def score_tile(c, carry):
        ks = pl.multiple_of(c * tk, tk)
        dots = _dot_nt(ki_ref[0, pl.ds(ks, tk), :], qi3)
        sc = jnp.zeros((tk, tq), F32)
        for i in range(IDX_HEADS):
            sc = sc + wi_ref[0, i:i + 1, :] * jnp.maximum(dots[:, i * tq:(i + 1) * tq], 0.0)
        key_ref[c] = _order_key(jnp.where(visible(c), sc + 0.0, -jnp.inf))
        return carry

    lax.fori_loop(0, n_proc, score_tile, 0)

    def count(pred_of_tile):
        def body(c, part):
            return part + jnp.where(pred_of_tile(key_ref[c]), 1.0, 0.0)

        return jnp.sum(lax.fori_loop(0, n_proc, body, jnp.zeros((tk, tq), F32)), axis=0, keepdims=True)

    thr = _kth_largest_key(lambda cand: count(lambda k: k >= cand), (1, tq), n_keep)
    need = n_keep - count(lambda k: k > thr)
    _init_softmax(m_ref, l_ref, acc_ref)
    before = _tri(tk, "gt")
    qs = [jnp.concatenate([q_ref[0, GROUP * g + j] for j in range(GROUP)], axis=0) for g in range(N_KV)]

    def attend(c, ties_seen):
        ks = pl.multiple_of(c * tk, tk)
        key = key_ref[c]
        tied = jnp.where(key == thr, 1.0, 0.0)
        rank = _dot(before, tied.astype(BF16)) + ties_seen
        keep = (key > thr) | ((key == thr) & (rank < need))
        bias = jnp.where(keep & visible(c), 0.0, NEG)
        bias = jnp.concatenate([bias] * GROUP, axis=1)
        s = [_dot_nt(k_ref[0, g, pl.ds(ks, tk), :], qs[g]) for g in range(N_KV)]
        for g in range(N_KV):
            _online_t(s[g], None, vt_ref[0, c, HD * g:HD * (g + 1), :], m_ref, l_ref, acc_ref, g, bias=bias)
        return ties_seen + jnp.sum(tied, axis=0, keepdims=True)

    lax.fori_loop(0, n_proc, attend, jnp.zeros((1, tq), F32))
    for g in range(N_KV):
        o_ref[0, :, KV_W * g:KV_W * (g + 1)] = _group_out_t(acc_ref[g], l_ref[g], tq)


def _dsa_prompt(qhm, qi3, wi, ki3, khm, vt):
    B, _, T, _ = qhm.shape
    tq = 128
    m = GROUP * tq
    return pl.pallas_call(
        functools.partial(_dsa_kernel, tq=tq, n_keep=min(IDX_TOPK, T // 4)), grid=(B, T // tq),
        in_specs=[pl.BlockSpec((1, N_HEADS, tq, HD), lambda b, i: (b, 0, i, 0)),
                  pl.BlockSpec((1, IDX_HEADS, tq, 4 * IDX_DIM), lambda b, i: (b, 0, i, 0)),
                  pl.BlockSpec((1, IDX_HEADS, tq), lambda b, i: (b, 0, i)),
                  pl.BlockSpec((1, T, 4 * IDX_DIM), lambda b, i: (b, 0, 0)),
                  pl.BlockSpec((1, N_KV, T, HD), lambda b, i: (b, 0, 0, 0)),
                  pl.BlockSpec((1, T // KV_TILE, KV_W, KV_TILE), lambda b, i: (b, 0, 0, 0))],
        out_specs=pl.BlockSpec((1, tq, D_MODEL), lambda b, i: (b, i, 0)),
        out_shape=jax.ShapeDtypeStruct((B, T, D_MODEL), BF16),
        scratch_shapes=[pltpu.VMEM((T // KV_TILE, KV_TILE, tq), I32)] + _softmax_scratch_t(N_KV, m),
        compiler_params=_cp("parallel", "arbitrary"), name="dsa_p")(qhm, qi3, wi.transpose(0, 2, 1), ki3, khm, vt)


def _page_specs(layer, n_pages, nps, reverse):
    def spec(i):
        def idx(b, s, pt):
            p = s * nps + i
            return (layer, pt[b, n_pages - 1 - p if reverse else p], 0, 0)

        return pl.BlockSpec((None, None, KV_W, PAGE), idx)

    return [spec(i) for i in range(nps)]


def _pool_specs(rows, n_pages, nps, reverse):
    def spec(i):
        def idx(b, s, pt):
            p = s * nps + i
            return (pt[b, n_pages - 1 - p if reverse else p], 0, 0)

        return pl.BlockSpec((None, rows, PAGE), idx)

    return [spec(i) for i in range(nps)]


def _row_spec(width):
    return pl.BlockSpec((1, 1, width), lambda b, s, pt: (b, 0, 0))


def _head_spec(width):
    return pl.BlockSpec((1, N_HEADS, width), lambda b, s, pt: (b, 0, 0))


def _softmax_scratch():
    return [pltpu.VMEM((N_HEADS, 1), F32), pltpu.VMEM((N_HEADS, 1), F32), pltpu.VMEM((N_HEADS, KV_W), F32)]


def _new_key_start(q, knew, vnew, m_ref, l_ref, acc_ref):
    z = jnp.sum(q.astype(F32) * knew.astype(BF16).astype(F32), axis=1, keepdims=True)
    m_ref[...] = z
    l_ref[...] = jnp.ones_like(z)
    acc_ref[...] = jnp.broadcast_to(vnew.astype(BF16).astype(F32), acc_ref.shape)


def _cat_pages(refs):
    return jnp.concatenate([r[...].astype(BF16) for r in refs], axis=1)


def _later_sums(x, parts):
    n = x.shape[1] // PAGE
    stack = jnp.concatenate([p[:, PAGE * i:PAGE * (i + 1)] for p in parts(x) for i in range(n)], axis=0)
    out = _dot(stack, _tri(PAGE, "gt"))
    rows = x.shape[0]
    tot = None
    for k in range(out.shape[0] // (n * rows)):
        part = jnp.concatenate([out[(k * n + i) * rows:(k * n + i + 1) * rows] for i in range(n)], axis=1)
        tot = part if tot is None else tot + part
    return tot


def _page_totals(x):
    return [jnp.sum(x[:, PAGE * i:PAGE * (i + 1)], axis=1, keepdims=True) for i in range(x.shape[1] // PAGE)]


def _per_page(cols):
    return jnp.concatenate([jnp.broadcast_to(c, (c.shape[0], PAGE)) for c in cols], axis=1)


def _online_pages(s, valid, vcat, m_ref, l_ref, acc_ref):
    if valid is not None:
        s = jnp.where(valid, s, NEG)
    m_old = m_ref[...]
    m_new = jnp.maximum(m_old, jnp.max(s, axis=1, keepdims=True))
    alpha = jnp.exp(m_old - m_new)
    p = jnp.exp(s - m_new)
    if valid is not None:
        p = jnp.where(valid, p, 0.0)
    l_ref[...] = alpha * l_ref[...] + jnp.sum(p, axis=1, keepdims=True)
    acc_ref[...] = alpha * acc_ref[...] + _dot_nt(p.astype(BF16), vcat)
    m_ref[...] = m_new


def _dec_stick_kernel(pt_ref, q_ref, *refs, nps):
    k_refs, v_refs = refs[:nps], refs[nps:2 * nps]
    o_ref, r_ref, acc_ref = refs[2 * nps:]
    step = pl.program_id(1)

    @pl.when(step == 0)
    def _():
        r_ref[...] = jnp.zeros_like(r_ref)
        acc_ref[...] = jnp.zeros_like(acc_ref)

    ls, lk = _log_sigmoid_pair(_dot(q_ref[0], _cat_pages(k_refs)))
    after, run = [], r_ref[...]
    for tot in _page_totals(lk):
        after.append(run)
        run = run + tot
    w = jnp.exp(ls + _later_sums(lk, _split2) + _per_page(after))
    acc = acc_ref[...] + _dot_nt(w.astype(BF16), _cat_pages(v_refs))
    r_ref[...], acc_ref[...] = run, acc

    @pl.when(step == pl.num_programs(1) - 1)
    def _():
        o_ref[0] = acc


def _dec_call(kernel, name, grid_spec, DB):
    return pl.pallas_call(kernel, grid_spec=grid_spec, out_shape=jax.ShapeDtypeStruct((DB, N_HEADS, KV_W), F32),
                          compiler_params=_cp("parallel", "arbitrary"), name=name)


def _dec_stick(qx, kt, vt, page_table, layer):
    DB, n_pages = page_table.shape
    nps = math.gcd(n_pages, DEC_PAGES)
    pages = _page_specs(layer, n_pages, nps, True)
    grid_spec = pltpu.PrefetchScalarGridSpec(
        num_scalar_prefetch=1, grid=(DB, n_pages // nps), in_specs=[_head_spec(KV_W)] + pages + pages,
        out_specs=_head_spec(KV_W), scratch_shapes=[pltpu.VMEM((N_HEADS, 1), F32), pltpu.VMEM((N_HEADS, KV_W), F32)])
    return _dec_call(functools.partial(_dec_stick_kernel, nps=nps), "stick_s", grid_spec, DB)(
        page_table, qx, *([kt] * nps), *([vt] * nps))


def _dec_fox_kernel(pt_ref, q_ref, *refs, nps):
    k_refs, v_refs, lf_refs = refs[:nps], refs[nps:2 * nps], refs[2 * nps:3 * nps]
    kn_ref, vn_ref, lfn_ref, o_ref, s_ref, m_ref, l_ref, acc_ref = refs[3 * nps:]
    step = pl.program_id(1)

    @pl.when(step == 0)
    def _():
        s_ref[...] = lfn_ref[0]
        _new_key_start(q_ref[0], kn_ref[0], vn_ref[0], m_ref, l_ref, acc_ref)

    lf = jnp.concatenate([r[...] for r in lf_refs], axis=1)
    after, run = [], s_ref[...]
    for tot in _page_totals(lf):
        after.append(run)
        run = run + tot
    s_ref[...] = run
    s = _dot(q_ref[0], _cat_pages(k_refs)) + _later_sums(lf, _split3) + _per_page(after)
    _online_pages(s, None, _cat_pages(v_refs), m_ref, l_ref, acc_ref)

    @pl.when(step == pl.num_programs(1) - 1)
    def _():
        o_ref[0] = acc_ref[...] / l_ref[...]


def _dec_fox(qx, kt, vt, logf_t, knew, vnew, lfnew, page_table, layer):
    DB, n_pages = page_table.shape
    nps = math.gcd(n_pages, DEC_PAGES)
    pages = _page_specs(layer, n_pages, nps, True)
    grid_spec = pltpu.PrefetchScalarGridSpec(
        num_scalar_prefetch=1, grid=(DB, n_pages // nps),
        in_specs=[_head_spec(KV_W)] + pages + pages + _pool_specs(N_HEADS, n_pages, nps, True)
        + [_row_spec(KV_W), _row_spec(KV_W), _head_spec(1)],
        out_specs=_head_spec(KV_W), scratch_shapes=[pltpu.VMEM((N_HEADS, 1), F32)] + _softmax_scratch())
    return _dec_call(functools.partial(_dec_fox_kernel, nps=nps), "fox_s", grid_spec, DB)(
        page_table, qx, *([kt] * nps), *([vt] * nps), *([logf_t] * nps), knew, vnew, lfnew)


def _dec_kmean_kernel(pt_ref, *refs, nps):
    k_refs, o_ref = refs[:nps], refs[nps]
    step = pl.program_id(1)

    @pl.when(step == 0)
    def _():
        o_ref[...] = jnp.zeros_like(o_ref)

    ones = jnp.ones((8, PAGE), BF16)
    for i in range(nps):
        hi, mid, lo = _split3(k_refs[i][...])
        tot = _dot_nt(ones, hi) + _dot_nt(ones, mid) + _dot_nt(ones, lo)
        blk = (step * nps + i) // (MOBA_BLOCK // PAGE)
        o_ref[0, pl.ds(blk, 1), :] = o_ref[0, pl.ds(blk, 1), :] + tot[0:1, :] * (1.0 / MOBA_BLOCK)


def _dec_kmean(kt, page_table, layer):
    DB, n_pages = page_table.shape
    nps = math.gcd(n_pages, DEC_PAGES)
    nb = n_pages * PAGE // MOBA_BLOCK
    grid_spec = pltpu.PrefetchScalarGridSpec(
        num_scalar_prefetch=1, grid=(DB, n_pages // nps), in_specs=_page_specs(layer, n_pages, nps, False),
        out_specs=pl.BlockSpec((1, nb, KV_W), lambda b, s, pt: (b, 0, 0)))
    return pl.pallas_call(
        functools.partial(_dec_kmean_kernel, nps=nps), grid_spec=grid_spec,
        out_shape=jax.ShapeDtypeStruct((DB, nb, KV_W), F32),
        compiler_params=_cp("parallel", "arbitrary"), name="kmean_s")(page_table, *([kt] * nps))


def _dec_moba_kernel(pt_ref, q_ref, q32_ref, km_ref, *refs, nps, nb):
    k_refs, v_refs = refs[:nps], refs[nps:2 * nps]
    kn_ref, vn_ref, o_ref, sel_ref, m_ref, l_ref, acc_ref = refs[2 * nps:]
    step = pl.program_id(1)

    @pl.when(step == 0)
    def _():
        sel_ref[...] = _topk_rank_select(_dot_nt_f32(q32_ref[0], km_ref[0]), nb, MOBA_TOPK, nb)
        _new_key_start(q_ref[0], kn_ref[0], vn_ref[0], m_ref, l_ref, acc_ref)

    lane = _iota((N_HEADS, nb), 1)
    sel = sel_ref[...]
    bias = []
    for i in range(nps):
        blk = (step * nps + i) // (MOBA_BLOCK // PAGE)
        picked = jnp.sum(jnp.where(lane == blk, sel, 0.0), axis=1, keepdims=True) > 0.5
        bias.append(jnp.where(picked, 0.0, NEG))
    s = _dot(q_ref[0], _cat_pages(k_refs)) + _per_page(bias)
    _online_pages(s, None, _cat_pages(v_refs), m_ref, l_ref, acc_ref)

    @pl.when(step == pl.num_programs(1) - 1)
    def _():
        o_ref[0] = acc_ref[...] / l_ref[...]


def _dec_moba(qx, q32x, kmean, kt, vt, knew, vnew, page_table, layer):
    DB, n_pages = page_table.shape
    nps = math.gcd(n_pages, DEC_PAGES)
    nb = kmean.shape[1]
    pages = _page_specs(layer, n_pages, nps, False)
    grid_spec = pltpu.PrefetchScalarGridSpec(
        num_scalar_prefetch=1, grid=(DB, n_pages // nps),
        in_specs=[_head_spec(KV_W), _head_spec(KV_W), pl.BlockSpec((1, nb, KV_W), lambda b, s, pt: (b, 0, 0))]
        + pages + pages + [_row_spec(KV_W), _row_spec(KV_W)],
        out_specs=_head_spec(KV_W), scratch_shapes=[pltpu.VMEM((N_HEADS, nb), F32)] + _softmax_scratch())
    return _dec_call(functools.partial(_dec_moba_kernel, nps=nps, nb=nb), "moba_s", grid_spec, DB)(
        page_table, qx, q32x, kmean, *([kt] * nps), *([vt] * nps), knew, vnew)


def _dec_score_kernel(pt_ref, qi_ref, wi_ref, *refs, nps, n_pages):
    kidx_refs = refs[:nps]
    kin_ref, o_ref = refs[nps:]
    step = pl.program_id(1)
    qi, wi = qi_ref[0], wi_ref[0]

    @pl.when(step == 0)
    def _():
        o_ref[...] = jnp.full(o_ref.shape, -jnp.inf, F32)
        dn = jnp.sum(qi * kin_ref[0], axis=1, keepdims=True)
        sn = jnp.sum(wi * jnp.maximum(dn, 0.0), axis=0, keepdims=True) + 0.0
        o_ref[0, n_pages:n_pages + 1, :] = jnp.where(_iota((1, PAGE), 1) == 0, sn, -jnp.inf)

    for i in range(nps):
        dots = _dot_f32(qi, kidx_refs[i][...])
        o_ref[0, pl.ds(step * nps + i, 1), :] = jnp.sum(wi * jnp.maximum(dots, 0.0), axis=0, keepdims=True) + 0.0


def _dec_scores(qi3, wi3, kidx_t, kinew, page_table):
    DB, n_pages = page_table.shape
    assert n_pages < PAGE
    nps = math.gcd(n_pages, DEC_PAGES)
    grid_spec = pltpu.PrefetchScalarGridSpec(
        num_scalar_prefetch=1, grid=(DB, n_pages // nps),
        in_specs=[pl.BlockSpec((1, IDX_HEADS, IDX_DIM), lambda b, s, pt: (b, 0, 0)),
                  pl.BlockSpec((1, IDX_HEADS, 1), lambda b, s, pt: (b, 0, 0))]
        + _pool_specs(IDX_DIM, n_pages, nps, False) + [_row_spec(IDX_DIM)],
        out_specs=pl.BlockSpec((1, PAGE, PAGE), lambda b, s, pt: (b, 0, 0)))
    return pl.pallas_call(
        functools.partial(_dec_score_kernel, nps=nps, n_pages=n_pages), grid_spec=grid_spec,
        out_shape=jax.ShapeDtypeStruct((DB, PAGE, PAGE), F32),
        compiler_params=_cp("parallel", "arbitrary"), name="dsa_score_s")(page_table, qi3, wi3, *([kidx_t] * nps), kinew)


def _dec_select_kernel(sc_ref, o_ref, *, n_keys, n_keep):
    key = _order_key(sc_ref[0])

    def total(x):
        return jnp.sum(jnp.sum(x, axis=1, keepdims=True), axis=0, keepdims=True)

    thr = _kth_largest_key(lambda cand: total(jnp.where(key >= cand, 1.0, 0.0)), (1, 1), n_keep)
    need = n_keep - total(jnp.where(key > thr, 1.0, 0.0))
    tied = jnp.where(key == thr, 1.0, 0.0)
    in_row = _dot(tied.astype(BF16), _tri(PAGE, "lt"))
    row_tot = jnp.broadcast_to(jnp.sum(tied, axis=1, keepdims=True), tied.shape).astype(BF16)
    rank = in_row + _dot(_tri(PAGE, "gt"), row_tot)
    pos = _iota(key.shape, 0) * PAGE + _iota(key.shape, 1)
    keep = ((key > thr) | ((key == thr) & (rank < need))) & (pos < n_keys)
    o_ref[0] = jnp.where(keep, 1.0, 0.0)


def _dec_select(scores, n_keys):
    DB = scores.shape[0]
    spec = pl.BlockSpec((1, PAGE, PAGE), lambda b: (b, 0, 0))
    return pl.pallas_call(
        functools.partial(_dec_select_kernel, n_keys=n_keys, n_keep=min(IDX_TOPK, n_keys // 4)), grid=(DB,),
        in_specs=[spec], out_specs=spec, out_shape=jax.ShapeDtypeStruct(scores.shape, F32),
        compiler_params=_cp("parallel"), name="dsa_select_s")(scores)


def _dec_dsa_kernel(pt_ref, q_ref, *refs, nps, n_pages):
    k_refs, v_refs = refs[:nps], refs[nps:2 * nps]
    sel_ref, kn_ref, vn_ref, o_ref, m_ref, l_ref, acc_ref = refs[2 * nps:]
    step = pl.program_id(1)
    q = q_ref[0]

    @pl.when(step == 0)
    def _():
        zn = jnp.sum(q.astype(F32) * kn_ref[0].astype(BF16).astype(F32), axis=1, keepdims=True)
        keep = jnp.broadcast_to(sel_ref[0, n_pages:n_pages + 1, 0:1] > 0.5, zn.shape)
        m_ref[...] = jnp.where(keep, zn, NEG)
        l_ref[...] = jnp.where(keep, 1.0, 0.0)
        acc_ref[...] = jnp.where(keep, jnp.broadcast_to(vn_ref[0].astype(BF16).astype(F32), acc_ref.shape), 0.0)

    keep = jnp.concatenate([sel_ref[0, pl.ds(step * nps + i, 1), :] for i in range(nps)], axis=1)
    s = _dot(q, _cat_pages(k_refs))
    _online_pages(s, jnp.broadcast_to(keep > 0.5, s.shape), _cat_pages(v_refs), m_ref, l_ref, acc_ref)

    @pl.when(step == pl.num_programs(1) - 1)
    def _():
        o_ref[0] = acc_ref[...] / l_ref[...]


def _dec_dsa(qx, kt, vt, sel, knew, vnew, page_table, layer):
    DB, n_pages = page_table.shape
    nps = math.gcd(n_pages, DEC_PAGES)
    pages = _page_specs(layer, n_pages, nps, False)
    grid_spec = pltpu.PrefetchScalarGridSpec(
        num_scalar_prefetch=1, grid=(DB, n_pages // nps),
        in_specs=[_head_spec(KV_W)] + pages + pages
        + [pl.BlockSpec((1, PAGE, PAGE), lambda b, s, pt: (b, 0, 0)), _row_spec(KV_W), _row_spec(KV_W)],
        out_specs=_head_spec(KV_W), scratch_shapes=_softmax_scratch())
    return _dec_call(functools.partial(_dec_dsa_kernel, nps=nps, n_pages=n_pages), "dsa_s", grid_spec, DB)(
        page_table, qx, *([kt] * nps), *([vt] * nps), sel, knew, vnew)


def _rope_tables(pos):
    half = HD // 2
    inv_freq = jnp.power(ROPE_THETA, -jnp.arange(half, dtype=F32) * (2.0 / HD))
    ang = pos.astype(F32)[:, None] * inv_freq[None, :]
    cos, sin = jnp.cos(ang), jnp.sin(ang)
    return jnp.concatenate([cos] * 4, axis=1), jnp.concatenate([-sin, sin] * 2, axis=1)


def _pad_cols(w, mult=128):
    pad = (-w.shape[1]) % mult
    return jnp.pad(w, ((0, 0), (0, pad))) if pad else w


def _layer_weights(kind, w_qkv, fox_w_f, idx_w_q, idx_w_k, idx_w_w):
    cols = [w_qkv]
    if kind == FOX:
        cols.append(_pad_cols(fox_w_f))
    if kind == DSA:
        cols += [idx_w_q, _pad_cols(idx_w_k), _pad_cols(idx_w_w)]
    return jnp.concatenate(cols, axis=1).astype(BF16)


_HEAD_ONEHOT = np.arange(N_HEADS)[:, None] // GROUP == np.arange(N_KV)[None, :]


def _expand_heads(q, dtype):
    m = jnp.asarray(_HEAD_ONEHOT, q.dtype)
    return (q[:, :, None, :] * m[None, :, :, None]).reshape(q.shape[0], N_HEADS, KV_W).astype(dtype)


def _collapse_heads(ox):
    m = jnp.asarray(_HEAD_ONEHOT, ox.dtype)
    o = jnp.sum(ox.reshape(ox.shape[0], N_HEADS, N_KV, HD) * m[None, :, :, None], axis=2)
    return o.reshape(ox.shape[0], D_MODEL).astype(BF16)


def _pages_t(cache):
    depth, n_pool = cache.shape[:2]
    return cache.transpose(0, 1, 3, 4, 2).reshape(depth, n_pool, KV_W, PAGE)


def kernel(x_prompt, x_sample, cache_k, cache_v, cache_logf, cache_kidx, state_conv, page_table, norm_mix, norm_ffn, norm_final, w_qkv, w_o, fox_w_f, fox_b_f, idx_w_q, idx_w_k, idx_w_w, ffn_w_a, ffn_w_b, ffn_conv_w, ffn_conv_b, ffn_w_down):
    B, T, _ = x_prompt.shape
    DB = x_sample.shape[0]
    depth = cache_k.shape[0]
    n_pages = page_table.shape[1]
    past = n_pages * PAGE
    assert T % MOBA_BLOCK == 0 and past % MOBA_BLOCK == 0 and x_sample.shape[1] == 1
    cache_kt, cache_vt = _pages_t(cache_k), _pages_t(cache_v)
    rope_p = _rope_tables(jnp.arange(T, dtype=I32))
    rope_s = _rope_tables(jnp.full((DB,), past, I32))

    xp = x_prompt
    xs = x_sample.reshape(1, DB, D_MODEL)
    new_k_p, new_v_p, new_k_s, new_v_s, conv_p, conv_s = [], [], [], [], [], []
    logf_p = logf_s = kidx_p = kidx_s = None
    for i in range(depth):
        kind = i % 4
        rope = kind in (MOBA, DSA)
        w_all = _layer_weights(kind, w_qkv[i], fox_w_f, idx_w_q, idx_w_k, idx_w_w)
        wo_b = w_o[i].astype(BF16)
        wa_b, wb_b, wd_b = ffn_w_a[i].astype(BF16), ffn_w_b[i].astype(BF16), ffn_w_down[i].astype(BF16)

        outs = _proj(xp, norm_mix[i], w_all, kind, rope_p if rope else None, fox_b_f, True)
        qhm, k32, v32, khm, vhm = outs[:5]
        vt = outs[-1]
        new_k_p.append(k32.reshape(B, T, N_KV, HD))
        new_v_p.append(v32.reshape(B, T, N_KV, HD))
        if kind == STICK:
            o = _stick_prompt(qhm, khm, vt)
        elif kind == FOX:
            logf_p = outs[5]
            o = _fox_prompt(qhm, khm, vt, _cumsum_time_parts(logf_p))
        elif kind == MOBA:
            o = _moba_prompt(qhm, outs[5], _block_mean(k32), khm, vt)
        else:
            kidx_p, wi, qi3, ki3 = outs[6:10]
            o = _dsa_prompt(qhm, qi3, wi, ki3, khm, vt)
        x2, rows = _mix_ffn(xp.reshape(B * T, D_MODEL), o.reshape(B * T, D_MODEL), wo_b, norm_ffn[i], wa_b, wb_b,
                            ffn_conv_w[i], ffn_conv_b[i], wd_b, T)
        conv_p.append(rows)
        xp = x2.reshape(B, T, D_MODEL)

        outs = _proj(xs, norm_mix[i], w_all, kind, rope_s if rope else None, fox_b_f, False)
        qhm, k32, v32 = outs[:3]
        new_k_s.append(k32.reshape(DB, 1, N_KV, HD))
        new_v_s.append(v32.reshape(DB, 1, N_KV, HD))
        qx = _expand_heads(qhm[0].transpose(1, 0, 2), BF16)
        knew, vnew = k32.reshape(DB, 1, KV_W), v32.reshape(DB, 1, KV_W)
        if kind == STICK:
            ox = _dec_stick(qx, cache_kt, cache_vt, page_table, i)
        elif kind == FOX:
            lf = outs[5][0]
            logf_s = lf.reshape(DB, 1, N_HEADS)
            ox = _dec_fox(qx, cache_kt, cache_vt, cache_logf.transpose(0, 2, 1), knew, vnew,
                          lf.reshape(DB, N_HEADS, 1), page_table, i)
        elif kind == MOBA:
            q32x = _expand_heads(outs[5][0].transpose(1, 0, 2), F32)
            ox = _dec_moba(qx, q32x, _dec_kmean(cache_kt, page_table, i), cache_kt, cache_vt, knew, vnew, page_table, i)
        else:
            qihm, ki, wi = outs[5:8]
            kidx_s = ki.reshape(DB, 1, IDX_DIM)
            scores = _dec_scores(qihm[0].transpose(1, 0, 2), wi[0].reshape(DB, IDX_HEADS, 1),
                                 cache_kidx.transpose(0, 2, 1), kidx_s, page_table)
            ox = _dec_dsa(qx, cache_kt, cache_vt, _dec_select(scores, past + 1), knew, vnew, page_table, i)
        st = state_conv[i]
        s2, a_new = _mix_ffn(xs.reshape(DB, D_MODEL), _collapse_heads(ox), wo_b, norm_ffn[i], wa_b, wb_b,
                             ffn_conv_w[i], ffn_conv_b[i], wd_b, 1, prev=(st[:, 0], st[:, 1]))
        conv_s.append(jnp.stack([st[:, 1], a_new], axis=1))
        xs = s2.reshape(1, DB, D_MODEL)

    y_p = _final_norm(xp.reshape(B * T, D_MODEL), norm_final).reshape(B, T, D_MODEL)
    y_s = _final_norm(xs.reshape(DB, D_MODEL), norm_final).reshape(DB, 1, D_MODEL)
    return (y_p, y_s, jnp.stack(new_k_p), jnp.stack(new_v_p), jnp.stack(new_k_s), jnp.stack(new_v_s),
            logf_p, logf_s, kidx_p, kidx_s, jnp.stack(conv_p), jnp.stack(conv_s))
```

```python
import functools
import math

import jax
import jax.numpy as jnp
import numpy as np
from jax import lax
from jax.experimental import pallas as pl
from jax.experimental.pallas import tpu as pltpu

F32, BF16, I32 = jnp.float32, jnp.bfloat16, jnp.int32

D_MODEL = 1024
N_HEADS = 16
N_KV = 4
GROUP = N_HEADS // N_KV
HD = 64
KV_W = N_KV * HD
D_FF = 2816
CONV_W = 3
PAGE = 128
MOBA_BLOCK = 256
MOBA_TOPK = 3
IDX_HEADS = 8
IDX_DIM = 64
IDX_TOPK = 256
ROPE_THETA = 10000.0
RMS_EPS = 1e-6
STICK, FOX, MOBA, DSA = 0, 1, 2, 3
NEG = -1e30
M_FLOOR = -1e29
DEAD_LOG = -104.0
INT_MIN = -2147483648
VMEM_LIMIT = 48 * 1024 * 1024
FF_CHUNK = 256
FFN_TM = 512
PROJ_TM = 256
KV_TILE = 256
DEC_PAGES = 32
KV_PER_STEP = 4
Q_CHUNK = 128
FOX_AUG = 3 * GROUP


def _cp(*sem):
    return pltpu.CompilerParams(dimension_semantics=sem, vmem_limit_bytes=VMEM_LIMIT)


def _dot(a, b):
    return jnp.dot(a, b, preferred_element_type=F32)


def _dot_nt(a, b):
    return lax.dot_general(a, b, (((1,), (1,)), ((), ())), preferred_element_type=F32)


def _split2(x):
    hi = x.astype(BF16)
    lo = (x - hi.astype(F32)).astype(BF16)
    return hi, lo


def _split3(x):
    hi = x.astype(BF16)
    r = x - hi.astype(F32)
    mid = r.astype(BF16)
    lo = (r - mid.astype(F32)).astype(BF16)
    return hi, mid, lo


def _dot_f32(a, b):
    ah, al = _split2(a)
    bh, bl = _split2(b)
    return _dot(ah, bh) + _dot(al, bh) + _dot(ah, bl)


def _dot_nt_f32(a, b):
    ah, al = _split2(a)
    bh, bl = _split2(b)
    return _dot_nt(ah, bh) + _dot_nt(al, bh) + _dot_nt(ah, bl)


def _log_sigmoid_pair(z):
    l1p = jnp.log(1.0 + jnp.exp(-jnp.abs(z)))
    return jnp.minimum(z, 0.0) - l1p, jnp.minimum(-z, 0.0) - l1p


def _iota(shape, dim):
    return lax.broadcasted_iota(I32, shape, dim)


def _tri(n, kind):
    r, c = _iota((n, n), 0), _iota((n, n), 1)
    m = {"gt": r > c, "lt": r < c, "ge": r >= c}[kind]
    return jnp.where(m, 1.0, 0.0).astype(BF16)


def _rms_bf16(x, g):
    y = x * lax.rsqrt(jnp.mean(x * x, axis=-1, keepdims=True) + RMS_EPS)
    return (y * g).astype(BF16)


def _rope128(xc, cos, sin):
    lane = _iota(xc.shape, 1)
    first = (lane % HD) < (HD // 2)
    rolled = jnp.where(first, pltpu.roll(xc, 128 - HD // 2, 1), pltpu.roll(xc, HD // 2, 1))
    return xc * cos + rolled * sin


def _halves(xc):
    return xc[:, :HD], pltpu.roll(xc, HD, 1)[:, :HD]


def _place(parts):
    r, c = _iota((HD, KV_W), 0), _iota((HD, KV_W), 1)
    out = None
    for j, pj in enumerate(parts):
        e = jnp.where(c == r + HD * j, 1.0, 0.0).astype(BF16)
        t = _dot(pj, e)
        out = t if out is None else out + t
    return out


def _stack_heads(q_ref):
    return jnp.concatenate([q_ref[0, j] for j in range(GROUP)], axis=0)


def _init_softmax(m_ref, l_ref, acc_ref):
    m_ref[...] = jnp.full(m_ref.shape, M_FLOOR, F32)
    l_ref[...] = jnp.zeros_like(l_ref)
    acc_ref[...] = jnp.zeros_like(acc_ref)


def _online_t(s, valid, vt, m_ref, l_ref, acc_ref, idx, bias=None):
    m_all, l_all, acc_all = m_ref[idx], l_ref[idx], acc_ref[idx]
    m_out, l_out, acc_out = [], [], []
    for c0 in range(0, s.shape[1], Q_CHUNK):
        cols = slice(c0, c0 + Q_CHUNK)
        sc = s[:, cols]
        if bias is not None:
            sc = sc + bias[:, cols]
        if valid is not None:
            vc = valid[:, cols]
            sc = jnp.where(vc, sc, NEG)
        m_old = m_all[:, cols]
        m_new = jnp.maximum(m_old, jnp.max(sc, axis=0, keepdims=True))
        alpha = jnp.exp(m_old - m_new)
        p = jnp.exp(sc - m_new)
        if valid is not None:
            p = jnp.where(vc, p, 0.0)
        l_out.append(alpha * l_all[:, cols] + jnp.sum(p, axis=0, keepdims=True))
        acc_out.append(alpha * acc_all[:, cols] + _dot(vt, p.astype(BF16)))
        m_out.append(m_new)
    m_ref[idx] = jnp.concatenate(m_out, axis=1)
    l_ref[idx] = jnp.concatenate(l_out, axis=1)
    acc_ref[idx] = jnp.concatenate(acc_out, axis=1)


def _group_out_t(acc_t, l, tq):
    r = acc_t if l is None else acc_t / l
    o_t = jnp.concatenate([r[:, j * tq:(j + 1) * tq] for j in range(GROUP)], axis=0)
    return o_t.T.astype(BF16)


def _topk_select(gate, n_valid, k, axis):
    pos = _iota(gate.shape, axis)
    posf = pos.astype(F32)
    g = jnp.where(pos < n_valid, gate, -jnp.inf)
    sel = jnp.zeros(gate.shape, F32)
    for _ in range(k):
        best = jnp.max(g, axis=axis, keepdims=True)
        first = jnp.min(jnp.where(g == best, posf, float(gate.shape[axis])), axis=axis, keepdims=True)
        hit = posf == first
        sel = jnp.where(hit, 1.0, sel)
        g = jnp.where(hit, -jnp.inf, g)
    return jnp.where(pos < n_valid, sel, 0.0)


def _order_key(score):
    bits = lax.bitcast_convert_type(score, I32)
    return jnp.where(bits < 0, bits ^ jnp.int32(0x7FFFFFFF), bits)


def _kth_largest_key(count_ge, shape, k):
    def body(it, v):
        cand = v + lax.shift_left(jnp.int32(1), 31 - it)
        return jnp.where(count_ge(cand) >= k, cand, v)

    return lax.fori_loop(0, 32, body, jnp.full(shape, INT_MIN, I32))


def _proj_kernel(*refs, kind, rope, prompt, aug):
    it = iter(refs)
    x_ref, g_ref, w_ref = next(it), next(it), next(it)
    bf_ref = next(it) if kind == FOX else None
    cos_ref, sin_ref = (next(it), next(it)) if rope else (None, None)
    qpat_ref = next(it) if aug else None
    kpat_ref = next(it) if aug == "qk" else None
    qhm_ref, k32_ref, v32_ref, khm_ref, vhm_ref = next(it), next(it), next(it), next(it), next(it)
    low = _iota((x_ref.shape[1], 128), 1) < HD
    h = _rms_bf16(x_ref[0], g_ref[...])
    y = _dot(h, w_ref[...])
    cos = cos_ref[...] if rope else None
    sin = sin_ref[...] if rope else None

    def chunk(c, rot):
        xc = y[:, 128 * c:128 * (c + 1)]
        return _rope128(xc, cos, sin) if rot else xc

    q32hm_ref = next(it) if kind == MOBA else None
    for c in range(N_HEADS // 2):
        qc = chunk(c, rope)
        lo, hi = _halves(qc)
        if aug:
            for j, z in enumerate((qc, pltpu.roll(qc, HD, 1))):
                qhm_ref[0, 2 * c + j] = jnp.where(low, z * 0.125, qpat_ref[2 * c + j:2 * c + j + 1, :]).astype(BF16)
        else:
            qhm_ref[0, 2 * c] = (lo * 0.125).astype(BF16)
            qhm_ref[0, 2 * c + 1] = (hi * 0.125).astype(BF16)
        if kind == MOBA:
            q32hm_ref[0, 2 * c] = lo
            q32hm_ref[0, 2 * c + 1] = hi
    base = D_MODEL // 128
    for c in range(N_KV // 2):
        kc = chunk(base + c, rope)
        k32_ref[0, :, 128 * c:128 * (c + 1)] = kc
        vc = chunk(base + N_KV // 2 + c, False)
        v32_ref[0, :, 128 * c:128 * (c + 1)] = vc
        if aug == "qk":
            for j, z in enumerate((kc, pltpu.roll(kc, HD, 1))):
                khm_ref[0, 2 * c + j] = jnp.where(low, z, kpat_ref[...]).astype(BF16)
        else:
            klo, khi = _halves(kc)
            khm_ref[0, 2 * c] = klo.astype(BF16)
            khm_ref[0, 2 * c + 1] = khi.astype(BF16)
        vlo, vhi = _halves(vc)
        vhm_ref[0, 2 * c] = vlo.astype(BF16)
        vhm_ref[0, 2 * c + 1] = vhi.astype(BF16)
    base = (D_MODEL + 2 * KV_W) // 128
    if kind == FOX:
        logf_ref = next(it)
        zf = chunk(base, False)[:, :N_HEADS] + bf_ref[...]
        logf_ref[0] = _log_sigmoid_pair(zf)[0]
    if kind == DSA:
        qihm_ref, ki_ref, wi_ref = next(it), next(it), next(it)
        for c in range(IDX_HEADS // 2):
            lo, hi = _halves(chunk(base + c, True))
            qihm_ref[0, 2 * c] = lo
            qihm_ref[0, 2 * c + 1] = hi
        kic = chunk(base + IDX_HEADS // 2, True)
        ki_ref[0] = kic[:, :IDX_DIM]
        wi_ref[0] = chunk(base + IDX_HEADS // 2 + 1, False)[:, :IDX_HEADS] * (IDX_HEADS ** -0.5 * IDX_DIM ** -0.5)
        if prompt:
            qi3_ref, ki3_ref = next(it), next(it)
            low = _iota(kic.shape, 1) < IDX_DIM
            for c in range(IDX_HEADS // 2):
                xc = chunk(base + c, True)
                for j, z in enumerate((jnp.where(low, xc, 0.0), jnp.where(low, pltpu.roll(xc, IDX_DIM, 1), 0.0))):
                    hi = z.astype(BF16).astype(F32)
                    qi3_ref[0, 2 * c + j] = jnp.concatenate([hi + pltpu.roll(z - hi, IDX_DIM, 1), hi], axis=1).astype(BF16)
            hi = kic.astype(BF16).astype(F32)
            ki3_ref[0] = jnp.concatenate([hi + pltpu.roll(hi, IDX_DIM, 1), kic - hi], axis=1).astype(BF16)
    if prompt:
        vt_ref = next(it)
        vt_ref[0, 0] = y[:, D_MODEL + KV_W:D_MODEL + 2 * KV_W].T.astype(BF16)


def _score_features(kind, T):
    qpat = np.zeros((N_HEADS, 128), np.float32)
    if kind == FOX:
        for h in range(N_HEADS):
            qpat[h, HD + 3 * (h % GROUP):HD + 3 * (h % GROUP) + 3] = -1.0
        return "q", [jnp.asarray(qpat)]
    kpat = np.zeros((T, 128), np.float32)
    kpat[np.arange(T), HD + np.arange(T) // MOBA_BLOCK] = 1.0
    return "qk", [jnp.asarray(qpat), jnp.asarray(kpat)]


def _proj(x, g, w_all, kind, rope_tabs, fox_b, prompt):
    B, T, _ = x.shape
    tm = min(PROJ_TM, T)
    n_cols = w_all.shape[1]
    rope = rope_tabs is not None
    aug, pats = _score_features(kind, T) if prompt and kind in (FOX, MOBA) else (None, [])
    q_w = 2 * HD if aug else HD
    k_w = 2 * HD if aug == "qk" else HD
    row = lambda b, i: (b, i, 0)
    hm = lambda b, i: (b, 0, i, 0)
    in_specs = [pl.BlockSpec((1, tm, D_MODEL), row), pl.BlockSpec((1, D_MODEL), lambda b, i: (0, 0)),
                pl.BlockSpec((D_MODEL, n_cols), lambda b, i: (0, 0))]
    args = [x, g.reshape(1, D_MODEL), w_all]
    if kind == FOX:
        in_specs.append(pl.BlockSpec((1, N_HEADS), lambda b, i: (0, 0)))
        args.append(fox_b.reshape(1, N_HEADS))
    if rope:
        in_specs += [pl.BlockSpec((tm, 128), lambda b, i: (i, 0))] * 2
        args += list(rope_tabs)
    if aug:
        in_specs.append(pl.BlockSpec((N_HEADS, 128), lambda b, i: (0, 0)))
    if aug == "qk":
        in_specs.append(pl.BlockSpec((tm, 128), lambda b, i: (i, 0)))
    args += pats
    out_shape = [jax.ShapeDtypeStruct((B, N_HEADS, T, q_w), BF16), jax.ShapeDtypeStruct((B, T, KV_W), F32),
                 jax.ShapeDtypeStruct((B, T, KV_W), F32), jax.ShapeDtypeStruct((B, N_KV, T, k_w), BF16),
                 jax.ShapeDtypeStruct((B, N_KV, T, HD), BF16)]
    out_specs = [pl.BlockSpec((1, N_HEADS, tm, q_w), hm), pl.BlockSpec((1, tm, KV_W), row),
                 pl.BlockSpec((1, tm, KV_W), row), pl.BlockSpec((1, N_KV, tm, k_w), hm),
                 pl.BlockSpec((1, N_KV, tm, HD), hm)]
    if kind == MOBA:
        out_shape.append(jax.ShapeDtypeStruct((B, N_HEADS, T, HD), F32))
        out_specs.append(pl.BlockSpec((1, N_HEADS, tm, HD), hm))
    if kind == FOX:
        out_shape.append(jax.ShapeDtypeStruct((B, T, N_HEADS), F32))
        out_specs.append(pl.BlockSpec((1, tm, N_HEADS), row))
    if kind == DSA:
        out_shape += [jax.ShapeDtypeStruct((B, IDX_HEADS, T, IDX_DIM), F32),
                      jax.ShapeDtypeStruct((B, T, IDX_DIM), F32), jax.ShapeDtypeStruct((B, T, IDX_HEADS), F32)]
        out_specs += [pl.BlockSpec((1, IDX_HEADS, tm, IDX_DIM), hm), pl.BlockSpec((1, tm, IDX_DIM), row),
                      pl.BlockSpec((1, tm, IDX_HEADS), row)]
        if prompt:
            out_shape += [jax.ShapeDtypeStruct((B, IDX_HEADS, T, 4 * IDX_DIM), BF16),
                          jax.ShapeDtypeStruct((B, T, 4 * IDX_DIM), BF16)]
            out_specs += [pl.BlockSpec((1, IDX_HEADS, tm, 4 * IDX_DIM), hm), pl.BlockSpec((1, tm, 4 * IDX_DIM), row)]
    if prompt:
        assert tm == KV_TILE
        out_shape.append(jax.ShapeDtypeStruct((B, T // tm, KV_W, tm), BF16))
        out_specs.append(pl.BlockSpec((1, 1, KV_W, tm), lambda b, i: (b, i, 0, 0)))
    return pl.pallas_call(
        functools.partial(_proj_kernel, kind=kind, rope=rope, prompt=prompt, aug=aug),
        grid=(B, T // tm), in_specs=in_specs, out_specs=out_specs, out_shape=out_shape,
        compiler_params=_cp("parallel", "arbitrary"), name=f"proj_{kind}_{'p' if prompt else 's'}")(*args)


def _gelu_tanh(x):
    return 0.5 * x * (1.0 + jnp.tanh(0.7978845608028654 * (x + 0.044715 * x * x * x)))


def _ffn_kernel(*refs, prompt, tiles_per_seq):
    if prompt:
        x_ref, o_ref, wo_ref, g_ref, wa_ref, wb_ref, cw_ref, cb_ref, wd_ref, y_ref, conv_ref, u_ref, carry_ref = refs
    else:
        x_ref, o_ref, wo_ref, g_ref, wa_ref, wb_ref, cw_ref, cb_ref, wd_ref, p2_ref, p1_ref, y_ref, a_ref, u_ref = refs
    x = x_ref[...] + _dot(o_ref[...], wo_ref[...])
    h = _rms_bf16(x, g_ref[...])
    if prompt:
        @pl.when(pl.program_id(0) % tiles_per_seq == 0)
        def _():
            carry_ref[...] = jnp.zeros_like(carry_ref)

        row = _iota((x.shape[0], FF_CHUNK), 0)
    for c in range(D_FF // FF_CHUNK):
        cols = slice(c * FF_CHUNK, (c + 1) * FF_CHUNK)
        a = _dot(h, wa_ref[:, cols])
        gate = _dot(h, wb_ref[:, cols])
        if prompt:
            prev2, prev1 = carry_ref[0:1, cols], carry_ref[1:2, cols]
            back2 = pltpu.roll(a, 2, 0)
            a_m1 = jnp.where(row == 0, prev1, pltpu.roll(a, 1, 0))
            a_m2 = jnp.where(row == 0, prev2, jnp.where(row == 1, prev1, back2))
            carry_ref[0:2, cols] = back2[0:2, :]
            conv_ref[0, :, cols] = back2[0:2, :]
        else:
            a_m2, a_m1 = p2_ref[:, cols], p1_ref[:, cols]
            a_ref[:, cols] = a
        conv = cb_ref[:, cols] + cw_ref[0:1, cols] * a_m2 + cw_ref[1:2, cols] * a_m1 + cw_ref[2:3, cols] * a
        u_ref[:, cols] = (_gelu_tanh(conv) * gate).astype(BF16)
    y_ref[...] = x + _dot(u_ref[...], wd_ref[...])


def _mix_ffn(x2, o2, w_o, g, w_a, w_b, conv_w, conv_b, w_down, seq_len, prev=None):
    M = x2.shape[0]
    prompt = prev is None
    tm = min(FFN_TM, seq_len) if prompt else M
    tps = seq_len // tm if prompt else 1
    xs = pl.BlockSpec((tm, D_MODEL), lambda i: (i, 0))
    whole = lambda shape: pl.BlockSpec(shape, lambda i: (0,) * len(shape), pipeline_mode=pl.Buffered(1))
    in_specs = [xs, xs, whole((D_MODEL, D_MODEL)), whole((1, D_MODEL)), whole((D_MODEL, D_FF)), whole((D_MODEL, D_FF)),
                whole((CONV_W, D_FF)), whole((1, D_FF)), whole((D_FF, D_MODEL))]
    args = [x2, o2, w_o, g.reshape(1, D_MODEL), w_a, w_b, conv_w, conv_b.reshape(1, D_FF), w_down]
    scratch = [pltpu.VMEM((tm, D_FF), BF16)]
    if prompt:
        out_shape = [jax.ShapeDtypeStruct((M, D_MODEL), F32), jax.ShapeDtypeStruct((M // tm, CONV_W - 1, D_FF), F32)]
        out_specs = [xs, pl.BlockSpec((1, CONV_W - 1, D_FF), lambda i: (i, 0, 0))]
        scratch.append(pltpu.VMEM((8, D_FF), F32))
    else:
        ps = pl.BlockSpec((tm, D_FF), lambda i: (i, 0))
        in_specs += [ps, ps]
        args += list(prev)
        out_shape = [jax.ShapeDtypeStruct((M, D_MODEL), F32), jax.ShapeDtypeStruct((M, D_FF), F32)]
        out_specs = [xs, ps]
    y, extra = pl.pallas_call(
        functools.partial(_ffn_kernel, prompt=prompt, tiles_per_seq=tps), grid=(M // tm,),
        in_specs=in_specs, out_specs=out_specs, out_shape=out_shape, scratch_shapes=scratch,
        compiler_params=_cp("arbitrary"), name="ffn_p" if prompt else "ffn_s")(*args)
    return y, (extra[tps - 1::tps] if prompt else extra)


def _final_norm_kernel(x_ref, g_ref, y_ref):
    x = x_ref[...]
    y_ref[...] = x * lax.rsqrt(jnp.mean(x * x, axis=-1, keepdims=True) + RMS_EPS) * g_ref[...]


def _final_norm(x2, g):
    M = x2.shape[0]
    tm = min(1024, M)
    return pl.pallas_call(
        _final_norm_kernel, grid=(M // tm,),
        in_specs=[pl.BlockSpec((tm, D_MODEL), lambda i: (i, 0)), pl.BlockSpec((1, D_MODEL), lambda i: (0, 0))],
        out_specs=pl.BlockSpec((tm, D_MODEL), lambda i: (i, 0)),
        out_shape=jax.ShapeDtypeStruct((M, D_MODEL), F32), compiler_params=_cp("parallel"), name="final_norm")(x2, g.reshape(1, D_MODEL))


def _stick_kernel(q_ref, k_ref, vt_ref, o_ref, r_ref, acc_ref, *, tq, tk, nh):
    q0 = pl.program_id(2) * tq
    qs = [jnp.concatenate([q_ref[0, GROUP * h + j] for j in range(GROUP)], axis=0) for h in range(nh)]
    m = GROUP * tq
    later = _tri(tk, "lt")
    qpos = q0 + _iota((1, m), 1) % tq
    r_ref[...] = jnp.zeros_like(r_ref)
    acc_ref[...] = jnp.zeros_like(acc_ref)

    def tile(j, masked):
        ks = pl.multiple_of(j * tk, tk)
        zs = [_dot_nt(k_ref[0, h, pl.ds(ks, tk), :], qs[h]) for h in range(nh)]
        for h in range(nh):
            head_tile(j, h, zs[h], masked)

    def head_tile(j, h, z, masked):
        vt = vt_ref[0, j, HD * h:HD * (h + 1), :]
        kpos = j * tk + _iota((tk, 1), 0)
        r_all, acc_all = r_ref[h], acc_ref[h]
        chunks = [slice(c0, c0 + Q_CHUNK) for c0 in range(0, m, Q_CHUNK)]
        staged, r_out, acc_out = [], [], []
        for cols in chunks:
            zc = z[:, cols]
            ls = jnp.minimum(zc, 0.0) - jnp.log(1.0 + jnp.exp(-jnp.abs(zc)))
            lk = ls - zc
            past = kpos < qpos[:, cols] if masked else None
            if masked:
                lk = jnp.where(past, lk, 0.0)
            parts = _dot(later, jnp.concatenate(_split2(lk), axis=1))
            staged.append((ls, parts, past))
            r_out.append(r_all[:, cols] + jnp.sum(lk, axis=0, keepdims=True))
        for cols, (ls, parts, past) in zip(chunks, staged):
            w = jnp.exp(ls + parts[:, :Q_CHUNK] + parts[:, Q_CHUNK:] + r_all[:, cols])
            if masked:
                w = jnp.where(past, w, 0.0)
            acc_out.append(acc_all[:, cols] + _dot(vt, w.astype(BF16)))
        r_ref[h] = jnp.concatenate(r_out, axis=1)
        acc_ref[h] = jnp.concatenate(acc_out, axis=1)

    jd = q0 // tk
    tile(jd, True)

    def live():
        return jnp.max(r_ref[...]) > DEAD_LOG

    def body(carry):
        n, _ = carry
        tile(jd - 1 - n, False)
        return n + 1, live()

    lax.while_loop(lambda c: (c[0] < jd) & c[1], body, (jnp.int32(0), live()))
    for h in range(nh):
        o_ref[0, :, KV_W * h:KV_W * (h + 1)] = _group_out_t(acc_ref[h], None, tq)


def _stick_prompt(qhm, khm, vt):
    B, _, T, _ = qhm.shape
    tq, tk, nh = 256, KV_TILE, KV_PER_STEP
    m = GROUP * tq
    return pl.pallas_call(
        functools.partial(_stick_kernel, tq=tq, tk=tk, nh=nh), grid=(B, N_KV // nh, T // tq),
        in_specs=[pl.BlockSpec((1, nh * GROUP, tq, HD), lambda b, g, i: (b, g, i, 0)),
                  pl.BlockSpec((1, nh, T, HD), lambda b, g, i: (b, g, 0, 0)),
                  pl.BlockSpec((1, T // tk, nh * HD, tk), lambda b, g, i: (b, 0, g, 0))],
        out_specs=pl.BlockSpec((1, tq, nh * KV_W), lambda b, g, i: (b, i, g)),
        out_shape=jax.ShapeDtypeStruct((B, T, D_MODEL), BF16),
        scratch_shapes=[pltpu.VMEM((nh, 1, m), F32), pltpu.VMEM((nh, HD, m), F32)],
        compiler_params=_cp("parallel", "parallel", "arbitrary"), name="stick_p")(qhm, khm, vt)


def _cumsum_kernel(x_ref, hi_ref, mid_ref, lo_ref, carry_ref):
    @pl.when(pl.program_id(1) == 0)
    def _():
        carry_ref[...] = jnp.zeros_like(carry_ref)

    n = x_ref.shape[1]
    low = _tri(n, "ge")
    hi, mid, lo = _split3(x_ref[0])
    c = _dot(low, hi) + _dot(low, mid) + _dot(low, lo) + carry_ref[...]
    carry_ref[...] = c[n - 1:n, :]
    hi_ref[0], mid_ref[0], lo_ref[0] = _split3(c)


def _cumsum_time_parts(x):
    B, T, C = x.shape
    n = 256
    spec = pl.BlockSpec((1, n, C), lambda b, i: (b, i, 0))
    return pl.pallas_call(
        _cumsum_kernel, grid=(B, T // n), in_specs=[spec], out_specs=[spec] * 3,
        out_shape=[jax.ShapeDtypeStruct((B, T, C), BF16)] * 3, scratch_shapes=[pltpu.VMEM((1, C), F32)],
        compiler_params=_cp("parallel", "arbitrary"), name="cumsum_time")(x)


def _softmax_scratch_t(n, m):
    return [pltpu.VMEM((n, 1, m), F32), pltpu.VMEM((n, 1, m), F32), pltpu.VMEM((n, HD, m), F32)]


def _causal_kernel(q_ref, k_ref, vt_ref, o_ref, m_ref, l_ref, acc_ref, *, tq, tk, nh):
    q0 = pl.program_id(2) * tq
    qs = [jnp.concatenate([q_ref[0, GROUP * h + j] for j in range(GROUP)], axis=0) for h in range(nh)]
    m = GROUP * tq
    _init_softmax(m_ref, l_ref, acc_ref)
    qpos = q0 + _iota((1, m), 1) % tq

    def tile(j, valid):
        ks = pl.multiple_of(j * tk, tk)
        s = [_dot_nt(k_ref[0, h, pl.ds(ks, tk), :], qs[h]) for h in range(nh)]
        for h in range(nh):
            _online_t(s[h], valid, vt_ref[0, j, HD * h:HD * (h + 1), :], m_ref, l_ref, acc_ref, h)

    jd = q0 // tk
    tile(jd, (jd * tk + _iota((tk, 1), 0)) <= qpos)

    def body(n, carry):
        tile(n, None)
        return carry

    lax.fori_loop(0, jd, body, 0)
    for h in range(nh):
        o_ref[0, :, KV_W * h:KV_W * (h + 1)] = _group_out_t(acc_ref[h], l_ref[h], tq)


def _fox_prompt(q_aug, khm, vt, cum_parts):
    B, _, T, d_aug = q_aug.shape
    tq, tk = 256, KV_TILE
    c = jnp.stack(cum_parts, axis=-1).reshape(B, T, N_KV, FOX_AUG).transpose(0, 2, 1, 3)
    k_aug = jnp.concatenate([khm, c, jnp.zeros((B, N_KV, T, d_aug - HD - FOX_AUG), BF16)], axis=-1)
    m = GROUP * tq
    return pl.pallas_call(
        functools.partial(_causal_kernel, tq=tq, tk=tk, nh=KV_PER_STEP), grid=(B, N_KV // KV_PER_STEP, T // tq),
        in_specs=[pl.BlockSpec((1, KV_PER_STEP * GROUP, tq, d_aug), lambda b, g, i: (b, g, i, 0)),
                  pl.BlockSpec((1, KV_PER_STEP, T, d_aug), lambda b, g, i: (b, g, 0, 0)),
                  pl.BlockSpec((1, T // tk, KV_PER_STEP * HD, tk), lambda b, g, i: (b, 0, g, 0))],
        out_specs=pl.BlockSpec((1, tq, KV_PER_STEP * KV_W), lambda b, g, i: (b, i, g)),
        out_shape=jax.ShapeDtypeStruct((B, T, D_MODEL), BF16),
        scratch_shapes=_softmax_scratch_t(KV_PER_STEP, m),
        compiler_params=_cp("parallel", "parallel", "arbitrary"), name="fox_p")(q_aug, k_aug, vt)


def _block_mean_kernel(k_ref, o_ref):
    k = k_ref[0]
    nb = k.shape[0] // MOBA_BLOCK
    o_ref[0] = jnp.sum(k.reshape(nb, MOBA_BLOCK, KV_W), axis=1) * (1.0 / MOBA_BLOCK)


def _block_mean(k32):
    B, T, _ = k32.shape
    nb = T // MOBA_BLOCK
    return pl.pallas_call(
        _block_mean_kernel, grid=(B,), in_specs=[pl.BlockSpec((1, T, KV_W), lambda b: (b, 0, 0))],
        out_specs=pl.BlockSpec((1, nb, KV_W), lambda b: (b, 0, 0)),
        out_shape=jax.ShapeDtypeStruct((B, nb, KV_W), F32), compiler_params=_cp("parallel"), name="block_mean")(k32)


def _moba_kernel(q_ref, q32_ref, km_ref, k_ref, vt_ref, o_ref, m_ref, l_ref, acc_ref, *, tq, nb, nh):
    tk = MOBA_BLOCK
    q0 = pl.program_id(2) * tq
    own = q0 // tk
    stack = lambda ref, h: jnp.concatenate([ref[0, GROUP * h + j] for j in range(GROUP)], axis=0)
    m = GROUP * tq
    _init_softmax(m_ref, l_ref, acc_ref)
    row = _iota((nb, m), 0)
    qs = []
    for h in range(nh):
        sel = _topk_select(_dot_nt_f32(km_ref[0, h], stack(q32_ref, h)), own, MOBA_TOPK, 0)
        bias = jnp.where((sel > 0.5) | (row == own), 0.0, NEG)
        bias = jnp.concatenate([jnp.zeros((HD, m), F32), bias, jnp.zeros((HD - nb, m), F32)], axis=0)
        qs.append(stack(q_ref, h) + bias.T.astype(BF16))
    qpos = q0 + _iota((1, m), 1) % tq

    def tile(j, valid):
        ks = pl.multiple_of(j * tk, tk)
        s = [_dot_nt(k_ref[0, h, pl.ds(ks, tk), :], qs[h]) for h in range(nh)]
        for h in range(nh):
            _online_t(s[h], valid, vt_ref[0, j, HD * h:HD * (h + 1), :], m_ref, l_ref, acc_ref, h)

    tile(own, (own * tk + _iota((tk, 1), 0)) <= qpos)

    def body(j, carry):
        tile(j, None)
        return carry

    lax.fori_loop(0, own, body, 0)
    for h in range(nh):
        o_ref[0, :, KV_W * h:KV_W * (h + 1)] = _group_out_t(acc_ref[h], l_ref[h], tq)


def _moba_prompt(q_aug, q32hm, kmean, k_aug, vt):
    B, _, T, _ = q_aug.shape
    tq, nh = 256, KV_PER_STEP
    nb = T // MOBA_BLOCK
    assert nb <= HD
    m = GROUP * tq
    km = kmean.reshape(B, nb, N_KV, HD).transpose(0, 2, 1, 3)
    return pl.pallas_call(
        functools.partial(_moba_kernel, tq=tq, nb=nb, nh=nh), grid=(B, N_KV // nh, T // tq),
        in_specs=[pl.BlockSpec((1, nh * GROUP, tq, 2 * HD), lambda b, g, i: (b, g, i, 0)),
                  pl.BlockSpec((1, nh * GROUP, tq, HD), lambda b, g, i: (b, g, i, 0)),
                  pl.BlockSpec((1, nh, nb, HD), lambda b, g, i: (b, g, 0, 0)),
                  pl.BlockSpec((1, nh, T, 2 * HD), lambda b, g, i: (b, g, 0, 0)),
                  pl.BlockSpec((1, nb, nh * HD, MOBA_BLOCK), lambda b, g, i: (b, 0, g, 0))],
        out_specs=pl.BlockSpec((1, tq, nh * KV_W), lambda b, g, i: (b, i, g)),
        out_shape=jax.ShapeDtypeStruct((B, T, D_MODEL), BF16),
        scratch_shapes=_softmax_scratch_t(nh, m),
        compiler_params=_cp("parallel", "parallel", "arbitrary"), name="moba_p")(q_aug, q32hm, km, k_aug, vt)


def _dsa_kernel(q_ref, qi_ref, wi_ref, ki_ref, k_ref, vt_ref, o_ref, key_ref, m_ref, l_ref, acc_ref, *, tq, n_keep):
    tk = KV_TILE
    q0 = pl.program_id(1) * tq
    n_proc = (q0 + tq + tk - 1) // tk
    qpos = q0 + _iota((1, tq), 1)

    def visible(c):
        return (c * tk + _iota((tk, 1), 0)) <= qpos

    qi3 = jnp.concatenate([qi_ref[0, i] for i in range(IDX_HEADS)], axis=0)

    def score_tile(c, carry):
        ks = pl.multiple_of(c * tk, tk)
        dots = _dot_nt(ki_ref[0, pl.ds(ks, tk), :], qi3)
        sc = jnp.zeros((tk, tq), F32)
        for i in range(IDX_HEADS):
            sc = sc + wi_ref[0, i:i + 1, :] * jnp.maximum(dots[:, i * tq:(i + 1) * tq], 0.0)
        key_ref[c] = _order_key(jnp.where(visible(c), sc + 0.0, -jnp.inf))
        return carry

    lax.fori_loop(0, n_proc, score_tile, 0)

    def count(pred, cand):
        out = []
        for c0 in range(0, tq, Q_CHUNK):
            cols = slice(c0, c0 + Q_CHUNK)

            def body(c, part):
                return part + jnp.where(pred(key_ref[c, :, cols], cand[:, cols]), 1.0, 0.0)

            out.append(jnp.sum(lax.fori_loop(0, n_proc, body, jnp.zeros((tk, Q_CHUNK), F32)), axis=0, keepdims=True))
        return jnp.concatenate(out, axis=1)

    thr = _kth_largest_key(lambda cand: count(lambda k, v: k >= v, cand), (1, tq), n_keep)
    need = n_keep - count(lambda k, v: k > v, thr)
    _init_softmax(m_ref, l_ref, acc_ref)
    before = _tri(tk, "gt")
    qs = [jnp.concatenate([q_ref[0, GROUP * g + j] for j in range(GROUP)], axis=0) for g in range(N_KV)]

    def attend(c, ties_seen):
        ks = pl.multiple_of(c * tk, tk)
        key = key_ref[c]
        tied = jnp.where(key == thr, 1.0, 0.0)
        rank = _dot(before, tied.astype(BF16)) + ties_seen
        keep = (key > thr) | ((key == thr) & (rank < need))
        bias = jnp.where(keep & visible(c), 0.0, NEG)
        bias = jnp.concatenate([bias] * GROUP, axis=1)
        s = [_dot_nt(k_ref[0, g, pl.ds(ks, tk), :], qs[g]) for g in range(N_KV)]
        for g in range(N_KV):
            _online_t(s[g], None, vt_ref[0, c, HD * g:HD * (g + 1), :], m_ref, l_ref, acc_ref, g, bias=bias)
        return ties_seen + jnp.sum(tied, axis=0, keepdims=True)

    lax.fori_loop(0, n_proc, attend, jnp.zeros((1, tq), F32))
    for g in range(N_KV):
        o_ref[0, :, KV_W * g:KV_W * (g + 1)] = _group_out_t(acc_ref[g], l_ref[g], tq)


def _dsa_prompt(qhm, qi3, wi, ki3, khm, vt):
    B, _, T, _ = qhm.shape
    tq = 128
    m = GROUP * tq
    return pl.pallas_call(
        functools.partial(_dsa_kernel, tq=tq, n_keep=min(IDX_TOPK, T // 4)), grid=(B, T // tq),
        in_specs=[pl.BlockSpec((1, N_HEADS, tq, HD), lambda b, i: (b, 0, i, 0)),
                  pl.BlockSpec((1, IDX_HEADS, tq, 4 * IDX_DIM), lambda b, i: (b, 0, i, 0)),
                  pl.BlockSpec((1, IDX_HEADS, tq), lambda b, i: (b, 0, i)),
                  pl.BlockSpec((1, T, 4 * IDX_DIM), lambda b, i: (b, 0, 0)),
                  pl.BlockSpec((1, N_KV, T, HD), lambda b, i: (b, 0, 0, 0)),
                  pl.BlockSpec((1, T // KV_TILE, KV_W, KV_TILE), lambda b, i: (b, 0, 0, 0))],
        out_specs=pl.BlockSpec((1, tq, D_MODEL), lambda b, i: (b, i, 0)),
        out_shape=jax.ShapeDtypeStruct((B, T, D_MODEL), BF16),
        scratch_shapes=[pltpu.VMEM((T // KV_TILE, KV_TILE, tq), I32)] + _softmax_scratch_t(N_KV, m),
        compiler_params=_cp("parallel", "arbitrary"), name="dsa_p")(qhm, qi3, wi.transpose(0, 2, 1), ki3, khm, vt)


def _page_specs(layer, n_pages, nps, reverse):
    def spec(i):
        def idx(b, s, pt):
            p = s * nps + i
            return (layer, pt[b, n_pages - 1 - p if reverse else p], 0, 0)

        return pl.BlockSpec((None, None, KV_W, PAGE), idx)

    return [spec(i) for i in range(nps)]


def _pool_specs(rows, n_pages, nps, reverse):
    def spec(i):
        def idx(b, s, pt):
            p = s * nps + i
            return (pt[b, n_pages - 1 - p if reverse else p], 0, 0)

        return pl.BlockSpec((None, rows, PAGE), idx)

    return [spec(i) for i in range(nps)]


def _row_spec(width):
    return pl.BlockSpec((1, 1, width), lambda b, s, pt: (b, 0, 0))


def _head_spec(width):
    return pl.BlockSpec((1, N_HEADS, width), lambda b, s, pt: (b, 0, 0))


def _softmax_scratch():
    return [pltpu.VMEM((N_HEADS, 1), F32), pltpu.VMEM((N_HEADS, 1), F32), pltpu.VMEM((N_HEADS, KV_W), F32)]


def _new_key_start(q, knew, vnew, m_ref, l_ref, acc_ref):
    z = jnp.sum(q.astype(F32) * knew.astype(BF16).astype(F32), axis=1, keepdims=True)
    m_ref[...] = z
    l_ref[...] = jnp.ones_like(z)
    acc_ref[...] = jnp.broadcast_to(vnew.astype(BF16).astype(F32), acc_ref.shape)


def _cat_pages(refs):
    return jnp.concatenate([r[...].astype(BF16) for r in refs], axis=1)


def _later_sums(x, parts):
    n = x.shape[1] // PAGE
    stack = jnp.concatenate([p[:, PAGE * i:PAGE * (i + 1)] for p in parts(x) for i in range(n)], axis=0)
    out = _dot(stack, _tri(PAGE, "gt"))
    rows = x.shape[0]
    tot = None
    for k in range(out.shape[0] // (n * rows)):
        part = jnp.concatenate([out[(k * n + i) * rows:(k * n + i + 1) * rows] for i in range(n)], axis=1)
        tot = part if tot is None else tot + part
    return tot


def _page_totals(x):
    return [jnp.sum(x[:, PAGE * i:PAGE * (i + 1)], axis=1, keepdims=True) for i in range(x.shape[1] // PAGE)]


def _per_page(cols):
    return jnp.concatenate([jnp.broadcast_to(c, (c.shape[0], PAGE)) for c in cols], axis=1)


def _online_pages(s, valid, vcat, m_ref, l_ref, acc_ref):
    if valid is not None:
        s = jnp.where(valid, s, NEG)
    m_old = m_ref[...]
    m_new = jnp.maximum(m_old, jnp.max(s, axis=1, keepdims=True))
    alpha = jnp.exp(m_old - m_new)
    p = jnp.exp(s - m_new)
    if valid is not None:
        p = jnp.where(valid, p, 0.0)
    l_ref[...] = alpha * l_ref[...] + jnp.sum(p, axis=1, keepdims=True)
    acc_ref[...] = alpha * acc_ref[...] + _dot_nt(p.astype(BF16), vcat)
    m_ref[...] = m_new


def _dec_stick_kernel(pt_ref, q_ref, *refs, nps):
    k_refs, v_refs = refs[:nps], refs[nps:2 * nps]
    o_ref, r_ref, acc_ref = refs[2 * nps:]
    step = pl.program_id(1)

    @pl.when(step == 0)
    def _():
        r_ref[...] = jnp.zeros_like(r_ref)
        acc_ref[...] = jnp.zeros_like(acc_ref)

    ls, lk = _log_sigmoid_pair(_dot(q_ref[0], _cat_pages(k_refs)))
    after, run = [], r_ref[...]
    for tot in _page_totals(lk):
        after.append(run)
        run = run + tot
    w = jnp.exp(ls + _later_sums(lk, _split2) + _per_page(after))
    acc = acc_ref[...] + _dot_nt(w.astype(BF16), _cat_pages(v_refs))
    r_ref[...], acc_ref[...] = run, acc

    @pl.when(step == pl.num_programs(1) - 1)
    def _():
        o_ref[0] = acc


def _dec_call(kernel, name, grid_spec, DB):
    return pl.pallas_call(kernel, grid_spec=grid_spec, out_shape=jax.ShapeDtypeStruct((DB, N_HEADS, KV_W), F32),
                          compiler_params=_cp("parallel", "arbitrary"), name=name)


def _dec_stick(qx, kt, vt, page_table, layer):
    DB, n_pages = page_table.shape
    nps = math.gcd(n_pages, DEC_PAGES)
    pages = _page_specs(layer, n_pages, nps, True)
    grid_spec = pltpu.PrefetchScalarGridSpec(
        num_scalar_prefetch=1, grid=(DB, n_pages // nps), in_specs=[_head_spec(KV_W)] + pages + pages,
        out_specs=_head_spec(KV_W), scratch_shapes=[pltpu.VMEM((N_HEADS, 1), F32), pltpu.VMEM((N_HEADS, KV_W), F32)])
    return _dec_call(functools.partial(_dec_stick_kernel, nps=nps), "stick_s", grid_spec, DB)(
        page_table, qx, *([kt] * nps), *([vt] * nps))


def _dec_fox_kernel(pt_ref, q_ref, *refs, nps):
    k_refs, v_refs, lf_refs = refs[:nps], refs[nps:2 * nps], refs[2 * nps:3 * nps]
    kn_ref, vn_ref, lfn_ref, o_ref, s_ref, m_ref, l_ref, acc_ref = refs[3 * nps:]
    step = pl.program_id(1)

    @pl.when(step == 0)
    def _():
        s_ref[...] = lfn_ref[0]
        _new_key_start(q_ref[0], kn_ref[0], vn_ref[0], m_ref, l_ref, acc_ref)

    lf = jnp.concatenate([r[...] for r in lf_refs], axis=1)
    after, run = [], s_ref[...]
    for tot in _page_totals(lf):
        after.append(run)
        run = run + tot
    s_ref[...] = run
    s = _dot(q_ref[0], _cat_pages(k_refs)) + _later_sums(lf, _split3) + _per_page(after)
    _online_pages(s, None, _cat_pages(v_refs), m_ref, l_ref, acc_ref)

    @pl.when(step == pl.num_programs(1) - 1)
    def _():
        o_ref[0] = acc_ref[...] / l_ref[...]


def _dec_fox(qx, kt, vt, logf_t, knew, vnew, lfnew, page_table, layer):
    DB, n_pages = page_table.shape
    nps = math.gcd(n_pages, DEC_PAGES)
    pages = _page_specs(layer, n_pages, nps, True)
    grid_spec = pltpu.PrefetchScalarGridSpec(
        num_scalar_prefetch=1, grid=(DB, n_pages // nps),
        in_specs=[_head_spec(KV_W)] + pages + pages + _pool_specs(N_HEADS, n_pages, nps, True)
        + [_row_spec(KV_W), _row_spec(KV_W), _head_spec(1)],
        out_specs=_head_spec(KV_W), scratch_shapes=[pltpu.VMEM((N_HEADS, 1), F32)] + _softmax_scratch())
    return _dec_call(functools.partial(_dec_fox_kernel, nps=nps), "fox_s", grid_spec, DB)(
        page_table, qx, *([kt] * nps), *([vt] * nps), *([logf_t] * nps), knew, vnew, lfnew)


def _dec_kmean_kernel(pt_ref, *refs, nps):
    k_refs, o_ref = refs[:nps], refs[nps]
    step = pl.program_id(1)

    @pl.when(step == 0)
    def _():
        o_ref[...] = jnp.zeros_like(o_ref)

    ones = jnp.ones((8, PAGE), BF16)
    for i in range(nps):
        hi, mid, lo = _split3(k_refs[i][...])
        tot = _dot_nt(ones, hi) + _dot_nt(ones, mid) + _dot_nt(ones, lo)
        blk = (step * nps + i) // (MOBA_BLOCK // PAGE)
        o_ref[0, pl.ds(blk, 1), :] = o_ref[0, pl.ds(blk, 1), :] + tot[0:1, :] * (1.0 / MOBA_BLOCK)


def _dec_kmean(kt, page_table, layer):
    DB, n_pages = page_table.shape
    nps = math.gcd(n_pages, DEC_PAGES)
    nb = n_pages * PAGE // MOBA_BLOCK
    grid_spec = pltpu.PrefetchScalarGridSpec(
        num_scalar_prefetch=1, grid=(DB, n_pages // nps), in_specs=_page_specs(layer, n_pages, nps, False),
        out_specs=pl.BlockSpec((1, nb, KV_W), lambda b, s, pt: (b, 0, 0)))
    return pl.pallas_call(
        functools.partial(_dec_kmean_kernel, nps=nps), grid_spec=grid_spec,
        out_shape=jax.ShapeDtypeStruct((DB, nb, KV_W), F32),
        compiler_params=_cp("parallel", "arbitrary"), name="kmean_s")(page_table, *([kt] * nps))


def _dec_moba_kernel(pt_ref, q_ref, q32_ref, km_ref, *refs, nps, nb):
    k_refs, v_refs = refs[:nps], refs[nps:2 * nps]
    kn_ref, vn_ref, o_ref, sel_ref, m_ref, l_ref, acc_ref = refs[2 * nps:]
    step = pl.program_id(1)

    @pl.when(step == 0)
    def _():
        sel_ref[...] = _topk_select(_dot_nt_f32(q32_ref[0], km_ref[0]), nb, MOBA_TOPK, 1)
        _new_key_start(q_ref[0], kn_ref[0], vn_ref[0], m_ref, l_ref, acc_ref)

    lane = _iota((N_HEADS, nb), 1)
    sel = sel_ref[...]
    bias = []
    for i in range(nps):
        blk = (step * nps + i) // (MOBA_BLOCK // PAGE)
        picked = jnp.sum(jnp.where(lane == blk, sel, 0.0), axis=1, keepdims=True) > 0.5
        bias.append(jnp.where(picked, 0.0, NEG))
    s = _dot(q_ref[0], _cat_pages(k_refs)) + _per_page(bias)
    _online_pages(s, None, _cat_pages(v_refs), m_ref, l_ref, acc_ref)

    @pl.when(step == pl.num_programs(1) - 1)
    def _():
        o_ref[0] = acc_ref[...] / l_ref[...]


def _dec_moba(qx, q32x, kmean, kt, vt, knew, vnew, page_table, layer):
    DB, n_pages = page_table.shape
    nps = math.gcd(n_pages, DEC_PAGES)
    nb = kmean.shape[1]
    pages = _page_specs(layer, n_pages, nps, False)
    grid_spec = pltpu.PrefetchScalarGridSpec(
        num_scalar_prefetch=1, grid=(DB, n_pages // nps),
        in_specs=[_head_spec(KV_W), _head_spec(KV_W), pl.BlockSpec((1, nb, KV_W), lambda b, s, pt: (b, 0, 0))]
        + pages + pages + [_row_spec(KV_W), _row_spec(KV_W)],
        out_specs=_head_spec(KV_W), scratch_shapes=[pltpu.VMEM((N_HEADS, nb), F32)] + _softmax_scratch())
    return _dec_call(functools.partial(_dec_moba_kernel, nps=nps, nb=nb), "moba_s", grid_spec, DB)(
        page_table, qx, q32x, kmean, *([kt] * nps), *([vt] * nps), knew, vnew)


def _dec_score_kernel(pt_ref, qi_ref, wi_ref, *refs, nps, n_pages):
    kidx_refs = refs[:nps]
    kin_ref, o_ref = refs[nps:]
    step = pl.program_id(1)
    qi, wi = qi_ref[0], wi_ref[0]

    @pl.when(step == 0)
    def _():
        o_ref[...] = jnp.full(o_ref.shape, -jnp.inf, F32)
        dn = jnp.sum(qi * kin_ref[0], axis=1, keepdims=True)
        sn = jnp.sum(wi * jnp.maximum(dn, 0.0), axis=0, keepdims=True) + 0.0
        o_ref[0, n_pages:n_pages + 1, :] = jnp.where(_iota((1, PAGE), 1) == 0, sn, -jnp.inf)

    for i in range(nps):
        dots = _dot_f32(qi, kidx_refs[i][...])
        o_ref[0, pl.ds(step * nps + i, 1), :] = jnp.sum(wi * jnp.maximum(dots, 0.0), axis=0, keepdims=True) + 0.0


def _dec_scores(qi3, wi3, kidx_t, kinew, page_table):
    DB, n_pages = page_table.shape
    assert n_pages < PAGE
    nps = math.gcd(n_pages, DEC_PAGES)
    grid_spec = pltpu.PrefetchScalarGridSpec(
        num_scalar_prefetch=1, grid=(DB, n_pages // nps),
        in_specs=[pl.BlockSpec((1, IDX_HEADS, IDX_DIM), lambda b, s, pt: (b, 0, 0)),
                  pl.BlockSpec((1, IDX_HEADS, 1), lambda b, s, pt: (b, 0, 0))]
        + _pool_specs(IDX_DIM, n_pages, nps, False) + [_row_spec(IDX_DIM)],
        out_specs=pl.BlockSpec((1, PAGE, PAGE), lambda b, s, pt: (b, 0, 0)))
    return pl.pallas_call(
        functools.partial(_dec_score_kernel, nps=nps, n_pages=n_pages), grid_spec=grid_spec,
        out_shape=jax.ShapeDtypeStruct((DB, PAGE, PAGE), F32),
        compiler_params=_cp("parallel", "arbitrary"), name="dsa_score_s")(page_table, qi3, wi3, *([kidx_t] * nps), kinew)


def _dec_select_kernel(sc_ref, o_ref, *, n_keys, n_keep):
    key = _order_key(sc_ref[0])

    def total(x):
        return jnp.sum(jnp.sum(x, axis=1, keepdims=True), axis=0, keepdims=True)

    thr = _kth_largest_key(lambda cand: total(jnp.where(key >= cand, 1.0, 0.0)), (1, 1), n_keep)
    need = n_keep - total(jnp.where(key > thr, 1.0, 0.0))
    tied = jnp.where(key == thr, 1.0, 0.0)
    in_row = _dot(tied.astype(BF16), _tri(PAGE, "lt"))
    row_tot = jnp.broadcast_to(jnp.sum(tied, axis=1, keepdims=True), tied.shape).astype(BF16)
    rank = in_row + _dot(_tri(PAGE, "gt"), row_tot)
    pos = _iota(key.shape, 0) * PAGE + _iota(key.shape, 1)
    keep = ((key > thr) | ((key == thr) & (rank < need))) & (pos < n_keys)
    o_ref[0] = jnp.where(keep, 1.0, 0.0)


def _dec_select(scores, n_keys):
    DB = scores.shape[0]
    spec = pl.BlockSpec((1, PAGE, PAGE), lambda b: (b, 0, 0))
    return pl.pallas_call(
        functools.partial(_dec_select_kernel, n_keys=n_keys, n_keep=min(IDX_TOPK, n_keys // 4)), grid=(DB,),
        in_specs=[spec], out_specs=spec, out_shape=jax.ShapeDtypeStruct(scores.shape, F32),
        compiler_params=_cp("parallel"), name="dsa_select_s")(scores)


def _dec_dsa_kernel(pt_ref, q_ref, *refs, nps, n_pages):
    k_refs, v_refs = refs[:nps], refs[nps:2 * nps]
    sel_ref, kn_ref, vn_ref, o_ref, m_ref, l_ref, acc_ref = refs[2 * nps:]
    step = pl.program_id(1)
    q = q_ref[0]

    @pl.when(step == 0)
    def _():
        zn = jnp.sum(q.astype(F32) * kn_ref[0].astype(BF16).astype(F32), axis=1, keepdims=True)
        keep = jnp.broadcast_to(sel_ref[0, n_pages:n_pages + 1, 0:1] > 0.5, zn.shape)
        m_ref[...] = jnp.where(keep, zn, NEG)
        l_ref[...] = jnp.where(keep, 1.0, 0.0)
        acc_ref[...] = jnp.where(keep, jnp.broadcast_to(vn_ref[0].astype(BF16).astype(F32), acc_ref.shape), 0.0)

    keep = jnp.concatenate([sel_ref[0, pl.ds(step * nps + i, 1), :] for i in range(nps)], axis=1)
    s = _dot(q, _cat_pages(k_refs))
    _online_pages(s, jnp.broadcast_to(keep > 0.5, s.shape), _cat_pages(v_refs), m_ref, l_ref, acc_ref)

    @pl.when(step == pl.num_programs(1) - 1)
    def _():
        o_ref[0] = acc_ref[...] / l_ref[...]


def _dec_dsa(qx, kt, vt, sel, knew, vnew, page_table, layer):
    DB, n_pages = page_table.shape
    nps = math.gcd(n_pages, DEC_PAGES)
    pages = _page_specs(layer, n_pages, nps, False)
    grid_spec = pltpu.PrefetchScalarGridSpec(
        num_scalar_prefetch=1, grid=(DB, n_pages // nps),
        in_specs=[_head_spec(KV_W)] + pages + pages
        + [pl.BlockSpec((1, PAGE, PAGE), lambda b, s, pt: (b, 0, 0)), _row_spec(KV_W), _row_spec(KV_W)],
        out_specs=_head_spec(KV_W), scratch_shapes=_softmax_scratch())
    return _dec_call(functools.partial(_dec_dsa_kernel, nps=nps, n_pages=n_pages), "dsa_s", grid_spec, DB)(
        page_table, qx, *([kt] * nps), *([vt] * nps), sel, knew, vnew)


def _rope_tables(pos):
    half = HD // 2
    inv_freq = jnp.power(ROPE_THETA, -jnp.arange(half, dtype=F32) * (2.0 / HD))
    ang = pos.astype(F32)[:, None] * inv_freq[None, :]
    cos, sin = jnp.cos(ang), jnp.sin(ang)
    return jnp.concatenate([cos] * 4, axis=1), jnp.concatenate([-sin, sin] * 2, axis=1)


def _pad_cols(w, mult=128):
    pad = (-w.shape[1]) % mult
    return jnp.pad(w, ((0, 0), (0, pad))) if pad else w


def _layer_weights(kind, w_qkv, fox_w_f, idx_w_q, idx_w_k, idx_w_w):
    cols = [w_qkv]
    if kind == FOX:
        cols.append(_pad_cols(fox_w_f))
    if kind == DSA:
        cols += [idx_w_q, _pad_cols(idx_w_k), _pad_cols(idx_w_w)]
    return jnp.concatenate(cols, axis=1).astype(BF16)


_HEAD_ONEHOT = np.arange(N_HEADS)[:, None] // GROUP == np.arange(N_KV)[None, :]


def _expand_heads(q, dtype):
    m = jnp.asarray(_HEAD_ONEHOT, q.dtype)
    return (q[:, :, None, :] * m[None, :, :, None]).reshape(q.shape[0], N_HEADS, KV_W).astype(dtype)


def _collapse_heads(ox):
    m = jnp.asarray(_HEAD_ONEHOT, ox.dtype)
    o = jnp.sum(ox.reshape(ox.shape[0], N_HEADS, N_KV, HD) * m[None, :, :, None], axis=2)
    return o.reshape(ox.shape[0], D_MODEL).astype(BF16)


def _pages_t(cache):
    depth, n_pool = cache.shape[:2]
    return cache.transpose(0, 1, 3, 4, 2).reshape(depth, n_pool, KV_W, PAGE)


def kernel(x_prompt, x_sample, cache_k, cache_v, cache_logf, cache_kidx, state_conv, page_table, norm_mix, norm_ffn, norm_final, w_qkv, w_o, fox_w_f, fox_b_f, idx_w_q, idx_w_k, idx_w_w, ffn_w_a, ffn_w_b, ffn_conv_w, ffn_conv_b, ffn_w_down):
    B, T, _ = x_prompt.shape
    DB = x_sample.shape[0]
    depth = cache_k.shape[0]
    n_pages = page_table.shape[1]
    past = n_pages * PAGE
    assert T % MOBA_BLOCK == 0 and past % MOBA_BLOCK == 0 and x_sample.shape[1] == 1
    cache_kt, cache_vt = _pages_t(cache_k), _pages_t(cache_v)
    rope_p = _rope_tables(jnp.arange(T, dtype=I32))
    rope_s = _rope_tables(jnp.full((DB,), past, I32))

    xp = x_prompt
    xs = x_sample.reshape(1, DB, D_MODEL)
    new_k_p, new_v_p, new_k_s, new_v_s, conv_p, conv_s = [], [], [], [], [], []
    logf_p = logf_s = kidx_p = kidx_s = None
    for i in range(depth):
        kind = i % 4
        rope = kind in (MOBA, DSA)
        w_all = _layer_weights(kind, w_qkv[i], fox_w_f, idx_w_q, idx_w_k, idx_w_w)
        wo_b = w_o[i].astype(BF16)
        wa_b, wb_b, wd_b = ffn_w_a[i].astype(BF16), ffn_w_b[i].astype(BF16), ffn_w_down[i].astype(BF16)

        outs = _proj(xp, norm_mix[i], w_all, kind, rope_p if rope else None, fox_b_f, True)
        qhm, k32, v32, khm, vhm = outs[:5]
        vt = outs[-1]
        new_k_p.append(k32.reshape(B, T, N_KV, HD))
        new_v_p.append(v32.reshape(B, T, N_KV, HD))
        if kind == STICK:
            o = _stick_prompt(qhm, khm, vt)
        elif kind == FOX:
            logf_p = outs[5]
            o = _fox_prompt(qhm, khm, vt, _cumsum_time_parts(logf_p))
        elif kind == MOBA:
            o = _moba_prompt(qhm, outs[5], _block_mean(k32), khm, vt)
        else:
            kidx_p, wi, qi3, ki3 = outs[6:10]
            o = _dsa_prompt(qhm, qi3, wi, ki3, khm, vt)
        x2, rows = _mix_ffn(xp.reshape(B * T, D_MODEL), o.reshape(B * T, D_MODEL), wo_b, norm_ffn[i], wa_b, wb_b,
                            ffn_conv_w[i], ffn_conv_b[i], wd_b, T)
        conv_p.append(rows)
        xp = x2.reshape(B, T, D_MODEL)

        outs = _proj(xs, norm_mix[i], w_all, kind, rope_s if rope else None, fox_b_f, False)
        qhm, k32, v32 = outs[:3]
        new_k_s.append(k32.reshape(DB, 1, N_KV, HD))
        new_v_s.append(v32.reshape(DB, 1, N_KV, HD))
        qx = _expand_heads(qhm[0].transpose(1, 0, 2), BF16)
        knew, vnew = k32.reshape(DB, 1, KV_W), v32.reshape(DB, 1, KV_W)
        if kind == STICK:
            ox = _dec_stick(qx, cache_kt, cache_vt, page_table, i)
        elif kind == FOX:
            lf = outs[5][0]
            logf_s = lf.reshape(DB, 1, N_HEADS)
            ox = _dec_fox(qx, cache_kt, cache_vt, cache_logf.transpose(0, 2, 1), knew, vnew,
                          lf.reshape(DB, N_HEADS, 1), page_table, i)
        elif kind == MOBA:
            q32x = _expand_heads(outs[5][0].transpose(1, 0, 2), F32)
            ox = _dec_moba(qx, q32x, _dec_kmean(cache_kt, page_table, i), cache_kt, cache_vt, knew, vnew, page_table, i)
        else:
            qihm, ki, wi = outs[5:8]
            kidx_s = ki.reshape(DB, 1, IDX_DIM)
            scores = _dec_scores(qihm[0].transpose(1, 0, 2), wi[0].reshape(DB, IDX_HEADS, 1),
                                 cache_kidx.transpose(0, 2, 1), kidx_s, page_table)
            ox = _dec_dsa(qx, cache_kt, cache_vt, _dec_select(scores, past + 1), knew, vnew, page_table, i)
        st = state_conv[i]
        s2, a_new = _mix_ffn(xs.reshape(DB, D_MODEL), _collapse_heads(ox), wo_b, norm_ffn[i], wa_b, wb_b,
                             ffn_conv_w[i], ffn_conv_b[i], wd_b, 1, prev=(st[:, 0], st[:, 1]))
        conv_s.append(jnp.stack([st[:, 1], a_new], axis=1))
        xs = s2.reshape(1, DB, D_MODEL)

    y_p = _final_norm(xp.reshape(B * T, D_MODEL), norm_final).reshape(B, T, D_MODEL)
    y_s = _final_norm(xs.reshape(DB, D_MODEL), norm_final).reshape(DB, 1, D_MODEL)
    return (y_p, y_s, jnp.stack(new_k_p), jnp.stack(new_v_p), jnp.stack(new_k_s), jnp.stack(new_v_s),
            logf_p, logf_s, kidx_p, kidx_s, jnp.stack(conv_p), jnp.stack(conv_s))
```

```python
import functools
import math

import jax
import jax.numpy as jnp
import numpy as np
from jax import lax
from jax.experimental import pallas as pl
from jax.experimental.pallas import tpu as pltpu

F32, BF16, I32, I16 = jnp.float32, jnp.bfloat16, jnp.int32, jnp.int16

D_MODEL = 1024
N_HEADS = 16
N_KV = 4
GROUP = N_HEADS // N_KV
HD = 64
KV_W = N_KV * HD
D_FF = 2816
CONV_W = 3
PAGE = 128
MOBA_BLOCK = 256
MOBA_TOPK = 3
IDX_HEADS = 8
IDX_DIM = 64
IDX_TOPK = 256
ROPE_THETA = 10000.0
RMS_EPS = 1e-6
STICK, FOX, MOBA, DSA = 0, 1, 2, 3
NEG = -1e30
M_FLOOR = -1e29
DEAD_LOG = -104.0
INT_MIN = -2147483648
VMEM_LIMIT = 48 * 1024 * 1024
FF_CHUNK = 256
FFN_TM = 512
PROJ_TM = 256
KV_TILE = 256
DEC_PAGES = 32
KV_PER_STEP = 4
Q_CHUNK = 128
FOX_AUG = 3 * GROUP


def _cp(*sem):
    return pltpu.CompilerParams(dimension_semantics=sem, vmem_limit_bytes=VMEM_LIMIT)


def _dot(a, b):
    return jnp.dot(a, b, preferred_element_type=F32)


def _dot_nt(a, b):
    return lax.dot_general(a, b, (((1,), (1,)), ((), ())), preferred_element_type=F32)


def _split2(x):
    hi = x.astype(BF16)
    lo = (x - hi.astype(F32)).astype(BF16)
    return hi, lo


def _split3(x):
    hi = x.astype(BF16)
    r = x - hi.astype(F32)
    mid = r.astype(BF16)
    lo = (r - mid.astype(F32)).astype(BF16)
    return hi, mid, lo


def _dot_f32(a, b):
    ah, al = _split2(a)
    bh, bl = _split2(b)
    return _dot(ah, bh) + _dot(al, bh) + _dot(ah, bl)


def _dot_nt_f32(a, b):
    ah, al = _split2(a)
    bh, bl = _split2(b)
    return _dot_nt(ah, bh) + _dot_nt(al, bh) + _dot_nt(ah, bl)


def _log_sigmoid_pair(z):
    l1p = jnp.log(1.0 + jnp.exp(-jnp.abs(z)))
    return jnp.minimum(z, 0.0) - l1p, jnp.minimum(-z, 0.0) - l1p


def _iota(shape, dim):
    return lax.broadcasted_iota(I32, shape, dim)


def _tri(n, kind):
    r, c = _iota((n, n), 0), _iota((n, n), 1)
    m = {"gt": r > c, "lt": r < c, "ge": r >= c}[kind]
    return jnp.where(m, 1.0, 0.0).astype(BF16)


def _rms_bf16(x, g):
    y = x * lax.rsqrt(jnp.mean(x * x, axis=-1, keepdims=True) + RMS_EPS)
    return (y * g).astype(BF16)


def _rope128(xc, cos, sin):
    lane = _iota(xc.shape, 1)
    first = (lane % HD) < (HD // 2)
    rolled = jnp.where(first, pltpu.roll(xc, 128 - HD // 2, 1), pltpu.roll(xc, HD // 2, 1))
    return xc * cos + rolled * sin


def _halves(xc):
    return xc[:, :HD], pltpu.roll(xc, HD, 1)[:, :HD]


def _place(parts):
    r, c = _iota((HD, KV_W), 0), _iota((HD, KV_W), 1)
    out = None
    for j, pj in enumerate(parts):
        e = jnp.where(c == r + HD * j, 1.0, 0.0).astype(BF16)
        t = _dot(pj, e)
        out = t if out is None else out + t
    return out


def _stack_heads(q_ref):
    return jnp.concatenate([q_ref[0, j] for j in range(GROUP)], axis=0)


def _init_softmax(m_ref, l_ref, acc_ref):
    m_ref[...] = jnp.full(m_ref.shape, M_FLOOR, F32)
    l_ref[...] = jnp.zeros_like(l_ref)
    acc_ref[...] = jnp.zeros_like(acc_ref)


def _online_t(s, valid, vt, m_ref, l_ref, acc_ref, idx, bias=None):
    m_all, l_all, acc_all = m_ref[idx], l_ref[idx], acc_ref[idx]
    m_out, l_out, acc_out = [], [], []
    for c0 in range(0, s.shape[1], Q_CHUNK):
        cols = slice(c0, c0 + Q_CHUNK)
        sc = s[:, cols]
        if bias is not None:
            sc = sc + bias[:, cols]
        if valid is not None:
            vc = valid[:, cols]
            sc = jnp.where(vc, sc, NEG)
        m_old = m_all[:, cols]
        m_new = jnp.maximum(m_old, jnp.max(sc, axis=0, keepdims=True))
        alpha = jnp.exp(m_old - m_new)
        p = jnp.exp(sc - m_new)
        if valid is not None:
            p = jnp.where(vc, p, 0.0)
        l_out.append(alpha * l_all[:, cols] + jnp.sum(p, axis=0, keepdims=True))
        acc_out.append(alpha * acc_all[:, cols] + _dot(vt, p.astype(BF16)))
        m_out.append(m_new)
    m_ref[idx] = jnp.concatenate(m_out, axis=1)
    l_ref[idx] = jnp.concatenate(l_out, axis=1)
    acc_ref[idx] = jnp.concatenate(acc_out, axis=1)


def _group_out_t(acc_t, l, tq):
    r = acc_t if l is None else acc_t / l
    o_t = jnp.concatenate([r[:, j * tq:(j + 1) * tq] for j in range(GROUP)], axis=0)
    return o_t.T.astype(BF16)


def _topk_select(gate, n_valid, k, axis):
    pos = _iota(gate.shape, axis)
    posf = pos.astype(F32)
    g = jnp.where(pos < n_valid, gate, -jnp.inf)
    sel = jnp.zeros(gate.shape, F32)
    for _ in range(k):
        best = jnp.max(g, axis=axis, keepdims=True)
        first = jnp.min(jnp.where(g == best, posf, float(gate.shape[axis])), axis=axis, keepdims=True)
        hit = posf == first
        sel = jnp.where(hit, 1.0, sel)
        g = jnp.where(hit, -jnp.inf, g)
    return jnp.where(pos < n_valid, sel, 0.0)


def _order_key(score):
    bits = lax.bitcast_convert_type(score, I32)
    return jnp.where(bits < 0, bits ^ jnp.int32(0x7FFFFFFF), bits)


def _kth_largest_key(count_ge, shape, k):
    def body(it, v):
        cand = v + lax.shift_left(jnp.int32(1), 31 - it)
        return jnp.where(count_ge(cand) >= k, cand, v)

    return lax.fori_loop(0, 32, body, jnp.full(shape, INT_MIN, I32))


def _proj_kernel(*refs, kind, rope, prompt, aug):
    it = iter(refs)
    x_ref, g_ref, w_ref = next(it), next(it), next(it)
    bf_ref = next(it) if kind == FOX else None
    cos_ref, sin_ref = (next(it), next(it)) if rope else (None, None)
    qpat_ref = next(it) if aug else None
    kpat_ref = next(it) if aug == "qk" else None
    qhm_ref, k32_ref, v32_ref, khm_ref, vhm_ref = next(it), next(it), next(it), next(it), next(it)
    low = _iota((x_ref.shape[1], 128), 1) < HD
    h = _rms_bf16(x_ref[0], g_ref[...])
    y = _dot(h, w_ref[...])
    cos = cos_ref[...] if rope else None
    sin = sin_ref[...] if rope else None

    def chunk(c, rot):
        xc = y[:, 128 * c:128 * (c + 1)]
        return _rope128(xc, cos, sin) if rot else xc

    q32hm_ref = next(it) if kind == MOBA else None
    for c in range(N_HEADS // 2):
        qc = chunk(c, rope)
        lo, hi = _halves(qc)
        if aug:
            for j, z in enumerate((qc, pltpu.roll(qc, HD, 1))):
                qhm_ref[0, 2 * c + j] = jnp.where(low, z * 0.125, qpat_ref[2 * c + j:2 * c + j + 1, :]).astype(BF16)
        else:
            qhm_ref[0, 2 * c] = (lo * 0.125).astype(BF16)
            qhm_ref[0, 2 * c + 1] = (hi * 0.125).astype(BF16)
        if kind == MOBA:
            q32hm_ref[0, 2 * c] = lo
            q32hm_ref[0, 2 * c + 1] = hi
    base = D_MODEL // 128
    for c in range(N_KV // 2):
        kc = chunk(base + c, rope)
        k32_ref[0, :, 128 * c:128 * (c + 1)] = kc
        vc = chunk(base + N_KV // 2 + c, False)
        v32_ref[0, :, 128 * c:128 * (c + 1)] = vc
        if aug == "qk":
            for j, z in enumerate((kc, pltpu.roll(kc, HD, 1))):
                khm_ref[0, 2 * c + j] = jnp.where(low, z, kpat_ref[...]).astype(BF16)
        else:
            klo, khi = _halves(kc)
            khm_ref[0, 2 * c] = klo.astype(BF16)
            khm_ref[0, 2 * c + 1] = khi.astype(BF16)
        vlo, vhi = _halves(vc)
        vhm_ref[0, 2 * c] = vlo.astype(BF16)
        vhm_ref[0, 2 * c + 1] = vhi.astype(BF16)
    base = (D_MODEL + 2 * KV_W) // 128
    if kind == FOX:
        logf_ref = next(it)
        zf = chunk(base, False)[:, :N_HEADS] + bf_ref[...]
        logf_ref[0] = _log_sigmoid_pair(zf)[0]
    if kind == DSA:
        qihm_ref, ki_ref, wi_ref = next(it), next(it), next(it)
        for c in range(IDX_HEADS // 2):
            lo, hi = _halves(chunk(base + c, True))
            qihm_ref[0, 2 * c] = lo
            qihm_ref[0, 2 * c + 1] = hi
        kic = chunk(base + IDX_HEADS // 2, True)
        ki_ref[0] = kic[:, :IDX_DIM]
        wi_ref[0] = chunk(base + IDX_HEADS // 2 + 1, False)[:, :IDX_HEADS] * (IDX_HEADS ** -0.5 * IDX_DIM ** -0.5)
        if prompt:
            qi3_ref, ki3_ref = next(it), next(it)
            low = _iota(kic.shape, 1) < IDX_DIM
            for c in range(IDX_HEADS // 2):
                xc = chunk(base + c, True)
                for j, z in enumerate((jnp.where(low, xc, 0.0), jnp.where(low, pltpu.roll(xc, IDX_DIM, 1), 0.0))):
                    hi = z.astype(BF16).astype(F32)
                    qi3_ref[0, 2 * c + j] = jnp.concatenate([hi + pltpu.roll(z - hi, IDX_DIM, 1), hi], axis=1).astype(BF16)
            hi = kic.astype(BF16).astype(F32)
            ki3_ref[0] = jnp.concatenate([hi + pltpu.roll(hi, IDX_DIM, 1), kic - hi], axis=1).astype(BF16)
    if prompt:
        vt_ref = next(it)
        vt_ref[0, 0] = y[:, D_MODEL + KV_W:D_MODEL + 2 * KV_W].T.astype(BF16)


def _score_features(kind, T):
    qpat = np.zeros((N_HEADS, 128), np.float32)
    if kind == FOX:
        for h in range(N_HEADS):
            qpat[h, HD + 3 * (h % GROUP):HD + 3 * (h % GROUP) + 3] = -1.0
        return "q", [jnp.asarray(qpat)]
    kpat = np.zeros((T, 128), np.float32)
    kpat[np.arange(T), HD + np.arange(T) // MOBA_BLOCK] = 1.0
    return "qk", [jnp.asarray(qpat), jnp.asarray(kpat)]


def _proj(x, g, w_all, kind, rope_tabs, fox_b, prompt):
    B, T, _ = x.shape
    tm = min(PROJ_TM, T)
    n_cols = w_all.shape[1]
    rope = rope_tabs is not None
    aug, pats = _score_features(kind, T) if prompt and kind in (FOX, MOBA) else (None, [])
    q_w = 2 * HD if aug else HD
    k_w = 2 * HD if aug == "qk" else HD
    row = lambda b, i: (b, i, 0)
    hm = lambda b, i: (b, 0, i, 0)
    in_specs = [pl.BlockSpec((1, tm, D_MODEL), row), pl.BlockSpec((1, D_MODEL), lambda b, i: (0, 0)),
                pl.BlockSpec((D_MODEL, n_cols), lambda b, i: (0, 0))]
    args = [x, g.reshape(1, D_MODEL), w_all]
    if kind == FOX:
        in_specs.append(pl.BlockSpec((1, N_HEADS), lambda b, i: (0, 0)))
        args.append(fox_b.reshape(1, N_HEADS))
    if rope:
        in_specs += [pl.BlockSpec((tm, 128), lambda b, i: (i, 0))] * 2
        args += list(rope_tabs)
    if aug:
        in_specs.append(pl.BlockSpec((N_HEADS, 128), lambda b, i: (0, 0)))
    if aug == "qk":
        in_specs.append(pl.BlockSpec((tm, 128), lambda b, i: (i, 0)))
    args += pats
    out_shape = [jax.ShapeDtypeStruct((B, N_HEADS, T, q_w), BF16), jax.ShapeDtypeStruct((B, T, KV_W), F32),
                 jax.ShapeDtypeStruct((B, T, KV_W), F32), jax.ShapeDtypeStruct((B, N_KV, T, k_w), BF16),
                 jax.ShapeDtypeStruct((B, N_KV, T, HD), BF16)]
    out_specs = [pl.BlockSpec((1, N_HEADS, tm, q_w), hm), pl.BlockSpec((1, tm, KV_W), row),
                 pl.BlockSpec((1, tm, KV_W), row), pl.BlockSpec((1, N_KV, tm, k_w), hm),
                 pl.BlockSpec((1, N_KV, tm, HD), hm)]
    if kind == MOBA:
        out_shape.append(jax.ShapeDtypeStruct((B, N_HEADS, T, HD), F32))
        out_specs.append(pl.BlockSpec((1, N_HEADS, tm, HD), hm))
    if kind == FOX:
        out_shape.append(jax.ShapeDtypeStruct((B, T, N_HEADS), F32))
        out_specs.append(pl.BlockSpec((1, tm, N_HEADS), row))
    if kind == DSA:
        out_shape += [jax.ShapeDtypeStruct((B, IDX_HEADS, T, IDX_DIM), F32),
                      jax.ShapeDtypeStruct((B, T, IDX_DIM), F32), jax.ShapeDtypeStruct((B, T, IDX_HEADS), F32)]
        out_specs += [pl.BlockSpec((1, IDX_HEADS, tm, IDX_DIM), hm), pl.BlockSpec((1, tm, IDX_DIM), row),
                      pl.BlockSpec((1, tm, IDX_HEADS), row)]
        if prompt:
            out_shape += [jax.ShapeDtypeStruct((B, IDX_HEADS, T, 4 * IDX_DIM), BF16),
                          jax.ShapeDtypeStruct((B, T, 4 * IDX_DIM), BF16)]
            out_specs += [pl.BlockSpec((1, IDX_HEADS, tm, 4 * IDX_DIM), hm), pl.BlockSpec((1, tm, 4 * IDX_DIM), row)]
    if prompt:
        assert tm == KV_TILE
        out_shape.append(jax.ShapeDtypeStruct((B, T // tm, KV_W, tm), BF16))
        out_specs.append(pl.BlockSpec((1, 1, KV_W, tm), lambda b, i: (b, i, 0, 0)))
    return pl.pallas_call(
        functools.partial(_proj_kernel, kind=kind, rope=rope, prompt=prompt, aug=aug),
        grid=(B, T // tm), in_specs=in_specs, out_specs=out_specs, out_shape=out_shape,
        compiler_params=_cp("parallel", "arbitrary"), name=f"proj_{kind}_{'p' if prompt else 's'}")(*args)


def _gelu_tanh(x):
    return 0.5 * x * (1.0 + jnp.tanh(0.7978845608028654 * (x + 0.044715 * x * x * x)))


def _ffn_kernel(*refs, prompt, tiles_per_seq):
    if prompt:
        x_ref, o_ref, wo_ref, g_ref, wa_ref, wb_ref, cw_ref, cb_ref, wd_ref, y_ref, conv_ref, u_ref, carry_ref = refs
    else:
        x_ref, o_ref, wo_ref, g_ref, wa_ref, wb_ref, cw_ref, cb_ref, wd_ref, p2_ref, p1_ref, y_ref, a_ref, u_ref = refs
    x = x_ref[...] + _dot(o_ref[...], wo_ref[...])
    h = _rms_bf16(x, g_ref[...])
    if prompt:
        @pl.when(pl.program_id(0) % tiles_per_seq == 0)
        def _():
            carry_ref[...] = jnp.zeros_like(carry_ref)

        row = _iota((x.shape[0], FF_CHUNK), 0)
    for c in range(D_FF // FF_CHUNK):
        cols = slice(c * FF_CHUNK, (c + 1) * FF_CHUNK)
        a = _dot(h, wa_ref[:, cols])
        gate = _dot(h, wb_ref[:, cols])
        if prompt:
            prev2, prev1 = carry_ref[0:1, cols], carry_ref[1:2, cols]
            back2 = pltpu.roll(a, 2, 0)
            a_m1 = jnp.where(row == 0, prev1, pltpu.roll(a, 1, 0))
            a_m2 = jnp.where(row == 0, prev2, jnp.where(row == 1, prev1, back2))
            carry_ref[0:2, cols] = back2[0:2, :]
            conv_ref[0, :, cols] = back2[0:2, :]
        else:
            a_m2, a_m1 = p2_ref[:, cols], p1_ref[:, cols]
            a_ref[:, cols] = a
        conv = cb_ref[:, cols] + cw_ref[0:1, cols] * a_m2 + cw_ref[1:2, cols] * a_m1 + cw_ref[2:3, cols] * a
        u_ref[:, cols] = (_gelu_tanh(conv) * gate).astype(BF16)
    y_ref[...] = x + _dot(u_ref[...], wd_ref[...])


def _mix_ffn(x2, o2, w_o, g, w_a, w_b, conv_w, conv_b, w_down, seq_len, prev=None):
    M = x2.shape[0]
    prompt = prev is None
    tm = min(FFN_TM, seq_len) if prompt else M
    tps = seq_len // tm if prompt else 1
    xs = pl.BlockSpec((tm, D_MODEL), lambda i: (i, 0))
    whole = lambda shape: pl.BlockSpec(shape, lambda i: (0,) * len(shape), pipeline_mode=pl.Buffered(1))
    in_specs = [xs, xs, whole((D_MODEL, D_MODEL)), whole((1, D_MODEL)), whole((D_MODEL, D_FF)), whole((D_MODEL, D_FF)),
                whole((CONV_W, D_FF)), whole((1, D_FF)), whole((D_FF, D_MODEL))]
    args = [x2, o2, w_o, g.reshape(1, D_MODEL), w_a, w_b, conv_w, conv_b.reshape(1, D_FF), w_down]
    scratch = [pltpu.VMEM((tm, D_FF), BF16)]
    if prompt:
        out_shape = [jax.ShapeDtypeStruct((M, D_MODEL), F32), jax.ShapeDtypeStruct((M // tm, CONV_W - 1, D_FF), F32)]
        out_specs = [xs, pl.BlockSpec((1, CONV_W - 1, D_FF), lambda i: (i, 0, 0))]
        scratch.append(pltpu.VMEM((8, D_FF), F32))
    else:
        ps = pl.BlockSpec((tm, D_FF), lambda i: (i, 0))
        in_specs += [ps, ps]
        args += list(prev)
        out_shape = [jax.ShapeDtypeStruct((M, D_MODEL), F32), jax.ShapeDtypeStruct((M, D_FF), F32)]
        out_specs = [xs, ps]
    y, extra = pl.pallas_call(
        functools.partial(_ffn_kernel, prompt=prompt, tiles_per_seq=tps), grid=(M // tm,),
        in_specs=in_specs, out_specs=out_specs, out_shape=out_shape, scratch_shapes=scratch,
        compiler_params=_cp("arbitrary"), name="ffn_p" if prompt else "ffn_s")(*args)
    return y, (extra[tps - 1::tps] if prompt else extra)


def _final_norm_kernel(x_ref, g_ref, y_ref):
    x = x_ref[...]
    y_ref[...] = x * lax.rsqrt(jnp.mean(x * x, axis=-1, keepdims=True) + RMS_EPS) * g_ref[...]


def _final_norm(x2, g):
    M = x2.shape[0]
    tm = min(1024, M)
    return pl.pallas_call(
        _final_norm_kernel, grid=(M // tm,),
        in_specs=[pl.BlockSpec((tm, D_MODEL), lambda i: (i, 0)), pl.BlockSpec((1, D_MODEL), lambda i: (0, 0))],
        out_specs=pl.BlockSpec((tm, D_MODEL), lambda i: (i, 0)),
        out_shape=jax.ShapeDtypeStruct((M, D_MODEL), F32), compiler_params=_cp("parallel"), name="final_norm")(x2, g.reshape(1, D_MODEL))


def _stick_kernel(q_ref, k_ref, vt_ref, o_ref, r_ref, acc_ref, *, tq, tk, nh):
    q0 = pl.program_id(2) * tq
    qs = [jnp.concatenate([q_ref[0, GROUP * h + j] for j in range(GROUP)], axis=0) for h in range(nh)]
    m = GROUP * tq
    later = _tri(tk, "lt")
    qpos = q0 + _iota((1, m), 1) % tq
    r_ref[...] = jnp.zeros_like(r_ref)
    acc_ref[...] = jnp.zeros_like(acc_ref)

    def tile(j, masked):
        ks = pl.multiple_of(j * tk, tk)
        zs = [_dot_nt(k_ref[0, h, pl.ds(ks, tk), :], qs[h]) for h in range(nh)]
        for h in range(nh):
            head_tile(j, h, zs[h], masked)

    def head_tile(j, h, z, masked):
        vt = vt_ref[0, j, HD * h:HD * (h + 1), :]
        kpos = j * tk + _iota((tk, 1), 0)
        r_all, acc_all = r_ref[h], acc_ref[h]
        chunks = [slice(c0, c0 + Q_CHUNK) for c0 in range(0, m, Q_CHUNK)]
        staged, r_out, acc_out = [], [], []
        for cols in chunks:
            zc = z[:, cols]
            ls = jnp.minimum(zc, 0.0) - jnp.log(1.0 + jnp.exp(-jnp.abs(zc)))
            lk = ls - zc
            past = kpos < qpos[:, cols] if masked else None
            if masked:
                lk = jnp.where(past, lk, 0.0)
            parts = _dot(later, jnp.concatenate(_split2(lk), axis=1))
            staged.append((ls, parts, past))
            r_out.append(r_all[:, cols] + jnp.sum(lk, axis=0, keepdims=True))
        for cols, (ls, parts, past) in zip(chunks, staged):
            w = jnp.exp(ls + parts[:, :Q_CHUNK] + parts[:, Q_CHUNK:] + r_all[:, cols])
            if masked:
                w = jnp.where(past, w, 0.0)
            acc_out.append(acc_all[:, cols] + _dot(vt, w.astype(BF16)))
        r_ref[h] = jnp.concatenate(r_out, axis=1)
        acc_ref[h] = jnp.concatenate(acc_out, axis=1)

    jd = q0 // tk
    tile(jd, True)

    def live():
        return jnp.max(r_ref[...]) > DEAD_LOG

    def body(carry):
        n, _ = carry
        tile(jd - 1 - n, False)
        return n + 1, live()

    lax.while_loop(lambda c: (c[0] < jd) & c[1], body, (jnp.int32(0), live()))
    for h in range(nh):
        o_ref[0, :, KV_W * h:KV_W * (h + 1)] = _group_out_t(acc_ref[h], None, tq)


def _stick_prompt(qhm, khm, vt):
    B, _, T, _ = qhm.shape
    tq, tk, nh = 256, KV_TILE, KV_PER_STEP
    m = GROUP * tq
    return pl.pallas_call(
        functools.partial(_stick_kernel, tq=tq, tk=tk, nh=nh), grid=(B, N_KV // nh, T // tq),
        in_specs=[pl.BlockSpec((1, nh * GROUP, tq, HD), lambda b, g, i: (b, g, i, 0)),
                  pl.BlockSpec((1, nh, T, HD), lambda b, g, i: (b, g, 0, 0)),
                  pl.BlockSpec((1, T // tk, nh * HD, tk), lambda b, g, i: (b, 0, g, 0))],
        out_specs=pl.BlockSpec((1, tq, nh * KV_W), lambda b, g, i: (b, i, g)),
        out_shape=jax.ShapeDtypeStruct((B, T, D_MODEL), BF16),
        scratch_shapes=[pltpu.VMEM((nh, 1, m), F32), pltpu.VMEM((nh, HD, m), F32)],
        compiler_params=_cp("parallel", "parallel", "arbitrary"), name="stick_p")(qhm, khm, vt)


def _cumsum_kernel(x_ref, hi_ref, mid_ref, lo_ref, carry_ref):
    @pl.when(pl.program_id(1) == 0)
    def _():
        carry_ref[...] = jnp.zeros_like(carry_ref)

    n = x_ref.shape[1]
    low = _tri(n, "ge")
    hi, mid, lo = _split3(x_ref[0])
    c = _dot(low, hi) + _dot(low, mid) + _dot(low, lo) + carry_ref[...]
    carry_ref[...] = c[n - 1:n, :]
    hi_ref[0], mid_ref[0], lo_ref[0] = _split3(c)


def _cumsum_time_parts(x):
    B, T, C = x.shape
    n = 256
    spec = pl.BlockSpec((1, n, C), lambda b, i: (b, i, 0))
    return pl.pallas_call(
        _cumsum_kernel, grid=(B, T // n), in_specs=[spec], out_specs=[spec] * 3,
        out_shape=[jax.ShapeDtypeStruct((B, T, C), BF16)] * 3, scratch_shapes=[pltpu.VMEM((1, C), F32)],
        compiler_params=_cp("parallel", "arbitrary"), name="cumsum_time")(x)


def _softmax_scratch_t(n, m):
    return [pltpu.VMEM((n, 1, m), F32), pltpu.VMEM((n, 1, m), F32), pltpu.VMEM((n, HD, m), F32)]


def _causal_kernel(q_ref, k_ref, vt_ref, o_ref, m_ref, l_ref, acc_ref, *, tq, tk, nh):
    q0 = pl.program_id(2) * tq
    qs = [jnp.concatenate([q_ref[0, GROUP * h + j] for j in range(GROUP)], axis=0) for h in range(nh)]
    m = GROUP * tq
    _init_softmax(m_ref, l_ref, acc_ref)
    qpos = q0 + _iota((1, m), 1) % tq

    def tile(j, valid):
        ks = pl.multiple_of(j * tk, tk)
        s = [_dot_nt(k_ref[0, h, pl.ds(ks, tk), :], qs[h]) for h in range(nh)]
        for h in range(nh):
            _online_t(s[h], valid, vt_ref[0, j, HD * h:HD * (h + 1), :], m_ref, l_ref, acc_ref, h)

    jd = q0 // tk
    tile(jd, (jd * tk + _iota((tk, 1), 0)) <= qpos)

    def body(n, carry):
        tile(n, None)
        return carry

    lax.fori_loop(0, jd, body, 0)
    for h in range(nh):
        o_ref[0, :, KV_W * h:KV_W * (h + 1)] = _group_out_t(acc_ref[h], l_ref[h], tq)


def _fox_prompt(q_aug, khm, vt, cum_parts):
    B, _, T, d_aug = q_aug.shape
    tq, tk = 256, KV_TILE
    c = jnp.stack(cum_parts, axis=-1).reshape(B, T, N_KV, FOX_AUG).transpose(0, 2, 1, 3)
    k_aug = jnp.concatenate([khm, c, jnp.zeros((B, N_KV, T, d_aug - HD - FOX_AUG), BF16)], axis=-1)
    m = GROUP * tq
    return pl.pallas_call(
        functools.partial(_causal_kernel, tq=tq, tk=tk, nh=KV_PER_STEP), grid=(B, N_KV // KV_PER_STEP, T // tq),
        in_specs=[pl.BlockSpec((1, KV_PER_STEP * GROUP, tq, d_aug), lambda b, g, i: (b, g, i, 0)),
                  pl.BlockSpec((1, KV_PER_STEP, T, d_aug), lambda b, g, i: (b, g, 0, 0)),
                  pl.BlockSpec((1, T // tk, KV_PER_STEP * HD, tk), lambda b, g, i: (b, 0, g, 0))],
        out_specs=pl.BlockSpec((1, tq, KV_PER_STEP * KV_W), lambda b, g, i: (b, i, g)),
        out_shape=jax.ShapeDtypeStruct((B, T, D_MODEL), BF16),
        scratch_shapes=_softmax_scratch_t(KV_PER_STEP, m),
        compiler_params=_cp("parallel", "parallel", "arbitrary"), name="fox_p")(q_aug, k_aug, vt)


def _block_mean_kernel(k_ref, o_ref):
    k = k_ref[0]
    nb = k.shape[0] // MOBA_BLOCK
    o_ref[0] = jnp.sum(k.reshape(nb, MOBA_BLOCK, KV_W), axis=1) * (1.0 / MOBA_BLOCK)


def _block_mean(k32):
    B, T, _ = k32.shape
    nb = T // MOBA_BLOCK
    return pl.pallas_call(
        _block_mean_kernel, grid=(B,), in_specs=[pl.BlockSpec((1, T, KV_W), lambda b: (b, 0, 0))],
        out_specs=pl.BlockSpec((1, nb, KV_W), lambda b: (b, 0, 0)),
        out_shape=jax.ShapeDtypeStruct((B, nb, KV_W), F32), compiler_params=_cp("parallel"), name="block_mean")(k32)


def _moba_kernel(q_ref, q32_ref, km_ref, k_ref, vt_ref, o_ref, m_ref, l_ref, acc_ref, *, tq, nb, nh):
    tk = MOBA_BLOCK
    q0 = pl.program_id(2) * tq
    own = q0 // tk
    stack = lambda ref, h: jnp.concatenate([ref[0, GROUP * h + j] for j in range(GROUP)], axis=0)
    m = GROUP * tq
    _init_softmax(m_ref, l_ref, acc_ref)
    row = _iota((nb, m), 0)
    qs = []
    for h in range(nh):
        sel = _topk_select(_dot_nt_f32(km_ref[0, h], stack(q32_ref, h)), own, MOBA_TOPK, 0)
        bias = jnp.where((sel > 0.5) | (row == own), 0.0, NEG)
        bias = jnp.concatenate([jnp.zeros((HD, m), F32), bias, jnp.zeros((HD - nb, m), F32)], axis=0)
        qs.append(stack(q_ref, h) + bias.T.astype(BF16))
    qpos = q0 + _iota((1, m), 1) % tq

    def tile(j, valid):
        ks = pl.multiple_of(j * tk, tk)
        s = [_dot_nt(k_ref[0, h, pl.ds(ks, tk), :], qs[h]) for h in range(nh)]
        for h in range(nh):
            _online_t(s[h], valid, vt_ref[0, j, HD * h:HD * (h + 1), :], m_ref, l_ref, acc_ref, h)

    tile(own, (own * tk + _iota((tk, 1), 0)) <= qpos)

    def body(j, carry):
        tile(j, None)
        return carry

    lax.fori_loop(0, own, body, 0)
    for h in range(nh):
        o_ref[0, :, KV_W * h:KV_W * (h + 1)] = _group_out_t(acc_ref[h], l_ref[h], tq)


def _moba_prompt(q_aug, q32hm, kmean, k_aug, vt):
    B, _, T, _ = q_aug.shape
    tq, nh = 256, KV_PER_STEP
    nb = T // MOBA_BLOCK
    assert nb <= HD
    m = GROUP * tq
    km = kmean.reshape(B, nb, N_KV, HD).transpose(0, 2, 1, 3)
    return pl.pallas_call(
        functools.partial(_moba_kernel, tq=tq, nb=nb, nh=nh), grid=(B, N_KV // nh, T // tq),
        in_specs=[pl.BlockSpec((1, nh * GROUP, tq, 2 * HD), lambda b, g, i: (b, g, i, 0)),
                  pl.BlockSpec((1, nh * GROUP, tq, HD), lambda b, g, i: (b, g, i, 0)),
                  pl.BlockSpec((1, nh, nb, HD), lambda b, g, i: (b, g, 0, 0)),
                  pl.BlockSpec((1, nh, T, 2 * HD), lambda b, g, i: (b, g, 0, 0)),
                  pl.BlockSpec((1, nb, nh * HD, MOBA_BLOCK), lambda b, g, i: (b, 0, g, 0))],
        out_specs=pl.BlockSpec((1, tq, nh * KV_W), lambda b, g, i: (b, i, g)),
        out_shape=jax.ShapeDtypeStruct((B, T, D_MODEL), BF16),
        scratch_shapes=_softmax_scratch_t(nh, m),
        compiler_params=_cp("parallel", "parallel", "arbitrary"), name="moba_p")(q_aug, q32hm, km, k_aug, vt)


def _dsa_kernel(q_ref, qi_ref, wi_ref, ki_ref, k_ref, vt_ref, o_ref, key_ref, hi_ref, lo_ref, eq_ref, m_ref, l_ref, acc_ref, *, tq, n_keep):
    tk = KV_TILE
    q0 = pl.program_id(1) * tq
    n_proc = (q0 + tq + tk - 1) // tk
    qpos = q0 + _iota((1, tq), 1)

    def visible(c):
        return (c * tk + _iota((tk, 1), 0)) <= qpos

    qi3 = jnp.concatenate([qi_ref[0, i] for i in range(IDX_HEADS)], axis=0)

    def score_tile(c, carry):
        ks = pl.multiple_of(c * tk, tk)
        dots = _dot_nt(ki_ref[0, pl.ds(ks, tk), :], qi3)
        sc = jnp.zeros((tk, tq), F32)
        for i in range(IDX_HEADS):
            sc = sc + wi_ref[0, i:i + 1, :] * jnp.maximum(dots[:, i * tq:(i + 1) * tq], 0.0)
        key = _order_key(jnp.where(visible(c), sc + 0.0, -jnp.inf))
        key_ref[c] = key
        hi_ref[c] = lax.shift_right_arithmetic(key, 16).astype(I16)
        lo_ref[c] = ((key & 0xFFFF) - 32768).astype(I16)
        return carry

    lax.fori_loop(0, n_proc, score_tile, 0)

    def count16(term):
        def body(c, part):
            return part + term(c, jnp.zeros((tk, tq), I16))

        return jnp.sum(lax.fori_loop(0, n_proc, body, jnp.zeros((tk, tq), I16)).astype(F32), axis=0, keepdims=True)

    def kth16(count_ge):
        def body(it, v):
            cand = v + lax.shift_left(jnp.int32(1), 15 - it)
            return jnp.where(count_ge(cand.astype(I16)) >= n_keep, cand, v)

        return lax.fori_loop(0, 16, body, jnp.full((1, tq), -32768, I32))

    one = jnp.ones((tk, tq), I16)
    thr_hi = kth16(lambda v: count16(lambda c, z: jnp.where(hi_ref[c] >= v, one, z)))
    thr_hi16 = thr_hi.astype(I16)
    above = count16(lambda c, z: jnp.where(hi_ref[c] > thr_hi16, one, z))

    def mark(c, carry):
        eq_ref[c] = jnp.where(hi_ref[c] == thr_hi16, one, jnp.zeros((tk, tq), I16))
        return carry

    lax.fori_loop(0, n_proc, mark, 0)
    thr_lo = kth16(lambda v: above + count16(lambda c, z: jnp.where(lo_ref[c] >= v, eq_ref[c], z)))
    thr_lo16 = thr_lo.astype(I16)
    thr = thr_hi * 65536 + (thr_lo + 32768)
    need = n_keep - above - count16(lambda c, z: jnp.where(lo_ref[c] > thr_lo16, eq_ref[c], z))
    _init_softmax(m_ref, l_ref, acc_ref)
    before = _tri(tk, "gt")
    qs = [jnp.concatenate([q_ref[0, GROUP * g + j] for j in range(GROUP)], axis=0) for g in range(N_KV)]

    def attend(c, ties_seen):
        ks = pl.multiple_of(c * tk, tk)
        key = key_ref[c]
        tied = jnp.where(key == thr, 1.0, 0.0)
        rank = _dot(before, tied.astype(BF16)) + ties_seen
        keep = (key > thr) | ((key == thr) & (rank < need))
        bias = jnp.where(keep & visible(c), 0.0, NEG)
        bias = jnp.concatenate([bias] * GROUP, axis=1)
        s = [_dot_nt(k_ref[0, g, pl.ds(ks, tk), :], qs[g]) for g in range(N_KV)]
        for g in range(N_KV):
            _online_t(s[g], None, vt_ref[0, c, HD * g:HD * (g + 1), :], m_ref, l_ref, acc_ref, g, bias=bias)
        return ties_seen + jnp.sum(tied, axis=0, keepdims=True)

    lax.fori_loop(0, n_proc, attend, jnp.zeros((1, tq), F32))
    for g in range(N_KV):
        o_ref[0, :, KV_W * g:KV_W * (g + 1)] = _group_out_t(acc_ref[g], l_ref[g], tq)


def _dsa_prompt(qhm, qi3, wi, ki3, khm, vt):
    B, _, T, _ = qhm.shape
    tq = 128
    m = GROUP * tq
    return pl.pallas_call(
        functools.partial(_dsa_kernel, tq=tq, n_keep=min(IDX_TOPK, T // 4)), grid=(B, T // tq),
        in_specs=[pl.BlockSpec((1, N_HEADS, tq, HD), lambda b, i: (b, 0, i, 0)),
                  pl.BlockSpec((1, IDX_HEADS, tq, 4 * IDX_DIM), lambda b, i: (b, 0, i, 0)),
                  pl.BlockSpec((1, IDX_HEADS, tq), lambda b, i: (b, 0, i)),
                  pl.BlockSpec((1, T, 4 * IDX_DIM), lambda b, i: (b, 0, 0)),
                  pl.BlockSpec((1, N_KV, T, HD), lambda b, i: (b, 0, 0, 0)),
                  pl.BlockSpec((1, T // KV_TILE, KV_W, KV_TILE), lambda b, i: (b, 0, 0, 0))],
        out_specs=pl.BlockSpec((1, tq, D_MODEL), lambda b, i: (b, i, 0)),
        out_shape=jax.ShapeDtypeStruct((B, T, D_MODEL), BF16),
        scratch_shapes=[pltpu.VMEM((T // KV_TILE, KV_TILE, tq), I32)] + [pltpu.VMEM((T // KV_TILE, KV_TILE, tq), I16)] * 3
        + _softmax_scratch_t(N_KV, m),
        compiler_params=_cp("parallel", "arbitrary"), name="dsa_p")(qhm, qi3, wi.transpose(0, 2, 1), ki3, khm, vt)


def _page_specs(layer, n_pages, nps, reverse):
    def spec(i):
        def idx(b, s, pt):
            p = s * nps + i
            return (layer, pt[b, n_pages - 1 - p if reverse else p], 0, 0)

        return pl.BlockSpec((None, None, KV_W, PAGE), idx)

    return [spec(i) for i in range(nps)]


def _pool_specs(rows, n_pages, nps, reverse):
    def spec(i):
        def idx(b, s, pt):
            p = s * nps + i
            return (pt[b, n_pages - 1 - p if reverse else p], 0, 0)

        return pl.BlockSpec((None, rows, PAGE), idx)

    return [spec(i) for i in range(nps)]


def _row_spec(width):
    return pl.BlockSpec((1, 1, width), lambda b, s, pt: (b, 0, 0))


def _head_spec(width):
    return pl.BlockSpec((1, N_HEADS, width), lambda b, s, pt: (b, 0, 0))


def _softmax_scratch():
    return [pltpu.VMEM((N_HEADS, 1), F32), pltpu.VMEM((N_HEADS, 1), F32), pltpu.VMEM((N_HEADS, KV_W), F32)]


def _new_key_start(q, knew, vnew, m_ref, l_ref, acc_ref):
    z = jnp.sum(q.astype(F32) * knew.astype(BF16).astype(F32), axis=1, keepdims=True)
    m_ref[...] = z
    l_ref[...] = jnp.ones_like(z)
    acc_ref[...] = jnp.broadcast_to(vnew.astype(BF16).astype(F32), acc_ref.shape)


def _cat_pages(refs):
    return jnp.concatenate([r[...].astype(BF16) for r in refs], axis=1)


def _later_sums(x, parts):
    n = x.shape[1] // PAGE
    stack = jnp.concatenate([p[:, PAGE * i:PAGE * (i + 1)] for p in parts(x) for i in range(n)], axis=0)
    out = _dot(stack, _tri(PAGE, "gt"))
    rows = x.shape[0]
    tot = None
    for k in range(out.shape[0] // (n * rows)):
        part = jnp.concatenate([out[(k * n + i) * rows:(k * n + i + 1) * rows] for i in range(n)], axis=1)
        tot = part if tot is None else tot + part
    return tot


def _page_totals(x):
    return [jnp.sum(x[:, PAGE * i:PAGE * (i + 1)], axis=1, keepdims=True) for i in range(x.shape[1] // PAGE)]


def _per_page(cols):
    return jnp.concatenate([jnp.broadcast_to(c, (c.shape[0], PAGE)) for c in cols], axis=1)


def _online_pages(s, valid, vcat, m_ref, l_ref, acc_ref):
    if valid is not None:
        s = jnp.where(valid, s, NEG)
    m_old = m_ref[...]
    m_new = jnp.maximum(m_old, jnp.max(s, axis=1, keepdims=True))
    alpha = jnp.exp(m_old - m_new)
    p = jnp.exp(s - m_new)
    if valid is not None:
        p = jnp.where(valid, p, 0.0)
    l_ref[...] = alpha * l_ref[...] + jnp.sum(p, axis=1, keepdims=True)
    acc_ref[...] = alpha * acc_ref[...] + _dot_nt(p.astype(BF16), vcat)
    m_ref[...] = m_new


def _dec_stick_kernel(pt_ref, q_ref, *refs, nps):
    k_refs, v_refs = refs[:nps], refs[nps:2 * nps]
    o_ref, r_ref, acc_ref = refs[2 * nps:]
    step = pl.program_id(1)

    @pl.when(step == 0)
    def _():
        r_ref[...] = jnp.zeros_like(r_ref)
        acc_ref[...] = jnp.zeros_like(acc_ref)

    ls, lk = _log_sigmoid_pair(_dot(q_ref[0], _cat_pages(k_refs)))
    after, run = [], r_ref[...]
    for tot in _page_totals(lk):
        after.append(run)
        run = run + tot
    w = jnp.exp(ls + _later_sums(lk, _split2) + _per_page(after))
    acc = acc_ref[...] + _dot_nt(w.astype(BF16), _cat_pages(v_refs))
    r_ref[...], acc_ref[...] = run, acc

    @pl.when(step == pl.num_programs(1) - 1)
    def _():
        o_ref[0] = acc


def _dec_call(kernel, name, grid_spec, DB):
    return pl.pallas_call(kernel, grid_spec=grid_spec, out_shape=jax.ShapeDtypeStruct((DB, N_HEADS, KV_W), F32),
                          compiler_params=_cp("parallel", "arbitrary"), name=name)


def _dec_stick(qx, kt, vt, page_table, layer):
    DB, n_pages = page_table.shape
    nps = math.gcd(n_pages, DEC_PAGES)
    pages = _page_specs(layer, n_pages, nps, True)
    grid_spec = pltpu.PrefetchScalarGridSpec(
        num_scalar_prefetch=1, grid=(DB, n_pages // nps), in_specs=[_head_spec(KV_W)] + pages + pages,
        out_specs=_head_spec(KV_W), scratch_shapes=[pltpu.VMEM((N_HEADS, 1), F32), pltpu.VMEM((N_HEADS, KV_W), F32)])
    return _dec_call(functools.partial(_dec_stick_kernel, nps=nps), "stick_s", grid_spec, DB)(
        page_table, qx, *([kt] * nps), *([vt] * nps))


def _dec_fox_kernel(pt_ref, q_ref, *refs, nps):
    k_refs, v_refs, lf_refs = refs[:nps], refs[nps:2 * nps], refs[2 * nps:3 * nps]
    kn_ref, vn_ref, lfn_ref, o_ref, s_ref, m_ref, l_ref, acc_ref = refs[3 * nps:]
    step = pl.program_id(1)

    @pl.when(step == 0)
    def _():
        s_ref[...] = lfn_ref[0]
        _new_key_start(q_ref[0], kn_ref[0], vn_ref[0], m_ref, l_ref, acc_ref)

    lf = jnp.concatenate([r[...] for r in lf_refs], axis=1)
    after, run = [], s_ref[...]
    for tot in _page_totals(lf):
        after.append(run)
        run = run + tot
    s_ref[...] = run
    s = _dot(q_ref[0], _cat_pages(k_refs)) + _later_sums(lf, _split3) + _per_page(after)
    _online_pages(s, None, _cat_pages(v_refs), m_ref, l_ref, acc_ref)

    @pl.when(step == pl.num_programs(1) - 1)
    def _():
        o_ref[0] = acc_ref[...] / l_ref[...]


def _dec_fox(qx, kt, vt, logf_t, knew, vnew, lfnew, page_table, layer):
    DB, n_pages = page_table.shape
    nps = math.gcd(n_pages, DEC_PAGES)
    pages = _page_specs(layer, n_pages, nps, True)
    grid_spec = pltpu.PrefetchScalarGridSpec(
        num_scalar_prefetch=1, grid=(DB, n_pages // nps),
        in_specs=[_head_spec(KV_W)] + pages + pages + _pool_specs(N_HEADS, n_pages, nps, True)
        + [_row_spec(KV_W), _row_spec(KV_W), _head_spec(1)],
        out_specs=_head_spec(KV_W), scratch_shapes=[pltpu.VMEM((N_HEADS, 1), F32)] + _softmax_scratch())
    return _dec_call(functools.partial(_dec_fox_kernel, nps=nps), "fox_s", grid_spec, DB)(
        page_table, qx, *([kt] * nps), *([vt] * nps), *([logf_t] * nps), knew, vnew, lfnew)


def _dec_kmean_kernel(pt_ref, *refs, nps):
    k_refs, o_ref = refs[:nps], refs[nps]
    step = pl.program_id(1)

    @pl.when(step == 0)
    def _():
        o_ref[...] = jnp.zeros_like(o_ref)

    ones = jnp.ones((8, PAGE), BF16)
    for i in range(nps):
        hi, mid, lo = _split3(k_refs[i][...])
        tot = _dot_nt(ones, hi) + _dot_nt(ones, mid) + _dot_nt(ones, lo)
        blk = (step * nps + i) // (MOBA_BLOCK // PAGE)
        o_ref[0, pl.ds(blk, 1), :] = o_ref[0, pl.ds(blk, 1), :] + tot[0:1, :] * (1.0 / MOBA_BLOCK)


def _dec_kmean(kt, page_table, layer):
    DB, n_pages = page_table.shape
    nps = math.gcd(n_pages, DEC_PAGES)
    nb = n_pages * PAGE // MOBA_BLOCK
    grid_spec = pltpu.PrefetchScalarGridSpec(
        num_scalar_prefetch=1, grid=(DB, n_pages // nps), in_specs=_page_specs(layer, n_pages, nps, False),
        out_specs=pl.BlockSpec((1, nb, KV_W), lambda b, s, pt: (b, 0, 0)))
    return pl.pallas_call(
        functools.partial(_dec_kmean_kernel, nps=nps), grid_spec=grid_spec,
        out_shape=jax.ShapeDtypeStruct((DB, nb, KV_W), F32),
        compiler_params=_cp("parallel", "arbitrary"), name="kmean_s")(page_table, *([kt] * nps))


def _dec_moba_kernel(pt_ref, q_ref, q32_ref, km_ref, *refs, nps, nb):
    k_refs, v_refs = refs[:nps], refs[nps:2 * nps]
    kn_ref, vn_ref, o_ref, sel_ref, m_ref, l_ref, acc_ref = refs[2 * nps:]
    step = pl.program_id(1)

    @pl.when(step == 0)
    def _():
        sel_ref[...] = _topk_select(_dot_nt_f32(q32_ref[0], km_ref[0]), nb, MOBA_TOPK, 1)
        _new_key_start(q_ref[0], kn_ref[0], vn_ref[0], m_ref, l_ref, acc_ref)

    lane = _iota((N_HEADS, nb), 1)
    sel = sel_ref[...]
    bias = []
    for i in range(nps):
        blk = (step * nps + i) // (MOBA_BLOCK // PAGE)
        picked = jnp.sum(jnp.where(lane == blk, sel, 0.0), axis=1, keepdims=True) > 0.5
        bias.append(jnp.where(picked, 0.0, NEG))
    s = _dot(q_ref[0], _cat_pages(k_refs)) + _per_page(bias)
    _online_pages(s, None, _cat_pages(v_refs), m_ref, l_ref, acc_ref)

    @pl.when(step == pl.num_programs(1) - 1)
    def _():
        o_ref[0] = acc_ref[...] / l_ref[...]


def _dec_moba(qx, q32x, kmean, kt, vt, knew, vnew, page_table, layer):
    DB, n_pages = page_table.shape
    nps = math.gcd(n_pages, DEC_PAGES)
    nb = kmean.shape[1]
    pages = _page_specs(layer, n_pages, nps, False)
    grid_spec = pltpu.PrefetchScalarGridSpec(
        num_scalar_prefetch=1, grid=(DB, n_pages // nps),
        in_specs=[_head_spec(KV_W), _head_spec(KV_W), pl.BlockSpec((1, nb, KV_W), lambda b, s, pt: (b, 0, 0))]
        + pages + pages + [_row_spec(KV_W), _row_spec(KV_W)],
        out_specs=_head_spec(KV_W), scratch_shapes=[pltpu.VMEM((N_HEADS, nb), F32)] + _softmax_scratch())
    return _dec_call(functools.partial(_dec_moba_kernel, nps=nps, nb=nb), "moba_s", grid_spec, DB)(
        page_table, qx, q32x, kmean, *([kt] * nps), *([vt] * nps), knew, vnew)


def _dec_score_kernel(pt_ref, qi_ref, wi_ref, *refs, nps, n_pages):
    kidx_refs = refs[:nps]
    kin_ref, o_ref = refs[nps:]
    step = pl.program_id(1)
    qi, wi = qi_ref[0], wi_ref[0]

    @pl.when(step == 0)
    def _():
        o_ref[...] = jnp.full(o_ref.shape, -jnp.inf, F32)
        dn = jnp.sum(qi * kin_ref[0], axis=1, keepdims=True)
        sn = jnp.sum(wi * jnp.maximum(dn, 0.0), axis=0, keepdims=True) + 0.0
        o_ref[0, n_pages:n_pages + 1, :] = jnp.where(_iota((1, PAGE), 1) == 0, sn, -jnp.inf)

    for i in range(nps):
        dots = _dot_f32(qi, kidx_refs[i][...])
        o_ref[0, pl.ds(step * nps + i, 1), :] = jnp.sum(wi * jnp.maximum(dots, 0.0), axis=0, keepdims=True) + 0.0


def _dec_scores(qi3, wi3, kidx_t, kinew, page_table):
    DB, n_pages = page_table.shape
    assert n_pages < PAGE
    nps = math.gcd(n_pages, DEC_PAGES)
    grid_spec = pltpu.PrefetchScalarGridSpec(
        num_scalar_prefetch=1, grid=(DB, n_pages // nps),
        in_specs=[pl.BlockSpec((1, IDX_HEADS, IDX_DIM), lambda b, s, pt: (b, 0, 0)),
                  pl.BlockSpec((1, IDX_HEADS, 1), lambda b, s, pt: (b, 0, 0))]
        + _pool_specs(IDX_DIM, n_pages, nps, False) + [_row_spec(IDX_DIM)],
        out_specs=pl.BlockSpec((1, PAGE, PAGE), lambda b, s, pt: (b, 0, 0)))
    return pl.pallas_call(
        functools.partial(_dec_score_kernel, nps=nps, n_pages=n_pages), grid_spec=grid_spec,
        out_shape=jax.ShapeDtypeStruct((DB, PAGE, PAGE), F32),
        compiler_params=_cp("parallel", "arbitrary"), name="dsa_score_s")(page_table, qi3, wi3, *([kidx_t] * nps), kinew)


def _dec_select_kernel(sc_ref, o_ref, *, n_keys, n_keep):
    key = _order_key(sc_ref[0])

    def total(x):
        return jnp.sum(jnp.sum(x, axis=1, keepdims=True), axis=0, keepdims=True)

    thr = _kth_largest_key(lambda cand: total(jnp.where(key >= cand, 1.0, 0.0)), (1, 1), n_keep)
    need = n_keep - total(jnp.where(key > thr, 1.0, 0.0))
    tied = jnp.where(key == thr, 1.0, 0.0)
    in_row = _dot(tied.astype(BF16), _tri(PAGE, "lt"))
    row_tot = jnp.broadcast_to(jnp.sum(tied, axis=1, keepdims=True), tied.shape).astype(BF16)
    rank = in_row + _dot(_tri(PAGE, "gt"), row_tot)
    pos = _iota(key.shape, 0) * PAGE + _iota(key.shape, 1)
    keep = ((key > thr) | ((key == thr) & (rank < need))) & (pos < n_keys)
    o_ref[0] = jnp.where(keep, 1.0, 0.0)


def _dec_select(scores, n_keys):
    DB = scores.shape[0]
    spec = pl.BlockSpec((1, PAGE, PAGE), lambda b: (b, 0, 0))
    return pl.pallas_call(
        functools.partial(_dec_select_kernel, n_keys=n_keys, n_keep=min(IDX_TOPK, n_keys // 4)), grid=(DB,),
        in_specs=[spec], out_specs=spec, out_shape=jax.ShapeDtypeStruct(scores.shape, F32),
        compiler_params=_cp("parallel"), name="dsa_select_s")(scores)


def _dec_dsa_kernel(pt_ref, q_ref, *refs, nps, n_pages):
    k_refs, v_refs = refs[:nps], refs[nps:2 * nps]
    sel_ref, kn_ref, vn_ref, o_ref, m_ref, l_ref, acc_ref = refs[2 * nps:]
    step = pl.program_id(1)
    q = q_ref[0]

    @pl.when(step == 0)
    def _():
        zn = jnp.sum(q.astype(F32) * kn_ref[0].astype(BF16).astype(F32), axis=1, keepdims=True)
        keep = jnp.broadcast_to(sel_ref[0, n_pages:n_pages + 1, 0:1] > 0.5, zn.shape)
        m_ref[...] = jnp.where(keep, zn, NEG)
        l_ref[...] = jnp.where(keep, 1.0, 0.0)
        acc_ref[...] = jnp.where(keep, jnp.broadcast_to(vn_ref[0].astype(BF16).astype(F32), acc_ref.shape), 0.0)

    keep = jnp.concatenate([sel_ref[0, pl.ds(step * nps + i, 1), :] for i in range(nps)], axis=1)
    s = _dot(q, _cat_pages(k_refs))
    _online_pages(s, jnp.broadcast_to(keep > 0.5, s.shape), _cat_pages(v_refs), m_ref, l_ref, acc_ref)

    @pl.when(step == pl.num_programs(1) - 1)
    def _():
        o_ref[0] = acc_ref[...] / l_ref[...]


def _dec_dsa(qx, kt, vt, sel, knew, vnew, page_table, layer):
    DB, n_pages = page_table.shape
    nps = math.gcd(n_pages, DEC_PAGES)
    pages = _page_specs(layer, n_pages, nps, False)
    grid_spec = pltpu.PrefetchScalarGridSpec(
        num_scalar_prefetch=1, grid=(DB, n_pages // nps),
        in_specs=[_head_spec(KV_W)] + pages + pages
        + [pl.BlockSpec((1, PAGE, PAGE), lambda b, s, pt: (b, 0, 0)), _row_spec(KV_W), _row_spec(KV_W)],
        out_specs=_head_spec(KV_W), scratch_shapes=_softmax_scratch())
    return _dec_call(functools.partial(_dec_dsa_kernel, nps=nps, n_pages=n_pages), "dsa_s", grid_spec, DB)(
        page_table, qx, *([kt] * nps), *([vt] * nps), sel, knew, vnew)


def _rope_tables(pos):
    half = HD // 2
    inv_freq = jnp.power(ROPE_THETA, -jnp.arange(half, dtype=F32) * (2.0 / HD))
    ang = pos.astype(F32)[:, None] * inv_freq[None, :]
    cos, sin = jnp.cos(ang), jnp.sin(ang)
    return jnp.concatenate([cos] * 4, axis=1), jnp.concatenate([-sin, sin] * 2, axis=1)


def _pad_cols(w, mult=128):
    pad = (-w.shape[1]) % mult
    return jnp.pad(w, ((0, 0), (0, pad))) if pad else w


def _layer_weights(kind, w_qkv, fox_w_f, idx_w_q, idx_w_k, idx_w_w):
    cols = [w_qkv]
    if kind == FOX:
        cols.append(_pad_cols(fox_w_f))
    if kind == DSA:
        cols += [idx_w_q, _pad_cols(idx_w_k), _pad_cols(idx_w_w)]
    return jnp.concatenate(cols, axis=1).astype(BF16)


_HEAD_ONEHOT = np.arange(N_HEADS)[:, None] // GROUP == np.arange(N_KV)[None, :]


def _expand_heads(q, dtype):
    m = jnp.asarray(_HEAD_ONEHOT, q.dtype)
    return (q[:, :, None, :] * m[None, :, :, None]).reshape(q.shape[0], N_HEADS, KV_W).astype(dtype)


def _collapse_heads(ox):
    m = jnp.asarray(_HEAD_ONEHOT, ox.dtype)
    o = jnp.sum(ox.reshape(ox.shape[0], N_HEADS, N_KV, HD) * m[None, :, :, None], axis=2)
    return o.reshape(ox.shape[0], D_MODEL).astype(BF16)


def _pages_t(cache):
    depth, n_pool = cache.shape[:2]
    return cache.transpose(0, 1, 3, 4, 2).reshape(depth, n_pool, KV_W, PAGE)


def kernel(x_prompt, x_sample, cache_k, cache_v, cache_logf, cache_kidx, state_conv, page_table, norm_mix, norm_ffn, norm_final, w_qkv, w_o, fox_w_f, fox_b_f, idx_w_q, idx_w_k, idx_w_w, ffn_w_a, ffn_w_b, ffn_conv_w, ffn_conv_b, ffn_w_down):
    B, T, _ = x_prompt.shape
    DB = x_sample.shape[0]
    depth = cache_k.shape[0]
    n_pages = page_table.shape[1]
    past = n_pages * PAGE
    assert T % MOBA_BLOCK == 0 and past % MOBA_BLOCK == 0 and x_sample.shape[1] == 1
    cache_kt, cache_vt = _pages_t(cache_k), _pages_t(cache_v)
    rope_p = _rope_tables(jnp.arange(T, dtype=I32))
    rope_s = _rope_tables(jnp.full((DB,), past, I32))

    xp = x_prompt
    xs = x_sample.reshape(1, DB, D_MODEL)
    new_k_p, new_v_p, new_k_s, new_v_s, conv_p, conv_s = [], [], [], [], [], []
    logf_p = logf_s = kidx_p = kidx_s = None
    for i in range(depth):
        kind = i % 4
        rope = kind in (MOBA, DSA)
        w_all = _layer_weights(kind, w_qkv[i], fox_w_f, idx_w_q, idx_w_k, idx_w_w)
        wo_b = w_o[i].astype(BF16)
        wa_b, wb_b, wd_b = ffn_w_a[i].astype(BF16), ffn_w_b[i].astype(BF16), ffn_w_down[i].astype(BF16)

        outs = _proj(xp, norm_mix[i], w_all, kind, rope_p if rope else None, fox_b_f, True)
        qhm, k32, v32, khm, vhm = outs[:5]
        vt = outs[-1]
        new_k_p.append(k32.reshape(B, T, N_KV, HD))
        new_v_p.append(v32.reshape(B, T, N_KV, HD))
        if kind == STICK:
            o = _stick_prompt(qhm, khm, vt)
        elif kind == FOX:
            logf_p = outs[5]
            o = _fox_prompt(qhm, khm, vt, _cumsum_time_parts(logf_p))
        elif kind == MOBA:
            o = _moba_prompt(qhm, outs[5], _block_mean(k32), khm, vt)
        else:
            kidx_p, wi, qi3, ki3 = outs[6:10]
            o = _dsa_prompt(qhm, qi3, wi, ki3, khm, vt)
        x2, rows = _mix_ffn(xp.reshape(B * T, D_MODEL), o.reshape(B * T, D_MODEL), wo_b, norm_ffn[i], wa_b, wb_b,
                            ffn_conv_w[i], ffn_conv_b[i], wd_b, T)
        conv_p.append(rows)
        xp = x2.reshape(B, T, D_MODEL)

        outs = _proj(xs, norm_mix[i], w_all, kind, rope_s if rope else None, fox_b_f, False)
        qhm, k32, v32 = outs[:3]
        new_k_s.append(k32.reshape(DB, 1, N_KV, HD))
        new_v_s.append(v32.reshape(DB, 1, N_KV, HD))
        qx = _expand_heads(qhm[0].transpose(1, 0, 2), BF16)
        knew, vnew = k32.reshape(DB, 1, KV_W), v32.reshape(DB, 1, KV_W)
        if kind == STICK:
            ox = _dec_stick(qx, cache_kt, cache_vt, page_table, i)
        elif kind == FOX:
            lf = outs[5][0]
            logf_s = lf.reshape(DB, 1, N_HEADS)
            ox = _dec_fox(qx, cache_kt, cache_vt, cache_logf.transpose(0, 2, 1), knew, vnew,
                          lf.reshape(DB, N_HEADS, 1), page_table, i)
        elif kind == MOBA:
            q32x = _expand_heads(outs[5][0].transpose(1, 0, 2), F32)
            ox = _dec_moba(qx, q32x, _dec_kmean(cache_kt, page_table, i), cache_kt, cache_vt, knew, vnew, page_table, i)
        else:
            qihm, ki, wi = outs[5:8]
            kidx_s = ki.reshape(DB, 1, IDX_DIM)
            scores = _dec_scores(qihm[0].transpose(1, 0, 2), wi[0].reshape(DB, IDX_HEADS, 1),
                                 cache_kidx.transpose(0, 2, 1), kidx_s, page_table)
            ox = _dec_dsa(qx, cache_kt, cache_vt, _dec_select(scores, past + 1), knew, vnew, page_table, i)
        st = state_conv[i]
        s2, a_new = _mix_ffn(xs.reshape(DB, D_MODEL), _collapse_heads(ox), wo_b, norm_ffn[i], wa_b, wb_b,
                             ffn_conv_w[i], ffn_conv_b[i], wd_b, 1, prev=(st[:, 0], st[:, 1]))
        conv_s.append(jnp.stack([st[:, 1], a_new], axis=1))
        xs = s2.reshape(1, DB, D_MODEL)

    y_p = _final_norm(xp.reshape(B * T, D_MODEL), norm_final).reshape(B, T, D_MODEL)
    y_s = _final_norm(xs.reshape(DB, D_MODEL), norm_final).reshape(DB, 1, D_MODEL)
    return (y_p, y_s, jnp.stack(new_k_p), jnp.stack(new_v_p), jnp.stack(new_k_s), jnp.stack(new_v_s),
            logf_p, logf_s, kidx_p, kidx_s, jnp.stack(conv_p), jnp.stack(conv_s))
```

```python
import functools
import math

import jax
import jax.numpy as jnp
import numpy as np
from jax import lax
from jax.experimental import pallas as pl
from jax.experimental.pallas import tpu as pltpu

F32, BF16, I32 = jnp.float32, jnp.bfloat16, jnp.int32

D_MODEL = 1024
N_HEADS = 16
N_KV = 4
GROUP = N_HEADS // N_KV
HD = 64
KV_W = N_KV * HD
D_FF = 2816
CONV_W = 3
PAGE = 128
MOBA_BLOCK = 256
MOBA_TOPK = 3
IDX_HEADS = 8
IDX_DIM = 64
IDX_TOPK = 256
ROPE_THETA = 10000.0
RMS_EPS = 1e-6
STICK, FOX, MOBA, DSA = 0, 1, 2, 3
NEG = -1e30
M_FLOOR = -1e29
LOG2E = 1.4426950408889634
DEAD_LOG = -104.0
INT_MIN = -2147483648
VMEM_LIMIT = 48 * 1024 * 1024
FF_CHUNK = 256
FFN_TM = 512
PROJ_TM = 256
KV_TILE = 256
DEC_PAGES = 32
KV_PER_STEP = 4
Q_CHUNK = 128
FOX_AUG = 3 * GROUP


def _cp(*sem):
    return pltpu.CompilerParams(dimension_semantics=sem, vmem_limit_bytes=VMEM_LIMIT)


def _dot(a, b):
    return jnp.dot(a, b, preferred_element_type=F32)


def _dot_nt(a, b):
    return lax.dot_general(a, b, (((1,), (1,)), ((), ())), preferred_element_type=F32)


def _split2(x):
    hi = x.astype(BF16)
    lo = (x - hi.astype(F32)).astype(BF16)
    return hi, lo


def _split3(x):
    hi = x.astype(BF16)
    r = x - hi.astype(F32)
    mid = r.astype(BF16)
    lo = (r - mid.astype(F32)).astype(BF16)
    return hi, mid, lo


def _dot_f32(a, b):
    ah, al = _split2(a)
    bh, bl = _split2(b)
    return _dot(ah, bh) + _dot(al, bh) + _dot(ah, bl)


def _dot_nt_f32(a, b):
    ah, al = _split2(a)
    bh, bl = _split2(b)
    return _dot_nt(ah, bh) + _dot_nt(al, bh) + _dot_nt(ah, bl)


def _log_sigmoid_pair(z):
    l1p = jnp.log(1.0 + jnp.exp(-jnp.abs(z)))
    return jnp.minimum(z, 0.0) - l1p, jnp.minimum(-z, 0.0) - l1p


def _iota(shape, dim):
    return lax.broadcasted_iota(I32, shape, dim)


def _tri(n, kind):
    r, c = _iota((n, n), 0), _iota((n, n), 1)
    m = {"gt": r > c, "lt": r < c, "ge": r >= c}[kind]
    return jnp.where(m, 1.0, 0.0).astype(BF16)


def _rms_bf16(x, g):
    y = x * lax.rsqrt(jnp.mean(x * x, axis=-1, keepdims=True) + RMS_EPS)
    return (y * g).astype(BF16)


def _rope128(xc, cos, sin):
    lane = _iota(xc.shape, 1)
    first = (lane % HD) < (HD // 2)
    rolled = jnp.where(first, pltpu.roll(xc, 128 - HD // 2, 1), pltpu.roll(xc, HD // 2, 1))
    return xc * cos + rolled * sin


def _halves(xc):
    return xc[:, :HD], pltpu.roll(xc, HD, 1)[:, :HD]


def _place(parts):
    r, c = _iota((HD, KV_W), 0), _iota((HD, KV_W), 1)
    out = None
    for j, pj in enumerate(parts):
        e = jnp.where(c == r + HD * j, 1.0, 0.0).astype(BF16)
        t = _dot(pj, e)
        out = t if out is None else out + t
    return out


def _stack_heads(q_ref):
    return jnp.concatenate([q_ref[0, j] for j in range(GROUP)], axis=0)


def _init_softmax(m_ref, l_ref, acc_ref):
    m_ref[...] = jnp.full(m_ref.shape, M_FLOOR, F32)
    l_ref[...] = jnp.zeros_like(l_ref)
    acc_ref[...] = jnp.zeros_like(acc_ref)


def _online_t(s, valid, vt, m_ref, l_ref, acc_ref, idx, bias=None):
    m_all, l_all, acc_all = m_ref[idx], l_ref[idx], acc_ref[idx]
    m_out, l_out, acc_out = [], [], []
    for c0 in range(0, s.shape[1], Q_CHUNK):
        cols = slice(c0, c0 + Q_CHUNK)
        sc = s[:, cols]
        if bias is not None:
            sc = sc + bias[:, cols]
        if valid is not None:
            vc = valid[:, cols]
            sc = jnp.where(vc, sc, NEG)
        m_old = m_all[:, cols]
        m_new = jnp.maximum(m_old, jnp.max(sc, axis=0, keepdims=True))
        alpha = jnp.exp2(m_old - m_new)
        p = jnp.exp2(sc - m_new)
        if valid is not None:
            p = jnp.where(vc, p, 0.0)
        l_out.append(alpha * l_all[:, cols] + jnp.sum(p, axis=0, keepdims=True))
        acc_out.append(alpha * acc_all[:, cols] + _dot(vt, p.astype(BF16)))
        m_out.append(m_new)
    m_ref[idx] = jnp.concatenate(m_out, axis=1)
    l_ref[idx] = jnp.concatenate(l_out, axis=1)
    acc_ref[idx] = jnp.concatenate(acc_out, axis=1)


def _group_out_t(acc_t, l, tq):
    r = acc_t if l is None else acc_t / l
    o_t = jnp.concatenate([r[:, j * tq:(j + 1) * tq] for j in range(GROUP)], axis=0)
    return o_t.T.astype(BF16)


def _topk_select(gate, n_valid, k, axis):
    pos = _iota(gate.shape, axis)
    posf = pos.astype(F32)
    g = jnp.where(pos < n_valid, gate, -jnp.inf)
    sel = jnp.zeros(gate.shape, F32)
    for _ in range(k):
        best = jnp.max(g, axis=axis, keepdims=True)
        first = jnp.min(jnp.where(g == best, posf, float(gate.shape[axis])), axis=axis, keepdims=True)
        hit = posf == first
        sel = jnp.where(hit, 1.0, sel)
        g = jnp.where(hit, -jnp.inf, g)
    return jnp.where(pos < n_valid, sel, 0.0)


def _order_key(score):
    bits = lax.bitcast_convert_type(score, I32)
    return jnp.where(bits < 0, bits ^ jnp.int32(0x7FFFFFFF), bits)


def _kth_largest_key(count_ge, shape, k):
    def body(it, v):
        cand = v + lax.shift_left(jnp.int32(1), 31 - it)
        return jnp.where(count_ge(cand) >= k, cand, v)

    return lax.fori_loop(0, 32, body, jnp.full(shape, INT_MIN, I32))


def _proj_kernel(*refs, kind, rope, prompt, aug):
    it = iter(refs)
    x_ref, g_ref, w_ref = next(it), next(it), next(it)
    bf_ref = next(it) if kind == FOX else None
    cos_ref, sin_ref = (next(it), next(it)) if rope else (None, None)
    qpat_ref = next(it) if aug else None
    kpat_ref = next(it) if aug == "qk" else None
    qhm_ref, k32_ref, v32_ref, khm_ref, vhm_ref = next(it), next(it), next(it), next(it), next(it)
    low = _iota((x_ref.shape[1], 128), 1) < HD
    h = _rms_bf16(x_ref[0], g_ref[...])
    y = _dot(h, w_ref[...])
    cos = cos_ref[...] if rope else None
    sin = sin_ref[...] if rope else None

    def chunk(c, rot):
        xc = y[:, 128 * c:128 * (c + 1)]
        return _rope128(xc, cos, sin) if rot else xc

    q32hm_ref = next(it) if kind == MOBA else None
    q_scale = HD ** -0.5 * (1.0 if kind == STICK else LOG2E)
    for c in range(N_HEADS // 2):
        qc = chunk(c, rope)
        lo, hi = _halves(qc)
        if aug:
            for j, z in enumerate((qc, pltpu.roll(qc, HD, 1))):
                qhm_ref[0, 2 * c + j] = jnp.where(low, z * q_scale, qpat_ref[2 * c + j:2 * c + j + 1, :]).astype(BF16)
        else:
            qhm_ref[0, 2 * c] = (lo * q_scale).astype(BF16)
            qhm_ref[0, 2 * c + 1] = (hi * q_scale).astype(BF16)
        if kind == MOBA:
            q32hm_ref[0, 2 * c] = lo
            q32hm_ref[0, 2 * c + 1] = hi
    base = D_MODEL // 128
    for c in range(N_KV // 2):
        kc = chunk(base + c, rope)
        k32_ref[0, :, 128 * c:128 * (c + 1)] = kc
        vc = chunk(base + N_KV // 2 + c, False)
        v32_ref[0, :, 128 * c:128 * (c + 1)] = vc
        if aug == "qk":
            for j, z in enumerate((kc, pltpu.roll(kc, HD, 1))):
                khm_ref[0, 2 * c + j] = jnp.where(low, z, kpat_ref[...]).astype(BF16)
        else:
            klo, khi = _halves(kc)
            khm_ref[0, 2 * c] = klo.astype(BF16)
            khm_ref[0, 2 * c + 1] = khi.astype(BF16)
        vlo, vhi = _halves(vc)
        vhm_ref[0, 2 * c] = vlo.astype(BF16)
        vhm_ref[0, 2 * c + 1] = vhi.astype(BF16)
    base = (D_MODEL + 2 * KV_W) // 128
    if kind == FOX:
        logf_ref = next(it)
        zf = chunk(base, False)[:, :N_HEADS] + bf_ref[...]
        logf_ref[0] = _log_sigmoid_pair(zf)[0]
    if kind == DSA:
        qihm_ref, ki_ref, wi_ref = next(it), next(it), next(it)
        for c in range(IDX_HEADS // 2):
            lo, hi = _halves(chunk(base + c, True))
            qihm_ref[0, 2 * c] = lo
            qihm_ref[0, 2 * c + 1] = hi
        kic = chunk(base + IDX_HEADS // 2, True)
        ki_ref[0] = kic[:, :IDX_DIM]
        wi_ref[0] = chunk(base + IDX_HEADS // 2 + 1, False)[:, :IDX_HEADS] * (IDX_HEADS ** -0.5 * IDX_DIM ** -0.5)
        if prompt:
            qi3_ref, ki3_ref = next(it), next(it)
            low = _iota(kic.shape, 1) < IDX_DIM
            for c in range(IDX_HEADS // 2):
                xc = chunk(base + c, True)
                for j, z in enumerate((jnp.where(low, xc, 0.0), jnp.where(low, pltpu.roll(xc, IDX_DIM, 1), 0.0))):
                    hi = z.astype(BF16).astype(F32)
                    qi3_ref[0, 2 * c + j] = jnp.concatenate([hi + pltpu.roll(z - hi, IDX_DIM, 1), hi], axis=1).astype(BF16)
            hi = kic.astype(BF16).astype(F32)
            ki3_ref[0] = jnp.concatenate([hi + pltpu.roll(hi, IDX_DIM, 1), kic - hi], axis=1).astype(BF16)
    if prompt:
        vt_ref = next(it)
        vt_ref[0, 0] = y[:, D_MODEL + KV_W:D_MODEL + 2 * KV_W].T.astype(BF16)


def _score_features(kind, T):
    qpat = np.zeros((N_HEADS, 128), np.float32)
    if kind == FOX:
        for h in range(N_HEADS):
            qpat[h, HD + 3 * (h % GROUP):HD + 3 * (h % GROUP) + 3] = -1.0
        return "q", [jnp.asarray(qpat)]
    kpat = np.zeros((T, 128), np.float32)
    kpat[np.arange(T), HD + np.arange(T) // MOBA_BLOCK] = 1.0
    return "qk", [jnp.asarray(qpat), jnp.asarray(kpat)]


def _proj(x, g, w_all, kind, rope_tabs, fox_b, prompt):
    B, T, _ = x.shape
    tm = min(PROJ_TM, T)
    n_cols = w_all.shape[1]
    rope = rope_tabs is not None
    aug, pats = _score_features(kind, T) if prompt and kind in (FOX, MOBA) else (None, [])
    q_w = 2 * HD if aug else HD
    k_w = 2 * HD if aug == "qk" else HD
    row = lambda b, i: (b, i, 0)
    hm = lambda b, i: (b, 0, i, 0)
    in_specs = [pl.BlockSpec((1, tm, D_MODEL), row), pl.BlockSpec((1, D_MODEL), lambda b, i: (0, 0)),
                pl.BlockSpec((D_MODEL, n_cols), lambda b, i: (0, 0))]
    args = [x, g.reshape(1, D_MODEL), w_all]
    if kind == FOX:
        in_specs.append(pl.BlockSpec((1, N_HEADS), lambda b, i: (0, 0)))
        args.append(fox_b.reshape(1, N_HEADS))
    if rope:
        in_specs += [pl.BlockSpec((tm, 128), lambda b, i: (i, 0))] * 2
        args += list(rope_tabs)
    if aug:
        in_specs.append(pl.BlockSpec((N_HEADS, 128), lambda b, i: (0, 0)))
    if aug == "qk":
        in_specs.append(pl.BlockSpec((tm, 128), lambda b, i: (i, 0)))
    args += pats
    out_shape = [jax.ShapeDtypeStruct((B, N_HEADS, T, q_w), BF16), jax.ShapeDtypeStruct((B, T, KV_W), F32),
                 jax.ShapeDtypeStruct((B, T, KV_W), F32), jax.ShapeDtypeStruct((B, N_KV, T, k_w), BF16),
                 jax.ShapeDtypeStruct((B, N_KV, T, HD), BF16)]
    out_specs = [pl.BlockSpec((1, N_HEADS, tm, q_w), hm), pl.BlockSpec((1, tm, KV_W), row),
                 pl.BlockSpec((1, tm, KV_W), row), pl.BlockSpec((1, N_KV, tm, k_w), hm),
                 pl.BlockSpec((1, N_KV, tm, HD), hm)]
    if kind == MOBA:
        out_shape.append(jax.ShapeDtypeStruct((B, N_HEADS, T, HD), F32))
        out_specs.append(pl.BlockSpec((1, N_HEADS, tm, HD), hm))
    if kind == FOX:
        out_shape.append(jax.ShapeDtypeStruct((B, T, N_HEADS), F32))
        out_specs.append(pl.BlockSpec((1, tm, N_HEADS), row))
    if kind == DSA:
        out_shape += [jax.ShapeDtypeStruct((B, IDX_HEADS, T, IDX_DIM), F32),
                      jax.ShapeDtypeStruct((B, T, IDX_DIM), F32), jax.ShapeDtypeStruct((B, T, IDX_HEADS), F32)]
        out_specs += [pl.BlockSpec((1, IDX_HEADS, tm, IDX_DIM), hm), pl.BlockSpec((1, tm, IDX_DIM), row),
                      pl.BlockSpec((1, tm, IDX_HEADS), row)]
        if prompt:
            out_shape += [jax.ShapeDtypeStruct((B, IDX_HEADS, T, 4 * IDX_DIM), BF16),
                          jax.ShapeDtypeStruct((B, T, 4 * IDX_DIM), BF16)]
            out_specs += [pl.BlockSpec((1, IDX_HEADS, tm, 4 * IDX_DIM), hm), pl.BlockSpec((1, tm, 4 * IDX_DIM), row)]
    if prompt:
        assert tm == KV_TILE
        out_shape.append(jax.ShapeDtypeStruct((B, T // tm, KV_W, tm), BF16))
        out_specs.append(pl.BlockSpec((1, 1, KV_W, tm), lambda b, i: (b, i, 0, 0)))
    return pl.pallas_call(
        functools.partial(_proj_kernel, kind=kind, rope=rope, prompt=prompt, aug=aug),
        grid=(B, T // tm), in_specs=in_specs, out_specs=out_specs, out_shape=out_shape,
        compiler_params=_cp("parallel", "arbitrary"), name=f"proj_{kind}_{'p' if prompt else 's'}")(*args)


def _gelu_tanh(x):
    return 0.5 * x * (1.0 + jnp.tanh(0.7978845608028654 * (x + 0.044715 * x * x * x)))


def _ffn_kernel(*refs, prompt, tiles_per_seq):
    if prompt:
        x_ref, o_ref, wo_ref, g_ref, wa_ref, wb_ref, cw_ref, cb_ref, wd_ref, y_ref, conv_ref, u_ref, carry_ref = refs
    else:
        x_ref, o_ref, wo_ref, g_ref, wa_ref, wb_ref, cw_ref, cb_ref, wd_ref, p2_ref, p1_ref, y_ref, a_ref, u_ref = refs
    x = x_ref[...] + _dot(o_ref[...], wo_ref[...])
    h = _rms_bf16(x, g_ref[...])
    if prompt:
        @pl.when(pl.program_id(0) % tiles_per_seq == 0)
        def _():
            carry_ref[...] = jnp.zeros_like(carry_ref)

        row = _iota((x.shape[0], FF_CHUNK), 0)
    for c in range(D_FF // FF_CHUNK):
        cols = slice(c * FF_CHUNK, (c + 1) * FF_CHUNK)
        a = _dot(h, wa_ref[:, cols])
        gate = _dot(h, wb_ref[:, cols])
        if prompt:
            prev2, prev1 = carry_ref[0:1, cols], carry_ref[1:2, cols]
            back2 = pltpu.roll(a, 2, 0)
            a_m1 = jnp.where(row == 0, prev1, pltpu.roll(a, 1, 0))
            a_m2 = jnp.where(row == 0, prev2, jnp.where(row == 1, prev1, back2))
            carry_ref[0:2, cols] = back2[0:2, :]
            conv_ref[0, :, cols] = back2[0:2, :]
        else:
            a_m2, a_m1 = p2_ref[:, cols], p1_ref[:, cols]
            a_ref[:, cols] = a
        conv = cb_ref[:, cols] + cw_ref[0:1, cols] * a_m2 + cw_ref[1:2, cols] * a_m1 + cw_ref[2:3, cols] * a
        u_ref[:, cols] = (_gelu_tanh(conv) * gate).astype(BF16)
    y_ref[...] = x + _dot(u_ref[...], wd_ref[...])


def _mix_ffn(x2, o2, w_o, g, w_a, w_b, conv_w, conv_b, w_down, seq_len, prev=None):
    M = x2.shape[0]
    prompt = prev is None
    tm = min(FFN_TM, seq_len) if prompt else M
    tps = seq_len // tm if prompt else 1
    xs = pl.BlockSpec((tm, D_MODEL), lambda i: (i, 0))
    whole = lambda shape: pl.BlockSpec(shape, lambda i: (0,) * len(shape), pipeline_mode=pl.Buffered(1))
    in_specs = [xs, xs, whole((D_MODEL, D_MODEL)), whole((1, D_MODEL)), whole((D_MODEL, D_FF)), whole((D_MODEL, D_FF)),
                whole((CONV_W, D_FF)), whole((1, D_FF)), whole((D_FF, D_MODEL))]
    args = [x2, o2, w_o, g.reshape(1, D_MODEL), w_a, w_b, conv_w, conv_b.reshape(1, D_FF), w_down]
    scratch = [pltpu.VMEM((tm, D_FF), BF16)]
    if prompt:
        out_shape = [jax.ShapeDtypeStruct((M, D_MODEL), F32), jax.ShapeDtypeStruct((M // tm, CONV_W - 1, D_FF), F32)]
        out_specs = [xs, pl.BlockSpec((1, CONV_W - 1, D_FF), lambda i: (i, 0, 0))]
        scratch.append(pltpu.VMEM((8, D_FF), F32))
    else:
        ps = pl.BlockSpec((tm, D_FF), lambda i: (i, 0))
        in_specs += [ps, ps]
        args += list(prev)
        out_shape = [jax.ShapeDtypeStruct((M, D_MODEL), F32), jax.ShapeDtypeStruct((M, D_FF), F32)]
        out_specs = [xs, ps]
    y, extra = pl.pallas_call(
        functools.partial(_ffn_kernel, prompt=prompt, tiles_per_seq=tps), grid=(M // tm,),
        in_specs=in_specs, out_specs=out_specs, out_shape=out_shape, scratch_shapes=scratch,
        compiler_params=_cp("arbitrary"), name="ffn_p" if prompt else "ffn_s")(*args)
    return y, (extra[tps - 1::tps] if prompt else extra)


def _final_norm_kernel(x_ref, g_ref, y_ref):
    x = x_ref[...]
    y_ref[...] = x * lax.rsqrt(jnp.mean(x * x, axis=-1, keepdims=True) + RMS_EPS) * g_ref[...]


def _final_norm(x2, g):
    M = x2.shape[0]
    tm = min(1024, M)
    return pl.pallas_call(
        _final_norm_kernel, grid=(M // tm,),
        in_specs=[pl.BlockSpec((tm, D_MODEL), lambda i: (i, 0)), pl.BlockSpec((1, D_MODEL), lambda i: (0, 0))],
        out_specs=pl.BlockSpec((tm, D_MODEL), lambda i: (i, 0)),
        out_shape=jax.ShapeDtypeStruct((M, D_MODEL), F32), compiler_params=_cp("parallel"), name="final_norm")(x2, g.reshape(1, D_MODEL))


def _stick_kernel(q_ref, k_ref, vt_ref, o_ref, r_ref, acc_ref, *, tq, tk, nh):
    q0 = pl.program_id(2) * tq
    qs = [jnp.concatenate([q_ref[0, GROUP * h + j] for j in range(GROUP)], axis=0) for h in range(nh)]
    m = GROUP * tq
    later = _tri(tk, "lt")
    qpos = q0 + _iota((1, m), 1) % tq
    r_ref[...] = jnp.zeros_like(r_ref)
    acc_ref[...] = jnp.zeros_like(acc_ref)

    def tile(j, masked):
        ks = pl.multiple_of(j * tk, tk)
        zs = [_dot_nt(k_ref[0, h, pl.ds(ks, tk), :], qs[h]) for h in range(nh)]
        for h in range(nh):
            head_tile(j, h, zs[h], masked)

    def head_tile(j, h, z, masked):
        vt = vt_ref[0, j, HD * h:HD * (h + 1), :]
        kpos = j * tk + _iota((tk, 1), 0)
        r_all, acc_all = r_ref[h], acc_ref[h]
        chunks = [slice(c0, c0 + Q_CHUNK) for c0 in range(0, m, Q_CHUNK)]
        staged, r_out, acc_out = [], [], []
        for cols in chunks:
            zc = z[:, cols]
            ls = jnp.minimum(zc, 0.0) - jnp.log(1.0 + jnp.exp(-jnp.abs(zc)))
            lk = ls - zc
            past = kpos < qpos[:, cols] if masked else None
            if masked:
                lk = jnp.where(past, lk, 0.0)
            parts = _dot(later, jnp.concatenate(_split2(lk), axis=1))
            staged.append((ls, parts, past))
            r_out.append(r_all[:, cols] + jnp.sum(lk, axis=0, keepdims=True))
        for cols, (ls, parts, past) in zip(chunks, staged):
            w = jnp.exp(ls + parts[:, :Q_CHUNK] + parts[:, Q_CHUNK:] + r_all[:, cols])
            if masked:
                w = jnp.where(past, w, 0.0)
            acc_out.append(acc_all[:, cols] + _dot(vt, w.astype(BF16)))
        r_ref[h] = jnp.concatenate(r_out, axis=1)
        acc_ref[h] = jnp.concatenate(acc_out, axis=1)

    jd = q0 // tk
    tile(jd, True)

    def live():
        return jnp.max(r_ref[...]) > DEAD_LOG

    def body(carry):
        n, _ = carry
        tile(jd - 1 - n, False)
        return n + 1, live()

    lax.while_loop(lambda c: (c[0] < jd) & c[1], body, (jnp.int32(0), live()))
    for h in range(nh):
        o_ref[0, :, KV_W * h:KV_W * (h + 1)] = _group_out_t(acc_ref[h], None, tq)


def _stick_prompt(qhm, khm, vt):
    B, _, T, _ = qhm.shape
    tq, tk, nh = 256, KV_TILE, KV_PER_STEP
    m = GROUP * tq
    return pl.pallas_call(
        functools.partial(_stick_kernel, tq=tq, tk=tk, nh=nh), grid=(B, N_KV // nh, T // tq),
        in_specs=[pl.BlockSpec((1, nh * GROUP, tq, HD), lambda b, g, i: (b, g, i, 0)),
                  pl.BlockSpec((1, nh, T, HD), lambda b, g, i: (b, g, 0, 0)),
                  pl.BlockSpec((1, T // tk, nh * HD, tk), lambda b, g, i: (b, 0, g, 0))],
        out_specs=pl.BlockSpec((1, tq, nh * KV_W), lambda b, g, i: (b, i, g)),
        out_shape=jax.ShapeDtypeStruct((B, T, D_MODEL), BF16),
        scratch_shapes=[pltpu.VMEM((nh, 1, m), F32), pltpu.VMEM((nh, HD, m), F32)],
        compiler_params=_cp("parallel", "parallel", "arbitrary"), name="stick_p")(qhm, khm, vt)


def _cumsum_kernel(x_ref, hi_ref, mid_ref, lo_ref, carry_ref):
    @pl.when(pl.program_id(1) == 0)
    def _():
        carry_ref[...] = jnp.zeros_like(carry_ref)

    n = x_ref.shape[1]
    low = _tri(n, "ge")
    hi, mid, lo = _split3(x_ref[0])
    c = _dot(low, hi) + _dot(low, mid) + _dot(low, lo) + carry_ref[...]
    carry_ref[...] = c[n - 1:n, :]
    hi_ref[0], mid_ref[0], lo_ref[0] = _split3(c * LOG2E)


def _cumsum_time_parts(x):
    B, T, C = x.shape
    n = 256
    spec = pl.BlockSpec((1, n, C), lambda b, i: (b, i, 0))
    return pl.pallas_call(
        _cumsum_kernel, grid=(B, T // n), in_specs=[spec], out_specs=[spec] * 3,
        out_shape=[jax.ShapeDtypeStruct((B, T, C), BF16)] * 3, scratch_shapes=[pltpu.VMEM((1, C), F32)],
        compiler_params=_cp("parallel", "arbitrary"), name="cumsum_time")(x)


def _softmax_scratch_t(n, m):
    return [pltpu.VMEM((n, 1, m), F32), pltpu.VMEM((n, 1, m), F32), pltpu.VMEM((n, HD, m), F32)]


def _causal_kernel(q_ref, k_ref, vt_ref, o_ref, m_ref, l_ref, acc_ref, *, tq, tk, nh):
    q0 = pl.program_id(2) * tq
    qs = [jnp.concatenate([q_ref[0, GROUP * h + j] for j in range(GROUP)], axis=0) for h in range(nh)]
    m = GROUP * tq
    _init_softmax(m_ref, l_ref, acc_ref)
    qpos = q0 + _iota((1, m), 1) % tq

    def tile(j, valid):
        ks = pl.multiple_of(j * tk, tk)
        s = [_dot_nt(k_ref[0, h, pl.ds(ks, tk), :], qs[h]) for h in range(nh)]
        for h in range(nh):
            _online_t(s[h], valid, vt_ref[0, j, HD * h:HD * (h + 1), :], m_ref, l_ref, acc_ref, h)

    jd = q0 // tk
    tile(jd, (jd * tk + _iota((tk, 1), 0)) <= qpos)

    def body(n, carry):
        tile(n, None)
        return carry

    lax.fori_loop(0, jd, body, 0)
    for h in range(nh):
        o_ref[0, :, KV_W * h:KV_W * (h + 1)] = _group_out_t(acc_ref[h], l_ref[h], tq)


def _fox_prompt(q_aug, khm, vt, cum_parts):
    B, _, T, d_aug = q_aug.shape
    tq, tk = 256, KV_TILE
    c = jnp.stack(cum_parts, axis=-1).reshape(B, T, N_KV, FOX_AUG).transpose(0, 2, 1, 3)
    k_aug = jnp.concatenate([khm, c, jnp.zeros((B, N_KV, T, d_aug - HD - FOX_AUG), BF16)], axis=-1)
    m = GROUP * tq
    return pl.pallas_call(
        functools.partial(_causal_kernel, tq=tq, tk=tk, nh=KV_PER_STEP), grid=(B, N_KV // KV_PER_STEP, T // tq),
        in_specs=[pl.BlockSpec((1, KV_PER_STEP * GROUP, tq, d_aug), lambda b, g, i: (b, g, i, 0)),
                  pl.BlockSpec((1, KV_PER_STEP, T, d_aug), lambda b, g, i: (b, g, 0, 0)),
                  pl.BlockSpec((1, T // tk, KV_PER_STEP * HD, tk), lambda b, g, i: (b, 0, g, 0))],
        out_specs=pl.BlockSpec((1, tq, KV_PER_STEP * KV_W), lambda b, g, i: (b, i, g)),
        out_shape=jax.ShapeDtypeStruct((B, T, D_MODEL), BF16),
        scratch_shapes=_softmax_scratch_t(KV_PER_STEP, m),
        compiler_params=_cp("parallel", "parallel", "arbitrary"), name="fox_p")(q_aug, k_aug, vt)


def _block_mean_kernel(k_ref, o_ref):
    k = k_ref[0]
    nb = k.shape[0] // MOBA_BLOCK
    o_ref[0] = jnp.sum(k.reshape(nb, MOBA_BLOCK, KV_W), axis=1) * (1.0 / MOBA_BLOCK)


def _block_mean(k32):
    B, T, _ = k32.shape
    nb = T // MOBA_BLOCK
    return pl.pallas_call(
        _block_mean_kernel, grid=(B,), in_specs=[pl.BlockSpec((1, T, KV_W), lambda b: (b, 0, 0))],
        out_specs=pl.BlockSpec((1, nb, KV_W), lambda b: (b, 0, 0)),
        out_shape=jax.ShapeDtypeStruct((B, nb, KV_W), F32), compiler_params=_cp("parallel"), name="block_mean")(k32)


def _moba_kernel(q_ref, q32_ref, km_ref, k_ref, vt_ref, o_ref, m_ref, l_ref, acc_ref, *, tq, nb, nh):
    tk = MOBA_BLOCK
    q0 = pl.program_id(2) * tq
    own = q0 // tk
    stack = lambda ref, h: jnp.concatenate([ref[0, GROUP * h + j] for j in range(GROUP)], axis=0)
    m = GROUP * tq
    _init_softmax(m_ref, l_ref, acc_ref)
    row = _iota((nb, m), 0)
    qs = []
    for h in range(nh):
        sel = _topk_select(_dot_nt_f32(km_ref[0, h], stack(q32_ref, h)), own, MOBA_TOPK, 0)
        bias = jnp.where((sel > 0.5) | (row == own), 0.0, NEG)
        bias = jnp.concatenate([jnp.zeros((HD, m), F32), bias, jnp.zeros((HD - nb, m), F32)], axis=0)
        qs.append(stack(q_ref, h) + bias.T.astype(BF16))
    qpos = q0 + _iota((1, m), 1) % tq

    def tile(j, valid):
        ks = pl.multiple_of(j * tk, tk)
        s = [_dot_nt(k_ref[0, h, pl.ds(ks, tk), :], qs[h]) for h in range(nh)]
        for h in range(nh):
            _online_t(s[h], valid, vt_ref[0, j, HD * h:HD * (h + 1), :], m_ref, l_ref, acc_ref, h)

    tile(own, (own * tk + _iota((tk, 1), 0)) <= qpos)

    def body(j, carry):
        tile(j, None)
        return carry

    lax.fori_loop(0, own, body, 0)
    for h in range(nh):
        o_ref[0, :, KV_W * h:KV_W * (h + 1)] = _group_out_t(acc_ref[h], l_ref[h], tq)


def _moba_prompt(q_aug, q32hm, kmean, k_aug, vt):
    B, _, T, _ = q_aug.shape
    tq, nh = 256, KV_PER_STEP
    nb = T // MOBA_BLOCK
    assert nb <= HD
    m = GROUP * tq
    km = kmean.reshape(B, nb, N_KV, HD).transpose(0, 2, 1, 3)
    return pl.pallas_call(
        functools.partial(_moba_kernel, tq=tq, nb=nb, nh=nh), grid=(B, N_KV // nh, T // tq),
        in_specs=[pl.BlockSpec((1, nh * GROUP, tq, 2 * HD), lambda b, g, i: (b, g, i, 0)),
                  pl.BlockSpec((1, nh * GROUP, tq, HD), lambda b, g, i: (b, g, i, 0)),
                  pl.BlockSpec((1, nh, nb, HD), lambda b, g, i: (b, g, 0, 0)),
                  pl.BlockSpec((1, nh, T, 2 * HD), lambda b, g, i: (b, g, 0, 0)),
                  pl.BlockSpec((1, nb, nh * HD, MOBA_BLOCK), lambda b, g, i: (b, 0, g, 0))],
        out_specs=pl.BlockSpec((1, tq, nh * KV_W), lambda b, g, i: (b, i, g)),
        out_shape=jax.ShapeDtypeStruct((B, T, D_MODEL), BF16),
        scratch_shapes=_softmax_scratch_t(nh, m),
        compiler_params=_cp("parallel", "parallel", "arbitrary"), name="moba_p")(q_aug, q32hm, km, k_aug, vt)


def _dsa_kernel(q_ref, qi_ref, wi_ref, ki_ref, k_ref, vt_ref, o_ref, key_ref, m_ref, l_ref, acc_ref, *, tq, n_keep):
    tk = KV_TILE
    q0 = pl.program_id(1) * tq
    n_proc = (q0 + tq + tk - 1) // tk
    qpos = q0 + _iota((1, tq), 1)

    def visible(c):
        return (c * tk + _iota((tk, 1), 0)) <= qpos

    qi3 = jnp.concatenate([qi_ref[0, i] for i in range(IDX_HEADS)], axis=0)

    def score_tile(c, carry):
        ks = pl.multiple_of(c * tk, tk)
        dots = _dot_nt(ki_ref[0, pl.ds(ks, tk), :], qi3)
        sc = jnp.zeros((tk, tq), F32)
        for i in range(IDX_HEADS):
            sc = sc + wi_ref[0, i:i + 1, :] * jnp.maximum(dots[:, i * tq:(i + 1) * tq], 0.0)
        key_ref[c] = _order_key(jnp.where(visible(c), sc + 0.0, -jnp.inf))
        return carry

    lax.fori_loop(0, n_proc, score_tile, 0)

    def count(pred, cand):
        out = []
        for c0 in range(0, tq, Q_CHUNK):
            cols = slice(c0, c0 + Q_CHUNK)

            def body(c, part):
                return part + jnp.where(pred(key_ref[c, :, cols], cand[:, cols]), 1.0, 0.0)

            out.append(jnp.sum(lax.fori_loop(0, n_proc, body, jnp.zeros((tk, Q_CHUNK), F32)), axis=0, keepdims=True))
        return jnp.concatenate(out, axis=1)

    thr = _kth_largest_key(lambda cand: count(lambda k, v: k >= v, cand), (1, tq), n_keep)
    need = n_keep - count(lambda k, v: k > v, thr)
    _init_softmax(m_ref, l_ref, acc_ref)
    before = _tri(tk, "gt")
    qs = [jnp.concatenate([q_ref[0, GROUP * g + j] for j in range(GROUP)], axis=0) for g in range(N_KV)]

    def attend(c, ties_seen):
        ks = pl.multiple_of(c * tk, tk)
        key = key_ref[c]
        tied = jnp.where(key == thr, 1.0, 0.0)
        rank = _dot(before, tied.astype(BF16)) + ties_seen
        keep = (key > thr) | ((key == thr) & (rank < need))
        bias = jnp.where(keep & visible(c), 0.0, NEG)
        bias = jnp.concatenate([bias] * GROUP, axis=1)
        s = [_dot_nt(k_ref[0, g, pl.ds(ks, tk), :], qs[g]) for g in range(N_KV)]
        for g in range(N_KV):
            _online_t(s[g], None, vt_ref[0, c, HD * g:HD * (g + 1), :], m_ref, l_ref, acc_ref, g, bias=bias)
        return ties_seen + jnp.sum(tied, axis=0, keepdims=True)

    lax.fori_loop(0, n_proc, attend, jnp.zeros((1, tq), F32))
    for g in range(N_KV):
        o_ref[0, :, KV_W * g:KV_W * (g + 1)] = _group_out_t(acc_ref[g], l_ref[g], tq)


def _dsa_prompt(qhm, qi3, wi, ki3, khm, vt):
    B, _, T, _ = qhm.shape
    tq = 128
    m = GROUP * tq
    return pl.pallas_call(
        functools.partial(_dsa_kernel, tq=tq, n_keep=min(IDX_TOPK, T // 4)), grid=(B, T // tq),
        in_specs=[pl.BlockSpec((1, N_HEADS, tq, HD), lambda b, i: (b, 0, i, 0)),
                  pl.BlockSpec((1, IDX_HEADS, tq, 4 * IDX_DIM), lambda b, i: (b, 0, i, 0)),
                  pl.BlockSpec((1, IDX_HEADS, tq), lambda b, i: (b, 0, i)),
                  pl.BlockSpec((1, T, 4 * IDX_DIM), lambda b, i: (b, 0, 0)),
                  pl.BlockSpec((1, N_KV, T, HD), lambda b, i: (b, 0, 0, 0)),
                  pl.BlockSpec((1, T // KV_TILE, KV_W, KV_TILE), lambda b, i: (b, 0, 0, 0))],
        out_specs=pl.BlockSpec((1, tq, D_MODEL), lambda b, i: (b, i, 0)),
        out_shape=jax.ShapeDtypeStruct((B, T, D_MODEL), BF16),
        scratch_shapes=[pltpu.VMEM((T // KV_TILE, KV_TILE, tq), I32)] + _softmax_scratch_t(N_KV, m),
        compiler_params=_cp("parallel", "arbitrary"), name="dsa_p")(qhm, qi3, wi.transpose(0, 2, 1), ki3, khm, vt)


def _page_specs(layer, n_pages, nps, reverse):
    def spec(i):
        def idx(b, s, pt):
            p = s * nps + i
            return (layer, pt[b, n_pages - 1 - p if reverse else p], 0, 0)

        return pl.BlockSpec((None, None, KV_W, PAGE), idx)

    return [spec(i) for i in range(nps)]


def _pool_specs(rows, n_pages, nps, reverse):
    def spec(i):
        def idx(b, s, pt):
            p = s * nps + i
            return (pt[b, n_pages - 1 - p if reverse else p], 0, 0)

        return pl.BlockSpec((None, rows, PAGE), idx)

    return [spec(i) for i in range(nps)]


def _row_spec(width):
    return pl.BlockSpec((1, 1, width), lambda b, s, pt: (b, 0, 0))


def _head_spec(width):
    return pl.BlockSpec((1, N_HEADS, width), lambda b, s, pt: (b, 0, 0))


def _softmax_scratch():
    return [pltpu.VMEM((N_HEADS, 1), F32), pltpu.VMEM((N_HEADS, 1), F32), pltpu.VMEM((N_HEADS, KV_W), F32)]


def _new_key_start(q, knew, vnew, m_ref, l_ref, acc_ref):
    z = jnp.sum(q.astype(F32) * knew.astype(BF16).astype(F32), axis=1, keepdims=True)
    m_ref[...] = z
    l_ref[...] = jnp.ones_like(z)
    acc_ref[...] = jnp.broadcast_to(vnew.astype(BF16).astype(F32), acc_ref.shape)


def _cat_pages(refs):
    return jnp.concatenate([r[...].astype(BF16) for r in refs], axis=1)


def _later_sums(x, parts):
    n = x.shape[1] // PAGE
    stack = jnp.concatenate([p[:, PAGE * i:PAGE * (i + 1)] for p in parts(x) for i in range(n)], axis=0)
    out = _dot(stack, _tri(PAGE, "gt"))
    rows = x.shape[0]
    tot = None
    for k in range(out.shape[0] // (n * rows)):
        part = jnp.concatenate([out[(k * n + i) * rows:(k * n + i + 1) * rows] for i in range(n)], axis=1)
        tot = part if tot is None else tot + part
    return tot


def _page_totals(x):
    return [jnp.sum(x[:, PAGE * i:PAGE * (i + 1)], axis=1, keepdims=True) for i in range(x.shape[1] // PAGE)]


def _per_page(cols):
    return jnp.concatenate([jnp.broadcast_to(c, (c.shape[0], PAGE)) for c in cols], axis=1)


def _online_pages(s, valid, vcat, m_ref, l_ref, acc_ref):
    if valid is not None:
        s = jnp.where(valid, s, NEG)
    m_old = m_ref[...]
    m_new = jnp.maximum(m_old, jnp.max(s, axis=1, keepdims=True))
    alpha = jnp.exp2(m_old - m_new)
    p = jnp.exp2(s - m_new)
    if valid is not None:
        p = jnp.where(valid, p, 0.0)
    l_ref[...] = alpha * l_ref[...] + jnp.sum(p, axis=1, keepdims=True)
    acc_ref[...] = alpha * acc_ref[...] + _dot_nt(p.astype(BF16), vcat)
    m_ref[...] = m_new


def _dec_stick_kernel(pt_ref, q_ref, *refs, nps):
    k_refs, v_refs = refs[:nps], refs[nps:2 * nps]
    o_ref, r_ref, acc_ref = refs[2 * nps:]
    step = pl.program_id(1)

    @pl.when(step == 0)
    def _():
        r_ref[...] = jnp.zeros_like(r_ref)
        acc_ref[...] = jnp.zeros_like(acc_ref)

    ls, lk = _log_sigmoid_pair(_dot(q_ref[0], _cat_pages(k_refs)))
    after, run = [], r_ref[...]
    for tot in _page_totals(lk):
        after.append(run)
        run = run + tot
    w = jnp.exp(ls + _later_sums(lk, _split2) + _per_page(after))
    acc = acc_ref[...] + _dot_nt(w.astype(BF16), _cat_pages(v_refs))
    r_ref[...], acc_ref[...] = run, acc

    @pl.when(step == pl.num_programs(1) - 1)
    def _():
        o_ref[0] = acc


def _dec_call(kernel, name, grid_spec, DB):
    return pl.pallas_call(kernel, grid_spec=grid_spec, out_shape=jax.ShapeDtypeStruct((DB, N_HEADS, KV_W), F32),
                          compiler_params=_cp("parallel", "arbitrary"), name=name)


def _dec_stick(qx, kt, vt, page_table, layer):
    DB, n_pages = page_table.shape
    nps = math.gcd(n_pages, DEC_PAGES)
    pages = _page_specs(layer, n_pages, nps, True)
    grid_spec = pltpu.PrefetchScalarGridSpec(
        num_scalar_prefetch=1, grid=(DB, n_pages // nps), in_specs=[_head_spec(KV_W)] + pages + pages,
        out_specs=_head_spec(KV_W), scratch_shapes=[pltpu.VMEM((N_HEADS, 1), F32), pltpu.VMEM((N_HEADS, KV_W), F32)])
    return _dec_call(functools.partial(_dec_stick_kernel, nps=nps), "stick_s", grid_spec, DB)(
        page_table, qx, *([kt] * nps), *([vt] * nps))


def _dec_fox_kernel(pt_ref, q_ref, *refs, nps):
    k_refs, v_refs, lf_refs = refs[:nps], refs[nps:2 * nps], refs[2 * nps:3 * nps]
    kn_ref, vn_ref, lfn_ref, o_ref, s_ref, m_ref, l_ref, acc_ref = refs[3 * nps:]
    step = pl.program_id(1)

    @pl.when(step == 0)
    def _():
        s_ref[...] = lfn_ref[0]
        _new_key_start(q_ref[0], kn_ref[0], vn_ref[0], m_ref, l_ref, acc_ref)

    lf = jnp.concatenate([r[...] for r in lf_refs], axis=1)
    after, run = [], s_ref[...]
    for tot in _page_totals(lf):
        after.append(run)
        run = run + tot
    s_ref[...] = run
    s = _dot(q_ref[0], _cat_pages(k_refs)) + (_later_sums(lf, _split3) + _per_page(after)) * LOG2E
    _online_pages(s, None, _cat_pages(v_refs), m_ref, l_ref, acc_ref)

    @pl.when(step == pl.num_programs(1) - 1)
    def _():
        o_ref[0] = acc_ref[...] / l_ref[...]


def _dec_fox(qx, kt, vt, logf_t, knew, vnew, lfnew, page_table, layer):
    DB, n_pages = page_table.shape
    nps = math.gcd(n_pages, DEC_PAGES)
    pages = _page_specs(layer, n_pages, nps, True)
    grid_spec = pltpu.PrefetchScalarGridSpec(
        num_scalar_prefetch=1, grid=(DB, n_pages // nps),
        in_specs=[_head_spec(KV_W)] + pages + pages + _pool_specs(N_HEADS, n_pages, nps, True)
        + [_row_spec(KV_W), _row_spec(KV_W), _head_spec(1)],
        out_specs=_head_spec(KV_W), scratch_shapes=[pltpu.VMEM((N_HEADS, 1), F32)] + _softmax_scratch())
    return _dec_call(functools.partial(_dec_fox_kernel, nps=nps), "fox_s", grid_spec, DB)(
        page_table, qx, *([kt] * nps), *([vt] * nps), *([logf_t] * nps), knew, vnew, lfnew)


def _dec_kmean_kernel(pt_ref, *refs, nps):
    k_refs, o_ref = refs[:nps], refs[nps]
    step = pl.program_id(1)

    @pl.when(step == 0)
    def _():
        o_ref[...] = jnp.zeros_like(o_ref)

    ones = jnp.ones((8, PAGE), BF16)
    for i in range(nps):
        hi, mid, lo = _split3(k_refs[i][...])
        tot = _dot_nt(ones, hi) + _dot_nt(ones, mid) + _dot_nt(ones, lo)
        blk = (step * nps + i) // (MOBA_BLOCK // PAGE)
        o_ref[0, pl.ds(blk, 1), :] = o_ref[0, pl.ds(blk, 1), :] + tot[0:1, :] * (1.0 / MOBA_BLOCK)


def _dec_kmean(kt, page_table, layer):
    DB, n_pages = page_table.shape
    nps = math.gcd(n_pages, DEC_PAGES)
    nb = n_pages * PAGE // MOBA_BLOCK
    grid_spec = pltpu.PrefetchScalarGridSpec(
        num_scalar_prefetch=1, grid=(DB, n_pages // nps), in_specs=_page_specs(layer, n_pages, nps, False),
        out_specs=pl.BlockSpec((1, nb, KV_W), lambda b, s, pt: (b, 0, 0)))
    return pl.pallas_call(
        functools.partial(_dec_kmean_kernel, nps=nps), grid_spec=grid_spec,
        out_shape=jax.ShapeDtypeStruct((DB, nb, KV_W), F32),
        compiler_params=_cp("parallel", "arbitrary"), name="kmean_s")(page_table, *([kt] * nps))


def _dec_moba_kernel(pt_ref, q_ref, q32_ref, km_ref, *refs, nps, nb):
    k_refs, v_refs = refs[:nps], refs[nps:2 * nps]
    kn_ref, vn_ref, o_ref, sel_ref, m_ref, l_ref, acc_ref = refs[2 * nps:]
    step = pl.program_id(1)

    @pl.when(step == 0)
    def _():
        sel_ref[...] = _topk_select(_dot_nt_f32(q32_ref[0], km_ref[0]), nb, MOBA_TOPK, 1)
        _new_key_start(q_ref[0], kn_ref[0], vn_ref[0], m_ref, l_ref, acc_ref)

    lane = _iota((N_HEADS, nb), 1)
    sel = sel_ref[...]
    bias = []
    for i in range(nps):
        blk = (step * nps + i) // (MOBA_BLOCK // PAGE)
        picked = jnp.sum(jnp.where(lane == blk, sel, 0.0), axis=1, keepdims=True) > 0.5
        bias.append(jnp.where(picked, 0.0, NEG))
    s = _dot(q_ref[0], _cat_pages(k_refs)) + _per_page(bias)
    _online_pages(s, None, _cat_pages(v_refs), m_ref, l_ref, acc_ref)

    @pl.when(step == pl.num_programs(1) - 1)
    def _():
        o_ref[0] = acc_ref[...] / l_ref[...]


def _dec_moba(qx, q32x, kmean, kt, vt, knew, vnew, page_table, layer):
    DB, n_pages = page_table.shape
    nps = math.gcd(n_pages, DEC_PAGES)
    nb = kmean.shape[1]
    pages = _page_specs(layer, n_pages, nps, False)
    grid_spec = pltpu.PrefetchScalarGridSpec(
        num_scalar_prefetch=1, grid=(DB, n_pages // nps),
        in_specs=[_head_spec(KV_W), _head_spec(KV_W), pl.BlockSpec((1, nb, KV_W), lambda b, s, pt: (b, 0, 0))]
        + pages + pages + [_row_spec(KV_W), _row_spec(KV_W)],
        out_specs=_head_spec(KV_W), scratch_shapes=[pltpu.VMEM((N_HEADS, nb), F32)] + _softmax_scratch())
    return _dec_call(functools.partial(_dec_moba_kernel, nps=nps, nb=nb), "moba_s", grid_spec, DB)(
        page_table, qx, q32x, kmean, *([kt] * nps), *([vt] * nps), knew, vnew)


def _dec_score_kernel(pt_ref, qi_ref, wi_ref, *refs, nps, n_pages):
    kidx_refs = refs[:nps]
    kin_ref, o_ref = refs[nps:]
    step = pl.program_id(1)
    qi, wi = qi_ref[0], wi_ref[0]

    @pl.when(step == 0)
    def _():
        o_ref[...] = jnp.full(o_ref.shape, -jnp.inf, F32)
        dn = jnp.sum(qi * kin_ref[0], axis=1, keepdims=True)
        sn = jnp.sum(wi * jnp.maximum(dn, 0.0), axis=0, keepdims=True) + 0.0
        o_ref[0, n_pages:n_pages + 1, :] = jnp.where(_iota((1, PAGE), 1) == 0, sn, -jnp.inf)

    for i in range(nps):
        dots = _dot_f32(qi, kidx_refs[i][...])
        o_ref[0, pl.ds(step * nps + i, 1), :] = jnp.sum(wi * jnp.maximum(dots, 0.0), axis=0, keepdims=True) + 0.0


def _dec_scores(qi3, wi3, kidx_t, kinew, page_table):
    DB, n_pages = page_table.shape
    assert n_pages < PAGE
    nps = math.gcd(n_pages, DEC_PAGES)
    grid_spec = pltpu.PrefetchScalarGridSpec(
        num_scalar_prefetch=1, grid=(DB, n_pages // nps),
        in_specs=[pl.BlockSpec((1, IDX_HEADS, IDX_DIM), lambda b, s, pt: (b, 0, 0)),
                  pl.BlockSpec((1, IDX_HEADS, 1), lambda b, s, pt: (b, 0, 0))]
        + _pool_specs(IDX_DIM, n_pages, nps, False) + [_row_spec(IDX_DIM)],
        out_specs=pl.BlockSpec((1, PAGE, PAGE), lambda b, s, pt: (b, 0, 0)))
    return pl.pallas_call(
        functools.partial(_dec_score_kernel, nps=nps, n_pages=n_pages), grid_spec=grid_spec,
        out_shape=jax.ShapeDtypeStruct((DB, PAGE, PAGE), F32),
        compiler_params=_cp("parallel", "arbitrary"), name="dsa_score_s")(page_table, qi3, wi3, *([kidx_t] * nps), kinew)


def _dec_select_kernel(sc_ref, o_ref, *, n_keys, n_keep):
    key = _order_key(sc_ref[0])

    def total(x):
        return jnp.sum(jnp.sum(x, axis=1, keepdims=True), axis=0, keepdims=True)

    thr = _kth_largest_key(lambda cand: total(jnp.where(key >= cand, 1.0, 0.0)), (1, 1), n_keep)
    need = n_keep - total(jnp.where(key > thr, 1.0, 0.0))
    tied = jnp.where(key == thr, 1.0, 0.0)
    in_row = _dot(tied.astype(BF16), _tri(PAGE, "lt"))
    row_tot = jnp.broadcast_to(jnp.sum(tied, axis=1, keepdims=True), tied.shape).astype(BF16)
    rank = in_row + _dot(_tri(PAGE, "gt"), row_tot)
    pos = _iota(key.shape, 0) * PAGE + _iota(key.shape, 1)
    keep = ((key > thr) | ((key == thr) & (rank < need))) & (pos < n_keys)
    o_ref[0] = jnp.where(keep, 1.0, 0.0)


def _dec_select(scores, n_keys):
    DB = scores.shape[0]
    spec = pl.BlockSpec((1, PAGE, PAGE), lambda b: (b, 0, 0))
    return pl.pallas_call(
        functools.partial(_dec_select_kernel, n_keys=n_keys, n_keep=min(IDX_TOPK, n_keys // 4)), grid=(DB,),
        in_specs=[spec], out_specs=spec, out_shape=jax.ShapeDtypeStruct(scores.shape, F32),
        compiler_params=_cp("parallel"), name="dsa_select_s")(scores)


def _dec_dsa_kernel(pt_ref, q_ref, *refs, nps, n_pages):
    k_refs, v_refs = refs[:nps], refs[nps:2 * nps]
    sel_ref, kn_ref, vn_ref, o_ref, m_ref, l_ref, acc_ref = refs[2 * nps:]
    step = pl.program_id(1)
    q = q_ref[0]

    @pl.when(step == 0)
    def _():
        zn = jnp.sum(q.astype(F32) * kn_ref[0].astype(BF16).astype(F32), axis=1, keepdims=True)
        keep = jnp.broadcast_to(sel_ref[0, n_pages:n_pages + 1, 0:1] > 0.5, zn.shape)
        m_ref[...] = jnp.where(keep, zn, NEG)
        l_ref[...] = jnp.where(keep, 1.0, 0.0)
        acc_ref[...] = jnp.where(keep, jnp.broadcast_to(vn_ref[0].astype(BF16).astype(F32), acc_ref.shape), 0.0)

    keep = jnp.concatenate([sel_ref[0, pl.ds(step * nps + i, 1), :] for i in range(nps)], axis=1)
    s = _dot(q, _cat_pages(k_refs))
    _online_pages(s, jnp.broadcast_to(keep > 0.5, s.shape), _cat_pages(v_refs), m_ref, l_ref, acc_ref)

    @pl.when(step == pl.num_programs(1) - 1)
    def _():
        o_ref[0] = acc_ref[...] / l_ref[...]


def _dec_dsa(qx, kt, vt, sel, knew, vnew, page_table, layer):
    DB, n_pages = page_table.shape
    nps = math.gcd(n_pages, DEC_PAGES)
    pages = _page_specs(layer, n_pages, nps, False)
    grid_spec = pltpu.PrefetchScalarGridSpec(
        num_scalar_prefetch=1, grid=(DB, n_pages // nps),
        in_specs=[_head_spec(KV_W)] + pages + pages
        + [pl.BlockSpec((1, PAGE, PAGE), lambda b, s, pt: (b, 0, 0)), _row_spec(KV_W), _row_spec(KV_W)],
        out_specs=_head_spec(KV_W), scratch_shapes=_softmax_scratch())
    return _dec_call(functools.partial(_dec_dsa_kernel, nps=nps, n_pages=n_pages), "dsa_s", grid_spec, DB)(
        page_table, qx, *([kt] * nps), *([vt] * nps), sel, knew, vnew)


def _rope_tables(pos):
    half = HD // 2
    inv_freq = jnp.power(ROPE_THETA, -jnp.arange(half, dtype=F32) * (2.0 / HD))
    ang = pos.astype(F32)[:, None] * inv_freq[None, :]
    cos, sin = jnp.cos(ang), jnp.sin(ang)
    return jnp.concatenate([cos] * 4, axis=1), jnp.concatenate([-sin, sin] * 2, axis=1)


def _pad_cols(w, mult=128):
    pad = (-w.shape[1]) % mult
    return jnp.pad(w, ((0, 0), (0, pad))) if pad else w


def _layer_weights(kind, w_qkv, fox_w_f, idx_w_q, idx_w_k, idx_w_w):
    cols = [w_qkv]
    if kind == FOX:
        cols.append(_pad_cols(fox_w_f))
    if kind == DSA:
        cols += [idx_w_q, _pad_cols(idx_w_k), _pad_cols(idx_w_w)]
    return jnp.concatenate(cols, axis=1).astype(BF16)


_HEAD_ONEHOT = np.arange(N_HEADS)[:, None] // GROUP == np.arange(N_KV)[None, :]


def _expand_heads(q, dtype):
    m = jnp.asarray(_HEAD_ONEHOT, q.dtype)
    return (q[:, :, None, :] * m[None, :, :, None]).reshape(q.shape[0], N_HEADS, KV_W).astype(dtype)


def _collapse_heads(ox):
    m = jnp.asarray(_HEAD_ONEHOT, ox.dtype)
    o = jnp.sum(ox.reshape(ox.shape[0], N_HEADS, N_KV, HD) * m[None, :, :, None], axis=2)
    return o.reshape(ox.shape[0], D_MODEL).astype(BF16)


def _pages_t(cache):
    depth, n_pool = cache.shape[:2]
    return cache.transpose(0, 1, 3, 4, 2).reshape(depth, n_pool, KV_W, PAGE)


def kernel(x_prompt, x_sample, cache_k, cache_v, cache_logf, cache_kidx, state_conv, page_table, norm_mix, norm_ffn, norm_final, w_qkv, w_o, fox_w_f, fox_b_f, idx_w_q, idx_w_k, idx_w_w, ffn_w_a, ffn_w_b, ffn_conv_w, ffn_conv_b, ffn_w_down):
    B, T, _ = x_prompt.shape
    DB = x_sample.shape[0]
    depth = cache_k.shape[0]
    n_pages = page_table.shape[1]
    past = n_pages * PAGE
    assert T % MOBA_BLOCK == 0 and past % MOBA_BLOCK == 0 and x_sample.shape[1] == 1
    cache_kt, cache_vt = _pages_t(cache_k), _pages_t(cache_v)
    rope_p = _rope_tables(jnp.arange(T, dtype=I32))
    rope_s = _rope_tables(jnp.full((DB,), past, I32))

    xp = x_prompt
    xs = x_sample.reshape(1, DB, D_MODEL)
    new_k_p, new_v_p, new_k_s, new_v_s, conv_p, conv_s = [], [], [], [], [], []
    logf_p = logf_s = kidx_p = kidx_s = None
    for i in range(depth):
        kind = i % 4
        rope = kind in (MOBA, DSA)
        w_all = _layer_weights(kind, w_qkv[i], fox_w_f, idx_w_q, idx_w_k, idx_w_w)
        wo_b = w_o[i].astype(BF16)
        wa_b, wb_b, wd_b = ffn_w_a[i].astype(BF16), ffn_w_b[i].astype(BF16), ffn_w_down[i].astype(BF16)

        outs = _proj(xp, norm_mix[i], w_all, kind, rope_p if rope else None, fox_b_f, True)
        qhm, k32, v32, khm, vhm = outs[:5]
        vt = outs[-1]
        new_k_p.append(k32.reshape(B, T, N_KV, HD))
        new_v_p.append(v32.reshape(B, T, N_KV, HD))
        if kind == STICK:
            o = _stick_prompt(qhm, khm, vt)
        elif kind == FOX:
            logf_p = outs[5]
            o = _fox_prompt(qhm, khm, vt, _cumsum_time_parts(logf_p))
        elif kind == MOBA:
            o = _moba_prompt(qhm, outs[5], _block_mean(k32), khm, vt)
        else:
            kidx_p, wi, qi3, ki3 = outs[6:10]
            o = _dsa_prompt(qhm, qi3, wi, ki3, khm, vt)
        x2, rows = _mix_ffn(xp.reshape(B * T, D_MODEL), o.reshape(B * T, D_MODEL), wo_b, norm_ffn[i], wa_b, wb_b,
                            ffn_conv_w[i], ffn_conv_b[i], wd_b, T)
        conv_p.append(rows)
        xp = x2.reshape(B, T, D_MODEL)

        outs = _proj(xs, norm_mix[i], w_all, kind, rope_s if rope else None, fox_b_f, False)
        qhm, k32, v32 = outs[:3]
        new_k_s.append(k32.reshape(DB, 1, N_KV, HD))
        new_v_s.append(v32.reshape(DB, 1, N_KV, HD))
        qx = _expand_heads(qhm[0].transpose(1, 0, 2), BF16)
        knew, vnew = k32.reshape(DB, 1, KV_W), v32.reshape(DB, 1, KV_W)
        if kind == STICK:
            ox = _dec_stick(qx, cache_kt, cache_vt, page_table, i)
        elif kind == FOX:
            lf = outs[5][0]
            logf_s = lf.reshape(DB, 1, N_HEADS)
            ox = _dec_fox(qx, cache_kt, cache_vt, cache_logf.transpose(0, 2, 1), knew, vnew,
                          lf.reshape(DB, N_HEADS, 1), page_table, i)
        elif kind == MOBA:
            q32x = _expand_heads(outs[5][0].transpose(1, 0, 2), F32)
            ox = _dec_moba(qx, q32x, _dec_kmean(cache_kt, page_table, i), cache_kt, cache_vt, knew, vnew, page_table, i)
        else:
            qihm, ki, wi = outs[5:8]
            kidx_s = ki.reshape(DB, 1, IDX_DIM)
            scores = _dec_scores(qihm[0].transpose(1, 0, 2), wi[0].reshape(DB, IDX_HEADS, 1),
                                 cache_kidx.transpose(0, 2, 1), kidx_s, page_table)
            ox = _dec_dsa(qx, cache_kt, cache_vt, _dec_select(scores, past + 1), knew, vnew, page_table, i)
        st = state_conv[i]
        s2, a_new = _mix_ffn(xs.reshape(DB, D_MODEL), _collapse_heads(ox), wo_b, norm_ffn[i], wa_b, wb_b,
                             ffn_conv_w[i], ffn_conv_b[i], wd_b, 1, prev=(st[:, 0], st[:, 1]))
        conv_s.append(jnp.stack([st[:, 1], a_new], axis=1))
        xs = s2.reshape(1, DB, D_MODEL)

    y_p = _final_norm(xp.reshape(B * T, D_MODEL), norm_final).reshape(B, T, D_MODEL)
    y_s = _final_norm(xs.reshape(DB, D_MODEL), norm_final).reshape(DB, 1, D_MODEL)
    return (y_p, y_s, jnp.stack(new_k_p), jnp.stack(new_v_p), jnp.stack(new_k_s), jnp.stack(new_v_s),
            logf_p, logf_s, kidx_p, kidx_s, jnp.stack(conv_p), jnp.stack(conv_s))
```

```python
import functools
import math

import jax
import jax.numpy as jnp
import numpy as np
from jax import lax
from jax.experimental import pallas as pl
from jax.experimental.pallas import tpu as pltpu

F32, BF16, I32 = jnp.float32, jnp.bfloat16, jnp.int32

D_MODEL = 1024
N_HEADS = 16
N_KV = 4
GROUP = N_HEADS // N_KV
HD = 64
KV_W = N_KV * HD
D_FF = 2816
CONV_W = 3
PAGE = 128
MOBA_BLOCK = 256
MOBA_TOPK = 3
IDX_HEADS = 8
IDX_DIM = 64
IDX_TOPK = 256
ROPE_THETA = 10000.0
RMS_EPS = 1e-6
STICK, FOX, MOBA, DSA = 0, 1, 2, 3
NEG = -1e30
M_FLOOR = -1e29
LOG2E = 1.4426950408889634
DEAD_LOG = -104.0
INT_MIN = -2147483648
VMEM_LIMIT = 48 * 1024 * 1024
FF_CHUNK = 256
FFN_TM = 512
PROJ_TM = 256
KV_TILE = 256
DEC_PAGES = 64
KV_PER_STEP = 4
Q_CHUNK = 128
FOX_AUG = 3 * GROUP


def _cp(*sem):
    return pltpu.CompilerParams(dimension_semantics=sem, vmem_limit_bytes=VMEM_LIMIT)


def _dot(a, b):
    return jnp.dot(a, b, preferred_element_type=F32)


def _dot_nt(a, b):
    return lax.dot_general(a, b, (((1,), (1,)), ((), ())), preferred_element_type=F32)


def _split2(x):
    hi = x.astype(BF16)
    lo = (x - hi.astype(F32)).astype(BF16)
    return hi, lo


def _split3(x):
    hi = x.astype(BF16)
    r = x - hi.astype(F32)
    mid = r.astype(BF16)
    lo = (r - mid.astype(F32)).astype(BF16)
    return hi, mid, lo


def _dot_f32(a, b):
    ah, al = _split2(a)
    bh, bl = _split2(b)
    return _dot(ah, bh) + _dot(al, bh) + _dot(ah, bl)


def _dot_nt_f32(a, b):
    ah, al = _split2(a)
    bh, bl = _split2(b)
    return _dot_nt(ah, bh) + _dot_nt(al, bh) + _dot_nt(ah, bl)


def _log_sigmoid_pair(z):
    l1p = jnp.log(1.0 + jnp.exp(-jnp.abs(z)))
    return jnp.minimum(z, 0.0) - l1p, jnp.minimum(-z, 0.0) - l1p


def _iota(shape, dim):
    return lax.broadcasted_iota(I32, shape, dim)


def _tri(n, kind):
    r, c = _iota((n, n), 0), _iota((n, n), 1)
    m = {"gt": r > c, "lt": r < c, "ge": r >= c}[kind]
    return jnp.where(m, 1.0, 0.0).astype(BF16)


def _rms_bf16(x, g):
    y = x * lax.rsqrt(jnp.mean(x * x, axis=-1, keepdims=True) + RMS_EPS)
    return (y * g).astype(BF16)


def _rope128(xc, cos, sin):
    lane = _iota(xc.shape, 1)
    first = (lane % HD) < (HD // 2)
    rolled = jnp.where(first, pltpu.roll(xc, 128 - HD // 2, 1), pltpu.roll(xc, HD // 2, 1))
    return xc * cos + rolled * sin


def _halves(xc):
    return xc[:, :HD], pltpu.roll(xc, HD, 1)[:, :HD]


def _place(parts):
    r, c = _iota((HD, KV_W), 0), _iota((HD, KV_W), 1)
    out = None
    for j, pj in enumerate(parts):
        e = jnp.where(c == r + HD * j, 1.0, 0.0).astype(BF16)
        t = _dot(pj, e)
        out = t if out is None else out + t
    return out


def _stack_heads(q_ref):
    return jnp.concatenate([q_ref[0, j] for j in range(GROUP)], axis=0)


def _init_softmax(m_ref, l_ref, acc_ref):
    m_ref[...] = jnp.full(m_ref.shape, M_FLOOR, F32)
    l_ref[...] = jnp.zeros_like(l_ref)
    acc_ref[...] = jnp.zeros_like(acc_ref)


def _online_t(s, valid, vt, m_ref, l_ref, acc_ref, idx, bias=None):
    m_all, l_all, acc_all = m_ref[idx], l_ref[idx], acc_ref[idx]
    m_out, l_out, acc_out = [], [], []
    for c0 in range(0, s.shape[1], Q_CHUNK):
        cols = slice(c0, c0 + Q_CHUNK)
        sc = s[:, cols]
        if bias is not None:
            sc = sc + bias[:, cols]
        if valid is not None:
            vc = valid[:, cols]
            sc = jnp.where(vc, sc, NEG)
        m_old = m_all[:, cols]
        m_new = jnp.maximum(m_old, jnp.max(sc, axis=0, keepdims=True))
        alpha = jnp.exp2(m_old - m_new)
        p = jnp.exp2(sc - m_new)
        if valid is not None:
            p = jnp.where(vc, p, 0.0)
        l_out.append(alpha * l_all[:, cols] + jnp.sum(p, axis=0, keepdims=True))
        acc_out.append(alpha * acc_all[:, cols] + _dot(vt, p.astype(BF16)))
        m_out.append(m_new)
    m_ref[idx] = jnp.concatenate(m_out, axis=1)
    l_ref[idx] = jnp.concatenate(l_out, axis=1)
    acc_ref[idx] = jnp.concatenate(acc_out, axis=1)


def _group_out_t(acc_t, l, tq):
    r = acc_t if l is None else acc_t / l
    o_t = jnp.concatenate([r[:, j * tq:(j + 1) * tq] for j in range(GROUP)], axis=0)
    return o_t.T.astype(BF16)


def _topk_select(gate, n_valid, k, axis):
    pos = _iota(gate.shape, axis)
    posf = pos.astype(F32)
    g = jnp.where(pos < n_valid, gate, -jnp.inf)
    sel = jnp.zeros(gate.shape, F32)
    for _ in range(k):
        best = jnp.max(g, axis=axis, keepdims=True)
        first = jnp.min(jnp.where(g == best, posf, float(gate.shape[axis])), axis=axis, keepdims=True)
        hit = posf == first
        sel = jnp.where(hit, 1.0, sel)
        g = jnp.where(hit, -jnp.inf, g)
    return jnp.where(pos < n_valid, sel, 0.0)


def _order_key(score):
    bits = lax.bitcast_convert_type(score, I32)
    return jnp.where(bits < 0, bits ^ jnp.int32(0x7FFFFFFF), bits)


def _kth_largest_key(count_ge, shape, k):
    def body(it, v):
        cand = v + lax.shift_left(jnp.int32(1), 31 - it)
        return jnp.where(count_ge(cand) >= k, cand, v)

    return lax.fori_loop(0, 32, body, jnp.full(shape, INT_MIN, I32))


def _proj_kernel(*refs, kind, rope, prompt, aug):
    it = iter(refs)
    x_ref, g_ref, w_ref = next(it), next(it), next(it)
    bf_ref = next(it) if kind == FOX else None
    cos_ref, sin_ref = (next(it), next(it)) if rope else (None, None)
    qpat_ref = next(it) if aug else None
    kpat_ref = next(it) if aug == "qk" else None
    qhm_ref, k32_ref, v32_ref, khm_ref, vhm_ref = next(it), next(it), next(it), next(it), next(it)
    low = _iota((x_ref.shape[1], 128), 1) < HD
    h = _rms_bf16(x_ref[0], g_ref[...])
    y = _dot(h, w_ref[...])
    cos = cos_ref[...] if rope else None
    sin = sin_ref[...] if rope else None

    def chunk(c, rot):
        xc = y[:, 128 * c:128 * (c + 1)]
        return _rope128(xc, cos, sin) if rot else xc

    q32hm_ref = next(it) if kind == MOBA else None
    q_scale = HD ** -0.5 * (1.0 if kind == STICK else LOG2E)
    for c in range(N_HEADS // 2):
        qc = chunk(c, rope)
        lo, hi = _halves(qc)
        if aug:
            for j, z in enumerate((qc, pltpu.roll(qc, HD, 1))):
                qhm_ref[0, 2 * c + j] = jnp.where(low, z * q_scale, qpat_ref[2 * c + j:2 * c + j + 1, :]).astype(BF16)
        else:
            qhm_ref[0, 2 * c] = (lo * q_scale).astype(BF16)
            qhm_ref[0, 2 * c + 1] = (hi * q_scale).astype(BF16)
        if kind == MOBA:
            q32hm_ref[0, 2 * c] = lo
            q32hm_ref[0, 2 * c + 1] = hi
    base = D_MODEL // 128
    for c in range(N_KV // 2):
        kc = chunk(base + c, rope)
        k32_ref[0, :, 128 * c:128 * (c + 1)] = kc
        vc = chunk(base + N_KV // 2 + c, False)
        v32_ref[0, :, 128 * c:128 * (c + 1)] = vc
        if aug == "qk":
            for j, z in enumerate((kc, pltpu.roll(kc, HD, 1))):
                khm_ref[0, 2 * c + j] = jnp.where(low, z, kpat_ref[...]).astype(BF16)
        else:
            klo, khi = _halves(kc)
            khm_ref[0, 2 * c] = klo.astype(BF16)
            khm_ref[0, 2 * c + 1] = khi.astype(BF16)
        vlo, vhi = _halves(vc)
        vhm_ref[0, 2 * c] = vlo.astype(BF16)
        vhm_ref[0, 2 * c + 1] = vhi.astype(BF16)
    base = (D_MODEL + 2 * KV_W) // 128
    if kind == FOX:
        logf_ref = next(it)
        zf = chunk(base, False)[:, :N_HEADS] + bf_ref[...]
        logf_ref[0] = _log_sigmoid_pair(zf)[0]
    if kind == DSA:
        qihm_ref, ki_ref, wi_ref = next(it), next(it), next(it)
        for c in range(IDX_HEADS // 2):
            lo, hi = _halves(chunk(base + c, True))
            qihm_ref[0, 2 * c] = lo
            qihm_ref[0, 2 * c + 1] = hi
        kic = chunk(base + IDX_HEADS // 2, True)
        ki_ref[0] = kic[:, :IDX_DIM]
        wi_ref[0] = chunk(base + IDX_HEADS // 2 + 1, False)[:, :IDX_HEADS] * (IDX_HEADS ** -0.5 * IDX_DIM ** -0.5)
        if prompt:
            qi3_ref, ki3_ref = next(it), next(it)
            low = _iota(kic.shape, 1) < IDX_DIM
            for c in range(IDX_HEADS // 2):
                xc = chunk(base + c, True)
                for j, z in enumerate((jnp.where(low, xc, 0.0), jnp.where(low, pltpu.roll(xc, IDX_DIM, 1), 0.0))):
                    hi = z.astype(BF16).astype(F32)
                    qi3_ref[0, 2 * c + j] = jnp.concatenate([hi + pltpu.roll(z - hi, IDX_DIM, 1), hi], axis=1).astype(BF16)
            hi = kic.astype(BF16).astype(F32)
            ki3_ref[0] = jnp.concatenate([hi + pltpu.roll(hi, IDX_DIM, 1), kic - hi], axis=1).astype(BF16)
    if prompt:
        vt_ref = next(it)
        vt_ref[0, 0] = y[:, D_MODEL + KV_W:D_MODEL + 2 * KV_W].T.astype(BF16)


def _score_features(kind, T):
    qpat = np.zeros((N_HEADS, 128), np.float32)
    if kind == FOX:
        for h in range(N_HEADS):
            qpat[h, HD + 3 * (h % GROUP):HD + 3 * (h % GROUP) + 3] = -1.0
        return "q", [jnp.asarray(qpat)]
    kpat = np.zeros((T, 128), np.float32)
    kpat[np.arange(T), HD + np.arange(T) // MOBA_BLOCK] = 1.0
    return "qk", [jnp.asarray(qpat), jnp.asarray(kpat)]


def _proj(x, g, w_all, kind, rope_tabs, fox_b, prompt):
    B, T, _ = x.shape
    tm = min(PROJ_TM, T)
    n_cols = w_all.shape[1]
    rope = rope_tabs is not None
    aug, pats = _score_features(kind, T) if prompt and kind in (FOX, MOBA) else (None, [])
    q_w = 2 * HD if aug else HD
    k_w = 2 * HD if aug == "qk" else HD
    row = lambda b, i: (b, i, 0)
    hm = lambda b, i: (b, 0, i, 0)
    in_specs = [pl.BlockSpec((1, tm, D_MODEL), row), pl.BlockSpec((1, D_MODEL), lambda b, i: (0, 0)),
                pl.BlockSpec((D_MODEL, n_cols), lambda b, i: (0, 0))]
    args = [x, g.reshape(1, D_MODEL), w_all]
    if kind == FOX:
        in_specs.append(pl.BlockSpec((1, N_HEADS), lambda b, i: (0, 0)))
        args.append(fox_b.reshape(1, N_HEADS))
    if rope:
        in_specs += [pl.BlockSpec((tm, 128), lambda b, i: (i, 0))] * 2
        args += list(rope_tabs)
    if aug:
        in_specs.append(pl.BlockSpec((N_HEADS, 128), lambda b, i: (0, 0)))
    if aug == "qk":
        in_specs.append(pl.BlockSpec((tm, 128), lambda b, i: (i, 0)))
    args += pats
    out_shape = [jax.ShapeDtypeStruct((B, N_HEADS, T, q_w), BF16), jax.ShapeDtypeStruct((B, T, KV_W), F32),
                 jax.ShapeDtypeStruct((B, T, KV_W), F32), jax.ShapeDtypeStruct((B, N_KV, T, k_w), BF16),
                 jax.ShapeDtypeStruct((B, N_KV, T, HD), BF16)]
    out_specs = [pl.BlockSpec((1, N_HEADS, tm, q_w), hm), pl.BlockSpec((1, tm, KV_W), row),
                 pl.BlockSpec((1, tm, KV_W), row), pl.BlockSpec((1, N_KV, tm, k_w), hm),
                 pl.BlockSpec((1, N_KV, tm, HD), hm)]
    if kind == MOBA:
        out_shape.append(jax.ShapeDtypeStruct((B, N_HEADS, T, HD), F32))
        out_specs.append(pl.BlockSpec((1, N_HEADS, tm, HD), hm))
    if kind == FOX:
        out_shape.append(jax.ShapeDtypeStruct((B, T, N_HEADS), F32))
        out_specs.append(pl.BlockSpec((1, tm, N_HEADS), row))
    if kind == DSA:
        out_shape += [jax.ShapeDtypeStruct((B, IDX_HEADS, T, IDX_DIM), F32),
                      jax.ShapeDtypeStruct((B, T, IDX_DIM), F32), jax.ShapeDtypeStruct((B, T, IDX_HEADS), F32)]
        out_specs += [pl.BlockSpec((1, IDX_HEADS, tm, IDX_DIM), hm), pl.BlockSpec((1, tm, IDX_DIM), row),
                      pl.BlockSpec((1, tm, IDX_HEADS), row)]
        if prompt:
            out_shape += [jax.ShapeDtypeStruct((B, IDX_HEADS, T, 4 * IDX_DIM), BF16),
                          jax.ShapeDtypeStruct((B, T, 4 * IDX_DIM), BF16)]
            out_specs += [pl.BlockSpec((1, IDX_HEADS, tm, 4 * IDX_DIM), hm), pl.BlockSpec((1, tm, 4 * IDX_DIM), row)]
    if prompt:
        assert tm == KV_TILE
        out_shape.append(jax.ShapeDtypeStruct((B, T // tm, KV_W, tm), BF16))
        out_specs.append(pl.BlockSpec((1, 1, KV_W, tm), lambda b, i: (b, i, 0, 0)))
    return pl.pallas_call(
        functools.partial(_proj_kernel, kind=kind, rope=rope, prompt=prompt, aug=aug),
        grid=(B, T // tm), in_specs=in_specs, out_specs=out_specs, out_shape=out_shape,
        compiler_params=_cp("parallel", "arbitrary"), name=f"proj_{kind}_{'p' if prompt else 's'}")(*args)


def _gelu_tanh(x):
    return 0.5 * x * (1.0 + jnp.tanh(0.7978845608028654 * (x + 0.044715 * x * x * x)))


def _ffn_kernel(*refs, prompt, tiles_per_seq):
    if prompt:
        x_ref, o_ref, wo_ref, g_ref, wa_ref, wb_ref, cw_ref, cb_ref, wd_ref, y_ref, conv_ref, u_ref, carry_ref = refs
    else:
        x_ref, o_ref, wo_ref, g_ref, wa_ref, wb_ref, cw_ref, cb_ref, wd_ref, p2_ref, p1_ref, y_ref, a_ref, u_ref = refs
    x = x_ref[...] + _dot(o_ref[...], wo_ref[...])
    h = _rms_bf16(x, g_ref[...])
    if prompt:
        @pl.when(pl.program_id(0) % tiles_per_seq == 0)
        def _():
            carry_ref[...] = jnp.zeros_like(carry_ref)

        row = _iota((x.shape[0], FF_CHUNK), 0)
    for c in range(D_FF // FF_CHUNK):
        cols = slice(c * FF_CHUNK, (c + 1) * FF_CHUNK)
        a = _dot(h, wa_ref[:, cols])
        gate = _dot(h, wb_ref[:, cols])
        if prompt:
            prev2, prev1 = carry_ref[0:1, cols], carry_ref[1:2, cols]
            back2 = pltpu.roll(a, 2, 0)
            a_m1 = jnp.where(row == 0, prev1, pltpu.roll(a, 1, 0))
            a_m2 = jnp.where(row == 0, prev2, jnp.where(row == 1, prev1, back2))
            carry_ref[0:2, cols] = back2[0:2, :]
            conv_ref[0, :, cols] = back2[0:2, :]
        else:
            a_m2, a_m1 = p2_ref[:, cols], p1_ref[:, cols]
            a_ref[:, cols] = a
        conv = cb_ref[:, cols] + cw_ref[0:1, cols] * a_m2 + cw_ref[1:2, cols] * a_m1 + cw_ref[2:3, cols] * a
        u_ref[:, cols] = (_gelu_tanh(conv) * gate).astype(BF16)
    y_ref[...] = x + _dot(u_ref[...], wd_ref[...])


def _mix_ffn(x2, o2, w_o, g, w_a, w_b, conv_w, conv_b, w_down, seq_len, prev=None):
    M = x2.shape[0]
    prompt = prev is None
    tm = min(FFN_TM, seq_len) if prompt else M
    tps = seq_len // tm if prompt else 1
    xs = pl.BlockSpec((tm, D_MODEL), lambda i: (i, 0))
    whole = lambda shape: pl.BlockSpec(shape, lambda i: (0,) * len(shape), pipeline_mode=pl.Buffered(1))
    in_specs = [xs, xs, whole((D_MODEL, D_MODEL)), whole((1, D_MODEL)), whole((D_MODEL, D_FF)), whole((D_MODEL, D_FF)),
                whole((CONV_W, D_FF)), whole((1, D_FF)), whole((D_FF, D_MODEL))]
    args = [x2, o2, w_o, g.reshape(1, D_MODEL), w_a, w_b, conv_w, conv_b.reshape(1, D_FF), w_down]
    scratch = [pltpu.VMEM((tm, D_FF), BF16)]
    if prompt:
        out_shape = [jax.ShapeDtypeStruct((M, D_MODEL), F32), jax.ShapeDtypeStruct((M // tm, CONV_W - 1, D_FF), F32)]
        out_specs = [xs, pl.BlockSpec((1, CONV_W - 1, D_FF), lambda i: (i, 0, 0))]
        scratch.append(pltpu.VMEM((8, D_FF), F32))
    else:
        ps = pl.BlockSpec((tm, D_FF), lambda i: (i, 0))
        in_specs += [ps, ps]
        args += list(prev)
        out_shape = [jax.ShapeDtypeStruct((M, D_MODEL), F32), jax.ShapeDtypeStruct((M, D_FF), F32)]
        out_specs = [xs, ps]
    y, extra = pl.pallas_call(
        functools.partial(_ffn_kernel, prompt=prompt, tiles_per_seq=tps), grid=(M // tm,),
        in_specs=in_specs, out_specs=out_specs, out_shape=out_shape, scratch_shapes=scratch,
        compiler_params=_cp("arbitrary"), name="ffn_p" if prompt else "ffn_s")(*args)
    return y, (extra[tps - 1::tps] if prompt else extra)


def _final_norm_kernel(x_ref, g_ref, y_ref):
    x = x_ref[...]
    y_ref[...] = x * lax.rsqrt(jnp.mean(x * x, axis=-1, keepdims=True) + RMS_EPS) * g_ref[...]


def _final_norm(x2, g):
    M = x2.shape[0]
    tm = min(1024, M)
    return pl.pallas_call(
        _final_norm_kernel, grid=(M // tm,),
        in_specs=[pl.BlockSpec((tm, D_MODEL), lambda i: (i, 0)), pl.BlockSpec((1, D_MODEL), lambda i: (0, 0))],
        out_specs=pl.BlockSpec((tm, D_MODEL), lambda i: (i, 0)),
        out_shape=jax.ShapeDtypeStruct((M, D_MODEL), F32), compiler_params=_cp("parallel"), name="final_norm")(x2, g.reshape(1, D_MODEL))


def _stick_kernel(q_ref, k_ref, vt_ref, o_ref, r_ref, acc_ref, *, tq, tk, nh):
    q0 = pl.program_id(2) * tq
    qs = [jnp.concatenate([q_ref[0, GROUP * h + j] for j in range(GROUP)], axis=0) for h in range(nh)]
    m = GROUP * tq
    later = _tri(tk, "lt")
    qpos = q0 + _iota((1, m), 1) % tq
    r_ref[...] = jnp.zeros_like(r_ref)
    acc_ref[...] = jnp.zeros_like(acc_ref)

    def tile(j, masked):
        ks = pl.multiple_of(j * tk, tk)
        zs = [_dot_nt(k_ref[0, h, pl.ds(ks, tk), :], qs[h]) for h in range(nh)]
        for h in range(nh):
            head_tile(j, h, zs[h], masked)

    def head_tile(j, h, z, masked):
        vt = vt_ref[0, j, HD * h:HD * (h + 1), :]
        kpos = j * tk + _iota((tk, 1), 0)
        r_all, acc_all = r_ref[h], acc_ref[h]
        chunks = [slice(c0, c0 + Q_CHUNK) for c0 in range(0, m, Q_CHUNK)]
        staged, r_out, acc_out = [], [], []
        for cols in chunks:
            zc = z[:, cols]
            ls = jnp.minimum(zc, 0.0) - jnp.log(1.0 + jnp.exp(-jnp.abs(zc)))
            lk = ls - zc
            past = kpos < qpos[:, cols] if masked else None
            if masked:
                lk = jnp.where(past, lk, 0.0)
            parts = _dot(later, jnp.concatenate(_split2(lk), axis=1))
            staged.append((ls, parts, past))
            r_out.append(r_all[:, cols] + jnp.sum(lk, axis=0, keepdims=True))
        for cols, (ls, parts, past) in zip(chunks, staged):
            w = jnp.exp(ls + parts[:, :Q_CHUNK] + parts[:, Q_CHUNK:] + r_all[:, cols])
            if masked:
                w = jnp.where(past, w, 0.0)
            acc_out.append(acc_all[:, cols] + _dot(vt, w.astype(BF16)))
        r_ref[h] = jnp.concatenate(r_out, axis=1)
        acc_ref[h] = jnp.concatenate(acc_out, axis=1)

    jd = q0 // tk
    tile(jd, True)

    def live():
        return jnp.max(r_ref[...]) > DEAD_LOG

    def body(carry):
        n, _ = carry
        tile(jd - 1 - n, False)
        return n + 1, live()

    lax.while_loop(lambda c: (c[0] < jd) & c[1], body, (jnp.int32(0), live()))
    for h in range(nh):
        o_ref[0, :, KV_W * h:KV_W * (h + 1)] = _group_out_t(acc_ref[h], None, tq)


def _stick_prompt(qhm, khm, vt):
    B, _, T, _ = qhm.shape
    tq, tk, nh = 256, KV_TILE, KV_PER_STEP
    m = GROUP * tq
    return pl.pallas_call(
        functools.partial(_stick_kernel, tq=tq, tk=tk, nh=nh), grid=(B, N_KV // nh, T // tq),
        in_specs=[pl.BlockSpec((1, nh * GROUP, tq, HD), lambda b, g, i: (b, g, i, 0)),
                  pl.BlockSpec((1, nh, T, HD), lambda b, g, i: (b, g, 0, 0)),
                  pl.BlockSpec((1, T // tk, nh * HD, tk), lambda b, g, i: (b, 0, g, 0))],
        out_specs=pl.BlockSpec((1, tq, nh * KV_W), lambda b, g, i: (b, i, g)),
        out_shape=jax.ShapeDtypeStruct((B, T, D_MODEL), BF16),
        scratch_shapes=[pltpu.VMEM((nh, 1, m), F32), pltpu.VMEM((nh, HD, m), F32)],
        compiler_params=_cp("parallel", "parallel", "arbitrary"), name="stick_p")(qhm, khm, vt)


def _cumsum_kernel(x_ref, hi_ref, mid_ref, lo_ref, carry_ref):
    @pl.when(pl.program_id(1) == 0)
    def _():
        carry_ref[...] = jnp.zeros_like(carry_ref)

    n = x_ref.shape[1]
    low = _tri(n, "ge")
    hi, mid, lo = _split3(x_ref[0])
    c = _dot(low, hi) + _dot(low, mid) + _dot(low, lo) + carry_ref[...]
    carry_ref[...] = c[n - 1:n, :]
    hi_ref[0], mid_ref[0], lo_ref[0] = _split3(c * LOG2E)


def _cumsum_time_parts(x):
    B, T, C = x.shape
    n = 256
    spec = pl.BlockSpec((1, n, C), lambda b, i: (b, i, 0))
    return pl.pallas_call(
        _cumsum_kernel, grid=(B, T // n), in_specs=[spec], out_specs=[spec] * 3,
        out_shape=[jax.ShapeDtypeStruct((B, T, C), BF16)] * 3, scratch_shapes=[pltpu.VMEM((1, C), F32)],
        compiler_params=_cp("parallel", "arbitrary"), name="cumsum_time")(x)


def _softmax_scratch_t(n, m):
    return [pltpu.VMEM((n, 1, m), F32), pltpu.VMEM((n, 1, m), F32), pltpu.VMEM((n, HD, m), F32)]


def _causal_kernel(q_ref, k_ref, vt_ref, o_ref, m_ref, l_ref, acc_ref, *, tq, tk, nh):
    q0 = pl.program_id(2) * tq
    qs = [jnp.concatenate([q_ref[0, GROUP * h + j] for j in range(GROUP)], axis=0) for h in range(nh)]
    m = GROUP * tq
    _init_softmax(m_ref, l_ref, acc_ref)
    qpos = q0 + _iota((1, m), 1) % tq

    def tile(j, valid):
        ks = pl.multiple_of(j * tk, tk)
        s = [_dot_nt(k_ref[0, h, pl.ds(ks, tk), :], qs[h]) for h in range(nh)]
        for h in range(nh):
            _online_t(s[h], valid, vt_ref[0, j, HD * h:HD * (h + 1), :], m_ref, l_ref, acc_ref, h)

    jd = q0 // tk
    tile(jd, (jd * tk + _iota((tk, 1), 0)) <= qpos)

    def body(n, carry):
        tile(n, None)
        return carry

    lax.fori_loop(0, jd, body, 0)
    for h in range(nh):
        o_ref[0, :, KV_W * h:KV_W * (h + 1)] = _group_out_t(acc_ref[h], l_ref[h], tq)


def _fox_prompt(q_aug, khm, vt, cum_parts):
    B, _, T, d_aug = q_aug.shape
    tq, tk = 256, KV_TILE
    c = jnp.stack(cum_parts, axis=-1).reshape(B, T, N_KV, FOX_AUG).transpose(0, 2, 1, 3)
    k_aug = jnp.concatenate([khm, c, jnp.zeros((B, N_KV, T, d_aug - HD - FOX_AUG), BF16)], axis=-1)
    m = GROUP * tq
    return pl.pallas_call(
        functools.partial(_causal_kernel, tq=tq, tk=tk, nh=KV_PER_STEP), grid=(B, N_KV // KV_PER_STEP, T // tq),
        in_specs=[pl.BlockSpec((1, KV_PER_STEP * GROUP, tq, d_aug), lambda b, g, i: (b, g, i, 0)),
                  pl.BlockSpec((1, KV_PER_STEP, T, d_aug), lambda b, g, i: (b, g, 0, 0)),
                  pl.BlockSpec((1, T // tk, KV_PER_STEP * HD, tk), lambda b, g, i: (b, 0, g, 0))],
        out_specs=pl.BlockSpec((1, tq, KV_PER_STEP * KV_W), lambda b, g, i: (b, i, g)),
        out_shape=jax.ShapeDtypeStruct((B, T, D_MODEL), BF16),
        scratch_shapes=_softmax_scratch_t(KV_PER_STEP, m),
        compiler_params=_cp("parallel", "parallel", "arbitrary"), name="fox_p")(q_aug, k_aug, vt)


def _block_mean_kernel(k_ref, o_ref):
    k = k_ref[0]
    nb = k.shape[0] // MOBA_BLOCK
    o_ref[0] = jnp.sum(k.reshape(nb, MOBA_BLOCK, KV_W), axis=1) * (1.0 / MOBA_BLOCK)


def _block_mean(k32):
    B, T, _ = k32.shape
    nb = T // MOBA_BLOCK
    return pl.pallas_call(
        _block_mean_kernel, grid=(B,), in_specs=[pl.BlockSpec((1, T, KV_W), lambda b: (b, 0, 0))],
        out_specs=pl.BlockSpec((1, nb, KV_W), lambda b: (b, 0, 0)),
        out_shape=jax.ShapeDtypeStruct((B, nb, KV_W), F32), compiler_params=_cp("parallel"), name="block_mean")(k32)


def _moba_kernel(q_ref, q32_ref, km_ref, k_ref, vt_ref, o_ref, m_ref, l_ref, acc_ref, *, tq, nb, nh):
    tk = MOBA_BLOCK
    q0 = pl.program_id(2) * tq
    own = q0 // tk
    stack = lambda ref, h: jnp.concatenate([ref[0, GROUP * h + j] for j in range(GROUP)], axis=0)
    m = GROUP * tq
    _init_softmax(m_ref, l_ref, acc_ref)
    row = _iota((nb, m), 0)
    qs = []
    for h in range(nh):
        sel = _topk_select(_dot_nt_f32(km_ref[0, h], stack(q32_ref, h)), own, MOBA_TOPK, 0)
        bias = jnp.where((sel > 0.5) | (row == own), 0.0, NEG)
        bias = jnp.concatenate([jnp.zeros((HD, m), F32), bias, jnp.zeros((HD - nb, m), F32)], axis=0)
        qs.append(stack(q_ref, h) + bias.T.astype(BF16))
    qpos = q0 + _iota((1, m), 1) % tq

    def tile(j, valid):
        ks = pl.multiple_of(j * tk, tk)
        s = [_dot_nt(k_ref[0, h, pl.ds(ks, tk), :], qs[h]) for h in range(nh)]
        for h in range(nh):
            _online_t(s[h], valid, vt_ref[0, j, HD * h:HD * (h + 1), :], m_ref, l_ref, acc_ref, h)

    tile(own, (own * tk + _iota((tk, 1), 0)) <= qpos)

    def body(j, carry):
        tile(j, None)
        return carry

    lax.fori_loop(0, own, body, 0)
    for h in range(nh):
        o_ref[0, :, KV_W * h:KV_W * (h + 1)] = _group_out_t(acc_ref[h], l_ref[h], tq)


def _moba_prompt(q_aug, q32hm, kmean, k_aug, vt):
    B, _, T, _ = q_aug.shape
    tq, nh = 256, KV_PER_STEP
    nb = T // MOBA_BLOCK
    assert nb <= HD
    m = GROUP * tq
    km = kmean.reshape(B, nb, N_KV, HD).transpose(0, 2, 1, 3)
    return pl.pallas_call(
        functools.partial(_moba_kernel, tq=tq, nb=nb, nh=nh), grid=(B, N_KV // nh, T // tq),
        in_specs=[pl.BlockSpec((1, nh * GROUP, tq, 2 * HD), lambda b, g, i: (b, g, i, 0)),
                  pl.BlockSpec((1, nh * GROUP, tq, HD), lambda b, g, i: (b, g, i, 0)),
                  pl.BlockSpec((1, nh, nb, HD), lambda b, g, i: (b, g, 0, 0)),
                  pl.BlockSpec((1, nh, T, 2 * HD), lambda b, g, i: (b, g, 0, 0)),
                  pl.BlockSpec((1, nb, nh * HD, MOBA_BLOCK), lambda b, g, i: (b, 0, g, 0))],
        out_specs=pl.BlockSpec((1, tq, nh * KV_W), lambda b, g, i: (b, i, g)),
        out_shape=jax.ShapeDtypeStruct((B, T, D_MODEL), BF16),
        scratch_shapes=_softmax_scratch_t(nh, m),
        compiler_params=_cp("parallel", "parallel", "arbitrary"), name="moba_p")(q_aug, q32hm, km, k_aug, vt)


def _dsa_kernel(q_ref, qi_ref, wi_ref, ki_ref, k_ref, vt_ref, o_ref, key_ref, m_ref, l_ref, acc_ref, *, tq, n_keep):
    tk = KV_TILE
    q0 = pl.program_id(1) * tq
    n_proc = (q0 + tq + tk - 1) // tk
    qpos = q0 + _iota((1, tq), 1)

    def visible(c):
        return (c * tk + _iota((tk, 1), 0)) <= qpos

    qi3 = jnp.concatenate([qi_ref[0, i] for i in range(IDX_HEADS)], axis=0)

    def score_tile(c, carry):
        ks = pl.multiple_of(c * tk, tk)
        dots = _dot_nt(ki_ref[0, pl.ds(ks, tk), :], qi3)
        sc = jnp.zeros((tk, tq), F32)
        for i in range(IDX_HEADS):
            sc = sc + wi_ref[0, i:i + 1, :] * jnp.maximum(dots[:, i * tq:(i + 1) * tq], 0.0)
        key_ref[c] = _order_key(jnp.where(visible(c), sc + 0.0, -jnp.inf))
        return carry

    lax.fori_loop(0, n_proc, score_tile, 0)

    def count(pred, cand):
        out = []
        for c0 in range(0, tq, Q_CHUNK):
            cols = slice(c0, c0 + Q_CHUNK)

            def body(c, part):
                return part + jnp.where(pred(key_ref[c, :, cols], cand[:, cols]), 1.0, 0.0)

            out.append(jnp.sum(lax.fori_loop(0, n_proc, body, jnp.zeros((tk, Q_CHUNK), F32)), axis=0, keepdims=True))
        return jnp.concatenate(out, axis=1)

    thr = _kth_largest_key(lambda cand: count(lambda k, v: k >= v, cand), (1, tq), n_keep)
    need = n_keep - count(lambda k, v: k > v, thr)
    _init_softmax(m_ref, l_ref, acc_ref)
    before = _tri(tk, "gt")
    qs = [jnp.concatenate([q_ref[0, GROUP * g + j] for j in range(GROUP)], axis=0) for g in range(N_KV)]

    def attend(c, ties_seen):
        ks = pl.multiple_of(c * tk, tk)
        key = key_ref[c]
        tied = jnp.where(key == thr, 1.0, 0.0)
        rank = _dot(before, tied.astype(BF16)) + ties_seen
        keep = (key > thr) | ((key == thr) & (rank < need))
        bias = jnp.where(keep & visible(c), 0.0, NEG)
        bias = jnp.concatenate([bias] * GROUP, axis=1)
        s = [_dot_nt(k_ref[0, g, pl.ds(ks, tk), :], qs[g]) for g in range(N_KV)]
        for g in range(N_KV):
            _online_t(s[g], None, vt_ref[0, c, HD * g:HD * (g + 1), :], m_ref, l_ref, acc_ref, g, bias=bias)
        return ties_seen + jnp.sum(tied, axis=0, keepdims=True)

    lax.fori_loop(0, n_proc, attend, jnp.zeros((1, tq), F32))
    for g in range(N_KV):
        o_ref[0, :, KV_W * g:KV_W * (g + 1)] = _group_out_t(acc_ref[g], l_ref[g], tq)


def _dsa_prompt(qhm, qi3, wi, ki3, khm, vt):
    B, _, T, _ = qhm.shape
    tq = 128
    m = GROUP * tq
    return pl.pallas_call(
        functools.partial(_dsa_kernel, tq=tq, n_keep=min(IDX_TOPK, T // 4)), grid=(B, T // tq),
        in_specs=[pl.BlockSpec((1, N_HEADS, tq, HD), lambda b, i: (b, 0, i, 0)),
                  pl.BlockSpec((1, IDX_HEADS, tq, 4 * IDX_DIM), lambda b, i: (b, 0, i, 0)),
                  pl.BlockSpec((1, IDX_HEADS, tq), lambda b, i: (b, 0, i)),
                  pl.BlockSpec((1, T, 4 * IDX_DIM), lambda b, i: (b, 0, 0)),
                  pl.BlockSpec((1, N_KV, T, HD), lambda b, i: (b, 0, 0, 0)),
                  pl.BlockSpec((1, T // KV_TILE, KV_W, KV_TILE), lambda b, i: (b, 0, 0, 0))],
        out_specs=pl.BlockSpec((1, tq, D_MODEL), lambda b, i: (b, i, 0)),
        out_shape=jax.ShapeDtypeStruct((B, T, D_MODEL), BF16),
        scratch_shapes=[pltpu.VMEM((T // KV_TILE, KV_TILE, tq), I32)] + _softmax_scratch_t(N_KV, m),
        compiler_params=_cp("parallel", "arbitrary"), name="dsa_p")(qhm, qi3, wi.transpose(0, 2, 1), ki3, khm, vt)


def _page_specs(layer, n_pages, nps, reverse):
    def spec(i):
        def idx(b, s, pt):
            p = s * nps + i
            return (layer, pt[b, n_pages - 1 - p if reverse else p], 0, 0)

        return pl.BlockSpec((None, None, KV_W, PAGE), idx)

    return [spec(i) for i in range(nps)]


def _pool_specs(rows, n_pages, nps, reverse):
    def spec(i):
        def idx(b, s, pt):
            p = s * nps + i
            return (pt[b, n_pages - 1 - p if reverse else p], 0, 0)

        return pl.BlockSpec((None, rows, PAGE), idx)

    return [spec(i) for i in range(nps)]


def _row_spec(width):
    return pl.BlockSpec((1, 1, width), lambda b, s, pt: (b, 0, 0))


def _head_spec(width):
    return pl.BlockSpec((1, N_HEADS, width), lambda b, s, pt: (b, 0, 0))


def _softmax_scratch():
    return [pltpu.VMEM((N_HEADS, 1), F32), pltpu.VMEM((N_HEADS, 1), F32), pltpu.VMEM((N_HEADS, KV_W), F32)]


def _new_key_start(q, knew, vnew, m_ref, l_ref, acc_ref):
    z = jnp.sum(q.astype(F32) * knew.astype(BF16).astype(F32), axis=1, keepdims=True)
    m_ref[...] = z
    l_ref[...] = jnp.ones_like(z)
    acc_ref[...] = jnp.broadcast_to(vnew.astype(BF16).astype(F32), acc_ref.shape)


def _cat_pages(refs):
    return jnp.concatenate([r[...].astype(BF16) for r in refs], axis=1)


def _later_sums(x, parts):
    n = x.shape[1] // PAGE
    stack = jnp.concatenate([p[:, PAGE * i:PAGE * (i + 1)] for p in parts(x) for i in range(n)], axis=0)
    out = _dot(stack, _tri(PAGE, "gt"))
    rows = x.shape[0]
    tot = None
    for k in range(out.shape[0] // (n * rows)):
        part = jnp.concatenate([out[(k * n + i) * rows:(k * n + i + 1) * rows] for i in range(n)], axis=1)
        tot = part if tot is None else tot + part
    return tot


def _page_totals(x):
    return [jnp.sum(x[:, PAGE * i:PAGE * (i + 1)], axis=1, keepdims=True) for i in range(x.shape[1] // PAGE)]


def _per_page(cols):
    return jnp.concatenate([jnp.broadcast_to(c, (c.shape[0], PAGE)) for c in cols], axis=1)


def _online_pages(s, valid, vcat, m_ref, l_ref, acc_ref):
    if valid is not None:
        s = jnp.where(valid, s, NEG)
    m_old = m_ref[...]
    m_new = jnp.maximum(m_old, jnp.max(s, axis=1, keepdims=True))
    alpha = jnp.exp2(m_old - m_new)
    p = jnp.exp2(s - m_new)
    if valid is not None:
        p = jnp.where(valid, p, 0.0)
    l_ref[...] = alpha * l_ref[...] + jnp.sum(p, axis=1, keepdims=True)
    acc_ref[...] = alpha * acc_ref[...] + _dot_nt(p.astype(BF16), vcat)
    m_ref[...] = m_new


def _dec_stick_kernel(pt_ref, q_ref, *refs, nps):
    k_refs, v_refs = refs[:nps], refs[nps:2 * nps]
    o_ref, r_ref, acc_ref = refs[2 * nps:]
    step = pl.program_id(1)

    @pl.when(step == 0)
    def _():
        r_ref[...] = jnp.zeros_like(r_ref)
        acc_ref[...] = jnp.zeros_like(acc_ref)

    ls, lk = _log_sigmoid_pair(_dot(q_ref[0], _cat_pages(k_refs)))
    after, run = [], r_ref[...]
    for tot in _page_totals(lk):
        after.append(run)
        run = run + tot
    w = jnp.exp(ls + _later_sums(lk, _split2) + _per_page(after))
    acc = acc_ref[...] + _dot_nt(w.astype(BF16), _cat_pages(v_refs))
    r_ref[...], acc_ref[...] = run, acc

    @pl.when(step == pl.num_programs(1) - 1)
    def _():
        o_ref[0] = acc


def _dec_call(kernel, name, grid_spec, DB):
    return pl.pallas_call(kernel, grid_spec=grid_spec, out_shape=jax.ShapeDtypeStruct((DB, N_HEADS, KV_W), F32),
                          compiler_params=_cp("parallel", "arbitrary"), name=name)


def _dec_stick(qx, kt, vt, page_table, layer):
    DB, n_pages = page_table.shape
    nps = math.gcd(n_pages, DEC_PAGES)
    pages = _page_specs(layer, n_pages, nps, True)
    grid_spec = pltpu.PrefetchScalarGridSpec(
        num_scalar_prefetch=1, grid=(DB, n_pages // nps), in_specs=[_head_spec(KV_W)] + pages + pages,
        out_specs=_head_spec(KV_W), scratch_shapes=[pltpu.VMEM((N_HEADS, 1), F32), pltpu.VMEM((N_HEADS, KV_W), F32)])
    return _dec_call(functools.partial(_dec_stick_kernel, nps=nps), "stick_s", grid_spec, DB)(
        page_table, qx, *([kt] * nps), *([vt] * nps))


def _dec_fox_kernel(pt_ref, q_ref, *refs, nps):
    k_refs, v_refs, lf_refs = refs[:nps], refs[nps:2 * nps], refs[2 * nps:3 * nps]
    kn_ref, vn_ref, lfn_ref, o_ref, s_ref, m_ref, l_ref, acc_ref = refs[3 * nps:]
    step = pl.program_id(1)

    @pl.when(step == 0)
    def _():
        s_ref[...] = lfn_ref[0]
        _new_key_start(q_ref[0], kn_ref[0], vn_ref[0], m_ref, l_ref, acc_ref)

    lf = jnp.concatenate([r[...] for r in lf_refs], axis=1)
    after, run = [], s_ref[...]
    for tot in _page_totals(lf):
        after.append(run)
        run = run + tot
    s_ref[...] = run
    s = _dot(q_ref[0], _cat_pages(k_refs)) + (_later_sums(lf, _split3) + _per_page(after)) * LOG2E
    _online_pages(s, None, _cat_pages(v_refs), m_ref, l_ref, acc_ref)

    @pl.when(step == pl.num_programs(1) - 1)
    def _():
        o_ref[0] = acc_ref[...] / l_ref[...]


def _dec_fox(qx, kt, vt, logf_t, knew, vnew, lfnew, page_table, layer):
    DB, n_pages = page_table.shape
    nps = math.gcd(n_pages, DEC_PAGES)
    pages = _page_specs(layer, n_pages, nps, True)
    grid_spec = pltpu.PrefetchScalarGridSpec(
        num_scalar_prefetch=1, grid=(DB, n_pages // nps),
        in_specs=[_head_spec(KV_W)] + pages + pages + _pool_specs(N_HEADS, n_pages, nps, True)
        + [_row_spec(KV_W), _row_spec(KV_W), _head_spec(1)],
        out_specs=_head_spec(KV_W), scratch_shapes=[pltpu.VMEM((N_HEADS, 1), F32)] + _softmax_scratch())
    return _dec_call(functools.partial(_dec_fox_kernel, nps=nps), "fox_s", grid_spec, DB)(
        page_table, qx, *([kt] * nps), *([vt] * nps), *([logf_t] * nps), knew, vnew, lfnew)


def _dec_kmean_kernel(pt_ref, *refs, nps):
    k_refs, o_ref = refs[:nps], refs[nps]
    step = pl.program_id(1)

    @pl.when(step == 0)
    def _():
        o_ref[...] = jnp.zeros_like(o_ref)

    ones = jnp.ones((8, PAGE), BF16)
    for i in range(nps):
        hi, mid, lo = _split3(k_refs[i][...])
        tot = _dot_nt(ones, hi) + _dot_nt(ones, mid) + _dot_nt(ones, lo)
        blk = (step * nps + i) // (MOBA_BLOCK // PAGE)
        o_ref[0, pl.ds(blk, 1), :] = o_ref[0, pl.ds(blk, 1), :] + tot[0:1, :] * (1.0 / MOBA_BLOCK)


def _dec_kmean(kt, page_table, layer):
    DB, n_pages = page_table.shape
    nps = math.gcd(n_pages, DEC_PAGES)
    nb = n_pages * PAGE // MOBA_BLOCK
    grid_spec = pltpu.PrefetchScalarGridSpec(
        num_scalar_prefetch=1, grid=(DB, n_pages // nps), in_specs=_page_specs(layer, n_pages, nps, False),
        out_specs=pl.BlockSpec((1, nb, KV_W), lambda b, s, pt: (b, 0, 0)))
    return pl.pallas_call(
        functools.partial(_dec_kmean_kernel, nps=nps), grid_spec=grid_spec,
        out_shape=jax.ShapeDtypeStruct((DB, nb, KV_W), F32),
        compiler_params=_cp("parallel", "arbitrary"), name="kmean_s")(page_table, *([kt] * nps))


def _dec_moba_kernel(pt_ref, q_ref, q32_ref, km_ref, *refs, nps, nb):
    k_refs, v_refs = refs[:nps], refs[nps:2 * nps]
    kn_ref, vn_ref, o_ref, sel_ref, m_ref, l_ref, acc_ref = refs[2 * nps:]
    step = pl.program_id(1)

    @pl.when(step == 0)
    def _():
        sel_ref[...] = _topk_select(_dot_nt_f32(q32_ref[0], km_ref[0]), nb, MOBA_TOPK, 1)
        _new_key_start(q_ref[0], kn_ref[0], vn_ref[0], m_ref, l_ref, acc_ref)

    lane = _iota((N_HEADS, nb), 1)
    sel = sel_ref[...]
    bias = []
    for i in range(nps):
        blk = (step * nps + i) // (MOBA_BLOCK // PAGE)
        picked = jnp.sum(jnp.where(lane == blk, sel, 0.0), axis=1, keepdims=True) > 0.5
        bias.append(jnp.where(picked, 0.0, NEG))
    s = _dot(q_ref[0], _cat_pages(k_refs)) + _per_page(bias)
    _online_pages(s, None, _cat_pages(v_refs), m_ref, l_ref, acc_ref)

    @pl.when(step == pl.num_programs(1) - 1)
    def _():
        o_ref[0] = acc_ref[...] / l_ref[...]


def _dec_moba(qx, q32x, kmean, kt, vt, knew, vnew, page_table, layer):
    DB, n_pages = page_table.shape
    nps = math.gcd(n_pages, DEC_PAGES)
    nb = kmean.shape[1]
    pages = _page_specs(layer, n_pages, nps, False)
    grid_spec = pltpu.PrefetchScalarGridSpec(
        num_scalar_prefetch=1, grid=(DB, n_pages // nps),
        in_specs=[_head_spec(KV_W), _head_spec(KV_W), pl.BlockSpec((1, nb, KV_W), lambda b, s, pt: (b, 0, 0))]
        + pages + pages + [_row_spec(KV_W), _row_spec(KV_W)],
        out_specs=_head_spec(KV_W), scratch_shapes=[pltpu.VMEM((N_HEADS, nb), F32)] + _softmax_scratch())
    return _dec_call(functools.partial(_dec_moba_kernel, nps=nps, nb=nb), "moba_s", grid_spec, DB)(
        page_table, qx, q32x, kmean, *([kt] * nps), *([vt] * nps), knew, vnew)


def _dec_score_kernel(pt_ref, qi_ref, wi_ref, *refs, nps, n_pages):
    kidx_refs = refs[:nps]
    kin_ref, o_ref = refs[nps:]
    step = pl.program_id(1)
    qi, wi = qi_ref[0], wi_ref[0]

    @pl.when(step == 0)
    def _():
        o_ref[...] = jnp.full(o_ref.shape, -jnp.inf, F32)
        dn = jnp.sum(qi * kin_ref[0], axis=1, keepdims=True)
        sn = jnp.sum(wi * jnp.maximum(dn, 0.0), axis=0, keepdims=True) + 0.0
        o_ref[0, n_pages:n_pages + 1, :] = jnp.where(_iota((1, PAGE), 1) == 0, sn, -jnp.inf)

    for i in range(nps):
        dots = _dot_f32(qi, kidx_refs[i][...])
        o_ref[0, pl.ds(step * nps + i, 1), :] = jnp.sum(wi * jnp.maximum(dots, 0.0), axis=0, keepdims=True) + 0.0


def _dec_scores(qi3, wi3, kidx_t, kinew, page_table):
    DB, n_pages = page_table.shape
    assert n_pages < PAGE
    nps = math.gcd(n_pages, DEC_PAGES)
    grid_spec = pltpu.PrefetchScalarGridSpec(
        num_scalar_prefetch=1, grid=(DB, n_pages // nps),
        in_specs=[pl.BlockSpec((1, IDX_HEADS, IDX_DIM), lambda b, s, pt: (b, 0, 0)),
                  pl.BlockSpec((1, IDX_HEADS, 1), lambda b, s, pt: (b, 0, 0))]
        + _pool_specs(IDX_DIM, n_pages, nps, False) + [_row_spec(IDX_DIM)],
        out_specs=pl.BlockSpec((1, PAGE, PAGE), lambda b, s, pt: (b, 0, 0)))
    return pl.pallas_call(
        functools.partial(_dec_score_kernel, nps=nps, n_pages=n_pages), grid_spec=grid_spec,
        out_shape=jax.ShapeDtypeStruct((DB, PAGE, PAGE), F32),
        compiler_params=_cp("parallel", "arbitrary"), name="dsa_score_s")(page_table, qi3, wi3, *([kidx_t] * nps), kinew)


def _dec_select_kernel(sc_ref, o_ref, *, n_keys, n_keep):
    key = _order_key(sc_ref[0])

    def total(x):
        return jnp.sum(jnp.sum(x, axis=1, keepdims=True), axis=0, keepdims=True)

    thr = _kth_largest_key(lambda cand: total(jnp.where(key >= cand, 1.0, 0.0)), (1, 1), n_keep)
    need = n_keep - total(jnp.where(key > thr, 1.0, 0.0))
    tied = jnp.where(key == thr, 1.0, 0.0)
    in_row = _dot(tied.astype(BF16), _tri(PAGE, "lt"))
    row_tot = jnp.broadcast_to(jnp.sum(tied, axis=1, keepdims=True), tied.shape).astype(BF16)
    rank = in_row + _dot(_tri(PAGE, "gt"), row_tot)
    pos = _iota(key.shape, 0) * PAGE + _iota(key.shape, 1)
    keep = ((key > thr) | ((key == thr) & (rank < need))) & (pos < n_keys)
    o_ref[0] = jnp.where(keep, 1.0, 0.0)


def _dec_select(scores, n_keys):
    DB = scores.shape[0]
    spec = pl.BlockSpec((1, PAGE, PAGE), lambda b: (b, 0, 0))
    return pl.pallas_call(
        functools.partial(_dec_select_kernel, n_keys=n_keys, n_keep=min(IDX_TOPK, n_keys // 4)), grid=(DB,),
        in_specs=[spec], out_specs=spec, out_shape=jax.ShapeDtypeStruct(scores.shape, F32),
        compiler_params=_cp("parallel"), name="dsa_select_s")(scores)


def _dec_dsa_kernel(pt_ref, q_ref, *refs, nps, n_pages):
    k_refs, v_refs = refs[:nps], refs[nps:2 * nps]
    sel_ref, kn_ref, vn_ref, o_ref, m_ref, l_ref, acc_ref = refs[2 * nps:]
    step = pl.program_id(1)
    q = q_ref[0]

    @pl.when(step == 0)
    def _():
        zn = jnp.sum(q.astype(F32) * kn_ref[0].astype(BF16).astype(F32), axis=1, keepdims=True)
        keep = jnp.broadcast_to(sel_ref[0, n_pages:n_pages + 1, 0:1] > 0.5, zn.shape)
        m_ref[...] = jnp.where(keep, zn, NEG)
        l_ref[...] = jnp.where(keep, 1.0, 0.0)
        acc_ref[...] = jnp.where(keep, jnp.broadcast_to(vn_ref[0].astype(BF16).astype(F32), acc_ref.shape), 0.0)

    keep = jnp.concatenate([sel_ref[0, pl.ds(step * nps + i, 1), :] for i in range(nps)], axis=1)
    s = _dot(q, _cat_pages(k_refs))
    _online_pages(s, jnp.broadcast_to(keep > 0.5, s.shape), _cat_pages(v_refs), m_ref, l_ref, acc_ref)

    @pl.when(step == pl.num_programs(1) - 1)
    def _():
        o_ref[0] = acc_ref[...] / l_ref[...]


def _dec_dsa(qx, kt, vt, sel, knew, vnew, page_table, layer):
    DB, n_pages = page_table.shape
    nps = math.gcd(n_pages, DEC_PAGES)
    pages = _page_specs(layer, n_pages, nps, False)
    grid_spec = pltpu.PrefetchScalarGridSpec(
        num_scalar_prefetch=1, grid=(DB, n_pages // nps),
        in_specs=[_head_spec(KV_W)] + pages + pages
        + [pl.BlockSpec((1, PAGE, PAGE), lambda b, s, pt: (b, 0, 0)), _row_spec(KV_W), _row_spec(KV_W)],
        out_specs=_head_spec(KV_W), scratch_shapes=_softmax_scratch())
    return _dec_call(functools.partial(_dec_dsa_kernel, nps=nps, n_pages=n_pages), "dsa_s", grid_spec, DB)(
        page_table, qx, *([kt] * nps), *([vt] * nps), sel, knew, vnew)


def _rope_tables(pos):
    half = HD // 2
    inv_freq = jnp.power(ROPE_THETA, -jnp.arange(half, dtype=F32) * (2.0 / HD))
    ang = pos.astype(F32)[:, None] * inv_freq[None, :]
    cos, sin = jnp.cos(ang), jnp.sin(ang)
    return jnp.concatenate([cos] * 4, axis=1), jnp.concatenate([-sin, sin] * 2, axis=1)


def _pad_cols(w, mult=128):
    pad = (-w.shape[1]) % mult
    return jnp.pad(w, ((0, 0), (0, pad))) if pad else w


def _layer_weights(kind, w_qkv, fox_w_f, idx_w_q, idx_w_k, idx_w_w):
    cols = [w_qkv]
    if kind == FOX:
        cols.append(_pad_cols(fox_w_f))
    if kind == DSA:
        cols += [idx_w_q, _pad_cols(idx_w_k), _pad_cols(idx_w_w)]
    return jnp.concatenate(cols, axis=1).astype(BF16)


_HEAD_ONEHOT = np.arange(N_HEADS)[:, None] // GROUP == np.arange(N_KV)[None, :]


def _expand_heads(q, dtype):
    m = jnp.asarray(_HEAD_ONEHOT, q.dtype)
    return (q[:, :, None, :] * m[None, :, :, None]).reshape(q.shape[0], N_HEADS, KV_W).astype(dtype)


def _collapse_heads(ox):
    m = jnp.asarray(_HEAD_ONEHOT, ox.dtype)
    o = jnp.sum(ox.reshape(ox.shape[0], N_HEADS, N_KV, HD) * m[None, :, :, None], axis=2)
    return o.reshape(ox.shape[0], D_MODEL).astype(BF16)


def _pages_t(cache):
    depth, n_pool = cache.shape[:2]
    return cache.transpose(0, 1, 3, 4, 2).reshape(depth, n_pool, KV_W, PAGE)


def kernel(x_prompt, x_sample, cache_k, cache_v, cache_logf, cache_kidx, state_conv, page_table, norm_mix, norm_ffn, norm_final, w_qkv, w_o, fox_w_f, fox_b_f, idx_w_q, idx_w_k, idx_w_w, ffn_w_a, ffn_w_b, ffn_conv_w, ffn_conv_b, ffn_w_down):
    B, T, _ = x_prompt.shape
    DB = x_sample.shape[0]
    depth = cache_k.shape[0]
    n_pages = page_table.shape[1]
    past = n_pages * PAGE
    assert T % MOBA_BLOCK == 0 and past % MOBA_BLOCK == 0 and x_sample.shape[1] == 1
    cache_kt, cache_vt = _pages_t(cache_k), _pages_t(cache_v)
    rope_p = _rope_tables(jnp.arange(T, dtype=I32))
    rope_s = _rope_tables(jnp.full((DB,), past, I32))

    xp = x_prompt
    xs = x_sample.reshape(1, DB, D_MODEL)
    new_k_p, new_v_p, new_k_s, new_v_s, conv_p, conv_s = [], [], [], [], [], []
    logf_p = logf_s = kidx_p = kidx_s = None
    for i in range(depth):
        kind = i % 4
        rope = kind in (MOBA, DSA)
        w_all = _layer_weights(kind, w_qkv[i], fox_w_f, idx_w_q, idx_w_k, idx_w_w)
        wo_b = w_o[i].astype(BF16)
        wa_b, wb_b, wd_b = ffn_w_a[i].astype(BF16), ffn_w_b[i].astype(BF16), ffn_w_down[i].astype(BF16)

        outs = _proj(xp, norm_mix[i], w_all, kind, rope_p if rope else None, fox_b_f, True)
        qhm, k32, v32, khm, vhm = outs[:5]
        vt = outs[-1]
        new_k_p.append(k32.reshape(B, T, N_KV, HD))
        new_v_p.append(v32.reshape(B, T, N_KV, HD))
        if kind == STICK:
            o = _stick_prompt(qhm, khm, vt)
        elif kind == FOX:
            logf_p = outs[5]
            o = _fox_prompt(qhm, khm, vt, _cumsum_time_parts(logf_p))
        elif kind == MOBA:
            o = _moba_prompt(qhm, outs[5], _block_mean(k32), khm, vt)
        else:
            kidx_p, wi, qi3, ki3 = outs[6:10]
            o = _dsa_prompt(qhm, qi3, wi, ki3, khm, vt)
        x2, rows = _mix_ffn(xp.reshape(B * T, D_MODEL), o.reshape(B * T, D_MODEL), wo_b, norm_ffn[i], wa_b, wb_b,
                            ffn_conv_w[i], ffn_conv_b[i], wd_b, T)
        conv_p.append(rows)
        xp = x2.reshape(B, T, D_MODEL)

        outs = _proj(xs, norm_mix[i], w_all, kind, rope_s if rope else None, fox_b_f, False)
        qhm, k32, v32 = outs[:3]
        new_k_s.append(k32.reshape(DB, 1, N_KV, HD))
        new_v_s.append(v32.reshape(DB, 1, N_KV, HD))
        qx = _expand_heads(qhm[0].transpose(1, 0, 2), BF16)
        knew, vnew = k32.reshape(DB, 1, KV_W), v32.reshape(DB, 1, KV_W)
        if kind == STICK:
            ox = _dec_stick(qx, cache_kt, cache_vt, page_table, i)
        elif kind == FOX:
            lf = outs[5][0]
            logf_s = lf.reshape(DB, 1, N_HEADS)
            ox = _dec_fox(qx, cache_kt, cache_vt, cache_logf.transpose(0, 2, 1), knew, vnew,
                          lf.reshape(DB, N_HEADS, 1), page_table, i)
        elif kind == MOBA:
            q32x = _expand_heads(outs[5][0].transpose(1, 0, 2), F32)
            ox = _dec_moba(qx, q32x, _dec_kmean(cache_kt, page_table, i), cache_kt, cache_vt, knew, vnew, page_table, i)
        else:
            qihm, ki, wi = outs[5:8]
            kidx_s = ki.reshape(DB, 1, IDX_DIM)
            scores = _dec_scores(qihm[0].transpose(1, 0, 2), wi[0].reshape(DB, IDX_HEADS, 1),
                                 cache_kidx.transpose(0, 2, 1), kidx_s, page_table)
            ox = _dec_dsa(qx, cache_kt, cache_vt, _dec_select(scores, past + 1), knew, vnew, page_table, i)
        st = state_conv[i]
        s2, a_new = _mix_ffn(xs.reshape(DB, D_MODEL), _collapse_heads(ox), wo_b, norm_ffn[i], wa_b, wb_b,
                             ffn_conv_w[i], ffn_conv_b[i], wd_b, 1, prev=(st[:, 0], st[:, 1]))
        conv_s.append(jnp.stack([st[:, 1], a_new], axis=1))
        xs = s2.reshape(1, DB, D_MODEL)

    y_p = _final_norm(xp.reshape(B * T, D_MODEL), norm_final).reshape(B, T, D_MODEL)
    y_s = _final_norm(xs.reshape(DB, D_MODEL), norm_final).reshape(DB, 1, D_MODEL)
    return (y_p, y_s, jnp.stack(new_k_p), jnp.stack(new_v_p), jnp.stack(new_k_s), jnp.stack(new_v_s),
            logf_p, logf_s, kidx_p, kidx_s, jnp.stack(conv_p), jnp.stack(conv_s))
```

```python
import functools
import math

import jax
import jax.numpy as jnp
import numpy as np
from jax import lax
from jax.experimental import pallas as pl
from jax.experimental.pallas import tpu as pltpu

F32, BF16, I32 = jnp.float32, jnp.bfloat16, jnp.int32

D_MODEL = 1024
N_HEADS = 16
N_KV = 4
GROUP = N_HEADS // N_KV
HD = 64
KV_W = N_KV * HD
D_FF = 2816
CONV_W = 3
PAGE = 128
MOBA_BLOCK = 256
MOBA_TOPK = 3
IDX_HEADS = 8
IDX_DIM = 64
IDX_TOPK = 256
ROPE_THETA = 10000.0
RMS_EPS = 1e-6
STICK, FOX, MOBA, DSA = 0, 1, 2, 3
NEG = -1e30
M_FLOOR = -1e29
LOG2E = 1.4426950408889634
DEAD_LOG = -104.0
INT_MIN = -2147483648
VMEM_LIMIT = 48 * 1024 * 1024
FF_CHUNK = 256
FFN_TM = 512
PROJ_TM = 256
KV_TILE = 256
DEC_PAGES = 64
KV_PER_STEP = 4
Q_CHUNK = 128
FOX_AUG = 3 * GROUP


def _cp(*sem):
    return pltpu.CompilerParams(dimension_semantics=sem, vmem_limit_bytes=VMEM_LIMIT)


def _dot(a, b):
    return jnp.dot(a, b, preferred_element_type=F32)


def _dot_nt(a, b):
    return lax.dot_general(a, b, (((1,), (1,)), ((), ())), preferred_element_type=F32)


def _split2(x):
    hi = x.astype(BF16)
    lo = (x - hi.astype(F32)).astype(BF16)
    return hi, lo


def _split3(x):
    hi = x.astype(BF16)
    r = x - hi.astype(F32)
    mid = r.astype(BF16)
    lo = (r - mid.astype(F32)).astype(BF16)
    return hi, mid, lo


def _dot_f32(a, b):
    ah, al = _split2(a)
    bh, bl = _split2(b)
    return _dot(ah, bh) + _dot(al, bh) + _dot(ah, bl)


def _dot_nt_f32(a, b):
    ah, al = _split2(a)
    bh, bl = _split2(b)
    return _dot_nt(ah, bh) + _dot_nt(al, bh) + _dot_nt(ah, bl)


def _log_sigmoid_pair(z):
    l1p = jnp.log(1.0 + jnp.exp(-jnp.abs(z)))
    return jnp.minimum(z, 0.0) - l1p, jnp.minimum(-z, 0.0) - l1p


def _iota(shape, dim):
    return lax.broadcasted_iota(I32, shape, dim)


def _tri(n, kind):
    r, c = _iota((n, n), 0), _iota((n, n), 1)
    m = {"gt": r > c, "lt": r < c, "ge": r >= c}[kind]
    return jnp.where(m, 1.0, 0.0).astype(BF16)


def _rms_bf16(x, g):
    y = x * lax.rsqrt(jnp.mean(x * x, axis=-1, keepdims=True) + RMS_EPS)
    return (y * g).astype(BF16)


def _rope128(xc, cos, sin):
    lane = _iota(xc.shape, 1)
    first = (lane % HD) < (HD // 2)
    rolled = jnp.where(first, pltpu.roll(xc, 128 - HD // 2, 1), pltpu.roll(xc, HD // 2, 1))
    return xc * cos + rolled * sin


def _halves(xc):
    return xc[:, :HD], pltpu.roll(xc, HD, 1)[:, :HD]


def _place(parts):
    r, c = _iota((HD, KV_W), 0), _iota((HD, KV_W), 1)
    out = None
    for j, pj in enumerate(parts):
        e = jnp.where(c == r + HD * j, 1.0, 0.0).astype(BF16)
        t = _dot(pj, e)
        out = t if out is None else out + t
    return out


def _stack_heads(q_ref):
    return jnp.concatenate([q_ref[0, j] for j in range(GROUP)], axis=0)


def _init_softmax(m_ref, l_ref, acc_ref):
    m_ref[...] = jnp.full(m_ref.shape, M_FLOOR, F32)
    l_ref[...] = jnp.zeros_like(l_ref)
    acc_ref[...] = jnp.zeros_like(acc_ref)


def _online_t(s, valid, vt, m_ref, l_ref, acc_ref, idx, bias=None):
    m_all, l_all, acc_all = m_ref[idx], l_ref[idx], acc_ref[idx]
    m_out, l_out, acc_out = [], [], []
    for c0 in range(0, s.shape[1], Q_CHUNK):
        cols = slice(c0, c0 + Q_CHUNK)
        sc = s[:, cols]
        if bias is not None:
            sc = sc + bias[:, cols]
        if valid is not None:
            vc = valid[:, cols]
            sc = jnp.where(vc, sc, NEG)
        m_old = m_all[:, cols]
        m_new = jnp.maximum(m_old, jnp.max(sc, axis=0, keepdims=True))
        alpha = jnp.exp2(m_old - m_new)
        p = jnp.exp2(sc - m_new)
        if valid is not None:
            p = jnp.where(vc, p, 0.0)
        l_out.append(alpha * l_all[:, cols] + jnp.sum(p, axis=0, keepdims=True))
        acc_out.append(alpha * acc_all[:, cols] + _dot(vt, p.astype(BF16)))
        m_out.append(m_new)
    m_ref[idx] = jnp.concatenate(m_out, axis=1)
    l_ref[idx] = jnp.concatenate(l_out, axis=1)
    acc_ref[idx] = jnp.concatenate(acc_out, axis=1)


def _group_out_t(acc_t, l, tq):
    r = acc_t if l is None else acc_t / l
    o_t = jnp.concatenate([r[:, j * tq:(j + 1) * tq] for j in range(GROUP)], axis=0)
    return o_t.T.astype(BF16)


def _topk_select(gate, n_valid, k, axis):
    pos = _iota(gate.shape, axis)
    posf = pos.astype(F32)
    g = jnp.where(pos < n_valid, gate, -jnp.inf)
    sel = jnp.zeros(gate.shape, F32)
    for _ in range(k):
        best = jnp.max(g, axis=axis, keepdims=True)
        first = jnp.min(jnp.where(g == best, posf, float(gate.shape[axis])), axis=axis, keepdims=True)
        hit = posf == first
        sel = jnp.where(hit, 1.0, sel)
        g = jnp.where(hit, -jnp.inf, g)
    return jnp.where(pos < n_valid, sel, 0.0)


def _order_key(score):
    bits = lax.bitcast_convert_type(score, I32)
    return jnp.where(bits < 0, bits ^ jnp.int32(0x7FFFFFFF), bits)


def _kth_largest_key(count_ge, shape, k):
    def body(it, v):
        cand = v + lax.shift_left(jnp.int32(1), 31 - it)
        return jnp.where(count_ge(cand) >= k, cand, v)

    return lax.fori_loop(0, 32, body, jnp.full(shape, INT_MIN, I32))


def _proj_kernel(*refs, kind, rope, prompt, aug):
    it = iter(refs)
    x_ref, g_ref, w_ref = next(it), next(it), next(it)
    bf_ref = next(it) if kind == FOX else None
    cos_ref, sin_ref = (next(it), next(it)) if rope else (None, None)
    qpat_ref = next(it) if aug else None
    kpat_ref = next(it) if aug == "qk" else None
    qhm_ref, k32_ref, v32_ref, khm_ref, vhm_ref = next(it), next(it), next(it), next(it), next(it)
    low = _iota((x_ref.shape[1], 128), 1) < HD
    h = _rms_bf16(x_ref[0], g_ref[...])
    y = _dot(h, w_ref[...])
    cos = cos_ref[...] if rope else None
    sin = sin_ref[...] if rope else None

    def chunk(c, rot):
        xc = y[:, 128 * c:128 * (c + 1)]
        return _rope128(xc, cos, sin) if rot else xc

    q32hm_ref = next(it) if kind == MOBA else None
    q_scale = HD ** -0.5 * (1.0 if kind == STICK else LOG2E)
    for c in range(N_HEADS // 2):
        qc = chunk(c, rope)
        lo, hi = _halves(qc)
        if aug:
            for j, z in enumerate((qc, pltpu.roll(qc, HD, 1))):
                qhm_ref[0, 2 * c + j] = jnp.where(low, z * q_scale, qpat_ref[2 * c + j:2 * c + j + 1, :]).astype(BF16)
        else:
            qhm_ref[0, 2 * c] = (lo * q_scale).astype(BF16)
            qhm_ref[0, 2 * c + 1] = (hi * q_scale).astype(BF16)
        if kind == MOBA:
            q32hm_ref[0, 2 * c] = lo
            q32hm_ref[0, 2 * c + 1] = hi
    base = D_MODEL // 128
    for c in range(N_KV // 2):
        kc = chunk(base + c, rope)
        k32_ref[0, :, 128 * c:128 * (c + 1)] = kc
        vc = chunk(base + N_KV // 2 + c, False)
        v32_ref[0, :, 128 * c:128 * (c + 1)] = vc
        if aug == "qk":
            for j, z in enumerate((kc, pltpu.roll(kc, HD, 1))):
                khm_ref[0, 2 * c + j] = jnp.where(low, z, kpat_ref[...]).astype(BF16)
        else:
            klo, khi = _halves(kc)
            khm_ref[0, 2 * c] = klo.astype(BF16)
            khm_ref[0, 2 * c + 1] = khi.astype(BF16)
        vlo, vhi = _halves(vc)
        vhm_ref[0, 2 * c] = vlo.astype(BF16)
        vhm_ref[0, 2 * c + 1] = vhi.astype(BF16)
    base = (D_MODEL + 2 * KV_W) // 128
    if kind == FOX:
        logf_ref = next(it)
        zf = chunk(base, False)[:, :N_HEADS] + bf_ref[...]
        logf_ref[0] = _log_sigmoid_pair(zf)[0]
    if kind == DSA:
        qihm_ref, ki_ref, wi_ref = next(it), next(it), next(it)
        for c in range(IDX_HEADS // 2):
            lo, hi = _halves(chunk(base + c, True))
            qihm_ref[0, 2 * c] = lo
            qihm_ref[0, 2 * c + 1] = hi
        kic = chunk(base + IDX_HEADS // 2, True)
        ki_ref[0] = kic[:, :IDX_DIM]
        wi_ref[0] = chunk(base + IDX_HEADS // 2 + 1, False)[:, :IDX_HEADS] * (IDX_HEADS ** -0.5 * IDX_DIM ** -0.5)
        if prompt:
            qi3_ref, ki3_ref = next(it), next(it)
            low = _iota(kic.shape, 1) < IDX_DIM
            for c in range(IDX_HEADS // 2):
                xc = chunk(base + c, True)
                for j, z in enumerate((jnp.where(low, xc, 0.0), jnp.where(low, pltpu.roll(xc, IDX_DIM, 1), 0.0))):
                    hi = z.astype(BF16).astype(F32)
                    qi3_ref[0, 2 * c + j] = jnp.concatenate([hi + pltpu.roll(z - hi, IDX_DIM, 1), hi], axis=1).astype(BF16)
            hi = kic.astype(BF16).astype(F32)
            ki3_ref[0] = jnp.concatenate([hi + pltpu.roll(hi, IDX_DIM, 1), kic - hi], axis=1).astype(BF16)
    if prompt:
        vt_ref = next(it)
        vt_ref[0, 0] = y[:, D_MODEL + KV_W:D_MODEL + 2 * KV_W].T.astype(BF16)


def _score_features(kind, T):
    qpat = np.zeros((N_HEADS, 128), np.float32)
    if kind == FOX:
        for h in range(N_HEADS):
            qpat[h, HD + 3 * (h % GROUP):HD + 3 * (h % GROUP) + 3] = -1.0
        return "q", [jnp.asarray(qpat)]
    kpat = np.zeros((T, 128), np.float32)
    kpat[np.arange(T), HD + np.arange(T) // MOBA_BLOCK] = 1.0
    return "qk", [jnp.asarray(qpat), jnp.asarray(kpat)]


def _proj(x, g, w_all, kind, rope_tabs, fox_b, prompt):
    B, T, _ = x.shape
    tm = min(PROJ_TM, T)
    n_cols = w_all.shape[1]
    rope = rope_tabs is not None
    aug, pats = _score_features(kind, T) if prompt and kind in (FOX, MOBA) else (None, [])
    q_w = 2 * HD if aug else HD
    k_w = 2 * HD if aug == "qk" else HD
    row = lambda b, i: (b, i, 0)
    hm = lambda b, i: (b, 0, i, 0)
    in_specs = [pl.BlockSpec((1, tm, D_MODEL), row), pl.BlockSpec((1, D_MODEL), lambda b, i: (0, 0)),
                pl.BlockSpec((D_MODEL, n_cols), lambda b, i: (0, 0))]
    args = [x, g.reshape(1, D_MODEL), w_all]
    if kind == FOX:
        in_specs.append(pl.BlockSpec((1, N_HEADS), lambda b, i: (0, 0)))
        args.append(fox_b.reshape(1, N_HEADS))
    if rope:
        in_specs += [pl.BlockSpec((tm, 128), lambda b, i: (i, 0))] * 2
        args += list(rope_tabs)
    if aug:
        in_specs.append(pl.BlockSpec((N_HEADS, 128), lambda b, i: (0, 0)))
    if aug == "qk":
        in_specs.append(pl.BlockSpec((tm, 128), lambda b, i: (i, 0)))
    args += pats
    out_shape = [jax.ShapeDtypeStruct((B, N_HEADS, T, q_w), BF16), jax.ShapeDtypeStruct((B, T, KV_W), F32),
                 jax.ShapeDtypeStruct((B, T, KV_W), F32), jax.ShapeDtypeStruct((B, N_KV, T, k_w), BF16),
                 jax.ShapeDtypeStruct((B, N_KV, T, HD), BF16)]
    out_specs = [pl.BlockSpec((1, N_HEADS, tm, q_w), hm), pl.BlockSpec((1, tm, KV_W), row),
                 pl.BlockSpec((1, tm, KV_W), row), pl.BlockSpec((1, N_KV, tm, k_w), hm),
                 pl.BlockSpec((1, N_KV, tm, HD), hm)]
    if kind == MOBA:
        out_shape.append(jax.ShapeDtypeStruct((B, N_HEADS, T, HD), F32))
        out_specs.append(pl.BlockSpec((1, N_HEADS, tm, HD), hm))
    if kind == FOX:
        out_shape.append(jax.ShapeDtypeStruct((B, T, N_HEADS), F32))
        out_specs.append(pl.BlockSpec((1, tm, N_HEADS), row))
    if kind == DSA:
        out_shape += [jax.ShapeDtypeStruct((B, IDX_HEADS, T, IDX_DIM), F32),
                      jax.ShapeDtypeStruct((B, T, IDX_DIM), F32), jax.ShapeDtypeStruct((B, T, IDX_HEADS), F32)]
        out_specs += [pl.BlockSpec((1, IDX_HEADS, tm, IDX_DIM), hm), pl.BlockSpec((1, tm, IDX_DIM), row),
                      pl.BlockSpec((1, tm, IDX_HEADS), row)]
        if prompt:
            out_shape += [jax.ShapeDtypeStruct((B, IDX_HEADS, T, 4 * IDX_DIM), BF16),
                          jax.ShapeDtypeStruct((B, T, 4 * IDX_DIM), BF16)]
            out_specs += [pl.BlockSpec((1, IDX_HEADS, tm, 4 * IDX_DIM), hm), pl.BlockSpec((1, tm, 4 * IDX_DIM), row)]
    if prompt:
        assert tm == KV_TILE
        out_shape.append(jax.ShapeDtypeStruct((B, T // tm, KV_W, tm), BF16))
        out_specs.append(pl.BlockSpec((1, 1, KV_W, tm), lambda b, i: (b, i, 0, 0)))
    return pl.pallas_call(
        functools.partial(_proj_kernel, kind=kind, rope=rope, prompt=prompt, aug=aug),
        grid=(B, T // tm), in_specs=in_specs, out_specs=out_specs, out_shape=out_shape,
        compiler_params=_cp("parallel", "arbitrary"), name=f"proj_{kind}_{'p' if prompt else 's'}")(*args)


def _gelu_tanh(x):
    return 0.5 * x * (1.0 + jnp.tanh(0.7978845608028654 * (x + 0.044715 * x * x * x)))


def _ffn_kernel(*refs, prompt, tiles_per_seq, final):
    if final:
        gf_ref, refs = refs[0], refs[1:]
    if prompt:
        x_ref, o_ref, wo_ref, g_ref, wa_ref, wb_ref, cw_ref, cb_ref, wd_ref, y_ref, conv_ref, u_ref, carry_ref = refs
    else:
        x_ref, o_ref, wo_ref, g_ref, wa_ref, wb_ref, cw_ref, cb_ref, wd_ref, p2_ref, p1_ref, y_ref, a_ref, u_ref = refs
    x = x_ref[...] + _dot(o_ref[...], wo_ref[...])
    h = _rms_bf16(x, g_ref[...])
    if prompt:
        @pl.when(pl.program_id(0) % tiles_per_seq == 0)
        def _():
            carry_ref[...] = jnp.zeros_like(carry_ref)

        row = _iota((x.shape[0], FF_CHUNK), 0)
    for c in range(D_FF // FF_CHUNK):
        cols = slice(c * FF_CHUNK, (c + 1) * FF_CHUNK)
        a = _dot(h, wa_ref[:, cols])
        gate = _dot(h, wb_ref[:, cols])
        if prompt:
            prev2, prev1 = carry_ref[0:1, cols], carry_ref[1:2, cols]
            back2 = pltpu.roll(a, 2, 0)
            a_m1 = jnp.where(row == 0, prev1, pltpu.roll(a, 1, 0))
            a_m2 = jnp.where(row == 0, prev2, jnp.where(row == 1, prev1, back2))
            carry_ref[0:2, cols] = back2[0:2, :]
            conv_ref[0, :, cols] = back2[0:2, :]
        else:
            a_m2, a_m1 = p2_ref[:, cols], p1_ref[:, cols]
            a_ref[:, cols] = a
        conv = cb_ref[:, cols] + cw_ref[0:1, cols] * a_m2 + cw_ref[1:2, cols] * a_m1 + cw_ref[2:3, cols] * a
        u_ref[:, cols] = (_gelu_tanh(conv) * gate).astype(BF16)
    y = x + _dot(u_ref[...], wd_ref[...])
    if final:
        y = y * lax.rsqrt(jnp.mean(y * y, axis=-1, keepdims=True) + RMS_EPS) * gf_ref[...]
    y_ref[...] = y


def _mix_ffn(x2, o2, w_o, g, w_a, w_b, conv_w, conv_b, w_down, seq_len, prev=None, final_g=None):
    M = x2.shape[0]
    prompt = prev is None
    tm = min(FFN_TM, seq_len) if prompt else M
    tps = seq_len // tm if prompt else 1
    xs = pl.BlockSpec((tm, D_MODEL), lambda i: (i, 0))
    whole = lambda shape: pl.BlockSpec(shape, lambda i: (0,) * len(shape), pipeline_mode=pl.Buffered(1))
    in_specs = [xs, xs, whole((D_MODEL, D_MODEL)), whole((1, D_MODEL)), whole((D_MODEL, D_FF)), whole((D_MODEL, D_FF)),
                whole((CONV_W, D_FF)), whole((1, D_FF)), whole((D_FF, D_MODEL))]
    args = [x2, o2, w_o, g.reshape(1, D_MODEL), w_a, w_b, conv_w, conv_b.reshape(1, D_FF), w_down]
    scratch = [pltpu.VMEM((tm, D_FF), BF16)]
    if prompt:
        out_shape = [jax.ShapeDtypeStruct((M, D_MODEL), F32), jax.ShapeDtypeStruct((M // tm, CONV_W - 1, D_FF), F32)]
        out_specs = [xs, pl.BlockSpec((1, CONV_W - 1, D_FF), lambda i: (i, 0, 0))]
        scratch.append(pltpu.VMEM((8, D_FF), F32))
    else:
        ps = pl.BlockSpec((tm, D_FF), lambda i: (i, 0))
        in_specs += [ps, ps]
        args += list(prev)
        out_shape = [jax.ShapeDtypeStruct((M, D_MODEL), F32), jax.ShapeDtypeStruct((M, D_FF), F32)]
        out_specs = [xs, ps]
    if final_g is not None:
        in_specs = [whole((1, D_MODEL))] + in_specs
        args = [final_g.reshape(1, D_MODEL)] + args
    y, extra = pl.pallas_call(
        functools.partial(_ffn_kernel, prompt=prompt, tiles_per_seq=tps, final=final_g is not None), grid=(M // tm,),
        in_specs=in_specs, out_specs=out_specs, out_shape=out_shape, scratch_shapes=scratch,
        compiler_params=_cp("arbitrary"), name="ffn_p" if prompt else "ffn_s")(*args)
    return y, (extra[tps - 1::tps] if prompt else extra)


def _stick_kernel(q_ref, k_ref, vt_ref, o_ref, r_ref, acc_ref, *, tq, tk, nh):
    q0 = pl.program_id(2) * tq
    qs = [jnp.concatenate([q_ref[0, GROUP * h + j] for j in range(GROUP)], axis=0) for h in range(nh)]
    m = GROUP * tq
    later = _tri(tk, "lt")
    qpos = q0 + _iota((1, m), 1) % tq
    r_ref[...] = jnp.zeros_like(r_ref)
    acc_ref[...] = jnp.zeros_like(acc_ref)

    def tile(j, masked):
        ks = pl.multiple_of(j * tk, tk)
        zs = [_dot_nt(k_ref[0, h, pl.ds(ks, tk), :], qs[h]) for h in range(nh)]
        for h in range(nh):
            head_tile(j, h, zs[h], masked)

    def head_tile(j, h, z, masked):
        vt = vt_ref[0, j, HD * h:HD * (h + 1), :]
        kpos = j * tk + _iota((tk, 1), 0)
        r_all, acc_all = r_ref[h], acc_ref[h]
        chunks = [slice(c0, c0 + Q_CHUNK) for c0 in range(0, m, Q_CHUNK)]
        staged, r_out, acc_out = [], [], []
        for cols in chunks:
            zc = z[:, cols]
            ls = jnp.minimum(zc, 0.0) - jnp.log(1.0 + jnp.exp(-jnp.abs(zc)))
            lk = ls - zc
            past = kpos < qpos[:, cols] if masked else None
            if masked:
                lk = jnp.where(past, lk, 0.0)
            parts = _dot(later, jnp.concatenate(_split2(lk), axis=1))
            staged.append((ls, parts, past))
            r_out.append(r_all[:, cols] + jnp.sum(lk, axis=0, keepdims=True))
        for cols, (ls, parts, past) in zip(chunks, staged):
            w = jnp.exp(ls + parts[:, :Q_CHUNK] + parts[:, Q_CHUNK:] + r_all[:, cols])
            if masked:
                w = jnp.where(past, w, 0.0)
            acc_out.append(acc_all[:, cols] + _dot(vt, w.astype(BF16)))
        r_ref[h] = jnp.concatenate(r_out, axis=1)
        acc_ref[h] = jnp.concatenate(acc_out, axis=1)

    jd = q0 // tk
    tile(jd, True)

    def live():
        return jnp.max(r_ref[...]) > DEAD_LOG

    def body(carry):
        n, _ = carry
        tile(jd - 1 - n, False)
        return n + 1, live()

    lax.while_loop(lambda c: (c[0] < jd) & c[1], body, (jnp.int32(0), live()))
    for h in range(nh):
        o_ref[0, :, KV_W * h:KV_W * (h + 1)] = _group_out_t(acc_ref[h], None, tq)


def _stick_prompt(qhm, khm, vt):
    B, _, T, _ = qhm.shape
    tq, tk, nh = 256, KV_TILE, KV_PER_STEP
    m = GROUP * tq
    return pl.pallas_call(
        functools.partial(_stick_kernel, tq=tq, tk=tk, nh=nh), grid=(B, N_KV // nh, T // tq),
        in_specs=[pl.BlockSpec((1, nh * GROUP, tq, HD), lambda b, g, i: (b, g, i, 0)),
                  pl.BlockSpec((1, nh, T, HD), lambda b, g, i: (b, g, 0, 0)),
                  pl.BlockSpec((1, T // tk, nh * HD, tk), lambda b, g, i: (b, 0, g, 0))],
        out_specs=pl.BlockSpec((1, tq, nh * KV_W), lambda b, g, i: (b, i, g)),
        out_shape=jax.ShapeDtypeStruct((B, T, D_MODEL), BF16),
        scratch_shapes=[pltpu.VMEM((nh, 1, m), F32), pltpu.VMEM((nh, HD, m), F32)],
        compiler_params=_cp("parallel", "parallel", "arbitrary"), name="stick_p")(qhm, khm, vt)


def _cumsum_kernel(x_ref, hi_ref, mid_ref, lo_ref, carry_ref):
    @pl.when(pl.program_id(1) == 0)
    def _():
        carry_ref[...] = jnp.zeros_like(carry_ref)

    n = x_ref.shape[1]
    low = _tri(n, "ge")
    hi, mid, lo = _split3(x_ref[0])
    c = _dot(low, hi) + _dot(low, mid) + _dot(low, lo) + carry_ref[...]
    carry_ref[...] = c[n - 1:n, :]
    hi_ref[0], mid_ref[0], lo_ref[0] = _split3(c * LOG2E)


def _cumsum_time_parts(x):
    B, T, C = x.shape
    n = 256
    spec = pl.BlockSpec((1, n, C), lambda b, i: (b, i, 0))
    return pl.pallas_call(
        _cumsum_kernel, grid=(B, T // n), in_specs=[spec], out_specs=[spec] * 3,
        out_shape=[jax.ShapeDtypeStruct((B, T, C), BF16)] * 3, scratch_shapes=[pltpu.VMEM((1, C), F32)],
        compiler_params=_cp("parallel", "arbitrary"), name="cumsum_time")(x)


def _softmax_scratch_t(n, m):
    return [pltpu.VMEM((n, 1, m), F32), pltpu.VMEM((n, 1, m), F32), pltpu.VMEM((n, HD, m), F32)]


def _causal_kernel(q_ref, k_ref, vt_ref, o_ref, m_ref, l_ref, acc_ref, *, tq, tk, nh):
    q0 = pl.program_id(2) * tq
    qs = [jnp.concatenate([q_ref[0, GROUP * h + j] for j in range(GROUP)], axis=0) for h in range(nh)]
    m = GROUP * tq
    _init_softmax(m_ref, l_ref, acc_ref)
    qpos = q0 + _iota((1, m), 1) % tq

    def tile(j, valid):
        ks = pl.multiple_of(j * tk, tk)
        s = [_dot_nt(k_ref[0, h, pl.ds(ks, tk), :], qs[h]) for h in range(nh)]
        for h in range(nh):
            _online_t(s[h], valid, vt_ref[0, j, HD * h:HD * (h + 1), :], m_ref, l_ref, acc_ref, h)

    jd = q0 // tk
    tile(jd, (jd * tk + _iota((tk, 1), 0)) <= qpos)

    def body(n, carry):
        tile(n, None)
        return carry

    lax.fori_loop(0, jd, body, 0)
    for h in range(nh):
        o_ref[0, :, KV_W * h:KV_W * (h + 1)] = _group_out_t(acc_ref[h], l_ref[h], tq)


def _fox_prompt(q_aug, khm, vt, cum_parts):
    B, _, T, d_aug = q_aug.shape
    tq, tk = 256, KV_TILE
    c = jnp.stack(cum_parts, axis=-1).reshape(B, T, N_KV, FOX_AUG).transpose(0, 2, 1, 3)
    k_aug = jnp.concatenate([khm, c, jnp.zeros((B, N_KV, T, d_aug - HD - FOX_AUG), BF16)], axis=-1)
    m = GROUP * tq
    return pl.pallas_call(
        functools.partial(_causal_kernel, tq=tq, tk=tk, nh=KV_PER_STEP), grid=(B, N_KV // KV_PER_STEP, T // tq),
        in_specs=[pl.BlockSpec((1, KV_PER_STEP * GROUP, tq, d_aug), lambda b, g, i: (b, g, i, 0)),
                  pl.BlockSpec((1, KV_PER_STEP, T, d_aug), lambda b, g, i: (b, g, 0, 0)),
                  pl.BlockSpec((1, T // tk, KV_PER_STEP * HD, tk), lambda b, g, i: (b, 0, g, 0))],
        out_specs=pl.BlockSpec((1, tq, KV_PER_STEP * KV_W), lambda b, g, i: (b, i, g)),
        out_shape=jax.ShapeDtypeStruct((B, T, D_MODEL), BF16),
        scratch_shapes=_softmax_scratch_t(KV_PER_STEP, m),
        compiler_params=_cp("parallel", "parallel", "arbitrary"), name="fox_p")(q_aug, k_aug, vt)


def _block_mean_kernel(k_ref, o_ref):
    k = k_ref[0]
    nb = k.shape[0] // MOBA_BLOCK
    o_ref[0] = jnp.sum(k.reshape(nb, MOBA_BLOCK, KV_W), axis=1) * (1.0 / MOBA_BLOCK)


def _block_mean(k32):
    B, T, _ = k32.shape
    nb = T // MOBA_BLOCK
    return pl.pallas_call(
        _block_mean_kernel, grid=(B,), in_specs=[pl.BlockSpec((1, T, KV_W), lambda b: (b, 0, 0))],
        out_specs=pl.BlockSpec((1, nb, KV_W), lambda b: (b, 0, 0)),
        out_shape=jax.ShapeDtypeStruct((B, nb, KV_W), F32), compiler_params=_cp("parallel"), name="block_mean")(k32)


def _moba_kernel(q_ref, q32_ref, km_ref, k_ref, vt_ref, o_ref, m_ref, l_ref, acc_ref, *, tq, nb, nh):
    tk = MOBA_BLOCK
    q0 = pl.program_id(2) * tq
    own = q0 // tk
    stack = lambda ref, h: jnp.concatenate([ref[0, GROUP * h + j] for j in range(GROUP)], axis=0)
    m = GROUP * tq
    _init_softmax(m_ref, l_ref, acc_ref)
    row = _iota((nb, m), 0)
    qs = []
    for h in range(nh):
        sel = _topk_select(_dot_nt_f32(km_ref[0, h], stack(q32_ref, h)), own, MOBA_TOPK, 0)
        bias = jnp.where((sel > 0.5) | (row == own), 0.0, NEG)
        bias = jnp.concatenate([jnp.zeros((HD, m), F32), bias, jnp.zeros((HD - nb, m), F32)], axis=0)
        qs.append(stack(q_ref, h) + bias.T.astype(BF16))
    qpos = q0 + _iota((1, m), 1) % tq

    def tile(j, valid):
        ks = pl.multiple_of(j * tk, tk)
        s = [_dot_nt(k_ref[0, h, pl.ds(ks, tk), :], qs[h]) for h in range(nh)]
        for h in range(nh):
            _online_t(s[h], valid, vt_ref[0, j, HD * h:HD * (h + 1), :], m_ref, l_ref, acc_ref, h)

    tile(own, (own * tk + _iota((tk, 1), 0)) <= qpos)

    def body(j, carry):
        tile(j, None)
        return carry

    lax.fori_loop(0, own, body, 0)
    for h in range(nh):
        o_ref[0, :, KV_W * h:KV_W * (h + 1)] = _group_out_t(acc_ref[h], l_ref[h], tq)


def _moba_prompt(q_aug, q32hm, kmean, k_aug, vt):
    B, _, T, _ = q_aug.shape
    tq, nh = 256, KV_PER_STEP
    nb = T // MOBA_BLOCK
    assert nb <= HD
    m = GROUP * tq
    km = kmean.reshape(B, nb, N_KV, HD).transpose(0, 2, 1, 3)
    return pl.pallas_call(
        functools.partial(_moba_kernel, tq=tq, nb=nb, nh=nh), grid=(B, N_KV // nh, T // tq),
        in_specs=[pl.BlockSpec((1, nh * GROUP, tq, 2 * HD), lambda b, g, i: (b, g, i, 0)),
                  pl.BlockSpec((1, nh * GROUP, tq, HD), lambda b, g, i: (b, g, i, 0)),
                  pl.BlockSpec((1, nh, nb, HD), lambda b, g, i: (b, g, 0, 0)),
                  pl.BlockSpec((1, nh, T, 2 * HD), lambda b, g, i: (b, g, 0, 0)),
                  pl.BlockSpec((1, nb, nh * HD, MOBA_BLOCK), lambda b, g, i: (b, 0, g, 0))],
        out_specs=pl.BlockSpec((1, tq, nh * KV_W), lambda b, g, i: (b, i, g)),
        out_shape=jax.ShapeDtypeStruct((B, T, D_MODEL), BF16),
        scratch_shapes=_softmax_scratch_t(nh, m),
        compiler_params=_cp("parallel", "parallel", "arbitrary"), name="moba_p")(q_aug, q32hm, km, k_aug, vt)


def _dsa_kernel(q_ref, qi_ref, wi_ref, ki_ref, k_ref, vt_ref, o_ref, key_ref, m_ref, l_ref, acc_ref, *, tq, n_keep):
    tk = KV_TILE
    q0 = pl.program_id(1) * tq
    n_proc = (q0 + tq + tk - 1) // tk
    qpos = q0 + _iota((1, tq), 1)

    def visible(c):
        return (c * tk + _iota((tk, 1), 0)) <= qpos

    qi3 = jnp.concatenate([qi_ref[0, i] for i in range(IDX_HEADS)], axis=0)

    def score_tile(c, carry):
        ks = pl.multiple_of(c * tk, tk)
        dots = _dot_nt(ki_ref[0, pl.ds(ks, tk), :], qi3)
        sc = jnp.zeros((tk, tq), F32)
        for i in range(IDX_HEADS):
            sc = sc + wi_ref[0, i:i + 1, :] * jnp.maximum(dots[:, i * tq:(i + 1) * tq], 0.0)
        key_ref[c] = _order_key(jnp.where(visible(c), sc + 0.0, -jnp.inf))
        return carry

    lax.fori_loop(0, n_proc, score_tile, 0)

    def count(pred, cand):
        out = []
        for c0 in range(0, tq, Q_CHUNK):
            cols = slice(c0, c0 + Q_CHUNK)

            def body(c, part):
                return part + jnp.where(pred(key_ref[c, :, cols], cand[:, cols]), 1.0, 0.0)

            out.append(jnp.sum(lax.fori_loop(0, n_proc, body, jnp.zeros((tk, Q_CHUNK), F32)), axis=0, keepdims=True))
        return jnp.concatenate(out, axis=1)

    thr = _kth_largest_key(lambda cand: count(lambda k, v: k >= v, cand), (1, tq), n_keep)
    need = n_keep - count(lambda k, v: k > v, thr)
    _init_softmax(m_ref, l_ref, acc_ref)
    before = _tri(tk, "gt")
    qs = [jnp.concatenate([q_ref[0, GROUP * g + j] for j in range(GROUP)], axis=0) for g in range(N_KV)]

    def attend(c, ties_seen):
        ks = pl.multiple_of(c * tk, tk)
        key = key_ref[c]
        tied = jnp.where(key == thr, 1.0, 0.0)
        rank = _dot(before, tied.astype(BF16)) + ties_seen
        keep = (key > thr) | ((key == thr) & (rank < need))
        bias = jnp.where(keep & visible(c), 0.0, NEG)
        bias = jnp.concatenate([bias] * GROUP, axis=1)
        s = [_dot_nt(k_ref[0, g, pl.ds(ks, tk), :], qs[g]) for g in range(N_KV)]
        for g in range(N_KV):
            _online_t(s[g], None, vt_ref[0, c, HD * g:HD * (g + 1), :], m_ref, l_ref, acc_ref, g, bias=bias)
        return ties_seen + jnp.sum(tied, axis=0, keepdims=True)

    lax.fori_loop(0, n_proc, attend, jnp.zeros((1, tq), F32))
    for g in range(N_KV):
        o_ref[0, :, KV_W * g:KV_W * (g + 1)] = _group_out_t(acc_ref[g], l_ref[g], tq)


def _dsa_prompt(qhm, qi3, wi, ki3, khm, vt):
    B, _, T, _ = qhm.shape
    tq = 128
    m = GROUP * tq
    return pl.pallas_call(
        functools.partial(_dsa_kernel, tq=tq, n_keep=min(IDX_TOPK, T // 4)), grid=(B, T // tq),
        in_specs=[pl.BlockSpec((1, N_HEADS, tq, HD), lambda b, i: (b, 0, i, 0)),
                  pl.BlockSpec((1, IDX_HEADS, tq, 4 * IDX_DIM), lambda b, i: (b, 0, i, 0)),
                  pl.BlockSpec((1, IDX_HEADS, tq), lambda b, i: (b, 0, i)),
                  pl.BlockSpec((1, T, 4 * IDX_DIM), lambda b, i: (b, 0, 0)),
                  pl.BlockSpec((1, N_KV, T, HD), lambda b, i: (b, 0, 0, 0)),
                  pl.BlockSpec((1, T // KV_TILE, KV_W, KV_TILE), lambda b, i: (b, 0, 0, 0))],
        out_specs=pl.BlockSpec((1, tq, D_MODEL), lambda b, i: (b, i, 0)),
        out_shape=jax.ShapeDtypeStruct((B, T, D_MODEL), BF16),
        scratch_shapes=[pltpu.VMEM((T // KV_TILE, KV_TILE, tq), I32)] + _softmax_scratch_t(N_KV, m),
        compiler_params=_cp("parallel", "arbitrary"), name="dsa_p")(qhm, qi3, wi.transpose(0, 2, 1), ki3, khm, vt)


def _page_specs(layer, n_pages, nps, reverse):
    def spec(i):
        def idx(b, s, pt):
            p = s * nps + i
            return (layer, pt[b, n_pages - 1 - p if reverse else p], 0, 0)

        return pl.BlockSpec((None, None, KV_W, PAGE), idx)

    return [spec(i) for i in range(nps)]


def _pool_specs(rows, n_pages, nps, reverse):
    def spec(i):
        def idx(b, s, pt):
            p = s * nps + i
            return (pt[b, n_pages - 1 - p if reverse else p], 0, 0)

        return pl.BlockSpec((None, rows, PAGE), idx)

    return [spec(i) for i in range(nps)]


def _row_spec(width):
    return pl.BlockSpec((1, 1, width), lambda b, s, pt: (b, 0, 0))


def _head_spec(width):
    return pl.BlockSpec((1, N_HEADS, width), lambda b, s, pt: (b, 0, 0))


def _softmax_scratch():
    return [pltpu.VMEM((N_HEADS, 1), F32), pltpu.VMEM((N_HEADS, 1), F32), pltpu.VMEM((N_HEADS, KV_W), F32)]


def _new_key_start(q, knew, vnew, m_ref, l_ref, acc_ref):
    z = jnp.sum(q.astype(F32) * knew.astype(BF16).astype(F32), axis=1, keepdims=True)
    m_ref[...] = z
    l_ref[...] = jnp.ones_like(z)
    acc_ref[...] = jnp.broadcast_to(vnew.astype(BF16).astype(F32), acc_ref.shape)


def _cat_pages(refs):
    return jnp.concatenate([r[...].astype(BF16) for r in refs], axis=1)


def _later_sums(x, parts):
    n = x.shape[1] // PAGE
    stack = jnp.concatenate([p[:, PAGE * i:PAGE * (i + 1)] for p in parts(x) for i in range(n)], axis=0)
    out = _dot(stack, _tri(PAGE, "gt"))
    rows = x.shape[0]
    tot = None
    for k in range(out.shape[0] // (n * rows)):
        part = jnp.concatenate([out[(k * n + i) * rows:(k * n + i + 1) * rows] for i in range(n)], axis=1)
        tot = part if tot is None else tot + part
    return tot


def _page_totals(x):
    return [jnp.sum(x[:, PAGE * i:PAGE * (i + 1)], axis=1, keepdims=True) for i in range(x.shape[1] // PAGE)]


def _per_page(cols):
    return jnp.concatenate([jnp.broadcast_to(c, (c.shape[0], PAGE)) for c in cols], axis=1)


def _online_pages(s, valid, vcat, m_ref, l_ref, acc_ref):
    if valid is not None:
        s = jnp.where(valid, s, NEG)
    m_old = m_ref[...]
    m_new = jnp.maximum(m_old, jnp.max(s, axis=1, keepdims=True))
    alpha = jnp.exp2(m_old - m_new)
    p = jnp.exp2(s - m_new)
    if valid is not None:
        p = jnp.where(valid, p, 0.0)
    l_ref[...] = alpha * l_ref[...] + jnp.sum(p, axis=1, keepdims=True)
    acc_ref[...] = alpha * acc_ref[...] + _dot_nt(p.astype(BF16), vcat)
    m_ref[...] = m_new


def _dec_stick_kernel(pt_ref, q_ref, *refs, nps):
    k_refs, v_refs = refs[:nps], refs[nps:2 * nps]
    o_ref, r_ref, acc_ref = refs[2 * nps:]
    step = pl.program_id(1)

    @pl.when(step == 0)
    def _():
        r_ref[...] = jnp.zeros_like(r_ref)
        acc_ref[...] = jnp.zeros_like(acc_ref)

    ls, lk = _log_sigmoid_pair(_dot(q_ref[0], _cat_pages(k_refs)))
    after, run = [], r_ref[...]
    for tot in _page_totals(lk):
        after.append(run)
        run = run + tot
    w = jnp.exp(ls + _later_sums(lk, _split2) + _per_page(after))
    acc = acc_ref[...] + _dot_nt(w.astype(BF16), _cat_pages(v_refs))
    r_ref[...], acc_ref[...] = run, acc

    @pl.when(step == pl.num_programs(1) - 1)
    def _():
        o_ref[0] = acc


def _dec_call(kernel, name, grid_spec, DB):
    return pl.pallas_call(kernel, grid_spec=grid_spec, out_shape=jax.ShapeDtypeStruct((DB, N_HEADS, KV_W), F32),
                          compiler_params=_cp("parallel", "arbitrary"), name=name)


def _dec_stick(qx, kt, vt, page_table, layer):
    DB, n_pages = page_table.shape
    nps = math.gcd(n_pages, DEC_PAGES)
    pages = _page_specs(layer, n_pages, nps, True)
    grid_spec = pltpu.PrefetchScalarGridSpec(
        num_scalar_prefetch=1, grid=(DB, n_pages // nps), in_specs=[_head_spec(KV_W)] + pages + pages,
        out_specs=_head_spec(KV_W), scratch_shapes=[pltpu.VMEM((N_HEADS, 1), F32), pltpu.VMEM((N_HEADS, KV_W), F32)])
    return _dec_call(functools.partial(_dec_stick_kernel, nps=nps), "stick_s", grid_spec, DB)(
        page_table, qx, *([kt] * nps), *([vt] * nps))


def _dec_fox_kernel(pt_ref, q_ref, *refs, nps):
    k_refs, v_refs, lf_refs = refs[:nps], refs[nps:2 * nps], refs[2 * nps:3 * nps]
    kn_ref, vn_ref, lfn_ref, o_ref, s_ref, m_ref, l_ref, acc_ref = refs[3 * nps:]
    step = pl.program_id(1)

    @pl.when(step == 0)
    def _():
        s_ref[...] = lfn_ref[0]
        _new_key_start(q_ref[0], kn_ref[0], vn_ref[0], m_ref, l_ref, acc_ref)

    lf = jnp.concatenate([r[...] for r in lf_refs], axis=1)
    after, run = [], s_ref[...]
    for tot in _page_totals(lf):
        after.append(run)
        run = run + tot
    s_ref[...] = run
    s = _dot(q_ref[0], _cat_pages(k_refs)) + (_later_sums(lf, _split3) + _per_page(after)) * LOG2E
    _online_pages(s, None, _cat_pages(v_refs), m_ref, l_ref, acc_ref)

    @pl.when(step == pl.num_programs(1) - 1)
    def _():
        o_ref[0] = acc_ref[...] / l_ref[...]


def _dec_fox(qx, kt, vt, logf_t, knew, vnew, lfnew, page_table, layer):
    DB, n_pages = page_table.shape
    nps = math.gcd(n_pages, DEC_PAGES)
    pages = _page_specs(layer, n_pages, nps, True)
    grid_spec = pltpu.PrefetchScalarGridSpec(
        num_scalar_prefetch=1, grid=(DB, n_pages // nps),
        in_specs=[_head_spec(KV_W)] + pages + pages + _pool_specs(N_HEADS, n_pages, nps, True)
        + [_row_spec(KV_W), _row_spec(KV_W), _head_spec(1)],
        out_specs=_head_spec(KV_W), scratch_shapes=[pltpu.VMEM((N_HEADS, 1), F32)] + _softmax_scratch())
    return _dec_call(functools.partial(_dec_fox_kernel, nps=nps), "fox_s", grid_spec, DB)(
        page_table, qx, *([kt] * nps), *([vt] * nps), *([logf_t] * nps), knew, vnew, lfnew)


def _dec_kmean_kernel(pt_ref, *refs, nps):
    k_refs, o_ref = refs[:nps], refs[nps]
    step = pl.program_id(1)

    @pl.when(step == 0)
    def _():
        o_ref[...] = jnp.zeros_like(o_ref)

    ones = jnp.ones((8, PAGE), BF16)
    for i in range(nps):
        hi, mid, lo = _split3(k_refs[i][...])
        tot = _dot_nt(ones, hi) + _dot_nt(ones, mid) + _dot_nt(ones, lo)
        blk = (step * nps + i) // (MOBA_BLOCK // PAGE)
        o_ref[0, pl.ds(blk, 1), :] = o_ref[0, pl.ds(blk, 1), :] + tot[0:1, :] * (1.0 / MOBA_BLOCK)


def _dec_kmean(kt, page_table, layer):
    DB, n_pages = page_table.shape
    nps = math.gcd(n_pages, DEC_PAGES)
    nb = n_pages * PAGE // MOBA_BLOCK
    grid_spec = pltpu.PrefetchScalarGridSpec(
        num_scalar_prefetch=1, grid=(DB, n_pages // nps), in_specs=_page_specs(layer, n_pages, nps, False),
        out_specs=pl.BlockSpec((1, nb, KV_W), lambda b, s, pt: (b, 0, 0)))
    return pl.pallas_call(
        functools.partial(_dec_kmean_kernel, nps=nps), grid_spec=grid_spec,
        out_shape=jax.ShapeDtypeStruct((DB, nb, KV_W), F32),
        compiler_params=_cp("parallel", "arbitrary"), name="kmean_s")(page_table, *([kt] * nps))


def _dec_moba_kernel(pt_ref, q_ref, q32_ref, km_ref, *refs, nps, nb):
    k_refs, v_refs = refs[:nps], refs[nps:2 * nps]
    kn_ref, vn_ref, o_ref, sel_ref, m_ref, l_ref, acc_ref = refs[2 * nps:]
    step = pl.program_id(1)

    @pl.when(step == 0)
    def _():
        sel_ref[...] = _topk_select(_dot_nt_f32(q32_ref[0], km_ref[0]), nb, MOBA_TOPK, 1)
        _new_key_start(q_ref[0], kn_ref[0], vn_ref[0], m_ref, l_ref, acc_ref)

    lane = _iota((N_HEADS, nb), 1)
    sel = sel_ref[...]
    bias = []
    for i in range(nps):
        blk = (step * nps + i) // (MOBA_BLOCK // PAGE)
        picked = jnp.sum(jnp.where(lane == blk, sel, 0.0), axis=1, keepdims=True) > 0.5
        bias.append(jnp.where(picked, 0.0, NEG))
    s = _dot(q_ref[0], _cat_pages(k_refs)) + _per_page(bias)
    _online_pages(s, None, _cat_pages(v_refs), m_ref, l_ref, acc_ref)

    @pl.when(step == pl.num_programs(1) - 1)
    def _():
        o_ref[0] = acc_ref[...] / l_ref[...]


def _dec_moba(qx, q32x, kmean, kt, vt, knew, vnew, page_table, layer):
    DB, n_pages = page_table.shape
    nps = math.gcd(n_pages, DEC_PAGES)
    nb = kmean.shape[1]
    pages = _page_specs(layer, n_pages, nps, False)
    grid_spec = pltpu.PrefetchScalarGridSpec(
        num_scalar_prefetch=1, grid=(DB, n_pages // nps),
        in_specs=[_head_spec(KV_W), _head_spec(KV_W), pl.BlockSpec((1, nb, KV_W), lambda b, s, pt: (b, 0, 0))]
        + pages + pages + [_row_spec(KV_W), _row_spec(KV_W)],
        out_specs=_head_spec(KV_W), scratch_shapes=[pltpu.VMEM((N_HEADS, nb), F32)] + _softmax_scratch())
    return _dec_call(functools.partial(_dec_moba_kernel, nps=nps, nb=nb), "moba_s", grid_spec, DB)(
        page_table, qx, q32x, kmean, *([kt] * nps), *([vt] * nps), knew, vnew)


def _dec_score_kernel(pt_ref, qi_ref, wi_ref, *refs, nps, n_pages):
    kidx_refs = refs[:nps]
    kin_ref, o_ref = refs[nps:]
    step = pl.program_id(1)
    qi, wi = qi_ref[0], wi_ref[0]

    @pl.when(step == 0)
    def _():
        o_ref[...] = jnp.full(o_ref.shape, -jnp.inf, F32)
        dn = jnp.sum(qi * kin_ref[0], axis=1, keepdims=True)
        sn = jnp.sum(wi * jnp.maximum(dn, 0.0), axis=0, keepdims=True) + 0.0
        o_ref[0, n_pages:n_pages + 1, :] = jnp.where(_iota((1, PAGE), 1) == 0, sn, -jnp.inf)

    for i in range(nps):
        dots = _dot_f32(qi, kidx_refs[i][...])
        o_ref[0, pl.ds(step * nps + i, 1), :] = jnp.sum(wi * jnp.maximum(dots, 0.0), axis=0, keepdims=True) + 0.0


def _dec_scores(qi3, wi3, kidx_t, kinew, page_table):
    DB, n_pages = page_table.shape
    assert n_pages < PAGE
    nps = math.gcd(n_pages, DEC_PAGES)
    grid_spec = pltpu.PrefetchScalarGridSpec(
        num_scalar_prefetch=1, grid=(DB, n_pages // nps),
        in_specs=[pl.BlockSpec((1, IDX_HEADS, IDX_DIM), lambda b, s, pt: (b, 0, 0)),
                  pl.BlockSpec((1, IDX_HEADS, 1), lambda b, s, pt: (b, 0, 0))]
        + _pool_specs(IDX_DIM, n_pages, nps, False) + [_row_spec(IDX_DIM)],
        out_specs=pl.BlockSpec((1, PAGE, PAGE), lambda b, s, pt: (b, 0, 0)))
    return pl.pallas_call(
        functools.partial(_dec_score_kernel, nps=nps, n_pages=n_pages), grid_spec=grid_spec,
        out_shape=jax.ShapeDtypeStruct((DB, PAGE, PAGE), F32),
        compiler_params=_cp("parallel", "arbitrary"), name="dsa_score_s")(page_table, qi3, wi3, *([kidx_t] * nps), kinew)


def _dec_select_kernel(sc_ref, o_ref, *, n_keys, n_keep):
    key = _order_key(sc_ref[0])

    def total(x):
        return jnp.sum(jnp.sum(x, axis=1, keepdims=True), axis=0, keepdims=True)

    thr = _kth_largest_key(lambda cand: total(jnp.where(key >= cand, 1.0, 0.0)), (1, 1), n_keep)
    need = n_keep - total(jnp.where(key > thr, 1.0, 0.0))
    tied = jnp.where(key == thr, 1.0, 0.0)
    in_row = _dot(tied.astype(BF16), _tri(PAGE, "lt"))
    row_tot = jnp.broadcast_to(jnp.sum(tied, axis=1, keepdims=True), tied.shape).astype(BF16)
    rank = in_row + _dot(_tri(PAGE, "gt"), row_tot)
    pos = _iota(key.shape, 0) * PAGE + _iota(key.shape, 1)
    keep = ((key > thr) | ((key == thr) & (rank < need))) & (pos < n_keys)
    o_ref[0] = jnp.where(keep, 1.0, 0.0)


def _dec_select(scores, n_keys):
    DB = scores.shape[0]
    spec = pl.BlockSpec((1, PAGE, PAGE), lambda b: (b, 0, 0))
    return pl.pallas_call(
        functools.partial(_dec_select_kernel, n_keys=n_keys, n_keep=min(IDX_TOPK, n_keys // 4)), grid=(DB,),
        in_specs=[spec], out_specs=spec, out_shape=jax.ShapeDtypeStruct(scores.shape, F32),
        compiler_params=_cp("parallel"), name="dsa_select_s")(scores)


def _dec_dsa_kernel(pt_ref, q_ref, *refs, nps, n_pages):
    k_refs, v_refs = refs[:nps], refs[nps:2 * nps]
    sel_ref, kn_ref, vn_ref, o_ref, m_ref, l_ref, acc_ref = refs[2 * nps:]
    step = pl.program_id(1)
    q = q_ref[0]

    @pl.when(step == 0)
    def _():
        zn = jnp.sum(q.astype(F32) * kn_ref[0].astype(BF16).astype(F32), axis=1, keepdims=True)
        keep = jnp.broadcast_to(sel_ref[0, n_pages:n_pages + 1, 0:1] > 0.5, zn.shape)
        m_ref[...] = jnp.where(keep, zn, NEG)
        l_ref[...] = jnp.where(keep, 1.0, 0.0)
        acc_ref[...] = jnp.where(keep, jnp.broadcast_to(vn_ref[0].astype(BF16).astype(F32), acc_ref.shape), 0.0)

    keep = jnp.concatenate([sel_ref[0, pl.ds(step * nps + i, 1), :] for i in range(nps)], axis=1)
    s = _dot(q, _cat_pages(k_refs))
    _online_pages(s, jnp.broadcast_to(keep > 0.5, s.shape), _cat_pages(v_refs), m_ref, l_ref, acc_ref)

    @pl.when(step == pl.num_programs(1) - 1)
    def _():
        o_ref[0] = acc_ref[...] / l_ref[...]


def _dec_dsa(qx, kt, vt, sel, knew, vnew, page_table, layer):
    DB, n_pages = page_table.shape
    nps = math.gcd(n_pages, DEC_PAGES)
    pages = _page_specs(layer, n_pages, nps, False)
    grid_spec = pltpu.PrefetchScalarGridSpec(
        num_scalar_prefetch=1, grid=(DB, n_pages // nps),
        in_specs=[_head_spec(KV_W)] + pages + pages
        + [pl.BlockSpec((1, PAGE, PAGE), lambda b, s, pt: (b, 0, 0)), _row_spec(KV_W), _row_spec(KV_W)],
        out_specs=_head_spec(KV_W), scratch_shapes=_softmax_scratch())
    return _dec_call(functools.partial(_dec_dsa_kernel, nps=nps, n_pages=n_pages), "dsa_s", grid_spec, DB)(
        page_table, qx, *([kt] * nps), *([vt] * nps), sel, knew, vnew)


def _rope_tables(pos):
    half = HD // 2
    inv_freq = jnp.power(ROPE_THETA, -jnp.arange(half, dtype=F32) * (2.0 / HD))
    ang = pos.astype(F32)[:, None] * inv_freq[None, :]
    cos, sin = jnp.cos(ang), jnp.sin(ang)
    return jnp.concatenate([cos] * 4, axis=1), jnp.concatenate([-sin, sin] * 2, axis=1)


def _pad_cols(w, mult=128):
    pad = (-w.shape[1]) % mult
    return jnp.pad(w, ((0, 0), (0, pad))) if pad else w


def _layer_weights(kind, w_qkv, fox_w_f, idx_w_q, idx_w_k, idx_w_w):
    cols = [w_qkv]
    if kind == FOX:
        cols.append(_pad_cols(fox_w_f))
    if kind == DSA:
        cols += [idx_w_q, _pad_cols(idx_w_k), _pad_cols(idx_w_w)]
    return jnp.concatenate(cols, axis=1).astype(BF16)


_HEAD_ONEHOT = np.arange(N_HEADS)[:, None] // GROUP == np.arange(N_KV)[None, :]


def _expand_heads(q, dtype):
    m = jnp.asarray(_HEAD_ONEHOT, q.dtype)
    return (q[:, :, None, :] * m[None, :, :, None]).reshape(q.shape[0], N_HEADS, KV_W).astype(dtype)


def _collapse_heads(ox):
    m = jnp.asarray(_HEAD_ONEHOT, ox.dtype)
    o = jnp.sum(ox.reshape(ox.shape[0], N_HEADS, N_KV, HD) * m[None, :, :, None], axis=2)
    return o.reshape(ox.shape[0], D_MODEL).astype(BF16)


def _pages_t(cache):
    depth, n_pool = cache.shape[:2]
    return cache.transpose(0, 1, 3, 4, 2).reshape(depth, n_pool, KV_W, PAGE)


def kernel(x_prompt, x_sample, cache_k, cache_v, cache_logf, cache_kidx, state_conv, page_table, norm_mix, norm_ffn, norm_final, w_qkv, w_o, fox_w_f, fox_b_f, idx_w_q, idx_w_k, idx_w_w, ffn_w_a, ffn_w_b, ffn_conv_w, ffn_conv_b, ffn_w_down):
    B, T, _ = x_prompt.shape
    DB = x_sample.shape[0]
    depth = cache_k.shape[0]
    n_pages = page_table.shape[1]
    past = n_pages * PAGE
    assert T % MOBA_BLOCK == 0 and past % MOBA_BLOCK == 0 and x_sample.shape[1] == 1
    cache_kt, cache_vt = _pages_t(cache_k), _pages_t(cache_v)
    rope_p = _rope_tables(jnp.arange(T, dtype=I32))
    rope_s = _rope_tables(jnp.full((DB,), past, I32))

    xp = x_prompt
    xs = x_sample.reshape(1, DB, D_MODEL)
    new_k_p, new_v_p, new_k_s, new_v_s, conv_p, conv_s = [], [], [], [], [], []
    logf_p = logf_s = kidx_p = kidx_s = None
    for i in range(depth):
        kind = i % 4
        rope = kind in (MOBA, DSA)
        w_all = _layer_weights(kind, w_qkv[i], fox_w_f, idx_w_q, idx_w_k, idx_w_w)
        wo_b = w_o[i].astype(BF16)
        wa_b, wb_b, wd_b = ffn_w_a[i].astype(BF16), ffn_w_b[i].astype(BF16), ffn_w_down[i].astype(BF16)

        outs = _proj(xp, norm_mix[i], w_all, kind, rope_p if rope else None, fox_b_f, True)
        qhm, k32, v32, khm, vhm = outs[:5]
        vt = outs[-1]
        new_k_p.append(k32.reshape(B, T, N_KV, HD))
        new_v_p.append(v32.reshape(B, T, N_KV, HD))
        if kind == STICK:
            o = _stick_prompt(qhm, khm, vt)
        elif kind == FOX:
            logf_p = outs[5]
            o = _fox_prompt(qhm, khm, vt, _cumsum_time_parts(logf_p))
        elif kind == MOBA:
            o = _moba_prompt(qhm, outs[5], _block_mean(k32), khm, vt)
        else:
            kidx_p, wi, qi3, ki3 = outs[6:10]
            o = _dsa_prompt(qhm, qi3, wi, ki3, khm, vt)
        x2, rows = _mix_ffn(xp.reshape(B * T, D_MODEL), o.reshape(B * T, D_MODEL), wo_b, norm_ffn[i], wa_b, wb_b,
                            ffn_conv_w[i], ffn_conv_b[i], wd_b, T, final_g=norm_final if i == depth - 1 else None)
        conv_p.append(rows)
        xp = x2.reshape(B, T, D_MODEL)

        outs = _proj(xs, norm_mix[i], w_all, kind, rope_s if rope else None, fox_b_f, False)
        qhm, k32, v32 = outs[:3]
        new_k_s.append(k32.reshape(DB, 1, N_KV, HD))
        new_v_s.append(v32.reshape(DB, 1, N_KV, HD))
        qx = _expand_heads(qhm[0].transpose(1, 0, 2), BF16)
        knew, vnew = k32.reshape(DB, 1, KV_W), v32.reshape(DB, 1, KV_W)
        if kind == STICK:
            ox = _dec_stick(qx, cache_kt, cache_vt, page_table, i)
        elif kind == FOX:
            lf = outs[5][0]
            logf_s = lf.reshape(DB, 1, N_HEADS)
            ox = _dec_fox(qx, cache_kt, cache_vt, cache_logf.transpose(0, 2, 1), knew, vnew,
                          lf.reshape(DB, N_HEADS, 1), page_table, i)
        elif kind == MOBA:
            q32x = _expand_heads(outs[5][0].transpose(1, 0, 2), F32)
            ox = _dec_moba(qx, q32x, _dec_kmean(cache_kt, page_table, i), cache_kt, cache_vt, knew, vnew, page_table, i)
        else:
            qihm, ki, wi = outs[5:8]
            kidx_s = ki.reshape(DB, 1, IDX_DIM)
            scores = _dec_scores(qihm[0].transpose(1, 0, 2), wi[0].reshape(DB, IDX_HEADS, 1),
                                 cache_kidx.transpose(0, 2, 1), kidx_s, page_table)
            ox = _dec_dsa(qx, cache_kt, cache_vt, _dec_select(scores, past + 1), knew, vnew, page_table, i)
        st = state_conv[i]
        s2, a_new = _mix_ffn(xs.reshape(DB, D_MODEL), _collapse_heads(ox), wo_b, norm_ffn[i], wa_b, wb_b,
                             ffn_conv_w[i], ffn_conv_b[i], wd_b, 1, prev=(st[:, 0], st[:, 1]),
                             final_g=norm_final if i == depth - 1 else None)
        conv_s.append(jnp.stack([st[:, 1], a_new], axis=1))
        xs = s2.reshape(1, DB, D_MODEL)

    y_p, y_s = xp, xs.reshape(DB, 1, D_MODEL)
    return (y_p, y_s, jnp.stack(new_k_p), jnp.stack(new_v_p), jnp.stack(new_k_s), jnp.stack(new_v_s),
            logf_p, logf_s, kidx_p, kidx_s, jnp.stack(conv_p), jnp.stack(conv_s))
```
